```python
import math
import jax, jax.numpy as jnp
from jax import lax
import numpy as np

D_MODEL = 4096
BATCH = 8
SEQ = 4096
DEPTH = 1

MIX_WIDTH = D_MODEL
A_HEADS = 16
NOPE_DIM = 128
ROPE_DIM = 64
V_DIM = 128
A_WIDTH = A_HEADS * V_DIM
Q_LORA = 1024
KV_LORA = 512
ROPE_THETA = 10000.0
Q_BLOCK = 128
G_WIDTH = MIX_WIDTH - A_WIDTH
G_HEADS = 16
G_HEAD_DIM = G_WIDTH // G_HEADS
CHUNK = 128
D_FF = ((8 * D_MODEL // 3 + 255) // 256) * 256
IN_WIDTH = Q_LORA + KV_LORA + ROPE_DIM + 2 * G_WIDTH
EPS = 1e-6

kernel_name = "hybrid_mla_gmlp_sandwich_block"


def rms_norm(x, g):
    xf = x.astype(jnp.float32)
    y = xf * lax.rsqrt(jnp.mean(xf * xf, axis=-1, keepdims=True) + EPS)
    return (y * g.astype(jnp.float32)).astype(x.dtype)


def layer_norm(x, g, b):
    xf = x.astype(jnp.float32)
    mu = jnp.mean(xf, axis=-1, keepdims=True)
    xc = xf - mu
    y = xc * lax.rsqrt(jnp.mean(xc * xc, axis=-1, keepdims=True) + EPS)
    return (y * g.astype(jnp.float32) + b.astype(jnp.float32)).astype(x.dtype)


def rope_tables(positions, dtype):
    inv_freq = 1.0 / (ROPE_THETA ** (jnp.arange(0, ROPE_DIM, 2, dtype=jnp.float32) / ROPE_DIM))
    ang = positions.astype(jnp.float32)[..., None] * inv_freq
    return jnp.cos(ang).astype(dtype), jnp.sin(ang).astype(dtype)


def apply_rope(t, cos, sin):
    half = t.shape[-1] // 2
    t1, t2 = t[..., :half], t[..., half:]
    return jnp.concatenate([t1 * cos - t2 * sin, t2 * cos + t1 * sin], axis=-1)


def mla_attention(q_nope, q_rope, k_nope, k_rope, v):
    B, S, H, _ = q_nope.shape
    nb = S // Q_BLOCK
    scale = 1.0 / math.sqrt(NOPE_DIM + ROPE_DIM)
    qn = q_nope.reshape(B, nb, Q_BLOCK, H, NOPE_DIM).transpose(1, 0, 2, 3, 4)
    qr = q_rope.reshape(B, nb, Q_BLOCK, H, ROPE_DIM).transpose(1, 0, 2, 3, 4)

    def block(args):
        qn_b, qr_b = args
        s = (jnp.einsum('bqhd,bkhd->bhqk', qn_b, k_nope, preferred_element_type=jnp.float32)
             + jnp.einsum('bqhr,bkr->bhqk', qr_b, k_rope, preferred_element_type=jnp.float32))
        p = jax.nn.softmax(s * scale, axis=-1).astype(v.dtype)
        return jnp.einsum('bhqk,bkhd->bqhd', p, v)

    out = lax.map(block, (qn, qr))
    return out.transpose(1, 0, 2, 3, 4).reshape(B, S, H * V_DIM)


def spatial_gating(u, v, v_ln_g, v_ln_b, w_s, b_s):
    B, S, _ = v.shape
    v = layer_norm(v, v_ln_g, v_ln_b)
    vc = v.reshape(B, S // CHUNK, CHUNK, G_HEADS, G_HEAD_DIM)
    s = jnp.einsum('gpq,bcqgd->bcpgd', w_s, vc) + jnp.transpose(b_s)[None, None, :, :, None]
    return u * s.reshape(B, S, G_WIDTH)


def _fwd_setup_inputs(seed: int = 0) -> dict:
    key = jax.random.key(seed)
    ks = jax.random.split(key, 24)
    f32 = jnp.float32
    L = DEPTH

    def nrm(k, shape, fan_in):
        return jax.random.normal(k, shape, f32) * (fan_in ** -0.5)

    def gain(k, n):
        return 1.0 + 0.02 * jax.random.normal(k, (L, n), f32)

    x = jax.random.normal(ks[0], (BATCH, SEQ, D_MODEL), f32)
    offs = jax.random.randint(ks[1], (BATCH, 1), 0, SEQ, dtype=jnp.int32)
    positions = jnp.arange(SEQ, dtype=jnp.int32)[None, :] + offs
    return {
        "x": x,
        "positions": positions,
        "pre_mix_norm": gain(ks[2], D_MODEL),
        "w_in": nrm(ks[3], (L, D_MODEL, IN_WIDTH), D_MODEL),
        "q_norm": gain(ks[4], Q_LORA),
        "kv_norm": gain(ks[5], KV_LORA),
        "w_uq": nrm(ks[6], (L, Q_LORA, A_HEADS * (NOPE_DIM + ROPE_DIM)), Q_LORA),
        "w_ukv": nrm(ks[7], (L, KV_LORA, A_HEADS * (NOPE_DIM + V_DIM)), KV_LORA),
        "v_ln_gain": gain(ks[8], G_WIDTH),
        "v_ln_bias": 0.02 * jax.random.normal(ks[9], (L, G_WIDTH), f32),
        "w_spatial": nrm(ks[10], (L, G_HEADS, CHUNK, CHUNK), CHUNK),
        "b_spatial": 1.0 + 0.1 * jax.random.normal(ks[11], (L, G_HEADS, CHUNK), f32),
        "attn_out_norm": gain(ks[12], A_WIDTH),
        "gmlp_out_norm": gain(ks[13], G_WIDTH),
        "w_out": nrm(ks[14], (L, MIX_WIDTH, D_MODEL), MIX_WIDTH),
        "post_mix_norm": gain(ks[15], D_MODEL),
        "pre_ffn_norm": gain(ks[16], D_MODEL),
        "w_gate": nrm(ks[17], (L, D_MODEL, D_FF), D_MODEL),
        "w_up": nrm(ks[18], (L, D_MODEL, D_FF), D_MODEL),
        "w_down": nrm(ks[19], (L, D_FF, D_MODEL), D_FF),
        "post_ffn_norm": gain(ks[20], D_MODEL),
    }


def _fwd_reference(x, positions, pre_mix_norm, w_in, q_norm, kv_norm, w_uq, w_ukv,
              v_ln_gain, v_ln_bias, w_spatial, b_spatial, attn_out_norm, gmlp_out_norm,
              w_out, post_mix_norm, pre_ffn_norm, w_gate, w_up, w_down, post_ffn_norm):
    B, S, _ = x.shape
    cos, sin = rope_tables(positions, x.dtype)
    splits = np.cumsum([Q_LORA, KV_LORA, ROPE_DIM, G_WIDTH]).tolist()

    for l in range(DEPTH):
        xn = rms_norm(x, pre_mix_norm[l])
        proj = jnp.einsum('bsd,de->bse', xn, w_in[l])
        q_c, kv_c, k_rope, u, v = jnp.split(proj, splits, axis=-1)

        q = jnp.einsum('bsr,re->bse', rms_norm(q_c, q_norm[l]), w_uq[l])
        q = q.reshape(B, S, A_HEADS, NOPE_DIM + ROPE_DIM)
        q_nope = q[..., :NOPE_DIM]
        q_rope = apply_rope(q[..., NOPE_DIM:], cos[:, :, None, :], sin[:, :, None, :])
        kv = jnp.einsum('bsr,re->bse', rms_norm(kv_c, kv_norm[l]), w_ukv[l])
        kv = kv.reshape(B, S, A_HEADS, NOPE_DIM + V_DIM)
        k_nope, v_a = kv[..., :NOPE_DIM], kv[..., NOPE_DIM:]
        k_rope = apply_rope(k_rope, cos, sin)
        a_out = mla_attention(q_nope, q_rope, k_nope, k_rope, v_a)

        u = jax.nn.gelu(u)
        v = jax.nn.gelu(v)
        g_out = spatial_gating(u, v, v_ln_gain[l], v_ln_bias[l], w_spatial[l], b_spatial[l])

        mixed = jnp.concatenate([rms_norm(a_out, attn_out_norm[l]),
                                 rms_norm(g_out, gmlp_out_norm[l])], axis=-1)
        mix_out = jnp.einsum('bse,ed->bsd', mixed, w_out[l])
        x = x + rms_norm(mix_out, post_mix_norm[l])

        hn = rms_norm(x, pre_ffn_norm[l])
        gate = jnp.einsum('bsd,df->bsf', hn, w_gate[l])
        up = jnp.einsum('bsd,df->bsf', hn, w_up[l])
        ffn = jnp.einsum('bsf,fd->bsd', jax.nn.silu(gate) * up, w_down[l])
        x = x + rms_norm(ffn, post_ffn_norm[l])

    return x


import jax as _jax
import jax.numpy as _jnp

TWIN_FORMAT = 'train_step'
FWD_PARAMS = ['x', 'positions', 'pre_mix_norm', 'w_in', 'q_norm', 'kv_norm', 'w_uq', 'w_ukv', 'v_ln_gain', 'v_ln_bias', 'w_spatial', 'b_spatial', 'attn_out_norm', 'gmlp_out_norm', 'w_out', 'post_mix_norm', 'pre_ffn_norm', 'w_gate', 'w_up', 'w_down', 'post_ffn_norm']
TWIN_WEIGHTS = ['pre_mix_norm', 'w_in', 'q_norm', 'kv_norm', 'w_uq', 'w_ukv', 'v_ln_gain', 'v_ln_bias', 'w_spatial', 'b_spatial', 'attn_out_norm', 'gmlp_out_norm', 'w_out', 'post_mix_norm', 'pre_ffn_norm', 'w_gate', 'w_up', 'w_down', 'post_ffn_norm']
TWIN_DIFF_INPUT = 'x'
TWIN_INPUTS = ['x', 'positions', 'pre_mix_norm', 'w_in', 'q_norm', 'kv_norm', 'w_uq', 'w_ukv', 'v_ln_gain', 'v_ln_bias', 'w_spatial', 'b_spatial', 'attn_out_norm', 'gmlp_out_norm', 'w_out', 'post_mix_norm', 'pre_ffn_norm', 'w_gate', 'w_up', 'w_down', 'post_ffn_norm', 'loss_target', 'm_pre_mix_norm', 'm_w_in', 'm_q_norm', 'm_kv_norm', 'm_w_uq', 'm_w_ukv', 'm_v_ln_gain', 'm_v_ln_bias', 'm_w_spatial', 'm_b_spatial', 'm_attn_out_norm', 'm_gmlp_out_norm', 'm_w_out', 'm_post_mix_norm', 'm_pre_ffn_norm', 'm_w_gate', 'm_w_up', 'm_w_down', 'm_post_ffn_norm', 'v_pre_mix_norm', 'v_w_in', 'v_q_norm', 'v_kv_norm', 'v_w_uq', 'v_w_ukv', 'v_v_ln_gain', 'v_v_ln_bias', 'v_w_spatial', 'v_b_spatial', 'v_attn_out_norm', 'v_gmlp_out_norm', 'v_w_out', 'v_post_mix_norm', 'v_pre_ffn_norm', 'v_w_gate', 'v_w_up', 'v_w_down', 'v_post_ffn_norm']
TWIN_OUTPUTS = ['loss', 'grad_x', 'grad_pre_mix_norm', 'grad_w_in', 'grad_q_norm', 'grad_kv_norm', 'grad_w_uq', 'grad_w_ukv', 'grad_v_ln_gain', 'grad_v_ln_bias', 'grad_w_spatial', 'grad_b_spatial', 'grad_attn_out_norm', 'grad_gmlp_out_norm', 'grad_w_out', 'grad_post_mix_norm', 'grad_pre_ffn_norm', 'grad_w_gate', 'grad_w_up', 'grad_w_down', 'grad_post_ffn_norm', 'delta_pre_mix_norm', 'delta_w_in', 'delta_q_norm', 'delta_kv_norm', 'delta_w_uq', 'delta_w_ukv', 'delta_v_ln_gain', 'delta_v_ln_bias', 'delta_w_spatial', 'delta_b_spatial', 'delta_attn_out_norm', 'delta_gmlp_out_norm', 'delta_w_out', 'delta_post_mix_norm', 'delta_pre_ffn_norm', 'delta_w_gate', 'delta_w_up', 'delta_w_down', 'delta_post_ffn_norm', 'new_m_pre_mix_norm', 'new_m_w_in', 'new_m_q_norm', 'new_m_kv_norm', 'new_m_w_uq', 'new_m_w_ukv', 'new_m_v_ln_gain', 'new_m_v_ln_bias', 'new_m_w_spatial', 'new_m_b_spatial', 'new_m_attn_out_norm', 'new_m_gmlp_out_norm', 'new_m_w_out', 'new_m_post_mix_norm', 'new_m_pre_ffn_norm', 'new_m_w_gate', 'new_m_w_up', 'new_m_w_down', 'new_m_post_ffn_norm', 'new_v_pre_mix_norm', 'new_v_w_in', 'new_v_q_norm', 'new_v_kv_norm', 'new_v_w_uq', 'new_v_w_ukv', 'new_v_v_ln_gain', 'new_v_v_ln_bias', 'new_v_w_spatial', 'new_v_b_spatial', 'new_v_attn_out_norm', 'new_v_gmlp_out_norm', 'new_v_w_out', 'new_v_post_mix_norm', 'new_v_pre_ffn_norm', 'new_v_w_gate', 'new_v_w_up', 'new_v_w_down', 'new_v_post_ffn_norm']
TWIN_LEAF_KINDS = {'loss': 'loss', 'grad_x': 'grad_x', 'grad_pre_mix_norm': 'grad_w', 'grad_w_in': 'grad_w', 'grad_q_norm': 'grad_w', 'grad_kv_norm': 'grad_w', 'grad_w_uq': 'grad_w', 'grad_w_ukv': 'grad_w', 'grad_v_ln_gain': 'grad_w', 'grad_v_ln_bias': 'grad_w', 'grad_w_spatial': 'grad_w', 'grad_b_spatial': 'grad_w', 'grad_attn_out_norm': 'grad_w', 'grad_gmlp_out_norm': 'grad_w', 'grad_w_out': 'grad_w', 'grad_post_mix_norm': 'grad_w', 'grad_pre_ffn_norm': 'grad_w', 'grad_w_gate': 'grad_w', 'grad_w_up': 'grad_w', 'grad_w_down': 'grad_w', 'grad_post_ffn_norm': 'grad_w', 'delta_pre_mix_norm': 'delta_w', 'delta_w_in': 'delta_w', 'delta_q_norm': 'delta_w', 'delta_kv_norm': 'delta_w', 'delta_w_uq': 'delta_w', 'delta_w_ukv': 'delta_w', 'delta_v_ln_gain': 'delta_w', 'delta_v_ln_bias': 'delta_w', 'delta_w_spatial': 'delta_w', 'delta_b_spatial': 'delta_w', 'delta_attn_out_norm': 'delta_w', 'delta_gmlp_out_norm': 'delta_w', 'delta_w_out': 'delta_w', 'delta_post_mix_norm': 'delta_w', 'delta_pre_ffn_norm': 'delta_w', 'delta_w_gate': 'delta_w', 'delta_w_up': 'delta_w', 'delta_w_down': 'delta_w', 'delta_post_ffn_norm': 'delta_w', 'new_m_pre_mix_norm': 'new_m', 'new_m_w_in': 'new_m', 'new_m_q_norm': 'new_m', 'new_m_kv_norm': 'new_m', 'new_m_w_uq': 'new_m', 'new_m_w_ukv': 'new_m', 'new_m_v_ln_gain': 'new_m', 'new_m_v_ln_bias': 'new_m', 'new_m_w_spatial': 'new_m', 'new_m_b_spatial': 'new_m', 'new_m_attn_out_norm': 'new_m', 'new_m_gmlp_out_norm': 'new_m', 'new_m_w_out': 'new_m', 'new_m_post_mix_norm': 'new_m', 'new_m_pre_ffn_norm': 'new_m', 'new_m_w_gate': 'new_m', 'new_m_w_up': 'new_m', 'new_m_w_down': 'new_m', 'new_m_post_ffn_norm': 'new_m', 'new_v_pre_mix_norm': 'new_v', 'new_v_w_in': 'new_v', 'new_v_q_norm': 'new_v', 'new_v_kv_norm': 'new_v', 'new_v_w_uq': 'new_v', 'new_v_w_ukv': 'new_v', 'new_v_v_ln_gain': 'new_v', 'new_v_v_ln_bias': 'new_v', 'new_v_w_spatial': 'new_v', 'new_v_b_spatial': 'new_v', 'new_v_attn_out_norm': 'new_v', 'new_v_gmlp_out_norm': 'new_v', 'new_v_w_out': 'new_v', 'new_v_post_mix_norm': 'new_v', 'new_v_pre_ffn_norm': 'new_v', 'new_v_w_gate': 'new_v', 'new_v_w_up': 'new_v', 'new_v_w_down': 'new_v', 'new_v_post_ffn_norm': 'new_v'}


def _forward(args):
    return _fwd_reference(*[args[k] for k in FWD_PARAMS])


def _output_shape():
    out = _jax.eval_shape(lambda: _forward(_fwd_setup_inputs(0)))
    return out.shape, out.dtype

N_MICROBATCH = 1
ADAM_LR = 0.001
ADAM_B1 = 0.9
ADAM_B2 = 0.999
ADAM_EPS = 1e-08
ADAM_WD = 0.01
ADAM_STEP = 10
PER_EXAMPLE_BATCH_AXIS = {'x': 0, 'positions': 0, 'loss_target': 0}
SHARED_INPUTS = []
_WEIGHT_DTYPES = {'pre_mix_norm': _jnp.float32, 'w_in': _jnp.float32, 'q_norm': _jnp.float32, 'kv_norm': _jnp.float32, 'w_uq': _jnp.float32, 'w_ukv': _jnp.float32, 'v_ln_gain': _jnp.float32, 'v_ln_bias': _jnp.float32, 'w_spatial': _jnp.float32, 'b_spatial': _jnp.float32, 'attn_out_norm': _jnp.float32, 'gmlp_out_norm': _jnp.float32, 'w_out': _jnp.float32, 'post_mix_norm': _jnp.float32, 'pre_ffn_norm': _jnp.float32, 'w_gate': _jnp.float32, 'w_up': _jnp.float32, 'w_down': _jnp.float32, 'post_ffn_norm': _jnp.float32}
MOMENT_SCALE = {'pre_mix_norm': 1.963411e-01, 'w_in': 1.680473e-01, 'q_norm': 1.871443e-01, 'kv_norm': 6.435479e-01, 'w_uq': 1.015437e-01, 'w_ukv': 1.829129e-01, 'v_ln_gain': 8.123167e-02, 'v_ln_bias': 7.764455e-02, 'w_spatial': 7.727772e-02, 'b_spatial': 7.944263e-02, 'attn_out_norm': 2.419275e-01, 'gmlp_out_norm': 2.093388e-01, 'w_out': 2.178013e-01, 'post_mix_norm': 8.020602e+00, 'pre_ffn_norm': 1.719182e-01, 'w_gate': 6.252440e-02, 'w_up': 9.218202e-02, 'w_down': 1.488759e-01, 'post_ffn_norm': 8.011796e+00}


def _to_microbatches(a, axis):
    t = _jnp.moveaxis(a, axis, 0)
    t = t.reshape((N_MICROBATCH, t.shape[0] // N_MICROBATCH) + t.shape[1:])
    return _jnp.moveaxis(t, 1, axis + 1)


def setup_inputs(seed: int = 0) -> dict:
    inp = _fwd_setup_inputs(seed)
    key = _jax.random.fold_in(_jax.random.key(seed), 7919)
    shape, _ = _output_shape()
    out = dict(inp)
    out["loss_target"] = _jax.random.normal(_jax.random.fold_in(key, 0), shape, _jnp.float32)
    for i, name in enumerate(TWIN_WEIGHTS):
        w = inp[name].astype(_jnp.float32)
        if MOMENT_SCALE is None:
            s = _jnp.sqrt(_jnp.mean(_jnp.square(w)) + 1e-30)
        else:
            s = MOMENT_SCALE[name]
        km, kv = _jax.random.split(_jax.random.fold_in(key, i + 1))
        out[name] = w
        out["m_" + name] = s * _jax.random.normal(km, w.shape, _jnp.float32)
        out["v_" + name] = (s * s) * _jax.random.uniform(kv, w.shape, _jnp.float32, 0.5, 1.5)
    if N_MICROBATCH > 1:
        for name, axis in PER_EXAMPLE_BATCH_AXIS.items():
            out[name] = _to_microbatches(out[name], axis)
    return {'x': out['x'], 'positions': out['positions'], 'pre_mix_norm': out['pre_mix_norm'], 'w_in': out['w_in'], 'q_norm': out['q_norm'], 'kv_norm': out['kv_norm'], 'w_uq': out['w_uq'], 'w_ukv': out['w_ukv'], 'v_ln_gain': out['v_ln_gain'], 'v_ln_bias': out['v_ln_bias'], 'w_spatial': out['w_spatial'], 'b_spatial': out['b_spatial'], 'attn_out_norm': out['attn_out_norm'], 'gmlp_out_norm': out['gmlp_out_norm'], 'w_out': out['w_out'], 'post_mix_norm': out['post_mix_norm'], 'pre_ffn_norm': out['pre_ffn_norm'], 'w_gate': out['w_gate'], 'w_up': out['w_up'], 'w_down': out['w_down'], 'post_ffn_norm': out['post_ffn_norm'], 'loss_target': out['loss_target'], 'm_pre_mix_norm': out['m_pre_mix_norm'], 'm_w_in': out['m_w_in'], 'm_q_norm': out['m_q_norm'], 'm_kv_norm': out['m_kv_norm'], 'm_w_uq': out['m_w_uq'], 'm_w_ukv': out['m_w_ukv'], 'm_v_ln_gain': out['m_v_ln_gain'], 'm_v_ln_bias': out['m_v_ln_bias'], 'm_w_spatial': out['m_w_spatial'], 'm_b_spatial': out['m_b_spatial'], 'm_attn_out_norm': out['m_attn_out_norm'], 'm_gmlp_out_norm': out['m_gmlp_out_norm'], 'm_w_out': out['m_w_out'], 'm_post_mix_norm': out['m_post_mix_norm'], 'm_pre_ffn_norm': out['m_pre_ffn_norm'], 'm_w_gate': out['m_w_gate'], 'm_w_up': out['m_w_up'], 'm_w_down': out['m_w_down'], 'm_post_ffn_norm': out['m_post_ffn_norm'], 'v_pre_mix_norm': out['v_pre_mix_norm'], 'v_w_in': out['v_w_in'], 'v_q_norm': out['v_q_norm'], 'v_kv_norm': out['v_kv_norm'], 'v_w_uq': out['v_w_uq'], 'v_w_ukv': out['v_w_ukv'], 'v_v_ln_gain': out['v_v_ln_gain'], 'v_v_ln_bias': out['v_v_ln_bias'], 'v_w_spatial': out['v_w_spatial'], 'v_b_spatial': out['v_b_spatial'], 'v_attn_out_norm': out['v_attn_out_norm'], 'v_gmlp_out_norm': out['v_gmlp_out_norm'], 'v_w_out': out['v_w_out'], 'v_post_mix_norm': out['v_post_mix_norm'], 'v_pre_ffn_norm': out['v_pre_ffn_norm'], 'v_w_gate': out['v_w_gate'], 'v_w_up': out['v_w_up'], 'v_w_down': out['v_w_down'], 'v_post_ffn_norm': out['v_post_ffn_norm']}


def _loss(weights, diff, rest, loss_target):
    with _jax.named_scope("forward"):
        args = {**rest, TWIN_DIFF_INPUT: diff, **{k: w.astype(_WEIGHT_DTYPES[k]) for k, w in weights.items()}}
        y = _forward(args)
    with _jax.named_scope("loss_head"):
        err = _jnp.square(y.astype(_jnp.float32) - loss_target)
        return 0.5 * _jnp.sum(_jnp.mean(err, axis=-1)) if err.ndim else 0.5 * err


def _adamw(w, g, m, v):
    m = ADAM_B1 * m + (1.0 - ADAM_B1) * g
    v = ADAM_B2 * v + (1.0 - ADAM_B2) * _jnp.square(g)
    m_hat = m / (1.0 - ADAM_B1 ** ADAM_STEP)
    v_hat = v / (1.0 - ADAM_B2 ** ADAM_STEP)
    delta = -ADAM_LR * (m_hat / (_jnp.sqrt(v_hat) + ADAM_EPS) + ADAM_WD * w)
    return delta, m, v


def reference(x, positions, pre_mix_norm, w_in, q_norm, kv_norm, w_uq, w_ukv, v_ln_gain, v_ln_bias, w_spatial, b_spatial, attn_out_norm, gmlp_out_norm, w_out, post_mix_norm, pre_ffn_norm, w_gate, w_up, w_down, post_ffn_norm, loss_target, m_pre_mix_norm, m_w_in, m_q_norm, m_kv_norm, m_w_uq, m_w_ukv, m_v_ln_gain, m_v_ln_bias, m_w_spatial, m_b_spatial, m_attn_out_norm, m_gmlp_out_norm, m_w_out, m_post_mix_norm, m_pre_ffn_norm, m_w_gate, m_w_up, m_w_down, m_post_ffn_norm, v_pre_mix_norm, v_w_in, v_q_norm, v_kv_norm, v_w_uq, v_w_ukv, v_v_ln_gain, v_v_ln_bias, v_w_spatial, v_b_spatial, v_attn_out_norm, v_gmlp_out_norm, v_w_out, v_post_mix_norm, v_pre_ffn_norm, v_w_gate, v_w_up, v_w_down, v_post_ffn_norm):
    given = dict(x=x, positions=positions, pre_mix_norm=pre_mix_norm, w_in=w_in, q_norm=q_norm, kv_norm=kv_norm, w_uq=w_uq, w_ukv=w_ukv, v_ln_gain=v_ln_gain, v_ln_bias=v_ln_bias, w_spatial=w_spatial, b_spatial=b_spatial, attn_out_norm=attn_out_norm, gmlp_out_norm=gmlp_out_norm, w_out=w_out, post_mix_norm=post_mix_norm, pre_ffn_norm=pre_ffn_norm, w_gate=w_gate, w_up=w_up, w_down=w_down, post_ffn_norm=post_ffn_norm, loss_target=loss_target, m_pre_mix_norm=m_pre_mix_norm, m_w_in=m_w_in, m_q_norm=m_q_norm, m_kv_norm=m_kv_norm, m_w_uq=m_w_uq, m_w_ukv=m_w_ukv, m_v_ln_gain=m_v_ln_gain, m_v_ln_bias=m_v_ln_bias, m_w_spatial=m_w_spatial, m_b_spatial=m_b_spatial, m_attn_out_norm=m_attn_out_norm, m_gmlp_out_norm=m_gmlp_out_norm, m_w_out=m_w_out, m_post_mix_norm=m_post_mix_norm, m_pre_ffn_norm=m_pre_ffn_norm, m_w_gate=m_w_gate, m_w_up=m_w_up, m_w_down=m_w_down, m_post_ffn_norm=m_post_ffn_norm, v_pre_mix_norm=v_pre_mix_norm, v_w_in=v_w_in, v_q_norm=v_q_norm, v_kv_norm=v_kv_norm, v_w_uq=v_w_uq, v_w_ukv=v_w_ukv, v_v_ln_gain=v_v_ln_gain, v_v_ln_bias=v_v_ln_bias, v_w_spatial=v_w_spatial, v_b_spatial=v_b_spatial, v_attn_out_norm=v_attn_out_norm, v_gmlp_out_norm=v_gmlp_out_norm, v_w_out=v_w_out, v_post_mix_norm=v_post_mix_norm, v_pre_ffn_norm=v_pre_ffn_norm, v_w_gate=v_w_gate, v_w_up=v_w_up, v_w_down=v_w_down, v_post_ffn_norm=v_post_ffn_norm)
    weights = {n: given[n] for n in TWIN_WEIGHTS}
    shared = {n: given[n] for n in SHARED_INPUTS}
    per_example = {n: given[n] for n in ['x', 'positions']}
    grad_fn = _jax.value_and_grad(_loss, argnums=(0, 1))

    def one_microbatch(ex, loss_target):
        ex = dict(ex)
        diff = ex.pop(TWIN_DIFF_INPUT)
        return grad_fn(weights, diff, {**shared, **ex}, loss_target)

    if N_MICROBATCH == 1:
        loss, (grad_w, grad_x) = one_microbatch(per_example, given["loss_target"])
    else:
        def body(carry, xs):
            loss_sum, grad_sum = carry
            l_k, (gw_k, gx_k) = one_microbatch(xs[0], xs[1])
            with _jax.named_scope("update"):
                return (loss_sum + l_k, _jax.tree.map(_jnp.add, grad_sum, gw_k)), gx_k

        init = (_jnp.zeros((), _jnp.float32), _jax.tree.map(_jnp.zeros_like, weights))
        (loss, grad_w), grad_x = _jax.lax.scan(body, init, (per_example, given["loss_target"]))
    with _jax.named_scope("update"):
        delta_w, new_m, new_v = {}, {}, {}
        for n in TWIN_WEIGHTS:
            delta_w[n], new_m[n], new_v[n] = _adamw(weights[n], grad_w[n], given["m_" + n], given["v_" + n])
    return (loss, grad_x, *[grad_w[n] for n in TWIN_WEIGHTS], *[delta_w[n] for n in TWIN_WEIGHTS],
            *[new_m[n] for n in TWIN_WEIGHTS], *[new_v[n] for n in TWIN_WEIGHTS])
```

```python
import math

import jax
import jax.numpy as jnp
from jax import lax
from jax.experimental import pallas as pl
from jax.experimental.pallas import tpu as pltpu

F32 = jnp.float32
BF16 = jnp.bfloat16
N_DEV = 8
MESH_AXES = ("x", "y", "c")
MESH = pl.DeviceIdType.MESH

NOPE_DIM = 128
ROPE_DIM = 64
V_DIM = 128
ROPE_THETA = 10000.0
CHUNK = 128
EPS = 1e-6
QK_HEAD = NOPE_DIM + ROPE_DIM
Q_EXT = 3 * 128
K_FULL = 2 * 128

ADAM_LR = 0.001
ADAM_B1 = 0.9
ADAM_B2 = 0.999
ADAM_EPS = 1e-08
ADAM_WD = 0.01
ADAM_STEP = 10

LANE = 128
SUBLANE = 8
VMEM_LIMIT = 56 * 1024 * 1024
ROW_BLOCK_ELEMS = 512 * 1024
ADAM_BLOCK_ELEMS = 256 * 1024

DN = {
    "nn": (((1,), (0,)), ((), ())),
    "nt": (((1,), (1,)), ((), ())),
    "tn": (((0,), (0,)), ((), ())),
}


def _pick(dim, pref, mult):
    t = min(pref, dim)
    t -= t % mult
    while t >= mult:
        if dim % t == 0:
            return t
        t -= mult
    return dim


def _round_up(n, m):
    return (n + m - 1) // m * m


def _params(sem):
    return pltpu.CompilerParams(dimension_semantics=sem, vmem_limit_bytes=VMEM_LIMIT)


def _mm(name, mode, pairs, grid, acc_shape, out_shape, out_dtype, out_block, out_map):
    n = len(pairs)
    nk = grid[-1]
    nd = len(grid)
    dn = DN[mode]

    def body(*refs):
        o_ref, acc = refs[2 * n], refs[2 * n + 1]
        k = pl.program_id(nd - 1)

        @pl.when(k == 0)
        def _():
            acc[...] = jnp.zeros_like(acc)

        s = None
        for i in range(n):
            d = lax.dot_general(refs[2 * i][...], refs[2 * i + 1][...], dn, preferred_element_type=F32)
            s = d if s is None else s + d
        acc[...] += s

        @pl.when(k == nk - 1)
        def _():
            o_ref[...] = acc[...].astype(o_ref.dtype)

    in_specs, args = [], []
    for a, a_blk, a_map, b, b_blk, b_map in pairs:
        in_specs += [pl.BlockSpec(a_blk, a_map), pl.BlockSpec(b_blk, b_map)]
        args += [a, b]
    return pl.pallas_call(
        body,
        name=name,
        grid=grid,
        in_specs=in_specs,
        out_specs=pl.BlockSpec(out_block, out_map),
        out_shape=jax.ShapeDtypeStruct(out_shape, out_dtype),
        scratch_shapes=[pltpu.VMEM(acc_shape, F32)],
        compiler_params=_params(("parallel",) * (nd - 1) + ("arbitrary",)),
    )(*args)


def _mm2(name, mode, a, b, out_dtype, tm_pref=1024, tn_pref=1024, tk_pref=512):
    if mode == "nn":
        (M, K), N = a.shape, b.shape[1]
    elif mode == "nt":
        (M, K), N = a.shape, b.shape[0]
    else:
        (K, M), N = a.shape, b.shape[1]
    tm, tn, tk = _pick(M, tm_pref, LANE), _pick(N, tn_pref, LANE), _pick(K, tk_pref, LANE)
    if mode == "tn":
        a_blk, a_map = (tk, tm), lambda i, j, k: (k, i)
    else:
        a_blk, a_map = (tm, tk), lambda i, j, k: (i, k)
    if mode == "nt":
        b_blk, b_map = (tn, tk), lambda i, j, k: (j, k)
    else:
        b_blk, b_map = (tk, tn), lambda i, j, k: (k, j)
    return _mm(name, mode, [(a, a_blk, a_map, b, b_blk, b_map)], (M // tm, N // tn, K // tk), (tm, tn),
               (M, N), out_dtype, (tm, tn), lambda i, j, k: (i, j))


def _ew(name, fn, grid, ins, outs, accs=()):
    ni, no, na = len(ins), len(outs), len(accs)
    nd = len(grid)

    def body(*refs):
        vals = fn(*[r[...] for r in refs[:ni]])
        if not isinstance(vals, (tuple, list)):
            vals = (vals,)
        for r, v in zip(refs[ni:ni + no], vals[:no]):
            r[...] = v.astype(r.dtype)
        if na:
            first = pl.program_id(nd - 1) == 0
            for r, v in zip(refs[ni + no:], vals[no:]):
                @pl.when(first)
                def _(r=r, v=v):
                    r[...] = v.astype(r.dtype)

                @pl.when(jnp.logical_not(first))
                def _(r=r, v=v):
                    r[...] += v.astype(r.dtype)

    all_outs = list(outs) + list(accs)
    sem = ("parallel",) * (nd - 1) + (("arbitrary",) if na else ("parallel",))
    res = pl.pallas_call(
        body,
        name=name,
        grid=grid,
        in_specs=[pl.BlockSpec(blk, imap) for (_, blk, imap) in ins],
        out_specs=[pl.BlockSpec(blk, imap) for (_, _, blk, imap) in all_outs],
        out_shape=[jax.ShapeDtypeStruct(s, d) for (s, d, _, _) in all_outs],
        compiler_params=_params(sem),
    )(*[a for (a, _, _) in ins])
    return res


def _rows(a, tm, width=None, cblk=0):
    width = a.shape[1] if width is None else width
    return (a, (tm, width), lambda i, c=cblk: (i, c))


def _vec(p):
    return (p, p.shape, lambda i: (0,) * p.ndim)


def _rows_out(shape, dtype, tm):
    return (shape, dtype, (tm, shape[1]), lambda i: (i, 0))


def _vec_out(shape):
    return (shape, F32, shape, lambda i: (0,) * len(shape))


def _row_tile(T, D):
    return _pick(T, max(SUBLANE, ROW_BLOCK_ELEMS // D), SUBLANE)


def _rms(x, g):
    r = lax.rsqrt(jnp.mean(x * x, axis=-1, keepdims=True) + EPS)
    return x * r * g


def _rms_bwd(x, g, dy):
    r = lax.rsqrt(jnp.mean(x * x, axis=-1, keepdims=True) + EPS)
    xh = x * r
    dg = jnp.sum(dy * xh, axis=0, keepdims=True)
    dxh = dy * g
    dx = r * (dxh - xh * jnp.mean(dxh * xh, axis=-1, keepdims=True))
    return dx, dg


GELU_C = math.sqrt(2.0 / math.pi)
GELU_A = 0.044715


def _gelu(x):
    return 0.5 * x * (1.0 + jnp.tanh(GELU_C * (x + GELU_A * x * x * x)))


def _gelu_grad(x):
    t = jnp.tanh(GELU_C * (x + GELU_A * x * x * x))
    return 0.5 * (1.0 + t) + 0.5 * x * (1.0 - t * t) * GELU_C * (1.0 + 3.0 * GELU_A * x * x)


def _rot(w):
    h = w.shape[-1] // 2
    return jnp.concatenate([-w[..., h:], w[..., :h]], axis=-1)


def _rot_t(dw):
    h = dw.shape[-1] // 2
    return jnp.concatenate([dw[..., h:], -dw[..., :h]], axis=-1)


def _coords():
    return lax.axis_index("x"), lax.axis_index("y"), lax.axis_index("c")


def _all_gather(name, shard):
    def body(x_ref, o_ref, send_sems, recv_sems, local_sem):
        xi, yi, ci = _coords()
        me, sibling = (xi, yi, ci), (xi, yi, 1 - ci)
        chips = [(1 - xi, yi), (xi, 1 - yi), (1 - xi, 1 - yi)]

        def slot(px, py, pc):
            return o_ref.at[4 * px + 2 * py + pc]

        def copy(k, block, to, src=None):
            return pltpu.make_async_remote_copy(
                src_ref=slot(*block) if src is None else src, dst_ref=slot(*block),
                send_sem=send_sems.at[k], recv_sem=recv_sems.at[k], device_id=to, device_id_type=MESH)

        mine = pltpu.make_async_copy(x_ref, slot(*me), local_sem)
        mine.start()
        first = [copy(0, me, sibling, src=x_ref)]
        first += [copy(1 + j, me, (*chip, ci), src=x_ref) for j, chip in enumerate(chips)]
        for cp in first:
            cp.start()
        passed = [copy(4 + j, (*chip, ci), sibling) for j, chip in enumerate(chips)]
        for j, chip in enumerate(chips):
            copy(1 + j, (*chip, ci), me).wait_recv()
            passed[j].start()
        copy(0, sibling, me).wait_recv()
        for j, chip in enumerate(chips):
            copy(4 + j, (*chip, 1 - ci), me).wait_recv()
        for cp in first + passed:
            cp.wait_send()
        mine.wait()

    return pl.pallas_call(
        body,
        name=name,
        out_shape=jax.ShapeDtypeStruct((N_DEV,) + shard.shape, shard.dtype),
        in_specs=[pl.BlockSpec(memory_space=pltpu.HBM)],
        out_specs=pl.BlockSpec(memory_space=pltpu.HBM),
        scratch_shapes=[pltpu.SemaphoreType.DMA((N_DEV - 1,)), pltpu.SemaphoreType.DMA((N_DEV - 1,)),
                        pltpu.SemaphoreType.DMA],
    )(shard)


def _all_to_all(name, parts):
    def body(x_ref, o_ref, send_sems, recv_sems, local_sem):
        xi, yi, ci = _coords()
        me = 4 * xi + 2 * yi + ci
        mine = pltpu.make_async_copy(x_ref.at[me], o_ref.at[me], local_sem)
        mine.start()
        copies = []
        for k in range(1, N_DEV):
            px = 1 - xi if k & 4 else xi
            py = 1 - yi if k & 2 else yi
            pc = 1 - ci if k & 1 else ci
            cp = pltpu.make_async_remote_copy(
                src_ref=x_ref.at[4 * px + 2 * py + pc], dst_ref=o_ref.at[me],
                send_sem=send_sems.at[k - 1], recv_sem=recv_sems.at[k - 1],
                device_id=(px, py, pc), device_id_type=MESH)
            cp.start()
            copies.append(cp)
        for cp in copies:
            cp.wait()
        mine.wait()

    return pl.pallas_call(
        body,
        name=name,
        out_shape=jax.ShapeDtypeStruct(parts.shape, parts.dtype),
        in_specs=[pl.BlockSpec(memory_space=pltpu.HBM)],
        out_specs=pl.BlockSpec(memory_space=pltpu.HBM),
        scratch_shapes=[pltpu.SemaphoreType.DMA((N_DEV - 1,)), pltpu.SemaphoreType.DMA((N_DEV - 1,)),
                        pltpu.SemaphoreType.DMA],
    )(parts)


def _attn_fwd(q_full, k_full, v_h):
    H, T, _ = q_full.shape
    tq = _pick(T, 256, LANE)

    def body(q_ref, k_ref, v_ref, o_ref):
        s = lax.dot_general(q_ref[...], k_ref[...], DN["nt"], preferred_element_type=F32)
        m = jnp.max(s, axis=-1, keepdims=True)
        p = jnp.exp(s - m)
        l = jnp.sum(p, axis=-1, keepdims=True)
        o = jnp.dot(p.astype(BF16), v_ref[...], preferred_element_type=F32)
        o_ref[...] = o / l

    return pl.pallas_call(
        body,
        name="attn_fwd",
        grid=(H, T // tq),
        in_specs=[pl.BlockSpec((None, tq, K_FULL), lambda h, i: (h, i, 0)),
                  pl.BlockSpec((None, T, K_FULL), lambda h, i: (h, 0, 0)),
                  pl.BlockSpec((None, T, V_DIM), lambda h, i: (h, 0, 0))],
        out_specs=pl.BlockSpec((tq, V_DIM), lambda h, i: (i, h)),
        out_shape=jax.ShapeDtypeStruct((T, H * V_DIM), F32),
        compiler_params=_params(("parallel", "parallel")),
    )(q_full, k_full, v_h)


def _attn_bwd(q_full, k_full, v_h, d_out):
    H, T, _ = q_full.shape
    tq = _pick(T, 256, LANE)

    def body(q_ref, k_ref, v_ref, do_ref, dq_ref, dk_ref, dv_ref):
        i = pl.program_id(1)
        q, k, v, do = q_ref[...], k_ref[...], v_ref[...], do_ref[...]
        st = lax.dot_general(k, q, DN["nt"], preferred_element_type=F32)
        m = jnp.max(st, axis=0, keepdims=True)
        e = jnp.exp(st - m)
        pt = e * (1.0 / jnp.sum(e, axis=0, keepdims=True))
        dpt = lax.dot_general(v, do, DN["nt"], preferred_element_type=F32)
        delta = jnp.sum(pt * dpt, axis=0, keepdims=True)
        dst = (pt * (dpt - delta)).astype(BF16)
        dv = jnp.dot(pt.astype(BF16), do, preferred_element_type=F32)
        dk = jnp.dot(dst, q, preferred_element_type=F32)
        dq_ref[...] = lax.dot_general(dst, k, DN["tn"], preferred_element_type=F32)

        @pl.when(i == 0)
        def _():
            dk_ref[...] = dk
            dv_ref[...] = dv

        @pl.when(i != 0)
        def _():
            dk_ref[...] += dk
            dv_ref[...] += dv

    return pl.pallas_call(
        body,
        name="attn_bwd",
        grid=(H, T // tq),
        in_specs=[pl.BlockSpec((None, tq, K_FULL), lambda h, i: (h, i, 0)),
                  pl.BlockSpec((None, T, K_FULL), lambda h, i: (h, 0, 0)),
                  pl.BlockSpec((None, T, V_DIM), lambda h, i: (h, 0, 0)),
                  pl.BlockSpec((tq, V_DIM), lambda h, i: (i, h))],
        out_specs=[pl.BlockSpec((None, tq, K_FULL), lambda h, i: (h, i, 0)),
                   pl.BlockSpec((None, T, K_FULL), lambda h, i: (h, 0, 0)),
                   pl.BlockSpec((None, T, V_DIM), lambda h, i: (h, 0, 0))],
        out_shape=[jax.ShapeDtypeStruct((H, T, K_FULL), F32), jax.ShapeDtypeStruct((H, T, K_FULL), F32),
                   jax.ShapeDtypeStruct((H, T, V_DIM), F32)],
        compiler_params=_params(("parallel", "arbitrary")),
    )(q_full, k_full, v_h, d_out)


def _sgu_forward_math(pb, lng, lnb, ws, bsb, gw, gh):
    u_raw, v_raw = pb[:, :gw], pb[:, gw:]
    u, v = _gelu(u_raw), _gelu(v_raw)
    vc = v - jnp.mean(v, axis=-1, keepdims=True)
    rstd = lax.rsqrt(jnp.mean(vc * vc, axis=-1, keepdims=True) + EPS)
    vhat = vc * rstd
    vln = vhat * lng + lnb
    hd = gw // gh
    s = jnp.concatenate(
        [jnp.dot(ws[g], vln[:, g * hd:(g + 1) * hd].astype(BF16), preferred_element_type=F32) + bsb[g]
         for g in range(gh)], axis=1)
    return u_raw, v_raw, u, rstd, vhat, vln, s


def _sgu_fwd(proj_b, lng, lnb, ws, bsb, gg):
    T, gw2 = proj_b.shape
    gw, gh = gw2 // 2, ws.shape[0]

    def fn(pb, lng, lnb, ws, bsb, gg):
        _, _, u, _, _, _, s = _sgu_forward_math(pb, lng, lnb, ws, bsb, gw, gh)
        return _rms(u * s, gg)

    return _ew("sgu_fwd", fn, (T // CHUNK,),
               [_rows(proj_b, CHUNK), _vec(lng), _vec(lnb), _vec(ws), _vec(bsb), _vec(gg)],
               [_rows_out((T, gw), BF16, CHUNK)])[0]


def _sgu_bwd(proj_b, d_mixed, lng, lnb, ws, ws_t, bsb, gg):
    T, gw2 = proj_b.shape
    gw, gh = gw2 // 2, ws.shape[0]
    hd = gw // gh

    def fn(pb, dmg, lng, lnb, ws, ws_t, bsb, gg):
        u_raw, v_raw, u, rstd, vhat, vln, s = _sgu_forward_math(pb, lng, lnb, ws, bsb, gw, gh)
        dgo, dgg = _rms_bwd(u * s, gg, dmg)
        du, ds = dgo * s, dgo * u
        ones = jnp.ones((SUBLANE, hd), F32)
        dws, dbs, dvln = [], [], []
        for g in range(gh):
            ds_g = ds[:, g * hd:(g + 1) * hd]
            ds_gb = ds_g.astype(BF16)
            dws.append(lax.dot_general(ds_gb, vln[:, g * hd:(g + 1) * hd].astype(BF16), DN["nt"],
                                       preferred_element_type=F32)[None])
            dbs.append(lax.dot_general(ones, ds_g, DN["nt"], preferred_element_type=F32,
                                       precision=lax.Precision.HIGHEST)[None])
            dvln.append(jnp.dot(ws_t[g], ds_gb, preferred_element_type=F32))
        dvln = jnp.concatenate(dvln, axis=1)
        dlng = jnp.sum(dvln * vhat, axis=0, keepdims=True)
        dlnb = jnp.sum(dvln, axis=0, keepdims=True)
        dvh = dvln * lng
        dv = rstd * (dvh - jnp.mean(dvh, axis=-1, keepdims=True) - vhat * jnp.mean(dvh * vhat, axis=-1, keepdims=True))
        d_pb = jnp.concatenate([du * _gelu_grad(u_raw), dv * _gelu_grad(v_raw)], axis=1)
        return d_pb, dgg, jnp.concatenate(dws, axis=0), jnp.concatenate(dbs, axis=0), dlng, dlnb

    return _ew("sgu_bwd", fn, (T // CHUNK,),
               [_rows(proj_b, CHUNK), _rows(d_mixed, CHUNK, gw, 1), _vec(lng), _vec(lnb), _vec(ws), _vec(ws_t),
                _vec(bsb), _vec(gg)],
               [_rows_out((T, gw2), BF16, CHUNK)],
               [_vec_out((1, gw)), _vec_out((gh, CHUNK, CHUNK)), _vec_out((gh, SUBLANE, CHUNK)), _vec_out((1, gw)),
                _vec_out((1, gw))])


def _adamw(name, parts, w, m, v):
    R, C = w.shape
    tm = _pick(R, max(SUBLANE, ADAM_BLOCK_ELEMS // C), SUBLANE)

    def fn(p, w, m, v):
        g = p[0].astype(F32)
        for i in range(1, N_DEV):
            g = g + p[i].astype(F32)
        m2 = ADAM_B1 * m + (1.0 - ADAM_B1) * g
        v2 = ADAM_B2 * v + (1.0 - ADAM_B2) * (g * g)
        m_hat = m2 / (1.0 - ADAM_B1 ** ADAM_STEP)
        v_hat = v2 / (1.0 - ADAM_B2 ** ADAM_STEP)
        delta = -ADAM_LR * (m_hat / (jnp.sqrt(v_hat) + ADAM_EPS) + ADAM_WD * w)
        return g, delta, m2, v2

    return _ew(name, fn, (R // tm,),
               [(parts, (N_DEV, tm, C), lambda i: (0, i, 0)), _rows(w, tm), _rows(m, tm), _rows(v, tm)],
               [_rows_out((R, C), F32, tm)] * 4)


def kernel(x, positions, pre_mix_norm, w_in, q_norm, kv_norm, w_uq, w_ukv, v_ln_gain, v_ln_bias, w_spatial, b_spatial, attn_out_norm, gmlp_out_norm, w_out, post_mix_norm, pre_ffn_norm, w_gate, w_up, w_down, post_ffn_norm, loss_target, m_pre_mix_norm, m_w_in, m_q_norm, m_kv_norm, m_w_uq, m_w_ukv, m_v_ln_gain, m_v_ln_bias, m_w_spatial, m_b_spatial, m_attn_out_norm, m_gmlp_out_norm, m_w_out, m_post_mix_norm, m_pre_ffn_norm, m_w_gate, m_w_up, m_w_down, m_post_ffn_norm, v_pre_mix_norm, v_w_in, v_q_norm, v_kv_norm, v_w_uq, v_w_ukv, v_v_ln_gain, v_v_ln_bias, v_w_spatial, v_b_spatial, v_attn_out_norm, v_gmlp_out_norm, v_w_out, v_post_mix_norm, v_pre_ffn_norm, v_w_gate, v_w_up, v_w_down, v_post_ffn_norm):
    given = dict(locals())
    T, D = x.shape[1], x.shape[2]
    QL, KVL = w_uq.shape[1], w_ukv.shape[1]
    GW, GH = v_ln_gain.shape[1], w_spatial.shape[1]
    IWS = w_in.shape[2]
    IW = IWS * N_DEV
    H = w_ukv.shape[2] * N_DEV // (NOPE_DIM + V_DIM)
    AW = H * V_DIM
    FS = w_gate.shape[2]
    FP = _round_up(FS, LANE)
    C0 = QL + KVL
    assert IW == C0 + ROPE_DIM + 2 * GW and AW + GW == D and AW == GW
    assert QL % KVL == 0 and C0 % K_FULL == 0 and w_spatial.shape[2] == CHUNK and T % CHUNK == 0
    assert w_uq.shape[2] * N_DEV == H * QK_HEAD and D % N_DEV == 0

    x2, tgt = x[0], loss_target[0]
    tm_d = _row_tile(T, D)

    inv_freq = 1.0 / (ROPE_THETA ** (jnp.arange(0, ROPE_DIM, 2, dtype=F32) / ROPE_DIM))
    ang = positions[0].astype(F32)[:, None] * inv_freq
    zpad = jnp.zeros((T, LANE - ROPE_DIM), F32)
    cosp = jnp.concatenate([jnp.cos(ang), jnp.cos(ang), zpad], axis=1)
    sinp = jnp.concatenate([jnp.sin(ang), jnp.sin(ang), zpad], axis=1)

    w_in_all = _all_gather("ag_w_in", w_in[0].astype(BF16)).transpose(1, 0, 2).reshape(D, IW)
    kr_w = w_in_all[:, C0:C0 + ROPE_DIM]
    z64 = jnp.zeros((D, LANE - ROPE_DIM), BF16)
    w_a = jnp.concatenate([w_in_all[:, :C0], kr_w, z64, _rot(kr_w), z64], axis=1)
    w_b = w_in_all[:, C0 + ROPE_DIM:]
    wq = _all_gather("ag_w_uq", w_uq[0].astype(BF16)).transpose(1, 0, 2).reshape(QL, H, QK_HEAD)
    zq = jnp.zeros((QL, H, LANE - ROPE_DIM), BF16)
    wq_ext = jnp.concatenate([wq[..., :NOPE_DIM], wq[..., NOPE_DIM:], zq, _rot(wq[..., NOPE_DIM:]), zq],
                             axis=-1).reshape(QL, H * Q_EXT)
    wkv = _all_gather("ag_w_ukv", w_ukv[0].astype(BF16)).transpose(1, 0, 2).reshape(KVL, H * (NOPE_DIM + V_DIM))
    wo = _all_gather("ag_w_out", w_out[0].astype(BF16)).reshape(D, D)
    wg = _all_gather("ag_w_gate", jnp.pad(w_gate[0].astype(BF16), ((0, 0), (0, FP - FS))))
    wu = _all_gather("ag_w_up", jnp.pad(w_up[0].astype(BF16), ((0, 0), (0, FP - FS))))
    wd = _all_gather("ag_w_down", jnp.pad(w_down[0].astype(BF16), ((0, FP - FS), (0, 0))))

    ws = w_spatial[0].astype(BF16)
    ws_t = jnp.swapaxes(ws, 1, 2)
    bsb = jnp.broadcast_to(b_spatial[0][:, :, None], (GH, CHUNK, GW // GH))

    xn = _ew("rms_x", lambda x, g: _rms(x, g), (T // tm_d,), [_rows(x2, tm_d), _vec(pre_mix_norm)],
             [_rows_out((T, D), BF16, tm_d)])[0]
    proj_a = _mm2("mm_proj_a", "nn", xn, w_a, F32, tn_pref=w_a.shape[1])
    proj_b = _mm2("mm_proj_b", "nn", xn, w_b, F32)

    tm_q = _row_tile(T, QL)
    qn = _ew("rms_q", lambda x, g: _rms(x, g), (T // tm_q,), [_rows(proj_a, tm_q, QL, 0), _vec(q_norm)],
             [_rows_out((T, QL), BF16, tm_q)])[0]
    kvn = _ew("rms_kv", lambda x, g: _rms(x, g), (T // tm_q,), [_rows(proj_a, tm_q, KVL, QL // KVL), _vec(kv_norm)],
              [_rows_out((T, KVL), BF16, tm_q)])[0]
    q_ext = _mm2("mm_q", "nn", qn, wq_ext, F32, tk_pref=1024)
    kv = _mm2("mm_kv", "nn", kvn, wkv, F32, tk_pref=1024)

    scale = 1.0 / math.sqrt(QK_HEAD)
    tm_h = _pick(T, 512, SUBLANE)

    def rope_q(q, c, s):
        return jnp.concatenate([q[:, :LANE], q[:, LANE:2 * LANE] * c + q[:, 2 * LANE:] * s], axis=1) * scale

    q_full = _ew("rope_q", rope_q, (T // tm_h, H),
                 [(q_ext, (tm_h, Q_EXT), lambda i, h: (i, h)), (cosp, (tm_h, LANE), lambda i, h: (i, 0)),
                  (sinp, (tm_h, LANE), lambda i, h: (i, 0))],
                 [((H, T, K_FULL), BF16, (None, tm_h, K_FULL), lambda i, h: (h, i, 0))])[0]

    def make_kv(kv, kr, c, s):
        return jnp.concatenate([kv[:, :LANE], kr[:, :LANE] * c + kr[:, LANE:] * s], axis=1), kv[:, LANE:]

    k_full, v_h = _ew("make_kv", make_kv, (T // tm_h, H),
                      [(kv, (tm_h, K_FULL), lambda i, h: (i, h)),
                       (proj_a, (tm_h, K_FULL), lambda i, h: (i, C0 // K_FULL)),
                       (cosp, (tm_h, LANE), lambda i, h: (i, 0)), (sinp, (tm_h, LANE), lambda i, h: (i, 0))],
                      [((H, T, K_FULL), BF16, (None, tm_h, K_FULL), lambda i, h: (h, i, 0)),
                       ((H, T, V_DIM), BF16, (None, tm_h, V_DIM), lambda i, h: (h, i, 0))])

    a_out = _attn_fwd(q_full, k_full, v_h)
    tm_a = _row_tile(T, AW)
    mixed_a = _ew("rms_attn", lambda x, g: _rms(x, g), (T // tm_a,), [_rows(a_out, tm_a), _vec(attn_out_norm)],
                  [_rows_out((T, AW), BF16, tm_a)])[0]
    mixed_g = _sgu_fwd(proj_b, v_ln_gain, v_ln_bias, ws, bsb, gmlp_out_norm)

    tm_, tn_, tk_ = _pick(T, 1024, LANE), _pick(D, 1024, LANE), _pick(AW, 512, LANE)
    nka = AW // tk_
    mix_out = _mm("mm_out", "nn",
                  [(mixed_a, (tm_, tk_), lambda i, j, k: (i, k), wo, (tk_, tn_), lambda i, j, k: (k, j)),
                   (mixed_g, (tm_, tk_), lambda i, j, k: (i, k), wo, (tk_, tn_), lambda i, j, k: (k + nka, j))],
                  (T // tm_, D // tn_, nka), (tm_, tn_), (T, D), F32, (tm_, tn_), lambda i, j, k: (i, j))

    def post_mix(x, mo, g3, g4):
        h = x + _rms(mo, g3)
        return h, _rms(h, g4)

    h_res, hn = _ew("post_mix", post_mix, (T // tm_d,),
                    [_rows(x2, tm_d), _rows(mix_out, tm_d), _vec(post_mix_norm), _vec(pre_ffn_norm)],
                    [_rows_out((T, D), F32, tm_d), _rows_out((T, D), BF16, tm_d)])

    tk_d = _pick(D, 512, LANE)
    tn_d = _pick(D, 1024, LANE)

    def ffn_in(name, w):
        return _mm(name, "nn", [(hn, (tm_, tk_d), lambda j, i, k: (i, k), w, (None, tk_d, FP), lambda j, i, k: (j, k, 0))],
                   (N_DEV, T // tm_, D // tk_d), (tm_, FP), (N_DEV, T, FP), BF16, (None, tm_, FP), lambda j, i, k: (j, i, 0))

    gate, up = ffn_in("mm_gate", wg), ffn_in("mm_up", wu)
    tm_f = _pick(T, 256, SUBLANE)
    blk3 = lambda a: (a, (None, tm_f, FP), lambda j, i: (j, i, 0))
    out3 = ((N_DEV, T, FP), BF16, (None, tm_f, FP), lambda j, i: (j, i, 0))

    def swiglu(g, u):
        g, u = g.astype(F32), u.astype(F32)
        return g * jax.nn.sigmoid(g) * u

    act = _ew("swiglu", swiglu, (N_DEV, T // tm_f), [blk3(gate), blk3(up)], [out3])[0]
    ffn = _mm("mm_down", "nn",
              [(act, (None, tm_, FP), lambda i, j, k: (k, i, 0), wd, (None, FP, tn_d), lambda i, j, k: (k, 0, j))],
              (T // tm_, D // tn_d, N_DEV), (tm_, tn_d), (T, D), F32, (tm_, tn_d), lambda i, j, k: (i, j))

    def loss_head(h, f, t, g5):
        y = h + _rms(f, g5)
        diff = y - t
        dy = diff / D
        d_f, dg5 = _rms_bwd(f, g5, dy)
        lsum = jnp.sum(jnp.sum(diff * diff, axis=0, keepdims=True), axis=1, keepdims=True) * (0.5 / D)
        return dy, d_f, dg5, lsum

    dy, d_ffn, dg_post_ffn, loss_part = _ew(
        "loss_head", loss_head, (T // tm_d,),
        [_rows(h_res, tm_d), _rows(ffn, tm_d), _rows(tgt, tm_d), _vec(post_ffn_norm)],
        [_rows_out((T, D), F32, tm_d), _rows_out((T, D), BF16, tm_d)], [_vec_out((1, D)), _vec_out((1, 1))])

    d_act = _mm("mm_dact", "nt",
                [(d_ffn, (tm_, tk_d), lambda j, i, k: (i, k), wd, (None, FP, tk_d), lambda j, i, k: (j, 0, k))],
                (N_DEV, T // tm_, D // tk_d), (tm_, FP), (N_DEV, T, FP), BF16, (None, tm_, FP), lambda j, i, k: (j, i, 0))

    def swiglu_bwd(da, g, u):
        da, g, u = da.astype(F32), g.astype(F32), u.astype(F32)
        sg = jax.nn.sigmoid(g)
        return da * u * sg * (1.0 + g * (1.0 - sg)), da * g * sg

    d_gate, d_up = _ew("swiglu_bwd", swiglu_bwd, (N_DEV, T // tm_f), [blk3(d_act), blk3(gate), blk3(up)], [out3, out3])

    tk_t = _pick(T, 512, LANE)
    dwd = _mm("mm_dwd", "tn",
              [(act, (None, tk_t, FP), lambda j, n, k: (j, k, 0), d_ffn, (tk_t, tn_d), lambda j, n, k: (k, n))],
              (N_DEV, D // tn_d, T // tk_t), (FP, tn_d), (N_DEV, FP, D), BF16, (None, FP, tn_d), lambda j, n, k: (j, 0, n))

    def ffn_dw(name, dz):
        return _mm(name, "tn",
                   [(hn, (tk_t, tn_d), lambda j, i, k: (k, i), dz, (None, tk_t, FP), lambda j, i, k: (j, k, 0))],
                   (N_DEV, D // tn_d, T // tk_t), (tn_d, FP), (N_DEV, D, FP), BF16, (None, tn_d, FP),
                   lambda j, i, k: (j, i, 0))

    dwg, dwu = ffn_dw("mm_dwg", d_gate), ffn_dw("mm_dwu", d_up)
    d_hn = _mm("mm_dhn", "nt",
               [(d_gate, (None, tm_, FP), lambda i, j, k: (k, i, 0), wg, (None, tn_d, FP), lambda i, j, k: (k, j, 0)),
                (d_up, (None, tm_, FP), lambda i, j, k: (k, i, 0), wu, (None, tn_d, FP), lambda i, j, k: (k, j, 0))],
               (T // tm_, D // tn_d, N_DEV), (tm_, tn_d), (T, D), F32, (tm_, tn_d), lambda i, j, k: (i, j))

    def pre_ffn_bwd(h, dhn, dy, mo, g4, g3):
        d1, dg4 = _rms_bwd(h, g4, dhn)
        dh = dy + d1
        dmo, dg3 = _rms_bwd(mo, g3, dh)
        return dh, dmo, dg4, dg3

    dh, d_mo, dg_pre_ffn, dg_post_mix = _ew(
        "pre_ffn_bwd", pre_ffn_bwd, (T // tm_d,),
        [_rows(h_res, tm_d), _rows(d_hn, tm_d), _rows(dy, tm_d), _rows(mix_out, tm_d), _vec(pre_ffn_norm),
         _vec(post_mix_norm)],
        [_rows_out((T, D), F32, tm_d), _rows_out((T, D), BF16, tm_d)], [_vec_out((1, D)), _vec_out((1, D))])

    d_mixed = _mm2("mm_dmixed", "nt", d_mo, wo, F32)
    dwo = jnp.concatenate([_mm2("mm_dwo_a", "tn", mixed_a, d_mo, BF16), _mm2("mm_dwo_g", "tn", mixed_g, d_mo, BF16)],
                          axis=0)

    d_a_out, dg_attn = _ew("rms_attn_bwd", lambda a, g, d: _rms_bwd(a, g, d), (T // tm_a,),
                           [_rows(a_out, tm_a), _vec(attn_out_norm), _rows(d_mixed, tm_a, AW, 0)],
                           [_rows_out((T, AW), BF16, tm_a)], [_vec_out((1, AW))])
    d_pb, dg_gmlp, dws, dbs, dlng, dlnb = _sgu_bwd(proj_b, d_mixed, v_ln_gain, v_ln_bias, ws, ws_t, bsb, gmlp_out_norm)

    dq_full, dk_full, dv_h = _attn_bwd(q_full, k_full, v_h, d_a_out)

    def rope_q_bwd(d, c, s):
        r = d[:, LANE:]
        return jnp.concatenate([d[:, :LANE], r * c, r * s], axis=1) * scale

    dq_ext = _ew("rope_q_bwd", rope_q_bwd, (T // tm_h, H),
                 [(dq_full, (None, tm_h, K_FULL), lambda i, h: (h, i, 0)), (cosp, (tm_h, LANE), lambda i, h: (i, 0)),
                  (sinp, (tm_h, LANE), lambda i, h: (i, 0))],
                 [((T, H * Q_EXT), BF16, (tm_h, Q_EXT), lambda i, h: (i, h))])[0]

    def make_kv_bwd(dk, dv, c, s):
        r = dk[:, LANE:]
        return jnp.concatenate([dk[:, :LANE], dv], axis=1), jnp.concatenate([r * c, r * s], axis=1)

    d_kv, d_kr = _ew("make_kv_bwd", make_kv_bwd, (T // tm_h, H),
                     [(dk_full, (None, tm_h, K_FULL), lambda i, h: (h, i, 0)),
                      (dv_h, (None, tm_h, V_DIM), lambda i, h: (h, i, 0)),
                      (cosp, (tm_h, LANE), lambda i, h: (i, 0)), (sinp, (tm_h, LANE), lambda i, h: (i, 0))],
                     [((T, H * K_FULL), BF16, (tm_h, K_FULL), lambda i, h: (i, h))],
                     [((T, K_FULL), F32, (tm_h, K_FULL), lambda i, h: (i, 0))])

    d_qn = _mm2("mm_dqn", "nt", dq_ext, wq_ext, F32)
    dwq_ext = _mm2("mm_dwq", "tn", qn, dq_ext, F32)
    d_kvn = _mm2("mm_dkvn", "nt", d_kv, wkv, F32)
    dwkv = _mm2("mm_dwkv", "tn", kvn, d_kv, BF16)

    d_qc, dg_q = _ew("rms_q_bwd", lambda a, g, d: _rms_bwd(a, g, d), (T // tm_q,),
                     [_rows(proj_a, tm_q, QL, 0), _vec(q_norm), _rows(d_qn, tm_q)],
                     [_rows_out((T, QL), BF16, tm_q)], [_vec_out((1, QL))])
    d_kvc, dg_kv = _ew("rms_kv_bwd", lambda a, g, d: _rms_bwd(a, g, d), (T // tm_q,),
                       [_rows(proj_a, tm_q, KVL, QL // KVL), _vec(kv_norm), _rows(d_kvn, tm_q)],
                       [_rows_out((T, KVL), BF16, tm_q)], [_vec_out((1, KVL))])
    d_pa = jnp.concatenate([d_qc, d_kvc, d_kr.astype(BF16)], axis=1)

    d_xn_a = _mm2("mm_dxn_a", "nt", d_pa, w_a, F32, tk_pref=w_a.shape[1])
    d_xn_b = _mm2("mm_dxn_b", "nt", d_pb, w_b, F32)
    dwa = _mm2("mm_dwa", "tn", xn, d_pa, F32, tn_pref=w_a.shape[1])
    dwb = _mm2("mm_dwb", "tn", xn, d_pb, BF16)

    def x_bwd(x, da, db, dh, g0):
        dx, dg0 = _rms_bwd(x, g0, da + db)
        return dh + dx, dg0

    grad_x, dg_pre_mix = _ew("x_bwd", x_bwd, (T // tm_d,),
                             [_rows(x2, tm_d), _rows(d_xn_a, tm_d), _rows(d_xn_b, tm_d), _rows(dh, tm_d),
                              _vec(pre_mix_norm)],
                             [_rows_out((T, D), F32, tm_d)], [_vec_out((1, D))])

    dkr_w = dwa[:, C0:C0 + ROPE_DIM] + _rot_t(dwa[:, C0 + LANE:C0 + LANE + ROPE_DIM])
    dw_in = jnp.concatenate([dwa[:, :C0].astype(BF16), dkr_w.astype(BF16), dwb], axis=1)
    dw_in = dw_in.reshape(D, N_DEV, IWS).transpose(1, 0, 2)
    dq3 = dwq_ext.reshape(QL, H, Q_EXT)
    dwq = jnp.concatenate([dq3[..., :NOPE_DIM],
                           dq3[..., LANE:LANE + ROPE_DIM] + _rot_t(dq3[..., 2 * LANE:2 * LANE + ROPE_DIM])], axis=-1)
    dwq = dwq.astype(BF16).reshape(QL, N_DEV, H * QK_HEAD // N_DEV).transpose(1, 0, 2)
    dwkv = dwkv.reshape(KVL, N_DEV, w_ukv.shape[2]).transpose(1, 0, 2)
    dwo = dwo.reshape(N_DEV, D // N_DEV, D)

    big = {
        "w_in": _all_to_all("a2a_w_in", dw_in),
        "w_uq": _all_to_all("a2a_w_uq", dwq),
        "w_ukv": _all_to_all("a2a_w_ukv", dwkv),
        "w_out": _all_to_all("a2a_w_out", dwo),
        "w_gate": _all_to_all("a2a_w_gate", dwg)[:, :, :FS],
        "w_up": _all_to_all("a2a_w_up", dwu)[:, :, :FS],
        "w_down": _all_to_all("a2a_w_down", dwd)[:, :FS, :],
    }

    small_names = ["pre_mix_norm", "q_norm", "kv_norm", "v_ln_gain", "v_ln_bias", "w_spatial", "b_spatial",
                   "attn_out_norm", "gmlp_out_norm", "post_mix_norm", "pre_ffn_norm", "post_ffn_norm"]
    small_grads = [dg_pre_mix, dg_q, dg_kv, dlng, dlnb, dws, dbs[:, 0, :], dg_attn, dg_gmlp, dg_post_mix, dg_pre_ffn,
                   dg_post_ffn]
    n_small = sum(g.size for g in small_grads)
    pad_small = _round_up(n_small, SUBLANE * LANE) - n_small

    def pack(arrs):
        flat = jnp.concatenate([a.reshape(-1) for a in arrs] + [jnp.zeros((pad_small,), F32)])
        return flat.reshape(-1, LANE)

    small_parts = _all_gather("ag_small_grads", pack(small_grads))
    sm = _adamw("adamw_small", small_parts, pack([given[n] for n in small_names]),
                pack([given["m_" + n] for n in small_names]), pack([given["v_" + n] for n in small_names]))

    results = {}
    off = 0
    for n in small_names:
        shp, size = given[n].shape, given[n].size
        results[n] = [r.reshape(-1)[off:off + size].reshape(shp) for r in sm]
        off += size
    for n, parts in big.items():
        w3 = given[n]
        shp2 = w3.shape[1:]
        res = _adamw("adamw_" + n, parts, w3.reshape(shp2), given["m_" + n].reshape(shp2), given["v_" + n].reshape(shp2))
        results[n] = [r.reshape(w3.shape) for r in res]

    loss = lax.psum(loss_part[0, 0], MESH_AXES)
    order = ["pre_mix_norm", "w_in", "q_norm", "kv_norm", "w_uq", "w_ukv", "v_ln_gain", "v_ln_bias", "w_spatial",
             "b_spatial", "attn_out_norm", "gmlp_out_norm", "w_out", "post_mix_norm", "pre_ffn_norm", "w_gate", "w_up",
             "w_down", "post_ffn_norm"]
    out = [loss, grad_x[None]]
    for kind in range(4):
        out += [results[n][kind] for n in order]
    return tuple(out)
```

```python
import math

import jax
import jax.numpy as jnp
from jax import lax
from jax.experimental import pallas as pl
from jax.experimental.pallas import tpu as pltpu

F32 = jnp.float32
BF16 = jnp.bfloat16
N_DEV = 8
MESH_AXES = ("x", "y", "c")
MESH = pl.DeviceIdType.MESH

NOPE_DIM = 128
ROPE_DIM = 64
V_DIM = 128
ROPE_THETA = 10000.0
CHUNK = 128
EPS = 1e-6
QK_HEAD = NOPE_DIM + ROPE_DIM
Q_EXT = 3 * 128
K_FULL = 2 * 128

ADAM_LR = 0.001
ADAM_B1 = 0.9
ADAM_B2 = 0.999
ADAM_EPS = 1e-08
ADAM_WD = 0.01
ADAM_STEP = 10

LANE = 128
SUBLANE = 8
VMEM_LIMIT = 56 * 1024 * 1024
ROW_BLOCK_ELEMS = 512 * 1024
ADAM_BLOCK_ELEMS = 256 * 1024

DN = {
    "nn": (((1,), (0,)), ((), ())),
    "nt": (((1,), (1,)), ((), ())),
    "tn": (((0,), (0,)), ((), ())),
}


def _pick(dim, pref, mult):
    t = min(pref, dim)
    t -= t % mult
    while t >= mult:
        if dim % t == 0:
            return t
        t -= mult
    return dim


def _round_up(n, m):
    return (n + m - 1) // m * m


def _params(sem):
    return pltpu.CompilerParams(dimension_semantics=sem, vmem_limit_bytes=VMEM_LIMIT)


def _dep_specs(deps, nd):
    return [pl.BlockSpec(d.shape, lambda *_, r=d.ndim: (0,) * r) for d in deps]


def _mm(name, mode, pairs, grid, acc_shape, out_shape, out_dtype, out_block, out_map, deps=()):
    n = len(pairs)
    nx = len(deps)
    nk = grid[-1]
    nd = len(grid)
    dn = DN[mode]

    def body(*refs):
        o_ref, acc = refs[2 * n + nx], refs[2 * n + nx + 1]
        k = pl.program_id(nd - 1)

        @pl.when(k == 0)
        def _():
            acc[...] = jnp.zeros_like(acc)

        s = None
        for i in range(n):
            d = lax.dot_general(refs[2 * i][...], refs[2 * i + 1][...], dn, preferred_element_type=F32)
            s = d if s is None else s + d
        acc[...] += s

        @pl.when(k == nk - 1)
        def _():
            o_ref[...] = acc[...].astype(o_ref.dtype)

    in_specs, args = [], []
    for a, a_blk, a_map, b, b_blk, b_map in pairs:
        in_specs += [pl.BlockSpec(a_blk, a_map), pl.BlockSpec(b_blk, b_map)]
        args += [a, b]
    return pl.pallas_call(
        body,
        name=name,
        grid=grid,
        in_specs=in_specs + _dep_specs(deps, nd),
        out_specs=pl.BlockSpec(out_block, out_map),
        out_shape=jax.ShapeDtypeStruct(out_shape, out_dtype),
        scratch_shapes=[pltpu.VMEM(acc_shape, F32)],
        compiler_params=_params(("parallel",) * (nd - 1) + ("arbitrary",)),
    )(*args, *deps)


def _mm2(name, mode, a, b, out_dtype, tm_pref=1024, tn_pref=1024, tk_pref=512, deps=()):
    if mode == "nn":
        (M, K), N = a.shape, b.shape[1]
    elif mode == "nt":
        (M, K), N = a.shape, b.shape[0]
    else:
        (K, M), N = a.shape, b.shape[1]
    tm, tn, tk = _pick(M, tm_pref, LANE), _pick(N, tn_pref, LANE), _pick(K, tk_pref, LANE)
    if mode == "tn":
        a_blk, a_map = (tk, tm), lambda i, j, k: (k, i)
    else:
        a_blk, a_map = (tm, tk), lambda i, j, k: (i, k)
    if mode == "nt":
        b_blk, b_map = (tn, tk), lambda i, j, k: (j, k)
    else:
        b_blk, b_map = (tk, tn), lambda i, j, k: (k, j)
    return _mm(name, mode, [(a, a_blk, a_map, b, b_blk, b_map)], (M // tm, N // tn, K // tk), (tm, tn),
               (M, N), out_dtype, (tm, tn), lambda i, j, k: (i, j), deps=deps)


def _ew(name, fn, grid, ins, outs, accs=(), deps=()):
    ni, no, na = len(ins), len(outs), len(accs)
    nd = len(grid)
    nx = len(deps)

    def body(*refs):
        vals = fn(*[r[...] for r in refs[:ni]])
        if not isinstance(vals, (tuple, list)):
            vals = (vals,)
        for r, v in zip(refs[ni + nx:ni + nx + no], vals[:no]):
            r[...] = v.astype(r.dtype)
        if na:
            first = pl.program_id(nd - 1) == 0
            for r, v in zip(refs[ni + nx + no:], vals[no:]):
                @pl.when(first)
                def _(r=r, v=v):
                    r[...] = v.astype(r.dtype)

                @pl.when(jnp.logical_not(first))
                def _(r=r, v=v):
                    r[...] += v.astype(r.dtype)

    all_outs = list(outs) + list(accs)
    sem = ("parallel",) * (nd - 1) + (("arbitrary",) if na else ("parallel",))
    res = pl.pallas_call(
        body,
        name=name,
        grid=grid,
        in_specs=[pl.BlockSpec(blk, imap) for (_, blk, imap) in ins] + _dep_specs(deps, nd),
        out_specs=[pl.BlockSpec(blk, imap) for (_, _, blk, imap) in all_outs],
        out_shape=[jax.ShapeDtypeStruct(s, d) for (s, d, _, _) in all_outs],
        compiler_params=_params(sem),
    )(*[a for (a, _, _) in ins], *deps)
    return res


def _rows(a, tm, width=None, cblk=0):
    width = a.shape[1] if width is None else width
    return (a, (tm, width), lambda i, c=cblk: (i, c))


def _vec(p):
    return (p, p.shape, lambda i: (0,) * p.ndim)


def _rows_out(shape, dtype, tm):
    return (shape, dtype, (tm, shape[1]), lambda i: (i, 0))


def _vec_out(shape):
    return (shape, F32, shape, lambda i: (0,) * len(shape))


def _row_tile(T, D):
    return _pick(T, max(SUBLANE, ROW_BLOCK_ELEMS // D), SUBLANE)


def _rms(x, g):
    r = lax.rsqrt(jnp.mean(x * x, axis=-1, keepdims=True) + EPS)
    return x * r * g


def _rms_bwd(x, g, dy):
    r = lax.rsqrt(jnp.mean(x * x, axis=-1, keepdims=True) + EPS)
    xh = x * r
    dg = jnp.sum(dy * xh, axis=0, keepdims=True)
    dxh = dy * g
    dx = r * (dxh - xh * jnp.mean(dxh * xh, axis=-1, keepdims=True))
    return dx, dg


GELU_C = math.sqrt(2.0 / math.pi)
GELU_A = 0.044715


def _gelu(x):
    return 0.5 * x * (1.0 + jnp.tanh(GELU_C * (x + GELU_A * x * x * x)))


def _gelu_grad(x):
    t = jnp.tanh(GELU_C * (x + GELU_A * x * x * x))
    return 0.5 * (1.0 + t) + 0.5 * x * (1.0 - t * t) * GELU_C * (1.0 + 3.0 * GELU_A * x * x)


def _rot(w):
    h = w.shape[-1] // 2
    return jnp.concatenate([-w[..., h:], w[..., :h]], axis=-1)


def _rot_t(dw):
    h = dw.shape[-1] // 2
    return jnp.concatenate([dw[..., h:], -dw[..., :h]], axis=-1)


def _coords():
    return lax.axis_index("x"), lax.axis_index("y"), lax.axis_index("c")


def _all_gather(name, shard):
    def body(x_ref, o_ref, send_sems, recv_sems, local_sem):
        xi, yi, ci = _coords()
        me, sibling = (xi, yi, ci), (xi, yi, 1 - ci)
        chips = [(1 - xi, yi), (xi, 1 - yi), (1 - xi, 1 - yi)]

        def slot(px, py, pc):
            return o_ref.at[4 * px + 2 * py + pc]

        def copy(k, block, to, src=None):
            return pltpu.make_async_remote_copy(
                src_ref=slot(*block) if src is None else src, dst_ref=slot(*block),
                send_sem=send_sems.at[k], recv_sem=recv_sems.at[k], device_id=to, device_id_type=MESH)

        mine = pltpu.make_async_copy(x_ref, slot(*me), local_sem)
        mine.start()
        first = [copy(0, me, sibling, src=x_ref)]
        first += [copy(1 + j, me, (*chip, ci), src=x_ref) for j, chip in enumerate(chips)]
        for cp in first:
            cp.start()
        passed = [copy(4 + j, (*chip, ci), sibling) for j, chip in enumerate(chips)]
        for j, chip in enumerate(chips):
            copy(1 + j, (*chip, ci), me).wait_recv()
            passed[j].start()
        copy(0, sibling, me).wait_recv()
        for j, chip in enumerate(chips):
            copy(4 + j, (*chip, 1 - ci), me).wait_recv()
        for cp in first + passed:
            cp.wait_send()
        mine.wait()

    return pl.pallas_call(
        body,
        name=name,
        out_shape=jax.ShapeDtypeStruct((N_DEV,) + shard.shape, shard.dtype),
        in_specs=[pl.BlockSpec(memory_space=pltpu.HBM)],
        out_specs=pl.BlockSpec(memory_space=pltpu.HBM),
        scratch_shapes=[pltpu.SemaphoreType.DMA((N_DEV - 1,)), pltpu.SemaphoreType.DMA((N_DEV - 1,)),
                        pltpu.SemaphoreType.DMA],
    )(shard)


HBM_SPEC = pl.BlockSpec(memory_space=pltpu.HBM)
SEM_SPEC = pl.BlockSpec(memory_space=pltpu.SEMAPHORE)
ANY_SPEC = pl.BlockSpec(memory_space=pl.ANY)
EFFECT = pltpu.SideEffectType.DATAFLOW_SIDE_EFFECTING
TOKEN = (SUBLANE, LANE)


def _my_slot():
    xi, yi, ci = _coords()
    return 4 * xi + 2 * yi + ci


def _own_slot_filled(block):
    land = lax.empty((N_DEV,) + block.shape, block.dtype)
    return lax.dynamic_update_slice(land, block[None], (_my_slot(),) + (0,) * block.ndim)


def _split_start(name, n, n_copies, issue, srcs, lands, deps):
    def body(*refs):
        src_refs, land_refs = refs[:n], refs[n:2 * n]
        outs = refs[2 * n + len(deps):]
        send, recv, token = outs[:n], outs[n:2 * n], outs[4 * n]
        for t in range(n):
            issue(t, src_refs[t], land_refs[t], send[t], recv[t])
        token[...] = jnp.zeros_like(token)

    sems = [pltpu.SemaphoreType.DMA((n_copies,))] * (2 * n)
    res = pl.pallas_call(
        body,
        name=name,
        in_specs=[HBM_SPEC] * (2 * n) + [ANY_SPEC] * len(deps),
        out_specs=[SEM_SPEC] * (2 * n) + [HBM_SPEC] * (2 * n) + [pl.BlockSpec(memory_space=pltpu.VMEM)],
        out_shape=sems + [pltpu.HBM(a.shape, a.dtype) for a in list(srcs) + list(lands)]
        + [jax.ShapeDtypeStruct(TOKEN, F32)],
        input_output_aliases={i: 2 * n + i for i in range(2 * n)},
        compiler_params=pltpu.CompilerParams(has_side_effects=EFFECT),
    )(*[pltpu.with_memory_space_constraint(a, pltpu.HBM) for a in list(srcs) + list(lands)], *deps)
    state = dict(n=n, n_copies=n_copies, send=res[:n], recv=res[n:2 * n], srcs=res[2 * n:3 * n],
                 lands=res[3 * n:4 * n])
    return state, res[4 * n]


def _split_wait(name, state, unit_src, after):
    n, n_copies = state["n"], state["n_copies"]

    def body(*refs):
        src_refs, land_refs = refs[:n], refs[n:2 * n]
        send, recv = refs[2 * n:3 * n], refs[3 * n:4 * n]
        me = _coords()
        for t in range(n):
            for k in range(n_copies):
                cp = pltpu.make_async_remote_copy(
                    src_ref=unit_src(src_refs[t]), dst_ref=land_refs[t].at[0], send_sem=send[t].at[k],
                    recv_sem=recv[t].at[k], device_id=me, device_id_type=MESH)
                cp.wait_send()
                cp.wait_recv()

    bufs = list(state["srcs"]) + list(state["lands"])
    res = pl.pallas_call(
        body,
        name=name,
        in_specs=[HBM_SPEC] * (2 * n) + [SEM_SPEC] * (2 * n) + [ANY_SPEC],
        out_specs=[HBM_SPEC] * (2 * n),
        out_shape=[pltpu.HBM(a.shape, a.dtype) for a in bufs],
        input_output_aliases={i: i for i in range(2 * n)},
        compiler_params=pltpu.CompilerParams(has_side_effects=EFFECT),
    )(*bufs, *state["send"], *state["recv"], after)
    return list(res[n:])


def _ag_start(name, shards, deps):
    def issue(t, x_ref, land_ref, send, recv):
        xi, yi, ci = _coords()
        to = [(xi, yi, 1 - ci), (1 - xi, yi, ci), (xi, 1 - yi, ci), (1 - xi, 1 - yi, ci)]
        for k, dev in enumerate(to):
            pltpu.make_async_remote_copy(src_ref=x_ref, dst_ref=land_ref.at[4 * xi + 2 * yi + ci], send_sem=send.at[k],
                                         recv_sem=recv.at[k], device_id=dev, device_id_type=MESH).start()

    return _split_start(name, len(shards), 4, issue, shards, [_own_slot_filled(s) for s in shards], deps)


def _ag_mid(name, state, after):
    lands = _split_wait(name + "_w1", state, lambda r: r, after)

    def issue(t, src_ref, land_ref, send, recv):
        xi, yi, ci = _coords()
        for k, (px, py) in enumerate([(1 - xi, yi), (xi, 1 - yi), (1 - xi, 1 - yi)]):
            rows = land_ref.at[4 * px + 2 * py + ci]
            pltpu.make_async_remote_copy(src_ref=rows, dst_ref=rows, send_sem=send.at[k], recv_sem=recv.at[k],
                                         device_id=(xi, yi, 1 - ci), device_id_type=MESH).start()

    n = len(lands)

    def body(*refs):
        land_refs = refs[:n]
        outs = refs[n:]
        send, recv, token = outs[:n], outs[n:2 * n], outs[3 * n]
        for t in range(n):
            issue(t, land_refs[t], land_refs[t], send[t], recv[t])
        token[...] = jnp.zeros_like(token)

    res = pl.pallas_call(
        body,
        name=name + "_s2",
        in_specs=[HBM_SPEC] * n,
        out_specs=[SEM_SPEC] * (2 * n) + [HBM_SPEC] * n + [pl.BlockSpec(memory_space=pltpu.VMEM)],
        out_shape=[pltpu.SemaphoreType.DMA((3,))] * (2 * n) + [pltpu.HBM(a.shape, a.dtype) for a in lands]
        + [jax.ShapeDtypeStruct(TOKEN, F32)],
        input_output_aliases={i: 2 * n + i for i in range(n)},
        compiler_params=pltpu.CompilerParams(has_side_effects=EFFECT),
    )(*lands)
    state2 = dict(n=n, send=res[:n], recv=res[n:2 * n], lands=res[2 * n:3 * n])
    return state2, res[3 * n]


def _ag_finish(name, state2, after):
    n = state2["n"]

    def body(*refs):
        land_refs, send, recv = refs[:n], refs[n:2 * n], refs[2 * n:3 * n]
        me = _coords()
        for t in range(n):
            for k in range(3):
                cp = pltpu.make_async_remote_copy(
                    src_ref=land_refs[t].at[0], dst_ref=land_refs[t].at[0], send_sem=send[t].at[k],
                    recv_sem=recv[t].at[k], device_id=me, device_id_type=MESH)
                cp.wait_send()
                cp.wait_recv()

    res = pl.pallas_call(
        body,
        name=name + "_w2",
        in_specs=[HBM_SPEC] * n + [SEM_SPEC] * (2 * n) + [ANY_SPEC],
        out_specs=[HBM_SPEC] * n,
        out_shape=[pltpu.HBM(a.shape, a.dtype) for a in state2["lands"]],
        input_output_aliases={i: i for i in range(n)},
        compiler_params=pltpu.CompilerParams(has_side_effects=EFFECT),
    )(*state2["lands"], *state2["send"], *state2["recv"], after)
    return list(res)


def _a2a_start(name, parts, deps):
    def issue(t, x_ref, land_ref, send, recv):
        xi, yi, ci = _coords()
        for k in range(N_DEV - 1):
            px = 1 - xi if (k + 1) & 4 else xi
            py = 1 - yi if (k + 1) & 2 else yi
            pc = 1 - ci if (k + 1) & 1 else ci
            pltpu.make_async_remote_copy(src_ref=x_ref.at[4 * px + 2 * py + pc], dst_ref=land_ref.at[4 * xi + 2 * yi + ci],
                                         send_sem=send.at[k], recv_sem=recv.at[k], device_id=(px, py, pc),
                                         device_id_type=MESH).start()

    lands = [_own_slot_filled(lax.dynamic_index_in_dim(p, _my_slot(), 0, keepdims=False)) for p in parts]
    return _split_start(name, len(parts), N_DEV - 1, issue, parts, lands, deps)


def _a2a_finish(name, state, after):
    return _split_wait(name + "_w", state, lambda r: r.at[0], after)


def _attn_fwd(q_full, k_full, v_h):
    H, T, _ = q_full.shape
    tq = _pick(T, 256, LANE)

    def body(q_ref, k_ref, v_ref, o_ref):
        s = lax.dot_general(q_ref[...], k_ref[...], DN["nt"], preferred_element_type=F32)
        m = jnp.max(s, axis=-1, keepdims=True)
        p = jnp.exp(s - m)
        l = jnp.sum(p, axis=-1, keepdims=True)
        o = jnp.dot(p.astype(BF16), v_ref[...], preferred_element_type=F32)
        o_ref[...] = o / l

    return pl.pallas_call(
        body,
        name="attn_fwd",
        grid=(H, T // tq),
        in_specs=[pl.BlockSpec((None, tq, K_FULL), lambda h, i: (h, i, 0)),
                  pl.BlockSpec((None, T, K_FULL), lambda h, i: (h, 0, 0)),
                  pl.BlockSpec((None, T, V_DIM), lambda h, i: (h, 0, 0))],
        out_specs=pl.BlockSpec((tq, V_DIM), lambda h, i: (i, h)),
        out_shape=jax.ShapeDtypeStruct((T, H * V_DIM), F32),
        compiler_params=_params(("parallel", "parallel")),
    )(q_full, k_full, v_h)


def _attn_bwd(q_full, k_full, v_h, d_out):
    H, T, _ = q_full.shape
    tq = _pick(T, 256, LANE)

    def body(q_ref, k_ref, v_ref, do_ref, dq_ref, dk_ref, dv_ref):
        i = pl.program_id(1)
        q, k, v, do = q_ref[...], k_ref[...], v_ref[...], do_ref[...]
        st = lax.dot_general(k, q, DN["nt"], preferred_element_type=F32)
        m = jnp.max(st, axis=0, keepdims=True)
        e = jnp.exp(st - m)
        pt = e * (1.0 / jnp.sum(e, axis=0, keepdims=True))
        dpt = lax.dot_general(v, do, DN["nt"], preferred_element_type=F32)
        delta = jnp.sum(pt * dpt, axis=0, keepdims=True)
        dst = (pt * (dpt - delta)).astype(BF16)
        dv = jnp.dot(pt.astype(BF16), do, preferred_element_type=F32)
        dk = jnp.dot(dst, q, preferred_element_type=F32)
        dq_ref[...] = lax.dot_general(dst, k, DN["tn"], preferred_element_type=F32)

        @pl.when(i == 0)
        def _():
            dk_ref[...] = dk
            dv_ref[...] = dv

        @pl.when(i != 0)
        def _():
            dk_ref[...] += dk
            dv_ref[...] += dv

    return pl.pallas_call(
        body,
        name="attn_bwd",
        grid=(H, T // tq),
        in_specs=[pl.BlockSpec((None, tq, K_FULL), lambda h, i: (h, i, 0)),
                  pl.BlockSpec((None, T, K_FULL), lambda h, i: (h, 0, 0)),
                  pl.BlockSpec((None, T, V_DIM), lambda h, i: (h, 0, 0)),
                  pl.BlockSpec((tq, V_DIM), lambda h, i: (i, h))],
        out_specs=[pl.BlockSpec((None, tq, K_FULL), lambda h, i: (h, i, 0)),
                   pl.BlockSpec((None, T, K_FULL), lambda h, i: (h, 0, 0)),
                   pl.BlockSpec((None, T, V_DIM), lambda h, i: (h, 0, 0))],
        out_shape=[jax.ShapeDtypeStruct((H, T, K_FULL), F32), jax.ShapeDtypeStruct((H, T, K_FULL), F32),
                   jax.ShapeDtypeStruct((H, T, V_DIM), F32)],
        compiler_params=_params(("parallel", "arbitrary")),
    )(q_full, k_full, v_h, d_out)


def _sgu_forward_math(pb, lng, lnb, ws, bsb, gw, gh):
    u_raw, v_raw = pb[:, :gw], pb[:, gw:]
    u, v = _gelu(u_raw), _gelu(v_raw)
    vc = v - jnp.mean(v, axis=-1, keepdims=True)
    rstd = lax.rsqrt(jnp.mean(vc * vc, axis=-1, keepdims=True) + EPS)
    vhat = vc * rstd
    vln = vhat * lng + lnb
    hd = gw // gh
    s = jnp.concatenate(
        [jnp.dot(ws[g], vln[:, g * hd:(g + 1) * hd].astype(BF16), preferred_element_type=F32) + bsb[g]
         for g in range(gh)], axis=1)
    return u_raw, v_raw, u, rstd, vhat, vln, s


def _sgu_fwd(proj_b, lng, lnb, ws, bsb, gg):
    T, gw2 = proj_b.shape
    gw, gh = gw2 // 2, ws.shape[0]

    def fn(pb, lng, lnb, ws, bsb, gg):
        _, _, u, _, _, _, s = _sgu_forward_math(pb, lng, lnb, ws, bsb, gw, gh)
        return _rms(u * s, gg)

    return _ew("sgu_fwd", fn, (T // CHUNK,),
               [_rows(proj_b, CHUNK), _vec(lng), _vec(lnb), _vec(ws), _vec(bsb), _vec(gg)],
               [_rows_out((T, gw), BF16, CHUNK)])[0]


def _sgu_bwd(proj_b, d_mixed, lng, lnb, ws, ws_t, bsb, gg):
    T, gw2 = proj_b.shape
    gw, gh = gw2 // 2, ws.shape[0]
    hd = gw // gh

    def fn(pb, dmg, lng, lnb, ws, ws_t, bsb, gg):
        u_raw, v_raw, u, rstd, vhat, vln, s = _sgu_forward_math(pb, lng, lnb, ws, bsb, gw, gh)
        dgo, dgg = _rms_bwd(u * s, gg, dmg)
        du, ds = dgo * s, dgo * u
        ones = jnp.ones((SUBLANE, hd), F32)
        dws, dbs, dvln = [], [], []
        for g in range(gh):
            ds_g = ds[:, g * hd:(g + 1) * hd]
            ds_gb = ds_g.astype(BF16)
            dws.append(lax.dot_general(ds_gb, vln[:, g * hd:(g + 1) * hd].astype(BF16), DN["nt"],
                                       preferred_element_type=F32)[None])
            dbs.append(lax.dot_general(ones, ds_g, DN["nt"], preferred_element_type=F32,
                                       precision=lax.Precision.HIGHEST)[None])
            dvln.append(jnp.dot(ws_t[g], ds_gb, preferred_element_type=F32))
        dvln = jnp.concatenate(dvln, axis=1)
        dlng = jnp.sum(dvln * vhat, axis=0, keepdims=True)
        dlnb = jnp.sum(dvln, axis=0, keepdims=True)
        dvh = dvln * lng
        dv = rstd * (dvh - jnp.mean(dvh, axis=-1, keepdims=True) - vhat * jnp.mean(dvh * vhat, axis=-1, keepdims=True))
        d_pb = jnp.concatenate([du * _gelu_grad(u_raw), dv * _gelu_grad(v_raw)], axis=1)
        return d_pb, dgg, jnp.concatenate(dws, axis=0), jnp.concatenate(dbs, axis=0), dlng, dlnb

    return _ew("sgu_bwd", fn, (T // CHUNK,),
               [_rows(proj_b, CHUNK), _rows(d_mixed, CHUNK, gw, 1), _vec(lng), _vec(lnb), _vec(ws), _vec(ws_t),
                _vec(bsb), _vec(gg)],
               [_rows_out((T, gw2), BF16, CHUNK)],
               [_vec_out((1, gw)), _vec_out((gh, CHUNK, CHUNK)), _vec_out((gh, SUBLANE, CHUNK)), _vec_out((1, gw)),
                _vec_out((1, gw))])


def _adamw(name, parts, w, m, v):
    R, C = w.shape
    tm = _pick(R, max(SUBLANE, ADAM_BLOCK_ELEMS // C), SUBLANE)

    def fn(p, w, m, v):
        g = p[0].astype(F32)
        for i in range(1, N_DEV):
            g = g + p[i].astype(F32)
        m2 = ADAM_B1 * m + (1.0 - ADAM_B1) * g
        v2 = ADAM_B2 * v + (1.0 - ADAM_B2) * (g * g)
        m_hat = m2 / (1.0 - ADAM_B1 ** ADAM_STEP)
        v_hat = v2 / (1.0 - ADAM_B2 ** ADAM_STEP)
        delta = -ADAM_LR * (m_hat / (jnp.sqrt(v_hat) + ADAM_EPS) + ADAM_WD * w)
        return g, delta, m2, v2

    return _ew(name, fn, (R // tm,),
               [(parts, (N_DEV, tm, C), lambda i: (0, i, 0)), _rows(w, tm), _rows(m, tm), _rows(v, tm)],
               [_rows_out((R, C), F32, tm)] * 4)


def kernel(x, positions, pre_mix_norm, w_in, q_norm, kv_norm, w_uq, w_ukv, v_ln_gain, v_ln_bias, w_spatial, b_spatial, attn_out_norm, gmlp_out_norm, w_out, post_mix_norm, pre_ffn_norm, w_gate, w_up, w_down, post_ffn_norm, loss_target, m_pre_mix_norm, m_w_in, m_q_norm, m_kv_norm, m_w_uq, m_w_ukv, m_v_ln_gain, m_v_ln_bias, m_w_spatial, m_b_spatial, m_attn_out_norm, m_gmlp_out_norm, m_w_out, m_post_mix_norm, m_pre_ffn_norm, m_w_gate, m_w_up, m_w_down, m_post_ffn_norm, v_pre_mix_norm, v_w_in, v_q_norm, v_kv_norm, v_w_uq, v_w_ukv, v_v_ln_gain, v_v_ln_bias, v_w_spatial, v_b_spatial, v_attn_out_norm, v_gmlp_out_norm, v_w_out, v_post_mix_norm, v_pre_ffn_norm, v_w_gate, v_w_up, v_w_down, v_post_ffn_norm):
    given = dict(locals())
    T, D = x.shape[1], x.shape[2]
    QL, KVL = w_uq.shape[1], w_ukv.shape[1]
    GW, GH = v_ln_gain.shape[1], w_spatial.shape[1]
    IWS = w_in.shape[2]
    IW = IWS * N_DEV
    H = w_ukv.shape[2] * N_DEV // (NOPE_DIM + V_DIM)
    AW = H * V_DIM
    FS = w_gate.shape[2]
    FP = _round_up(FS, LANE)
    C0 = QL + KVL
    assert IW == C0 + ROPE_DIM + 2 * GW and AW + GW == D and AW == GW
    assert QL % KVL == 0 and C0 % K_FULL == 0 and w_spatial.shape[2] == CHUNK and T % CHUNK == 0
    assert w_uq.shape[2] * N_DEV == H * QK_HEAD and D % N_DEV == 0

    x2, tgt = x[0], loss_target[0]
    tm_d = _row_tile(T, D)

    inv_freq = 1.0 / (ROPE_THETA ** (jnp.arange(0, ROPE_DIM, 2, dtype=F32) / ROPE_DIM))
    ang = positions[0].astype(F32)[:, None] * inv_freq
    zpad = jnp.zeros((T, LANE - ROPE_DIM), F32)
    cosp = jnp.concatenate([jnp.cos(ang), jnp.cos(ang), zpad], axis=1)
    sinp = jnp.concatenate([jnp.sin(ang), jnp.sin(ang), zpad], axis=1)

    st_a, tok = _ag_start("ag_a", [w_in[0].astype(BF16), w_uq[0].astype(BF16), w_ukv[0].astype(BF16)], [])
    st_o, tok = _ag_start("ag_o", [w_out[0].astype(BF16)], [tok])
    st_gu, tok = _ag_start("ag_gu", [jnp.pad(w_gate[0].astype(BF16), ((0, 0), (0, FP - FS))),
                                     jnp.pad(w_up[0].astype(BF16), ((0, 0), (0, FP - FS)))], [tok])
    st_d, tok = _ag_start("ag_d", [jnp.pad(w_down[0].astype(BF16), ((0, FP - FS), (0, 0)))], [tok])

    xn = _ew("rms_x", lambda x, g: _rms(x, g), (T // tm_d,), [_rows(x2, tm_d), _vec(pre_mix_norm)],
             [_rows_out((T, D), BF16, tm_d)], deps=[tok])[0]
    st_a, tok = _ag_mid("ag_a", st_a, xn)
    g_in, g_uq, g_ukv = _ag_finish("ag_a", st_a, tok)

    w_in_all = g_in.transpose(1, 0, 2).reshape(D, IW)
    kr_w = w_in_all[:, C0:C0 + ROPE_DIM]
    z64 = jnp.zeros((D, LANE - ROPE_DIM), BF16)
    w_a = jnp.concatenate([w_in_all[:, :C0], kr_w, z64, _rot(kr_w), z64], axis=1)
    w_b = w_in_all[:, C0 + ROPE_DIM:]
    wq = g_uq.transpose(1, 0, 2).reshape(QL, H, QK_HEAD)
    zq = jnp.zeros((QL, H, LANE - ROPE_DIM), BF16)
    wq_ext = jnp.concatenate([wq[..., :NOPE_DIM], wq[..., NOPE_DIM:], zq, _rot(wq[..., NOPE_DIM:]), zq],
                             axis=-1).reshape(QL, H * Q_EXT)
    wkv = g_ukv.transpose(1, 0, 2).reshape(KVL, H * (NOPE_DIM + V_DIM))

    ws = w_spatial[0].astype(BF16)
    ws_t = jnp.swapaxes(ws, 1, 2)
    bsb = jnp.broadcast_to(b_spatial[0][:, :, None], (GH, CHUNK, GW // GH))

    proj_a = _mm2("mm_proj_a", "nn", xn, w_a, F32, tn_pref=w_a.shape[1])
    proj_b = _mm2("mm_proj_b", "nn", xn, w_b, F32)

    tm_q = _row_tile(T, QL)
    qn = _ew("rms_q", lambda x, g: _rms(x, g), (T // tm_q,), [_rows(proj_a, tm_q, QL, 0), _vec(q_norm)],
             [_rows_out((T, QL), BF16, tm_q)])[0]
    kvn = _ew("rms_kv", lambda x, g: _rms(x, g), (T // tm_q,), [_rows(proj_a, tm_q, KVL, QL // KVL), _vec(kv_norm)],
              [_rows_out((T, KVL), BF16, tm_q)])[0]
    q_ext = _mm2("mm_q", "nn", qn, wq_ext, F32, tk_pref=1024)
    kv = _mm2("mm_kv", "nn", kvn, wkv, F32, tk_pref=1024)

    scale = 1.0 / math.sqrt(QK_HEAD)
    tm_h = _pick(T, 512, SUBLANE)

    def rope_q(q, c, s):
        return jnp.concatenate([q[:, :LANE], q[:, LANE:2 * LANE] * c + q[:, 2 * LANE:] * s], axis=1) * scale

    q_full = _ew("rope_q", rope_q, (T // tm_h, H),
                 [(q_ext, (tm_h, Q_EXT), lambda i, h: (i, h)), (cosp, (tm_h, LANE), lambda i, h: (i, 0)),
                  (sinp, (tm_h, LANE), lambda i, h: (i, 0))],
                 [((H, T, K_FULL), BF16, (None, tm_h, K_FULL), lambda i, h: (h, i, 0))])[0]

    def make_kv(kv, kr, c, s):
        return jnp.concatenate([kv[:, :LANE], kr[:, :LANE] * c + kr[:, LANE:] * s], axis=1), kv[:, LANE:]

    k_full, v_h = _ew("make_kv", make_kv, (T // tm_h, H),
                      [(kv, (tm_h, K_FULL), lambda i, h: (i, h)),
                       (proj_a, (tm_h, K_FULL), lambda i, h: (i, C0 // K_FULL)),
                       (cosp, (tm_h, LANE), lambda i, h: (i, 0)), (sinp, (tm_h, LANE), lambda i, h: (i, 0))],
                      [((H, T, K_FULL), BF16, (None, tm_h, K_FULL), lambda i, h: (h, i, 0)),
                       ((H, T, V_DIM), BF16, (None, tm_h, V_DIM), lambda i, h: (h, i, 0))])

    a_out = _attn_fwd(q_full, k_full, v_h)
    st_o, tok_o = _ag_mid("ag_o", st_o, a_out)
    st_gu, tok_gu = _ag_mid("ag_gu", st_gu, tok_o)
    tm_a = _row_tile(T, AW)
    mixed_a = _ew("rms_attn", lambda x, g: _rms(x, g), (T // tm_a,), [_rows(a_out, tm_a), _vec(attn_out_norm)],
                  [_rows_out((T, AW), BF16, tm_a)], deps=[tok_gu])[0]
    mixed_g = _sgu_fwd(proj_b, v_ln_gain, v_ln_bias, ws, bsb, gmlp_out_norm)
    wo = _ag_finish("ag_o", st_o, mixed_g)[0].reshape(D, D)

    tm_, tn_, tk_ = _pick(T, 1024, LANE), _pick(D, 1024, LANE), _pick(AW, 512, LANE)
    nka = AW // tk_
    mix_out = _mm("mm_out", "nn",
                  [(mixed_a, (tm_, tk_), lambda i, j, k: (i, k), wo, (tk_, tn_), lambda i, j, k: (k, j)),
                   (mixed_g, (tm_, tk_), lambda i, j, k: (i, k), wo, (tk_, tn_), lambda i, j, k: (k + nka, j))],
                  (T // tm_, D // tn_, nka), (tm_, tn_), (T, D), F32, (tm_, tn_), lambda i, j, k: (i, j))

    def post_mix(x, mo, g3, g4):
        h = x + _rms(mo, g3)
        return h, _rms(h, g4)

    h_res, hn = _ew("post_mix", post_mix, (T // tm_d,),
                    [_rows(x2, tm_d), _rows(mix_out, tm_d), _vec(post_mix_norm), _vec(pre_ffn_norm)],
                    [_rows_out((T, D), F32, tm_d), _rows_out((T, D), BF16, tm_d)])

    tk_d = _pick(D, 512, LANE)
    tn_d = _pick(D, 1024, LANE)

    def ffn_in(name, w, deps=()):
        return _mm(name, "nn", [(hn, (tm_, tk_d), lambda j, i, k: (i, k), w, (None, tk_d, FP), lambda j, i, k: (j, k, 0))],
                   (N_DEV, T // tm_, D // tk_d), (tm_, FP), (N_DEV, T, FP), BF16, (None, tm_, FP), lambda j, i, k: (j, i, 0),
                   deps=deps)

    wg, wu = _ag_finish("ag_gu", st_gu, hn)
    gate = ffn_in("mm_gate", wg)
    st_d, tok_d = _ag_mid("ag_d", st_d, gate)
    up = ffn_in("mm_up", wu, deps=[tok_d])
    tm_f = _pick(T, 256, SUBLANE)
    blk3 = lambda a: (a, (None, tm_f, FP), lambda j, i: (j, i, 0))
    out3 = ((N_DEV, T, FP), BF16, (None, tm_f, FP), lambda j, i: (j, i, 0))

    def swiglu(g, u):
        g, u = g.astype(F32), u.astype(F32)
        return g * jax.nn.sigmoid(g) * u

    act = _ew("swiglu", swiglu, (N_DEV, T // tm_f), [blk3(gate), blk3(up)], [out3])[0]
    wd = _ag_finish("ag_d", st_d, act)[0]
    ffn = _mm("mm_down", "nn",
              [(act, (None, tm_, FP), lambda i, j, k: (k, i, 0), wd, (None, FP, tn_d), lambda i, j, k: (k, 0, j))],
              (T // tm_, D // tn_d, N_DEV), (tm_, tn_d), (T, D), F32, (tm_, tn_d), lambda i, j, k: (i, j))

    def loss_head(h, f, t, g5):
        y = h + _rms(f, g5)
        diff = y - t
        dy = diff / D
        d_f, dg5 = _rms_bwd(f, g5, dy)
        lsum = jnp.sum(jnp.sum(diff * diff, axis=0, keepdims=True), axis=1, keepdims=True) * (0.5 / D)
        return dy, d_f, dg5, lsum

    dy, d_ffn, dg_post_ffn, loss_part = _ew(
        "loss_head", loss_head, (T // tm_d,),
        [_rows(h_res, tm_d), _rows(ffn, tm_d), _rows(tgt, tm_d), _vec(post_ffn_norm)],
        [_rows_out((T, D), F32, tm_d), _rows_out((T, D), BF16, tm_d)], [_vec_out((1, D)), _vec_out((1, 1))])

    d_act = _mm("mm_dact", "nt",
                [(d_ffn, (tm_, tk_d), lambda j, i, k: (i, k), wd, (None, FP, tk_d), lambda j, i, k: (j, 0, k))],
                (N_DEV, T // tm_, D // tk_d), (tm_, FP), (N_DEV, T, FP), BF16, (None, tm_, FP), lambda j, i, k: (j, i, 0))

    def swiglu_bwd(da, g, u):
        da, g, u = da.astype(F32), g.astype(F32), u.astype(F32)
        sg = jax.nn.sigmoid(g)
        return da * u * sg * (1.0 + g * (1.0 - sg)), da * g * sg

    d_gate, d_up = _ew("swiglu_bwd", swiglu_bwd, (N_DEV, T // tm_f), [blk3(d_act), blk3(gate), blk3(up)], [out3, out3])

    tk_t = _pick(T, 512, LANE)
    dwd = _mm("mm_dwd", "tn",
              [(act, (None, tk_t, FP), lambda j, n, k: (j, k, 0), d_ffn, (tk_t, tn_d), lambda j, n, k: (k, n))],
              (N_DEV, D // tn_d, T // tk_t), (FP, tn_d), (N_DEV, FP, D), BF16, (None, FP, tn_d), lambda j, n, k: (j, 0, n))

    def ffn_dw(name, dz, deps):
        return _mm(name, "tn",
                   [(hn, (tk_t, tn_d), lambda j, i, k: (k, i), dz, (None, tk_t, FP), lambda j, i, k: (j, k, 0))],
                   (N_DEV, D // tn_d, T // tk_t), (tn_d, FP), (N_DEV, D, FP), BF16, (None, tn_d, FP),
                   lambda j, i, k: (j, i, 0), deps=deps)

    x_d, tok = _a2a_start("a2a_d", [dwd], [])
    dwg = ffn_dw("mm_dwg", d_gate, [tok])
    x_g, tok = _a2a_start("a2a_g", [dwg], [])
    dwu = ffn_dw("mm_dwu", d_up, [tok])
    x_u, tok = _a2a_start("a2a_u", [dwu], [])
    d_hn = _mm("mm_dhn", "nt",
               [(d_gate, (None, tm_, FP), lambda i, j, k: (k, i, 0), wg, (None, tn_d, FP), lambda i, j, k: (k, j, 0)),
                (d_up, (None, tm_, FP), lambda i, j, k: (k, i, 0), wu, (None, tn_d, FP), lambda i, j, k: (k, j, 0))],
               (T // tm_, D // tn_d, N_DEV), (tm_, tn_d), (T, D), F32, (tm_, tn_d), lambda i, j, k: (i, j), deps=[tok])

    def pre_ffn_bwd(h, dhn, dy, mo, g4, g3):
        d1, dg4 = _rms_bwd(h, g4, dhn)
        dh = dy + d1
        dmo, dg3 = _rms_bwd(mo, g3, dh)
        return dh, dmo, dg4, dg3

    dh, d_mo, dg_pre_ffn, dg_post_mix = _ew(
        "pre_ffn_bwd", pre_ffn_bwd, (T // tm_d,),
        [_rows(h_res, tm_d), _rows(d_hn, tm_d), _rows(dy, tm_d), _rows(mix_out, tm_d), _vec(pre_ffn_norm),
         _vec(post_mix_norm)],
        [_rows_out((T, D), F32, tm_d), _rows_out((T, D), BF16, tm_d)], [_vec_out((1, D)), _vec_out((1, D))])

    d_mixed = _mm2("mm_dmixed", "nt", d_mo, wo, F32)
    dwo = jnp.concatenate([_mm2("mm_dwo_a", "tn", mixed_a, d_mo, BF16), _mm2("mm_dwo_g", "tn", mixed_g, d_mo, BF16)],
                          axis=0)
    x_o, tok = _a2a_start("a2a_o", [dwo.reshape(N_DEV, D // N_DEV, D)], [])

    d_a_out, dg_attn = _ew("rms_attn_bwd", lambda a, g, d: _rms_bwd(a, g, d), (T // tm_a,),
                           [_rows(a_out, tm_a), _vec(attn_out_norm), _rows(d_mixed, tm_a, AW, 0)],
                           [_rows_out((T, AW), BF16, tm_a)], [_vec_out((1, AW))], deps=[tok])
    d_pb, dg_gmlp, dws, dbs, dlng, dlnb = _sgu_bwd(proj_b, d_mixed, v_ln_gain, v_ln_bias, ws, ws_t, bsb, gmlp_out_norm)

    dq_full, dk_full, dv_h = _attn_bwd(q_full, k_full, v_h, d_a_out)

    def rope_q_bwd(d, c, s):
        r = d[:, LANE:]
        return jnp.concatenate([d[:, :LANE], r * c, r * s], axis=1) * scale

    dq_ext = _ew("rope_q_bwd", rope_q_bwd, (T // tm_h, H),
                 [(dq_full, (None, tm_h, K_FULL), lambda i, h: (h, i, 0)), (cosp, (tm_h, LANE), lambda i, h: (i, 0)),
                  (sinp, (tm_h, LANE), lambda i, h: (i, 0))],
                 [((T, H * Q_EXT), BF16, (tm_h, Q_EXT), lambda i, h: (i, h))])[0]

    def make_kv_bwd(dk, dv, c, s):
        r = dk[:, LANE:]
        return jnp.concatenate([dk[:, :LANE], dv], axis=1), jnp.concatenate([r * c, r * s], axis=1)

    d_kv, d_kr = _ew("make_kv_bwd", make_kv_bwd, (T // tm_h, H),
                     [(dk_full, (None, tm_h, K_FULL), lambda i, h: (h, i, 0)),
                      (dv_h, (None, tm_h, V_DIM), lambda i, h: (h, i, 0)),
                      (cosp, (tm_h, LANE), lambda i, h: (i, 0)), (sinp, (tm_h, LANE), lambda i, h: (i, 0))],
                     [((T, H * K_FULL), BF16, (tm_h, K_FULL), lambda i, h: (i, h))],
                     [((T, K_FULL), F32, (tm_h, K_FULL), lambda i, h: (i, 0))])

    d_qn = _mm2("mm_dqn", "nt", dq_ext, wq_ext, F32)
    dwq_ext = _mm2("mm_dwq", "tn", qn, dq_ext, F32)
    d_kvn = _mm2("mm_dkvn", "nt", d_kv, wkv, F32)
    dwkv = _mm2("mm_dwkv", "tn", kvn, d_kv, BF16)

    d_qc, dg_q = _ew("rms_q_bwd", lambda a, g, d: _rms_bwd(a, g, d), (T // tm_q,),
                     [_rows(proj_a, tm_q, QL, 0), _vec(q_norm), _rows(d_qn, tm_q)],
                     [_rows_out((T, QL), BF16, tm_q)], [_vec_out((1, QL))])
    d_kvc, dg_kv = _ew("rms_kv_bwd", lambda a, g, d: _rms_bwd(a, g, d), (T // tm_q,),
                       [_rows(proj_a, tm_q, KVL, QL // KVL), _vec(kv_norm), _rows(d_kvn, tm_q)],
                       [_rows_out((T, KVL), BF16, tm_q)], [_vec_out((1, KVL))])
    d_pa = jnp.concatenate([d_qc, d_kvc, d_kr.astype(BF16)], axis=1)

    dwa = _mm2("mm_dwa", "tn", xn, d_pa, F32, tn_pref=w_a.shape[1])
    dwb = _mm2("mm_dwb", "tn", xn, d_pb, BF16)
    dkr_w = dwa[:, C0:C0 + ROPE_DIM] + _rot_t(dwa[:, C0 + LANE:C0 + LANE + ROPE_DIM])
    dw_in = jnp.concatenate([dwa[:, :C0].astype(BF16), dkr_w.astype(BF16), dwb], axis=1)
    dw_in = dw_in.reshape(D, N_DEV, IWS).transpose(1, 0, 2)
    dq3 = dwq_ext.reshape(QL, H, Q_EXT)
    dwq = jnp.concatenate([dq3[..., :NOPE_DIM],
                           dq3[..., LANE:LANE + ROPE_DIM] + _rot_t(dq3[..., 2 * LANE:2 * LANE + ROPE_DIM])], axis=-1)
    dwq = dwq.astype(BF16).reshape(QL, N_DEV, H * QK_HEAD // N_DEV).transpose(1, 0, 2)
    dwkv = dwkv.reshape(KVL, N_DEV, w_ukv.shape[2]).transpose(1, 0, 2)
    x_e, tok = _a2a_start("a2a_e", [dw_in, dwq, dwkv], [])

    d_xn_a = _mm2("mm_dxn_a", "nt", d_pa, w_a, F32, tk_pref=w_a.shape[1], deps=[tok])
    d_xn_b = _mm2("mm_dxn_b", "nt", d_pb, w_b, F32)

    def x_bwd(x, da, db, dh, g0):
        dx, dg0 = _rms_bwd(x, g0, da + db)
        return dh + dx, dg0

    grad_x, dg_pre_mix = _ew("x_bwd", x_bwd, (T // tm_d,),
                             [_rows(x2, tm_d), _rows(d_xn_a, tm_d), _rows(d_xn_b, tm_d), _rows(dh, tm_d),
                              _vec(pre_mix_norm)],
                             [_rows_out((T, D), F32, tm_d)], [_vec_out((1, D))])

    def adam_big(n, parts):
        w3 = given[n]
        shp2 = w3.shape[1:]
        res = _adamw("adamw_" + n, parts, w3.reshape(shp2), given["m_" + n].reshape(shp2), given["v_" + n].reshape(shp2))
        return [r.reshape(w3.shape) for r in res]

    results = {}
    results["w_down"] = adam_big("w_down", _a2a_finish("a2a_d", x_d, grad_x)[0][:, :FS, :])
    results["w_gate"] = adam_big("w_gate", _a2a_finish("a2a_g", x_g, results["w_down"][0])[0][:, :, :FS])
    results["w_up"] = adam_big("w_up", _a2a_finish("a2a_u", x_u, results["w_gate"][0])[0][:, :, :FS])
    results["w_out"] = adam_big("w_out", _a2a_finish("a2a_o", x_o, results["w_up"][0])[0])
    e_in, e_uq, e_ukv = _a2a_finish("a2a_e", x_e, results["w_out"][0])
    results["w_in"] = adam_big("w_in", e_in)
    results["w_uq"] = adam_big("w_uq", e_uq)
    results["w_ukv"] = adam_big("w_ukv", e_ukv)

    small_names = ["pre_mix_norm", "q_norm", "kv_norm", "v_ln_gain", "v_ln_bias", "w_spatial", "b_spatial",
                   "attn_out_norm", "gmlp_out_norm", "post_mix_norm", "pre_ffn_norm", "post_ffn_norm"]
    small_grads = [dg_pre_mix, dg_q, dg_kv, dlng, dlnb, dws, dbs[:, 0, :], dg_attn, dg_gmlp, dg_post_mix, dg_pre_ffn,
                   dg_post_ffn]
    n_small = sum(g.size for g in small_grads)
    pad_small = _round_up(n_small, SUBLANE * LANE) - n_small

    def pack(arrs):
        flat = jnp.concatenate([a.reshape(-1) for a in arrs] + [jnp.zeros((pad_small,), F32)])
        return flat.reshape(-1, LANE)

    small_parts = _all_gather("ag_small_grads", pack(small_grads))
    sm = _adamw("adamw_small", small_parts, pack([given[n] for n in small_names]),
                pack([given["m_" + n] for n in small_names]), pack([given["v_" + n] for n in small_names]))

    off = 0
    for n in small_names:
        shp, size = given[n].shape, given[n].size
        results[n] = [r.reshape(-1)[off:off + size].reshape(shp) for r in sm]
        off += size

    loss = lax.psum(loss_part[0, 0], MESH_AXES)
    order = ["pre_mix_norm", "w_in", "q_norm", "kv_norm", "w_uq", "w_ukv", "v_ln_gain", "v_ln_bias", "w_spatial",
             "b_spatial", "attn_out_norm", "gmlp_out_norm", "w_out", "post_mix_norm", "pre_ffn_norm", "w_gate", "w_up",
             "w_down", "post_ffn_norm"]
    out = [loss, grad_x[None]]
    for kind in range(4):
        out += [results[n][kind] for n in order]
    return tuple(out)
```

```python
import math

import jax
import jax.numpy as jnp
from jax import lax
from jax.experimental import pallas as pl
from jax.experimental.pallas import tpu as pltpu

F32 = jnp.float32
BF16 = jnp.bfloat16
N_DEV = 8
MESH_AXES = ("x", "y", "c")
MESH = pl.DeviceIdType.MESH

NOPE_DIM = 128
ROPE_DIM = 64
V_DIM = 128
ROPE_THETA = 10000.0
CHUNK = 128
EPS = 1e-6
QK_HEAD = NOPE_DIM + ROPE_DIM
Q_EXT = 3 * 128
K_FULL = 2 * 128

ADAM_LR = 0.001
ADAM_B1 = 0.9
ADAM_B2 = 0.999
ADAM_EPS = 1e-08
ADAM_WD = 0.01
ADAM_STEP = 10

LANE = 128
SUBLANE = 8
VMEM_LIMIT = 56 * 1024 * 1024
MM_VMEM_BUDGET = 40 * 1024 * 1024
MM_TK_MAX = 2048
EPILOGUE_ROWS = 256
ATTN_TQ = 512
ATTN_TKV = 512
NEG_BIG = -1e30
ROW_BLOCK_ELEMS = 512 * 1024
ADAM_BLOCK_ELEMS = 256 * 1024

DN = {
    "nn": (((1,), (0,)), ((), ())),
    "nt": (((1,), (1,)), ((), ())),
    "tn": (((0,), (0,)), ((), ())),
}


def _pick(dim, pref, mult):
    t = min(pref, dim)
    t -= t % mult
    while t >= mult:
        if dim % t == 0:
            return t
        t -= mult
    return dim


def _round_up(n, m):
    return (n + m - 1) // m * m


def _params(sem):
    return pltpu.CompilerParams(dimension_semantics=sem, vmem_limit_bytes=VMEM_LIMIT)


def _dep_specs(deps, nd):
    return [pl.BlockSpec(d.shape, lambda *_, r=d.ndim: (0,) * r) for d in deps]


def _mm(name, mode, pairs, grid, acc_shape, out_shape, out_dtype, out_block, out_map, deps=(), extras=(),
        epilogue=None, n_out=1):
    n, ne, nx = len(pairs), len(extras), len(deps)
    nk = grid[-1]
    nd = len(grid)
    dn = DN[mode]
    use_acc = nk > 1 or epilogue is not None

    def body(*refs):
        e_refs = refs[2 * n:2 * n + ne]
        o_refs = refs[2 * n + ne + nx:2 * n + ne + nx + n_out]
        k = pl.program_id(nd - 1)

        def dot(i):
            return lax.dot_general(refs[2 * i][...], refs[2 * i + 1][...], dn, preferred_element_type=F32)

        if not use_acc:
            s = dot(0)
            for i in range(1, n):
                s = s + dot(i)
            o_refs[0][...] = s.astype(o_refs[0].dtype)
            return
        acc = refs[2 * n + ne + nx + n_out]

        @pl.when(k == 0)
        def _():
            acc[...] = dot(0)

        if nk > 1:
            @pl.when(k > 0)
            def _():
                acc[...] += dot(0)

        for i in range(1, n):
            acc[...] += dot(i)

        if epilogue is None:
            @pl.when(k == nk - 1)
            def _():
                o_refs[0][...] = acc[...].astype(o_refs[0].dtype)
        else:
            @pl.when(k == nk - 1)
            def _():
                rows = acc_shape[0]
                slab = _pick(rows, EPILOGUE_ROWS, SUBLANE)
                for r0 in range(0, rows, slab):
                    sl = pl.ds(r0, slab)
                    vals = epilogue(acc[sl, :], *[e[sl, :] for e in e_refs])
                    for r, v in zip(o_refs, vals):
                        r[sl, :] = v.astype(r.dtype)

    in_specs, args = [], []
    for a, a_blk, a_map, b, b_blk, b_map in pairs:
        in_specs += [pl.BlockSpec(a_blk, a_map), pl.BlockSpec(b_blk, b_map)]
        args += [a, b]
    in_specs += [pl.BlockSpec(blk, imap) for (_, blk, imap) in extras]
    args += [e for (e, _, _) in extras]
    if not isinstance(out_dtype, (tuple, list)):
        out_dtype = (out_dtype,) * n_out
    res = pl.pallas_call(
        body,
        name=name,
        grid=grid,
        in_specs=in_specs + _dep_specs(deps, nd),
        out_specs=[pl.BlockSpec(out_block, out_map)] * n_out,
        out_shape=[jax.ShapeDtypeStruct(out_shape, dt) for dt in out_dtype],
        scratch_shapes=[pltpu.VMEM(acc_shape, F32)] if use_acc else [],
        compiler_params=_params(("parallel",) * (nd - 1) + ("arbitrary",)),
    )(*args, *deps)
    return res[0] if n_out == 1 else res


def _fit_tk(K, tm, tn, pairs=1, out_bytes=4, fixed_extra=0):
    fixed = tm * tn * 4 + 2 * tm * tn * out_bytes + fixed_extra
    tk = _pick(K, MM_TK_MAX, LANE)
    while tk > LANE and fixed + pairs * 4 * (tm + tn) * tk > MM_VMEM_BUDGET:
        smaller = _pick(K, tk - LANE, LANE)
        if smaller >= tk:
            break
        tk = smaller
    return tk


def _mm2(name, mode, a, b, out_dtype, tm_pref=1024, tn_pref=1024, deps=()):
    if mode == "nn":
        (M, K), N = a.shape, b.shape[1]
    elif mode == "nt":
        (M, K), N = a.shape, b.shape[0]
    else:
        (K, M), N = a.shape, b.shape[1]
    tm, tn = _pick(M, tm_pref, LANE), _pick(N, tn_pref, LANE)
    tk = _fit_tk(K, tm, tn, out_bytes=jnp.dtype(out_dtype).itemsize)
    if mode == "tn":
        a_blk, a_map = (tk, tm), lambda i, j, k: (k, i)
    else:
        a_blk, a_map = (tm, tk), lambda i, j, k: (i, k)
    if mode == "nt":
        b_blk, b_map = (tn, tk), lambda i, j, k: (j, k)
    else:
        b_blk, b_map = (tk, tn), lambda i, j, k: (k, j)
    return _mm(name, mode, [(a, a_blk, a_map, b, b_blk, b_map)], (M // tm, N // tn, K // tk), (tm, tn),
               (M, N), out_dtype, (tm, tn), lambda i, j, k: (i, j), deps=deps)


def _ew(name, fn, grid, ins, outs, accs=(), deps=()):
    ni, no, na = len(ins), len(outs), len(accs)
    nd = len(grid)
    nx = len(deps)

    def body(*refs):
        vals = fn(*[r[...] for r in refs[:ni]])
        if not isinstance(vals, (tuple, list)):
            vals = (vals,)
        for r, v in zip(refs[ni + nx:ni + nx + no], vals[:no]):
            r[...] = v.astype(r.dtype)
        if na:
            first = pl.program_id(nd - 1) == 0
            for r, v in zip(refs[ni + nx + no:], vals[no:]):
                @pl.when(first)
                def _(r=r, v=v):
                    r[...] = v.astype(r.dtype)

                @pl.when(jnp.logical_not(first))
                def _(r=r, v=v):
                    r[...] += v.astype(r.dtype)

    all_outs = list(outs) + list(accs)
    sem = ("parallel",) * (nd - 1) + (("arbitrary",) if na else ("parallel",))
    res = pl.pallas_call(
        body,
        name=name,
        grid=grid,
        in_specs=[pl.BlockSpec(blk, imap) for (_, blk, imap) in ins] + _dep_specs(deps, nd),
        out_specs=[pl.BlockSpec(blk, imap) for (_, _, blk, imap) in all_outs],
        out_shape=[jax.ShapeDtypeStruct(s, d) for (s, d, _, _) in all_outs],
        compiler_params=_params(sem),
    )(*[a for (a, _, _) in ins], *deps)
    return res


def _rows(a, tm, width=None, cblk=0):
    width = a.shape[1] if width is None else width
    return (a, (tm, width), lambda i, c=cblk: (i, c))


def _vec(p):
    return (p, p.shape, lambda i: (0,) * p.ndim)


def _rows_out(shape, dtype, tm):
    return (shape, dtype, (tm, shape[1]), lambda i: (i, 0))


def _vec_out(shape):
    return (shape, F32, shape, lambda i: (0,) * len(shape))


def _row_tile(T, D):
    return _pick(T, max(SUBLANE, ROW_BLOCK_ELEMS // D), SUBLANE)


def _rms(x, g):
    r = lax.rsqrt(jnp.mean(x * x, axis=-1, keepdims=True) + EPS)
    return x * r * g


def _rms_bwd(x, g, dy):
    r = lax.rsqrt(jnp.mean(x * x, axis=-1, keepdims=True) + EPS)
    xh = x * r
    dg = jnp.sum(dy * xh, axis=0, keepdims=True)
    dxh = dy * g
    dx = r * (dxh - xh * jnp.mean(dxh * xh, axis=-1, keepdims=True))
    return dx, dg


GELU_C = math.sqrt(2.0 / math.pi)
GELU_A = 0.044715


def _gelu(x):
    return 0.5 * x * (1.0 + jnp.tanh(GELU_C * (x + GELU_A * x * x * x)))


def _gelu_grad(x):
    t = jnp.tanh(GELU_C * (x + GELU_A * x * x * x))
    return 0.5 * (1.0 + t) + 0.5 * x * (1.0 - t * t) * GELU_C * (1.0 + 3.0 * GELU_A * x * x)


def _rot(w):
    h = w.shape[-1] // 2
    return jnp.concatenate([-w[..., h:], w[..., :h]], axis=-1)


def _rot_t(dw):
    h = dw.shape[-1] // 2
    return jnp.concatenate([dw[..., h:], -dw[..., :h]], axis=-1)


def _coords():
    return lax.axis_index("x"), lax.axis_index("y"), lax.axis_index("c")


def _all_gather(name, shard):
    def body(x_ref, o_ref, send_sems, recv_sems, local_sem):
        xi, yi, ci = _coords()
        me, sibling = (xi, yi, ci), (xi, yi, 1 - ci)
        chips = [(1 - xi, yi), (xi, 1 - yi), (1 - xi, 1 - yi)]

        def slot(px, py, pc):
            return o_ref.at[4 * px + 2 * py + pc]

        def copy(k, block, to, src=None):
            return pltpu.make_async_remote_copy(
                src_ref=slot(*block) if src is None else src, dst_ref=slot(*block),
                send_sem=send_sems.at[k], recv_sem=recv_sems.at[k], device_id=to, device_id_type=MESH)

        mine = pltpu.make_async_copy(x_ref, slot(*me), local_sem)
        mine.start()
        first = [copy(0, me, sibling, src=x_ref)]
        first += [copy(1 + j, me, (*chip, ci), src=x_ref) for j, chip in enumerate(chips)]
        for cp in first:
            cp.start()
        passed = [copy(4 + j, (*chip, ci), sibling) for j, chip in enumerate(chips)]
        for j, chip in enumerate(chips):
            copy(1 + j, (*chip, ci), me).wait_recv()
            passed[j].start()
        copy(0, sibling, me).wait_recv()
        for j, chip in enumerate(chips):
            copy(4 + j, (*chip, 1 - ci), me).wait_recv()
        for cp in first + passed:
            cp.wait_send()
        mine.wait()

    return pl.pallas_call(
        body,
        name=name,
        out_shape=jax.ShapeDtypeStruct((N_DEV,) + shard.shape, shard.dtype),
        in_specs=[pl.BlockSpec(memory_space=pltpu.HBM)],
        out_specs=pl.BlockSpec(memory_space=pltpu.HBM),
        scratch_shapes=[pltpu.SemaphoreType.DMA((N_DEV - 1,)), pltpu.SemaphoreType.DMA((N_DEV - 1,)),
                        pltpu.SemaphoreType.DMA],
    )(shard)


HBM_SPEC = pl.BlockSpec(memory_space=pltpu.HBM)
SEM_SPEC = pl.BlockSpec(memory_space=pltpu.SEMAPHORE)
ANY_SPEC = pl.BlockSpec(memory_space=pl.ANY)
EFFECT = pltpu.SideEffectType.DATAFLOW_SIDE_EFFECTING
TOKEN = (SUBLANE, LANE)


def _my_slot():
    xi, yi, ci = _coords()
    return 4 * xi + 2 * yi + ci


def _own_slot_filled(block):
    land = lax.empty((N_DEV,) + block.shape, block.dtype)
    return lax.dynamic_update_slice(land, block[None], (_my_slot(),) + (0,) * block.ndim)


def _split_start(name, n, n_copies, issue, srcs, lands, deps):
    def body(*refs):
        src_refs, land_refs = refs[:n], refs[n:2 * n]
        outs = refs[2 * n + len(deps):]
        send, recv, token = outs[:n], outs[n:2 * n], outs[4 * n]
        for t in range(n):
            issue(t, src_refs[t], land_refs[t], send[t], recv[t])
        token[...] = jnp.zeros_like(token)

    sems = [pltpu.SemaphoreType.DMA((n_copies,))] * (2 * n)
    res = pl.pallas_call(
        body,
        name=name,
        in_specs=[HBM_SPEC] * (2 * n) + [ANY_SPEC] * len(deps),
        out_specs=[SEM_SPEC] * (2 * n) + [HBM_SPEC] * (2 * n) + [pl.BlockSpec(memory_space=pltpu.VMEM)],
        out_shape=sems + [pltpu.HBM(a.shape, a.dtype) for a in list(srcs) + list(lands)]
        + [jax.ShapeDtypeStruct(TOKEN, F32)],
        input_output_aliases={i: 2 * n + i for i in range(2 * n)},
        compiler_params=pltpu.CompilerParams(has_side_effects=EFFECT),
    )(*[pltpu.with_memory_space_constraint(a, pltpu.HBM) for a in list(srcs) + list(lands)], *deps)
    state = dict(n=n, n_copies=n_copies, send=res[:n], recv=res[n:2 * n], srcs=res[2 * n:3 * n],
                 lands=res[3 * n:4 * n])
    return state, res[4 * n]


def _split_wait(name, state, unit_src, after):
    n, n_copies = state["n"], state["n_copies"]

    def body(*refs):
        src_refs, land_refs = refs[:n], refs[n:2 * n]
        send, recv = refs[2 * n:3 * n], refs[3 * n:4 * n]
        me = _coords()
        for t in range(n):
            for k in range(n_copies):
                cp = pltpu.make_async_remote_copy(
                    src_ref=unit_src(src_refs[t]), dst_ref=land_refs[t].at[0], send_sem=send[t].at[k],
                    recv_sem=recv[t].at[k], device_id=me, device_id_type=MESH)
                cp.wait_send()
                cp.wait_recv()

    bufs = list(state["srcs"]) + list(state["lands"])
    res = pl.pallas_call(
        body,
        name=name,
        in_specs=[HBM_SPEC] * (2 * n) + [SEM_SPEC] * (2 * n) + [ANY_SPEC],
        out_specs=[HBM_SPEC] * (2 * n),
        out_shape=[pltpu.HBM(a.shape, a.dtype) for a in bufs],
        input_output_aliases={i: i for i in range(2 * n)},
        compiler_params=pltpu.CompilerParams(has_side_effects=EFFECT),
    )(*bufs, *state["send"], *state["recv"], after)
    return list(res[n:])


def _ag_start(name, shards, deps):
    def issue(t, x_ref, land_ref, send, recv):
        xi, yi, ci = _coords()
        to = [(xi, yi, 1 - ci), (1 - xi, yi, ci), (xi, 1 - yi, ci), (1 - xi, 1 - yi, ci)]
        for k, dev in enumerate(to):
            pltpu.make_async_remote_copy(src_ref=x_ref, dst_ref=land_ref.at[4 * xi + 2 * yi + ci], send_sem=send.at[k],
                                         recv_sem=recv.at[k], device_id=dev, device_id_type=MESH).start()

    return _split_start(name, len(shards), 4, issue, shards, [_own_slot_filled(s) for s in shards], deps)


def _ag_mid(name, state, after):
    lands = _split_wait(name + "_w1", state, lambda r: r, after)

    def issue(t, src_ref, land_ref, send, recv):
        xi, yi, ci = _coords()
        for k, (px, py) in enumerate([(1 - xi, yi), (xi, 1 - yi), (1 - xi, 1 - yi)]):
            rows = land_ref.at[4 * px + 2 * py + ci]
            pltpu.make_async_remote_copy(src_ref=rows, dst_ref=rows, send_sem=send.at[k], recv_sem=recv.at[k],
                                         device_id=(xi, yi, 1 - ci), device_id_type=MESH).start()

    n = len(lands)

    def body(*refs):
        land_refs = refs[:n]
        outs = refs[n:]
        send, recv, token = outs[:n], outs[n:2 * n], outs[3 * n]
        for t in range(n):
            issue(t, land_refs[t], land_refs[t], send[t], recv[t])
        token[...] = jnp.zeros_like(token)

    res = pl.pallas_call(
        body,
        name=name + "_s2",
        in_specs=[HBM_SPEC] * n,
        out_specs=[SEM_SPEC] * (2 * n) + [HBM_SPEC] * n + [pl.BlockSpec(memory_space=pltpu.VMEM)],
        out_shape=[pltpu.SemaphoreType.DMA((3,))] * (2 * n) + [pltpu.HBM(a.shape, a.dtype) for a in lands]
        + [jax.ShapeDtypeStruct(TOKEN, F32)],
        input_output_aliases={i: 2 * n + i for i in range(n)},
        compiler_params=pltpu.CompilerParams(has_side_effects=EFFECT),
    )(*lands)
    state2 = dict(n=n, send=res[:n], recv=res[n:2 * n], lands=res[2 * n:3 * n])
    return state2, res[3 * n]


def _ag_finish(name, state2, after):
    n = state2["n"]

    def body(*refs):
        land_refs, send, recv = refs[:n], refs[n:2 * n], refs[2 * n:3 * n]
        me = _coords()
        for t in range(n):
            for k in range(3):
                cp = pltpu.make_async_remote_copy(
                    src_ref=land_refs[t].at[0], dst_ref=land_refs[t].at[0], send_sem=send[t].at[k],
                    recv_sem=recv[t].at[k], device_id=me, device_id_type=MESH)
                cp.wait_send()
                cp.wait_recv()

    res = pl.pallas_call(
        body,
        name=name + "_w2",
        in_specs=[HBM_SPEC] * n + [SEM_SPEC] * (2 * n) + [ANY_SPEC],
        out_specs=[HBM_SPEC] * n,
        out_shape=[pltpu.HBM(a.shape, a.dtype) for a in state2["lands"]],
        input_output_aliases={i: i for i in range(n)},
        compiler_params=pltpu.CompilerParams(has_side_effects=EFFECT),
    )(*state2["lands"], *state2["send"], *state2["recv"], after)
    return list(res)


def _a2a_start(name, parts, deps):
    def issue(t, x_ref, land_ref, send, recv):
        xi, yi, ci = _coords()
        for k in range(N_DEV - 1):
            px = 1 - xi if (k + 1) & 4 else xi
            py = 1 - yi if (k + 1) & 2 else yi
            pc = 1 - ci if (k + 1) & 1 else ci
            pltpu.make_async_remote_copy(src_ref=x_ref.at[4 * px + 2 * py + pc], dst_ref=land_ref.at[4 * xi + 2 * yi + ci],
                                         send_sem=send.at[k], recv_sem=recv.at[k], device_id=(px, py, pc),
                                         device_id_type=MESH).start()

    lands = [_own_slot_filled(lax.dynamic_index_in_dim(p, _my_slot(), 0, keepdims=False)) for p in parts]
    return _split_start(name, len(parts), N_DEV - 1, issue, parts, lands, deps)


def _a2a_finish(name, state, after):
    return _split_wait(name + "_w", state, lambda r: r.at[0], after)


def _attn_fwd(q_full, k_full, v_h):
    H, T, _ = q_full.shape
    tq, tkv = _pick(T, ATTN_TQ, LANE), _pick(T, ATTN_TKV, LANE)

    def body(q_ref, k_ref, v_ref, o_ref, lse_ref):
        q = q_ref[...]
        m = jnp.full((tq, 1), NEG_BIG, F32)
        l = jnp.zeros((tq, 1), F32)
        acc = jnp.zeros((tq, V_DIM), F32)
        for c in range(T // tkv):
            keys = pl.ds(c * tkv, tkv)
            s = lax.dot_general(q, k_ref[keys, :], DN["nt"], preferred_element_type=F32)
            m_new = jnp.maximum(m, jnp.max(s, axis=-1, keepdims=True))
            alpha = jnp.exp(m - m_new)
            p = jnp.exp(s - m_new)
            l = alpha * l + jnp.sum(p, axis=-1, keepdims=True)
            acc = alpha * acc + jnp.dot(p.astype(BF16), v_ref[keys, :], preferred_element_type=F32)
            m = m_new
        o_ref[...] = acc / l
        lse_ref[...] = m + jnp.log(l)

    return pl.pallas_call(
        body,
        name="attn_fwd",
        grid=(H, T // tq),
        in_specs=[pl.BlockSpec((None, tq, K_FULL), lambda h, i: (h, i, 0)),
                  pl.BlockSpec((None, T, K_FULL), lambda h, i: (h, 0, 0)),
                  pl.BlockSpec((None, T, V_DIM), lambda h, i: (h, 0, 0))],
        out_specs=[pl.BlockSpec((tq, V_DIM), lambda h, i: (i, h)),
                   pl.BlockSpec((None, tq, 1), lambda h, i: (h, i, 0))],
        out_shape=[jax.ShapeDtypeStruct((T, H * V_DIM), F32), jax.ShapeDtypeStruct((H, T, 1), F32)],
        compiler_params=_params(("parallel", "parallel")),
    )(q_full, k_full, v_h)


def _attn_bwd(q_full, k_full, v_h, d_out, out, lse_rows):
    H, T, _ = q_full.shape
    tq, tkv = _pick(T, ATTN_TQ, LANE), _pick(T, ATTN_TKV, LANE)

    def body(q_ref, k_ref, v_ref, do_ref, o_ref, lse_ref, dq_ref, dk_ref, dv_ref):
        i = pl.program_id(1)

        @pl.when(i == 0)
        def _():
            dk_ref[...] = jnp.zeros_like(dk_ref)
            dv_ref[...] = jnp.zeros_like(dv_ref)

        q, do, lse = q_ref[...], do_ref[...], lse_ref[...]
        delta = lax.dot_general(jnp.ones((SUBLANE, V_DIM), F32), do.astype(F32) * o_ref[...], DN["nt"],
                                preferred_element_type=F32, precision=lax.Precision.HIGHEST)[0:1]
        dq = jnp.zeros((tq, K_FULL), F32)
        for c in range(T // tkv):
            keys = pl.ds(c * tkv, tkv)
            kc, vc = k_ref[keys, :], v_ref[keys, :]
            pt = jnp.exp(lax.dot_general(kc, q, DN["nt"], preferred_element_type=F32) - lse)
            dpt = lax.dot_general(vc, do, DN["nt"], preferred_element_type=F32)
            dst = (pt * (dpt - delta)).astype(BF16)
            dv_ref[keys, :] += jnp.dot(pt.astype(BF16), do, preferred_element_type=F32)
            dk_ref[keys, :] += jnp.dot(dst, q, preferred_element_type=F32)
            dq = dq + lax.dot_general(dst, kc, DN["tn"], preferred_element_type=F32)
        dq_ref[...] = dq

    return pl.pallas_call(
        body,
        name="attn_bwd",
        grid=(H, T // tq),
        in_specs=[pl.BlockSpec((None, tq, K_FULL), lambda h, i: (h, i, 0)),
                  pl.BlockSpec((None, T, K_FULL), lambda h, i: (h, 0, 0)),
                  pl.BlockSpec((None, T, V_DIM), lambda h, i: (h, 0, 0)),
                  pl.BlockSpec((tq, V_DIM), lambda h, i: (i, h)),
                  pl.BlockSpec((tq, V_DIM), lambda h, i: (i, h)),
                  pl.BlockSpec((None, 1, tq), lambda h, i: (h, 0, i))],
        out_specs=[pl.BlockSpec((None, tq, K_FULL), lambda h, i: (h, i, 0)),
                   pl.BlockSpec((None, T, K_FULL), lambda h, i: (h, 0, 0)),
                   pl.BlockSpec((None, T, V_DIM), lambda h, i: (h, 0, 0))],
        out_shape=[jax.ShapeDtypeStruct((H, T, K_FULL), F32), jax.ShapeDtypeStruct((H, T, K_FULL), F32),
                   jax.ShapeDtypeStruct((H, T, V_DIM), F32)],
        compiler_params=_params(("parallel", "arbitrary")),
    )(q_full, k_full, v_h, d_out, out, lse_rows)


def _sgu_forward_math(pb, lng, lnb, ws, bsb, gw, gh):
    u_raw, v_raw = pb[:, :gw], pb[:, gw:]
    u, v = _gelu(u_raw), _gelu(v_raw)
    vc = v - jnp.mean(v, axis=-1, keepdims=True)
    rstd = lax.rsqrt(jnp.mean(vc * vc, axis=-1, keepdims=True) + EPS)
    vhat = vc * rstd
    vln = vhat * lng + lnb
    hd = gw // gh
    s = jnp.concatenate(
        [jnp.dot(ws[g], vln[:, g * hd:(g + 1) * hd].astype(BF16), preferred_element_type=F32) + bsb[g]
         for g in range(gh)], axis=1)
    return u_raw, v_raw, u, rstd, vhat, vln, s


def _sgu_fwd(proj_b, lng, lnb, ws, bsb, gg):
    T, gw2 = proj_b.shape
    gw, gh = gw2 // 2, ws.shape[0]

    def fn(pb, lng, lnb, ws, bsb, gg):
        _, _, u, _, _, _, s = _sgu_forward_math(pb, lng, lnb, ws, bsb, gw, gh)
        return _rms(u * s, gg)

    return _ew("sgu_fwd", fn, (T // CHUNK,),
               [_rows(proj_b, CHUNK), _vec(lng), _vec(lnb), _vec(ws), _vec(bsb), _vec(gg)],
               [_rows_out((T, gw), BF16, CHUNK)])[0]


def _sgu_bwd(proj_b, d_mixed, lng, lnb, ws, ws_t, bsb, gg):
    T, gw2 = proj_b.shape
    gw, gh = gw2 // 2, ws.shape[0]
    hd = gw // gh

    def fn(pb, dmg, lng, lnb, ws, ws_t, bsb, gg):
        u_raw, v_raw, u, rstd, vhat, vln, s = _sgu_forward_math(pb, lng, lnb, ws, bsb, gw, gh)
        dgo, dgg = _rms_bwd(u * s, gg, dmg)
        du, ds = dgo * s, dgo * u
        ones = jnp.ones((SUBLANE, hd), F32)
        dws, dbs, dvln = [], [], []
        for g in range(gh):
            ds_g = ds[:, g * hd:(g + 1) * hd]
            ds_gb = ds_g.astype(BF16)
            dws.append(lax.dot_general(ds_gb, vln[:, g * hd:(g + 1) * hd].astype(BF16), DN["nt"],
                                       preferred_element_type=F32)[None])
            dbs.append(lax.dot_general(ones, ds_g, DN["nt"], preferred_element_type=F32,
                                       precision=lax.Precision.HIGHEST)[None])
            dvln.append(jnp.dot(ws_t[g], ds_gb, preferred_element_type=F32))
        dvln = jnp.concatenate(dvln, axis=1)
        dlng = jnp.sum(dvln * vhat, axis=0, keepdims=True)
        dlnb = jnp.sum(dvln, axis=0, keepdims=True)
        dvh = dvln * lng
        dv = rstd * (dvh - jnp.mean(dvh, axis=-1, keepdims=True) - vhat * jnp.mean(dvh * vhat, axis=-1, keepdims=True))
        d_pb = jnp.concatenate([du * _gelu_grad(u_raw), dv * _gelu_grad(v_raw)], axis=1)
        return d_pb, dgg, jnp.concatenate(dws, axis=0), jnp.concatenate(dbs, axis=0), dlng, dlnb

    return _ew("sgu_bwd", fn, (T // CHUNK,),
               [_rows(proj_b, CHUNK), _rows(d_mixed, CHUNK, gw, 1), _vec(lng), _vec(lnb), _vec(ws), _vec(ws_t),
                _vec(bsb), _vec(gg)],
               [_rows_out((T, gw2), BF16, CHUNK)],
               [_vec_out((1, gw)), _vec_out((gh, CHUNK, CHUNK)), _vec_out((gh, SUBLANE, CHUNK)), _vec_out((1, gw)),
                _vec_out((1, gw))])


def _adamw(name, parts, w, m, v):
    R, C = w.shape
    tm = _pick(R, max(SUBLANE, ADAM_BLOCK_ELEMS // C), SUBLANE)

    def fn(p, w, m, v):
        g = p[0].astype(F32)
        for i in range(1, N_DEV):
            g = g + p[i].astype(F32)
        m2 = ADAM_B1 * m + (1.0 - ADAM_B1) * g
        v2 = ADAM_B2 * v + (1.0 - ADAM_B2) * (g * g)
        m_hat = m2 / (1.0 - ADAM_B1 ** ADAM_STEP)
        v_hat = v2 / (1.0 - ADAM_B2 ** ADAM_STEP)
        delta = -ADAM_LR * (m_hat / (jnp.sqrt(v_hat) + ADAM_EPS) + ADAM_WD * w)
        return g, delta, m2, v2

    return _ew(name, fn, (R // tm,),
               [(parts, (N_DEV, tm, C), lambda i: (0, i, 0)), _rows(w, tm), _rows(m, tm), _rows(v, tm)],
               [_rows_out((R, C), F32, tm)] * 4)


def kernel(x, positions, pre_mix_norm, w_in, q_norm, kv_norm, w_uq, w_ukv, v_ln_gain, v_ln_bias, w_spatial, b_spatial, attn_out_norm, gmlp_out_norm, w_out, post_mix_norm, pre_ffn_norm, w_gate, w_up, w_down, post_ffn_norm, loss_target, m_pre_mix_norm, m_w_in, m_q_norm, m_kv_norm, m_w_uq, m_w_ukv, m_v_ln_gain, m_v_ln_bias, m_w_spatial, m_b_spatial, m_attn_out_norm, m_gmlp_out_norm, m_w_out, m_post_mix_norm, m_pre_ffn_norm, m_w_gate, m_w_up, m_w_down, m_post_ffn_norm, v_pre_mix_norm, v_w_in, v_q_norm, v_kv_norm, v_w_uq, v_w_ukv, v_v_ln_gain, v_v_ln_bias, v_w_spatial, v_b_spatial, v_attn_out_norm, v_gmlp_out_norm, v_w_out, v_post_mix_norm, v_pre_ffn_norm, v_w_gate, v_w_up, v_w_down, v_post_ffn_norm):
    given = dict(locals())
    T, D = x.shape[1], x.shape[2]
    QL, KVL = w_uq.shape[1], w_ukv.shape[1]
    GW, GH = v_ln_gain.shape[1], w_spatial.shape[1]
    IWS = w_in.shape[2]
    IW = IWS * N_DEV
    H = w_ukv.shape[2] * N_DEV // (NOPE_DIM + V_DIM)
    AW = H * V_DIM
    FS = w_gate.shape[2]
    FP = _round_up(FS, LANE)
    C0 = QL + KVL
    assert IW == C0 + ROPE_DIM + 2 * GW and AW + GW == D and AW == GW
    assert QL % KVL == 0 and C0 % K_FULL == 0 and w_spatial.shape[2] == CHUNK and T % CHUNK == 0
    assert w_uq.shape[2] * N_DEV == H * QK_HEAD and D % N_DEV == 0

    x2, tgt = x[0], loss_target[0]
    tm_d = _row_tile(T, D)

    inv_freq = 1.0 / (ROPE_THETA ** (jnp.arange(0, ROPE_DIM, 2, dtype=F32) / ROPE_DIM))
    ang = positions[0].astype(F32)[:, None] * inv_freq
    zpad = jnp.zeros((T, LANE - ROPE_DIM), F32)
    cosp = jnp.concatenate([jnp.cos(ang), jnp.cos(ang), zpad], axis=1)
    sinp = jnp.concatenate([jnp.sin(ang), jnp.sin(ang), zpad], axis=1)

    st_a, tok = _ag_start("ag_a", [w_in[0].astype(BF16), w_uq[0].astype(BF16), w_ukv[0].astype(BF16)], [])
    st_o, tok = _ag_start("ag_o", [w_out[0].astype(BF16)], [tok])
    st_gu, tok = _ag_start("ag_gu", [jnp.pad(w_gate[0].astype(BF16), ((0, 0), (0, FP - FS))),
                                     jnp.pad(w_up[0].astype(BF16), ((0, 0), (0, FP - FS)))], [tok])
    st_d, tok = _ag_start("ag_d", [jnp.pad(w_down[0].astype(BF16), ((0, FP - FS), (0, 0)))], [tok])

    xn = _ew("rms_x", lambda x, g: _rms(x, g), (T // tm_d,), [_rows(x2, tm_d), _vec(pre_mix_norm)],
             [_rows_out((T, D), BF16, tm_d)], deps=[tok])[0]
    st_a, tok = _ag_mid("ag_a", st_a, xn)
    g_in, g_uq, g_ukv = _ag_finish("ag_a", st_a, tok)

    w_in_all = g_in.transpose(1, 0, 2).reshape(D, IW)
    kr_w = w_in_all[:, C0:C0 + ROPE_DIM]
    z64 = jnp.zeros((D, LANE - ROPE_DIM), BF16)
    w_a = jnp.concatenate([w_in_all[:, :C0], kr_w, z64, _rot(kr_w), z64], axis=1)
    w_b = w_in_all[:, C0 + ROPE_DIM:]
    wq = g_uq.transpose(1, 0, 2).reshape(QL, H, QK_HEAD)
    zq = jnp.zeros((QL, H, LANE - ROPE_DIM), BF16)
    wq_ext = jnp.concatenate([wq[..., :NOPE_DIM], wq[..., NOPE_DIM:], zq, _rot(wq[..., NOPE_DIM:]), zq],
                             axis=-1).reshape(QL, H * Q_EXT)
    wkv = g_ukv.transpose(1, 0, 2).reshape(KVL, H * (NOPE_DIM + V_DIM))

    ws = w_spatial[0].astype(BF16)
    ws_t = jnp.swapaxes(ws, 1, 2)
    bsb = jnp.broadcast_to(b_spatial[0][:, :, None], (GH, CHUNK, GW // GH))

    proj_a = _mm2("mm_proj_a", "nn", xn, w_a, F32, tn_pref=w_a.shape[1])
    proj_b = _mm2("mm_proj_b", "nn", xn, w_b, F32)

    tm_q = _row_tile(T, QL)
    qn = _ew("rms_q", lambda x, g: _rms(x, g), (T // tm_q,), [_rows(proj_a, tm_q, QL, 0), _vec(q_norm)],
             [_rows_out((T, QL), BF16, tm_q)])[0]
    kvn = _ew("rms_kv", lambda x, g: _rms(x, g), (T // tm_q,), [_rows(proj_a, tm_q, KVL, QL // KVL), _vec(kv_norm)],
              [_rows_out((T, KVL), BF16, tm_q)])[0]
    q_ext = _mm2("mm_q", "nn", qn, wq_ext, F32)
    kv = _mm2("mm_kv", "nn", kvn, wkv, F32)

    scale = 1.0 / math.sqrt(QK_HEAD)
    tm_h = _pick(T, 512, SUBLANE)

    def rope_q(q, c, s):
        return jnp.concatenate([q[:, :LANE], q[:, LANE:2 * LANE] * c + q[:, 2 * LANE:] * s], axis=1) * scale

    q_full = _ew("rope_q", rope_q, (T // tm_h, H),
                 [(q_ext, (tm_h, Q_EXT), lambda i, h: (i, h)), (cosp, (tm_h, LANE), lambda i, h: (i, 0)),
                  (sinp, (tm_h, LANE), lambda i, h: (i, 0))],
                 [((H, T, K_FULL), BF16, (None, tm_h, K_FULL), lambda i, h: (h, i, 0))])[0]

    def make_kv(kv, kr, c, s):
        return jnp.concatenate([kv[:, :LANE], kr[:, :LANE] * c + kr[:, LANE:] * s], axis=1), kv[:, LANE:]

    k_full, v_h = _ew("make_kv", make_kv, (T // tm_h, H),
                      [(kv, (tm_h, K_FULL), lambda i, h: (i, h)),
                       (proj_a, (tm_h, K_FULL), lambda i, h: (i, C0 // K_FULL)),
                       (cosp, (tm_h, LANE), lambda i, h: (i, 0)), (sinp, (tm_h, LANE), lambda i, h: (i, 0))],
                      [((H, T, K_FULL), BF16, (None, tm_h, K_FULL), lambda i, h: (h, i, 0)),
                       ((H, T, V_DIM), BF16, (None, tm_h, V_DIM), lambda i, h: (h, i, 0))])

    a_out, lse = _attn_fwd(q_full, k_full, v_h)
    st_o, tok_o = _ag_mid("ag_o", st_o, a_out)
    st_gu, tok_gu = _ag_mid("ag_gu", st_gu, tok_o)
    tm_a = _row_tile(T, AW)
    mixed_a = _ew("rms_attn", lambda x, g: _rms(x, g), (T // tm_a,), [_rows(a_out, tm_a), _vec(attn_out_norm)],
                  [_rows_out((T, AW), BF16, tm_a)], deps=[tok_gu])[0]
    mixed_g = _sgu_fwd(proj_b, v_ln_gain, v_ln_bias, ws, bsb, gmlp_out_norm)
    wo = _ag_finish("ag_o", st_o, mixed_g)[0].reshape(D, D)

    tm_, tn_ = _pick(T, 1024, LANE), _pick(D, 1024, LANE)
    tk_ = _fit_tk(AW, tm_, tn_, pairs=2)
    nka = AW // tk_
    mix_out = _mm("mm_out", "nn",
                  [(mixed_a, (tm_, tk_), lambda i, j, k: (i, k), wo, (tk_, tn_), lambda i, j, k: (k, j)),
                   (mixed_g, (tm_, tk_), lambda i, j, k: (i, k), wo, (tk_, tn_), lambda i, j, k: (k + nka, j))],
                  (T // tm_, D // tn_, nka), (tm_, tn_), (T, D), F32, (tm_, tn_), lambda i, j, k: (i, j))

    def post_mix(x, mo, g3, g4):
        h = x + _rms(mo, g3)
        return h, _rms(h, g4)

    h_res, hn = _ew("post_mix", post_mix, (T // tm_d,),
                    [_rows(x2, tm_d), _rows(mix_out, tm_d), _vec(post_mix_norm), _vec(pre_ffn_norm)],
                    [_rows_out((T, D), F32, tm_d), _rows_out((T, D), BF16, tm_d)])

    tn_d = _pick(D, 1024, LANE)
    shard_blk = (None, tm_, FP)
    shard_bytes = 2 * tm_ * FP * 2
    tk_d = _fit_tk(D, tm_, FP, out_bytes=4, fixed_extra=2 * shard_bytes)

    def ffn_in(name, w, deps=(), **kw):
        return _mm(name, "nn", [(hn, (tm_, tk_d), lambda j, i, k: (i, k), w, (None, tk_d, FP), lambda j, i, k: (j, k, 0))],
                   (N_DEV, T // tm_, D // tk_d), (tm_, FP), (N_DEV, T, FP), BF16, shard_blk, lambda j, i, k: (j, i, 0),
                   deps=deps, **kw)

    def swiglu(u, g):
        g = g.astype(F32)
        return u, g * jax.nn.sigmoid(g) * u

    wg, wu = _ag_finish("ag_gu", st_gu, hn)
    gate = ffn_in("mm_gate", wg)
    st_d, tok_d = _ag_mid("ag_d", st_d, gate)
    up, act = ffn_in("mm_up", wu, deps=[tok_d], extras=[(gate, shard_blk, lambda j, i, k: (j, i, 0))],
                     epilogue=swiglu, n_out=2)
    wd = _ag_finish("ag_d", st_d, act)[0]
    ffn = _mm("mm_down", "nn",
              [(act, (None, tm_, FP), lambda i, j, k: (k, i, 0), wd, (None, FP, tn_d), lambda i, j, k: (k, 0, j))],
              (T // tm_, D // tn_d, N_DEV), (tm_, tn_d), (T, D), F32, (tm_, tn_d), lambda i, j, k: (i, j))

    def loss_head(h, f, t, g5):
        y = h + _rms(f, g5)
        diff = y - t
        dy = diff / D
        d_f, dg5 = _rms_bwd(f, g5, dy)
        lsum = jnp.sum(jnp.sum(diff * diff, axis=0, keepdims=True), axis=1, keepdims=True) * (0.5 / D)
        return dy, d_f, dg5, lsum

    dy, d_ffn, dg_post_ffn, loss_part = _ew(
        "loss_head", loss_head, (T // tm_d,),
        [_rows(h_res, tm_d), _rows(ffn, tm_d), _rows(tgt, tm_d), _vec(post_ffn_norm)],
        [_rows_out((T, D), F32, tm_d), _rows_out((T, D), BF16, tm_d)], [_vec_out((1, D)), _vec_out((1, 1))])

    def swiglu_bwd(da, g, u):
        g, u = g.astype(F32), u.astype(F32)
        sg = jax.nn.sigmoid(g)
        return da * u * sg * (1.0 + g * (1.0 - sg)), da * g * sg

    d_gate, d_up = _mm("mm_dact", "nt",
                       [(d_ffn, (tm_, tk_d), lambda j, i, k: (i, k), wd, (None, FP, tk_d), lambda j, i, k: (j, 0, k))],
                       (N_DEV, T // tm_, D // tk_d), (tm_, FP), (N_DEV, T, FP), BF16, shard_blk, lambda j, i, k: (j, i, 0),
                       extras=[(gate, shard_blk, lambda j, i, k: (j, i, 0)), (up, shard_blk, lambda j, i, k: (j, i, 0))],
                       epilogue=swiglu_bwd, n_out=2)

    tk_t = _fit_tk(T, tn_d, FP, out_bytes=2)
    dwd = _mm("mm_dwd", "tn",
              [(act, (None, tk_t, FP), lambda j, n, k: (j, k, 0), d_ffn, (tk_t, tn_d), lambda j, n, k: (k, n))],
              (N_DEV, D // tn_d, T // tk_t), (FP, tn_d), (N_DEV, FP, D), BF16, (None, FP, tn_d), lambda j, n, k: (j, 0, n))

    def ffn_dw(name, dz, deps):
        return _mm(name, "tn",
                   [(hn, (tk_t, tn_d), lambda j, i, k: (k, i), dz, (None, tk_t, FP), lambda j, i, k: (j, k, 0))],
                   (N_DEV, D // tn_d, T // tk_t), (tn_d, FP), (N_DEV, D, FP), BF16, (None, tn_d, FP),
                   lambda j, i, k: (j, i, 0), deps=deps)

    x_d, tok = _a2a_start("a2a_d", [dwd], [])
    dwg = ffn_dw("mm_dwg", d_gate, [tok])
    x_g, tok = _a2a_start("a2a_g", [dwg], [])
    dwu = ffn_dw("mm_dwu", d_up, [tok])
    x_u, tok = _a2a_start("a2a_u", [dwu], [])
    d_hn = _mm("mm_dhn", "nt",
               [(d_gate, (None, tm_, FP), lambda i, j, k: (k, i, 0), wg, (None, tn_d, FP), lambda i, j, k: (k, j, 0)),
                (d_up, (None, tm_, FP), lambda i, j, k: (k, i, 0), wu, (None, tn_d, FP), lambda i, j, k: (k, j, 0))],
               (T // tm_, D // tn_d, N_DEV), (tm_, tn_d), (T, D), F32, (tm_, tn_d), lambda i, j, k: (i, j), deps=[tok])

    def pre_ffn_bwd(h, dhn, dy, mo, g4, g3):
        d1, dg4 = _rms_bwd(h, g4, dhn)
        dh = dy + d1
        dmo, dg3 = _rms_bwd(mo, g3, dh)
        return dh, dmo, dg4, dg3

    dh, d_mo, dg_pre_ffn, dg_post_mix = _ew(
        "pre_ffn_bwd", pre_ffn_bwd, (T // tm_d,),
        [_rows(h_res, tm_d), _rows(d_hn, tm_d), _rows(dy, tm_d), _rows(mix_out, tm_d), _vec(pre_ffn_norm),
         _vec(post_mix_norm)],
        [_rows_out((T, D), F32, tm_d), _rows_out((T, D), BF16, tm_d)], [_vec_out((1, D)), _vec_out((1, D))])

    d_mixed = _mm2("mm_dmixed", "nt", d_mo, wo, F32)
    dwo = jnp.concatenate([_mm2("mm_dwo_a", "tn", mixed_a, d_mo, BF16), _mm2("mm_dwo_g", "tn", mixed_g, d_mo, BF16)],
                          axis=0)
    x_o, tok = _a2a_start("a2a_o", [dwo.reshape(N_DEV, D // N_DEV, D)], [])

    d_a_out, dg_attn = _ew("rms_attn_bwd", lambda a, g, d: _rms_bwd(a, g, d), (T // tm_a,),
                           [_rows(a_out, tm_a), _vec(attn_out_norm), _rows(d_mixed, tm_a, AW, 0)],
                           [_rows_out((T, AW), BF16, tm_a)], [_vec_out((1, AW))], deps=[tok])
    d_pb, dg_gmlp, dws, dbs, dlng, dlnb = _sgu_bwd(proj_b, d_mixed, v_ln_gain, v_ln_bias, ws, ws_t, bsb, gmlp_out_norm)

    dq_full, dk_full, dv_h = _attn_bwd(q_full, k_full, v_h, d_a_out, a_out, lse.reshape(H, 1, T))

    def rope_q_bwd(d, c, s):
        r = d[:, LANE:]
        return jnp.concatenate([d[:, :LANE], r * c, r * s], axis=1) * scale

    dq_ext = _ew("rope_q_bwd", rope_q_bwd, (T // tm_h, H),
                 [(dq_full, (None, tm_h, K_FULL), lambda i, h: (h, i, 0)), (cosp, (tm_h, LANE), lambda i, h: (i, 0)),
                  (sinp, (tm_h, LANE), lambda i, h: (i, 0))],
                 [((T, H * Q_EXT), BF16, (tm_h, Q_EXT), lambda i, h: (i, h))])[0]

    def make_kv_bwd(dk, dv, c, s):
        r = dk[:, LANE:]
        return jnp.concatenate([dk[:, :LANE], dv], axis=1), jnp.concatenate([r * c, r * s], axis=1)

    d_kv, d_kr = _ew("make_kv_bwd", make_kv_bwd, (T // tm_h, H),
                     [(dk_full, (None, tm_h, K_FULL), lambda i, h: (h, i, 0)),
                      (dv_h, (None, tm_h, V_DIM), lambda i, h: (h, i, 0)),
                      (cosp, (tm_h, LANE), lambda i, h: (i, 0)), (sinp, (tm_h, LANE), lambda i, h: (i, 0))],
                     [((T, H * K_FULL), BF16, (tm_h, K_FULL), lambda i, h: (i, h))],
                     [((T, K_FULL), F32, (tm_h, K_FULL), lambda i, h: (i, 0))])

    d_qn = _mm2("mm_dqn", "nt", dq_ext, wq_ext, F32)
    dwq_ext = _mm2("mm_dwq", "tn", qn, dq_ext, F32)
    d_kvn = _mm2("mm_dkvn", "nt", d_kv, wkv, F32)
    dwkv = _mm2("mm_dwkv", "tn", kvn, d_kv, BF16)

    d_qc, dg_q = _ew("rms_q_bwd", lambda a, g, d: _rms_bwd(a, g, d), (T // tm_q,),
                     [_rows(proj_a, tm_q, QL, 0), _vec(q_norm), _rows(d_qn, tm_q)],
                     [_rows_out((T, QL), BF16, tm_q)], [_vec_out((1, QL))])
    d_kvc, dg_kv = _ew("rms_kv_bwd", lambda a, g, d: _rms_bwd(a, g, d), (T // tm_q,),
                       [_rows(proj_a, tm_q, KVL, QL // KVL), _vec(kv_norm), _rows(d_kvn, tm_q)],
                       [_rows_out((T, KVL), BF16, tm_q)], [_vec_out((1, KVL))])
    d_pa = jnp.concatenate([d_qc, d_kvc, d_kr.astype(BF16)], axis=1)

    dwa = _mm2("mm_dwa", "tn", xn, d_pa, F32, tn_pref=w_a.shape[1])
    dwb = _mm2("mm_dwb", "tn", xn, d_pb, BF16)
    dkr_w = dwa[:, C0:C0 + ROPE_DIM] + _rot_t(dwa[:, C0 + LANE:C0 + LANE + ROPE_DIM])
    dw_in = jnp.concatenate([dwa[:, :C0].astype(BF16), dkr_w.astype(BF16), dwb], axis=1)
    dw_in = dw_in.reshape(D, N_DEV, IWS).transpose(1, 0, 2)
    dq3 = dwq_ext.reshape(QL, H, Q_EXT)
    dwq = jnp.concatenate([dq3[..., :NOPE_DIM],
                           dq3[..., LANE:LANE + ROPE_DIM] + _rot_t(dq3[..., 2 * LANE:2 * LANE + ROPE_DIM])], axis=-1)
    dwq = dwq.astype(BF16).reshape(QL, N_DEV, H * QK_HEAD // N_DEV).transpose(1, 0, 2)
    dwkv = dwkv.reshape(KVL, N_DEV, w_ukv.shape[2]).transpose(1, 0, 2)
    x_e, tok = _a2a_start("a2a_e", [dw_in, dwq, dwkv], [])

    d_xn_a = _mm2("mm_dxn_a", "nt", d_pa, w_a, F32, deps=[tok])
    d_xn_b = _mm2("mm_dxn_b", "nt", d_pb, w_b, F32)

    def x_bwd(x, da, db, dh, g0):
        dx, dg0 = _rms_bwd(x, g0, da + db)
        return dh + dx, dg0

    grad_x, dg_pre_mix = _ew("x_bwd", x_bwd, (T // tm_d,),
                             [_rows(x2, tm_d), _rows(d_xn_a, tm_d), _rows(d_xn_b, tm_d), _rows(dh, tm_d),
                              _vec(pre_mix_norm)],
                             [_rows_out((T, D), F32, tm_d)], [_vec_out((1, D))])

    def adam_big(n, parts):
        w3 = given[n]
        shp2 = w3.shape[1:]
        res = _adamw("adamw_" + n, parts, w3.reshape(shp2), given["m_" + n].reshape(shp2), given["v_" + n].reshape(shp2))
        return [r.reshape(w3.shape) for r in res]

    results = {}
    results["w_down"] = adam_big("w_down", _a2a_finish("a2a_d", x_d, grad_x)[0][:, :FS, :])
    results["w_gate"] = adam_big("w_gate", _a2a_finish("a2a_g", x_g, results["w_down"][0])[0][:, :, :FS])
    results["w_up"] = adam_big("w_up", _a2a_finish("a2a_u", x_u, results["w_gate"][0])[0][:, :, :FS])
    results["w_out"] = adam_big("w_out", _a2a_finish("a2a_o", x_o, results["w_up"][0])[0])
    e_in, e_uq, e_ukv = _a2a_finish("a2a_e", x_e, results["w_out"][0])
    results["w_in"] = adam_big("w_in", e_in)
    results["w_uq"] = adam_big("w_uq", e_uq)
    results["w_ukv"] = adam_big("w_ukv", e_ukv)

    small_names = ["pre_mix_norm", "q_norm", "kv_norm", "v_ln_gain", "v_ln_bias", "w_spatial", "b_spatial",
                   "attn_out_norm", "gmlp_out_norm", "post_mix_norm", "pre_ffn_norm", "post_ffn_norm"]
    small_grads = [dg_pre_mix, dg_q, dg_kv, dlng, dlnb, dws, dbs[:, 0, :], dg_attn, dg_gmlp, dg_post_mix, dg_pre_ffn,
                   dg_post_ffn]
    n_small = sum(g.size for g in small_grads)
    pad_small = _round_up(n_small, SUBLANE * LANE) - n_small

    def pack(arrs):
        flat = jnp.concatenate([a.reshape(-1) for a in arrs] + [jnp.zeros((pad_small,), F32)])
        return flat.reshape(-1, LANE)

    small_parts = _all_gather("ag_small_grads", pack(small_grads))
    sm = _adamw("adamw_small", small_parts, pack([given[n] for n in small_names]),
                pack([given["m_" + n] for n in small_names]), pack([given["v_" + n] for n in small_names]))

    off = 0
    for n in small_names:
        shp, size = given[n].shape, given[n].size
        results[n] = [r.reshape(-1)[off:off + size].reshape(shp) for r in sm]
        off += size

    loss = lax.psum(loss_part[0, 0], MESH_AXES)
    order = ["pre_mix_norm", "w_in", "q_norm", "kv_norm", "w_uq", "w_ukv", "v_ln_gain", "v_ln_bias", "w_spatial",
             "b_spatial", "attn_out_norm", "gmlp_out_norm", "w_out", "post_mix_norm", "pre_ffn_norm", "w_gate", "w_up",
             "w_down", "post_ffn_norm"]
    out = [loss, grad_x[None]]
    for kind in range(4):
        out += [results[n][kind] for n in order]
    return tuple(out)
```

```python
import math

import jax
import jax.numpy as jnp
from jax import lax
from jax.experimental import pallas as pl
from jax.experimental.pallas import tpu as pltpu

F32 = jnp.float32
BF16 = jnp.bfloat16
N_DEV = 8
MESH_AXES = ("x", "y", "c")
MESH = pl.DeviceIdType.MESH

NOPE_DIM = 128
ROPE_DIM = 64
V_DIM = 128
ROPE_THETA = 10000.0
CHUNK = 128
EPS = 1e-6
QK_HEAD = NOPE_DIM + ROPE_DIM
Q_EXT = 3 * 128
K_FULL = 2 * 128

ADAM_LR = 0.001
ADAM_B1 = 0.9
ADAM_B2 = 0.999
ADAM_EPS = 1e-08
ADAM_WD = 0.01
ADAM_STEP = 10

LANE = 128
SUBLANE = 8
VMEM_LIMIT = 56 * 1024 * 1024
MM_VMEM_BUDGET = 40 * 1024 * 1024
MM_TK_MAX = 2048
EPILOGUE_ROWS = 256
ATTN_TQ = 512
ATTN_TKV = 512
NEG_BIG = -1e30
ROW_BLOCK_ELEMS = 512 * 1024
ADAM_BLOCK_ELEMS = 256 * 1024

DN = {
    "nn": (((1,), (0,)), ((), ())),
    "nt": (((1,), (1,)), ((), ())),
    "tn": (((0,), (0,)), ((), ())),
}


def _pick(dim, pref, mult):
    t = min(pref, dim)
    t -= t % mult
    while t >= mult:
        if dim % t == 0:
            return t
        t -= mult
    return dim


def _round_up(n, m):
    return (n + m - 1) // m * m


def _params(sem):
    return pltpu.CompilerParams(dimension_semantics=sem, vmem_limit_bytes=VMEM_LIMIT)


def _dep_specs(deps, nd):
    return [pl.BlockSpec(d.shape, lambda *_, r=d.ndim: (0,) * r) for d in deps]


def _mm(name, mode, pairs, grid, acc_shape, out_shape, out_dtype, out_block, out_map, deps=(), extras=(),
        epilogue=None, n_out=1):
    n, ne, nx = len(pairs), len(extras), len(deps)
    nk = grid[-1]
    nd = len(grid)
    dn = DN[mode]
    use_acc = nk > 1 or epilogue is not None

    def body(*refs):
        e_refs = refs[2 * n:2 * n + ne]
        o_refs = refs[2 * n + ne + nx:2 * n + ne + nx + n_out]
        k = pl.program_id(nd - 1)

        def dot(i):
            return lax.dot_general(refs[2 * i][...], refs[2 * i + 1][...], dn, preferred_element_type=F32)

        if not use_acc:
            s = dot(0)
            for i in range(1, n):
                s = s + dot(i)
            o_refs[0][...] = s.astype(o_refs[0].dtype)
            return
        acc = refs[2 * n + ne + nx + n_out]

        @pl.when(k == 0)
        def _():
            acc[...] = dot(0)

        if nk > 1:
            @pl.when(k > 0)
            def _():
                acc[...] += dot(0)

        for i in range(1, n):
            acc[...] += dot(i)

        if epilogue is None:
            @pl.when(k == nk - 1)
            def _():
                o_refs[0][...] = acc[...].astype(o_refs[0].dtype)
        else:
            @pl.when(k == nk - 1)
            def _():
                rows = acc_shape[0]
                slab = _pick(rows, EPILOGUE_ROWS, SUBLANE)
                for r0 in range(0, rows, slab):
                    sl = pl.ds(r0, slab)
                    vals = epilogue(acc[sl, :], *[e[sl, :] for e in e_refs])
                    for r, v in zip(o_refs, vals):
                        r[sl, :] = v.astype(r.dtype)

    in_specs, args = [], []
    for a, a_blk, a_map, b, b_blk, b_map in pairs:
        in_specs += [pl.BlockSpec(a_blk, a_map), pl.BlockSpec(b_blk, b_map)]
        args += [a, b]
    in_specs += [pl.BlockSpec(blk, imap) for (_, blk, imap) in extras]
    args += [e for (e, _, _) in extras]
    if not isinstance(out_dtype, (tuple, list)):
        out_dtype = (out_dtype,) * n_out
    res = pl.pallas_call(
        body,
        name=name,
        grid=grid,
        in_specs=in_specs + _dep_specs(deps, nd),
        out_specs=[pl.BlockSpec(out_block, out_map)] * n_out,
        out_shape=[jax.ShapeDtypeStruct(out_shape, dt) for dt in out_dtype],
        scratch_shapes=[pltpu.VMEM(acc_shape, F32)] if use_acc else [],
        compiler_params=_params(("parallel",) * (nd - 1) + ("arbitrary",)),
    )(*args, *deps)
    return res[0] if n_out == 1 else res


def _fit_tk(K, tm, tn, pairs=1, out_bytes=4, fixed_extra=0):
    fixed = tm * tn * 4 + 2 * tm * tn * out_bytes + fixed_extra
    tk = _pick(K, MM_TK_MAX, LANE)
    while tk > LANE and fixed + pairs * 4 * (tm + tn) * tk > MM_VMEM_BUDGET:
        smaller = _pick(K, tk - LANE, LANE)
        if smaller >= tk:
            break
        tk = smaller
    return tk


def _mm2(name, mode, a, b, out_dtype, tm_pref=1024, tn_pref=1024, deps=()):
    if mode == "nn":
        (M, K), N = a.shape, b.shape[1]
    elif mode == "nt":
        (M, K), N = a.shape, b.shape[0]
    else:
        (K, M), N = a.shape, b.shape[1]
    tm, tn = _pick(M, tm_pref, LANE), _pick(N, tn_pref, LANE)
    tk = _fit_tk(K, tm, tn, out_bytes=jnp.dtype(out_dtype).itemsize)
    if mode == "tn":
        a_blk, a_map = (tk, tm), lambda i, j, k: (k, i)
    else:
        a_blk, a_map = (tm, tk), lambda i, j, k: (i, k)
    if mode == "nt":
        b_blk, b_map = (tn, tk), lambda i, j, k: (j, k)
    else:
        b_blk, b_map = (tk, tn), lambda i, j, k: (k, j)
    return _mm(name, mode, [(a, a_blk, a_map, b, b_blk, b_map)], (M // tm, N // tn, K // tk), (tm, tn),
               (M, N), out_dtype, (tm, tn), lambda i, j, k: (i, j), deps=deps)


def _ew(name, fn, grid, ins, outs, accs=(), deps=()):
    ni, no, na = len(ins), len(outs), len(accs)
    nd = len(grid)
    nx = len(deps)

    def body(*refs):
        vals = fn(*[r[...] for r in refs[:ni]])
        if not isinstance(vals, (tuple, list)):
            vals = (vals,)
        for r, v in zip(refs[ni + nx:ni + nx + no], vals[:no]):
            r[...] = v.astype(r.dtype)
        if na:
            first = pl.program_id(nd - 1) == 0
            for r, v in zip(refs[ni + nx + no:], vals[no:]):
                @pl.when(first)
                def _(r=r, v=v):
                    r[...] = v.astype(r.dtype)

                @pl.when(jnp.logical_not(first))
                def _(r=r, v=v):
                    r[...] += v.astype(r.dtype)

    all_outs = list(outs) + list(accs)
    sem = ("parallel",) * (nd - 1) + (("arbitrary",) if na else ("parallel",))
    res = pl.pallas_call(
        body,
        name=name,
        grid=grid,
        in_specs=[pl.BlockSpec(blk, imap) for (_, blk, imap) in ins] + _dep_specs(deps, nd),
        out_specs=[pl.BlockSpec(blk, imap) for (_, _, blk, imap) in all_outs],
        out_shape=[jax.ShapeDtypeStruct(s, d) for (s, d, _, _) in all_outs],
        compiler_params=_params(sem),
    )(*[a for (a, _, _) in ins], *deps)
    return res


def _rows(a, tm, width=None, cblk=0):
    width = a.shape[1] if width is None else width
    return (a, (tm, width), lambda i, c=cblk: (i, c))


def _vec(p):
    return (p, p.shape, lambda i: (0,) * p.ndim)


def _rows_out(shape, dtype, tm):
    return (shape, dtype, (tm, shape[1]), lambda i: (i, 0))


def _vec_out(shape):
    return (shape, F32, shape, lambda i: (0,) * len(shape))


def _row_tile(T, D):
    return _pick(T, max(SUBLANE, ROW_BLOCK_ELEMS // D), SUBLANE)


def _rms(x, g):
    r = lax.rsqrt(jnp.mean(x * x, axis=-1, keepdims=True) + EPS)
    return x * r * g


def _rms_bwd(x, g, dy):
    r = lax.rsqrt(jnp.mean(x * x, axis=-1, keepdims=True) + EPS)
    xh = x * r
    dg = jnp.sum(dy * xh, axis=0, keepdims=True)
    dxh = dy * g
    dx = r * (dxh - xh * jnp.mean(dxh * xh, axis=-1, keepdims=True))
    return dx, dg


GELU_C = math.sqrt(2.0 / math.pi)
GELU_A = 0.044715


def _gelu(x):
    return 0.5 * x * (1.0 + jnp.tanh(GELU_C * (x + GELU_A * x * x * x)))


def _gelu_grad(x):
    t = jnp.tanh(GELU_C * (x + GELU_A * x * x * x))
    return 0.5 * (1.0 + t) + 0.5 * x * (1.0 - t * t) * GELU_C * (1.0 + 3.0 * GELU_A * x * x)


def _rot(w):
    h = w.shape[-1] // 2
    return jnp.concatenate([-w[..., h:], w[..., :h]], axis=-1)


def _rot_t(dw):
    h = dw.shape[-1] // 2
    return jnp.concatenate([dw[..., h:], -dw[..., :h]], axis=-1)


def _coords():
    return lax.axis_index("x"), lax.axis_index("y"), lax.axis_index("c")


def _all_gather(name, shard):
    def body(x_ref, o_ref, send_sems, recv_sems, local_sem):
        xi, yi, ci = _coords()
        me, sibling = (xi, yi, ci), (xi, yi, 1 - ci)
        chips = [(1 - xi, yi), (xi, 1 - yi), (1 - xi, 1 - yi)]

        def slot(px, py, pc):
            return o_ref.at[4 * px + 2 * py + pc]

        def copy(k, block, to, src=None):
            return pltpu.make_async_remote_copy(
                src_ref=slot(*block) if src is None else src, dst_ref=slot(*block),
                send_sem=send_sems.at[k], recv_sem=recv_sems.at[k], device_id=to, device_id_type=MESH)

        mine = pltpu.make_async_copy(x_ref, slot(*me), local_sem)
        mine.start()
        first = [copy(0, me, sibling, src=x_ref)]
        first += [copy(1 + j, me, (*chip, ci), src=x_ref) for j, chip in enumerate(chips)]
        for cp in first:
            cp.start()
        passed = [copy(4 + j, (*chip, ci), sibling) for j, chip in enumerate(chips)]
        for j, chip in enumerate(chips):
            copy(1 + j, (*chip, ci), me).wait_recv()
            passed[j].start()
        copy(0, sibling, me).wait_recv()
        for j, chip in enumerate(chips):
            copy(4 + j, (*chip, 1 - ci), me).wait_recv()
        for cp in first + passed:
            cp.wait_send()
        mine.wait()

    return pl.pallas_call(
        body,
        name=name,
        out_shape=jax.ShapeDtypeStruct((N_DEV,) + shard.shape, shard.dtype),
        in_specs=[pl.BlockSpec(memory_space=pltpu.HBM)],
        out_specs=pl.BlockSpec(memory_space=pltpu.HBM),
        scratch_shapes=[pltpu.SemaphoreType.DMA((N_DEV - 1,)), pltpu.SemaphoreType.DMA((N_DEV - 1,)),
                        pltpu.SemaphoreType.DMA],
    )(shard)


HBM_SPEC = pl.BlockSpec(memory_space=pltpu.HBM)
SEM_SPEC = pl.BlockSpec(memory_space=pltpu.SEMAPHORE)
ANY_SPEC = pl.BlockSpec(memory_space=pl.ANY)
EFFECT = pltpu.SideEffectType.DATAFLOW_SIDE_EFFECTING
TOKEN = (SUBLANE, LANE)


def _my_slot():
    xi, yi, ci = _coords()
    return 4 * xi + 2 * yi + ci


def _own_slot_filled(block):
    land = lax.empty((N_DEV,) + block.shape, block.dtype)
    return lax.dynamic_update_slice(land, block[None], (_my_slot(),) + (0,) * block.ndim)


def _split_start(name, n, n_copies, issue, srcs, lands, deps):
    def body(*refs):
        src_refs, land_refs = refs[:n], refs[n:2 * n]
        outs = refs[2 * n + len(deps):]
        send, recv, token = outs[:n], outs[n:2 * n], outs[4 * n]
        for t in range(n):
            issue(t, src_refs[t], land_refs[t], send[t], recv[t])
        token[...] = jnp.zeros_like(token)

    sems = [pltpu.SemaphoreType.DMA((n_copies,))] * (2 * n)
    res = pl.pallas_call(
        body,
        name=name,
        in_specs=[HBM_SPEC] * (2 * n) + [ANY_SPEC] * len(deps),
        out_specs=[SEM_SPEC] * (2 * n) + [HBM_SPEC] * (2 * n) + [pl.BlockSpec(memory_space=pltpu.VMEM)],
        out_shape=sems + [pltpu.HBM(a.shape, a.dtype) for a in list(srcs) + list(lands)]
        + [jax.ShapeDtypeStruct(TOKEN, F32)],
        input_output_aliases={i: 2 * n + i for i in range(2 * n)},
        compiler_params=pltpu.CompilerParams(has_side_effects=EFFECT),
    )(*[pltpu.with_memory_space_constraint(a, pltpu.HBM) for a in list(srcs) + list(lands)], *deps)
    state = dict(n=n, n_copies=n_copies, send=res[:n], recv=res[n:2 * n], srcs=res[2 * n:3 * n],
                 lands=res[3 * n:4 * n])
    return state, res[4 * n]


def _split_wait(name, state, unit_src, after):
    n, n_copies = state["n"], state["n_copies"]

    def body(*refs):
        src_refs, land_refs = refs[:n], refs[n:2 * n]
        send, recv = refs[2 * n:3 * n], refs[3 * n:4 * n]
        me = _coords()
        for t in range(n):
            for k in range(n_copies):
                cp = pltpu.make_async_remote_copy(
                    src_ref=unit_src(src_refs[t]), dst_ref=land_refs[t].at[0], send_sem=send[t].at[k],
                    recv_sem=recv[t].at[k], device_id=me, device_id_type=MESH)
                cp.wait_send()
                cp.wait_recv()

    bufs = list(state["srcs"]) + list(state["lands"])
    res = pl.pallas_call(
        body,
        name=name,
        in_specs=[HBM_SPEC] * (2 * n) + [SEM_SPEC] * (2 * n) + [ANY_SPEC],
        out_specs=[HBM_SPEC] * (2 * n),
        out_shape=[pltpu.HBM(a.shape, a.dtype) for a in bufs],
        input_output_aliases={i: i for i in range(2 * n)},
        compiler_params=pltpu.CompilerParams(has_side_effects=EFFECT),
    )(*bufs, *state["send"], *state["recv"], after)
    return list(res[n:])


def _ag_start(name, shards, deps):
    def issue(t, x_ref, land_ref, send, recv):
        xi, yi, ci = _coords()
        to = [(xi, yi, 1 - ci), (1 - xi, yi, ci), (xi, 1 - yi, ci), (1 - xi, 1 - yi, ci)]
        for k, dev in enumerate(to):
            pltpu.make_async_remote_copy(src_ref=x_ref, dst_ref=land_ref.at[4 * xi + 2 * yi + ci], send_sem=send.at[k],
                                         recv_sem=recv.at[k], device_id=dev, device_id_type=MESH).start()

    return _split_start(name, len(shards), 4, issue, shards, [_own_slot_filled(s) for s in shards], deps)


def _ag_mid(name, state, after):
    lands = _split_wait(name + "_w1", state, lambda r: r, after)

    def issue(t, src_ref, land_ref, send, recv):
        xi, yi, ci = _coords()
        for k, (px, py) in enumerate([(1 - xi, yi), (xi, 1 - yi), (1 - xi, 1 - yi)]):
            rows = land_ref.at[4 * px + 2 * py + ci]
            pltpu.make_async_remote_copy(src_ref=rows, dst_ref=rows, send_sem=send.at[k], recv_sem=recv.at[k],
                                         device_id=(xi, yi, 1 - ci), device_id_type=MESH).start()

    n = len(lands)

    def body(*refs):
        land_refs = refs[:n]
        outs = refs[n:]
        send, recv, token = outs[:n], outs[n:2 * n], outs[3 * n]
        for t in range(n):
            issue(t, land_refs[t], land_refs[t], send[t], recv[t])
        token[...] = jnp.zeros_like(token)

    res = pl.pallas_call(
        body,
        name=name + "_s2",
        in_specs=[HBM_SPEC] * n,
        out_specs=[SEM_SPEC] * (2 * n) + [HBM_SPEC] * n + [pl.BlockSpec(memory_space=pltpu.VMEM)],
        out_shape=[pltpu.SemaphoreType.DMA((3,))] * (2 * n) + [pltpu.HBM(a.shape, a.dtype) for a in lands]
        + [jax.ShapeDtypeStruct(TOKEN, F32)],
        input_output_aliases={i: 2 * n + i for i in range(n)},
        compiler_params=pltpu.CompilerParams(has_side_effects=EFFECT),
    )(*lands)
    state2 = dict(n=n, send=res[:n], recv=res[n:2 * n], lands=res[2 * n:3 * n])
    return state2, res[3 * n]


def _ag_finish(name, state2, after):
    n = state2["n"]

    def body(*refs):
        land_refs, send, recv = refs[:n], refs[n:2 * n], refs[2 * n:3 * n]
        me = _coords()
        for t in range(n):
            for k in range(3):
                cp = pltpu.make_async_remote_copy(
                    src_ref=land_refs[t].at[0], dst_ref=land_refs[t].at[0], send_sem=send[t].at[k],
                    recv_sem=recv[t].at[k], device_id=me, device_id_type=MESH)
                cp.wait_send()
                cp.wait_recv()

    res = pl.pallas_call(
        body,
        name=name + "_w2",
        in_specs=[HBM_SPEC] * n + [SEM_SPEC] * (2 * n) + [ANY_SPEC],
        out_specs=[HBM_SPEC] * n,
        out_shape=[pltpu.HBM(a.shape, a.dtype) for a in state2["lands"]],
        input_output_aliases={i: i for i in range(n)},
        compiler_params=pltpu.CompilerParams(has_side_effects=EFFECT),
    )(*state2["lands"], *state2["send"], *state2["recv"], after)
    return list(res)


def _a2a_start(name, parts, deps):
    def issue(t, x_ref, land_ref, send, recv):
        xi, yi, ci = _coords()
        for k in range(N_DEV - 1):
            px = 1 - xi if (k + 1) & 4 else xi
            py = 1 - yi if (k + 1) & 2 else yi
            pc = 1 - ci if (k + 1) & 1 else ci
            pltpu.make_async_remote_copy(src_ref=x_ref.at[4 * px + 2 * py + pc], dst_ref=land_ref.at[4 * xi + 2 * yi + ci],
                                         send_sem=send.at[k], recv_sem=recv.at[k], device_id=(px, py, pc),
                                         device_id_type=MESH).start()

    lands = [_own_slot_filled(lax.dynamic_index_in_dim(p, _my_slot(), 0, keepdims=False)) for p in parts]
    return _split_start(name, len(parts), N_DEV - 1, issue, parts, lands, deps)


def _a2a_finish(name, state, after):
    return _split_wait(name + "_w", state, lambda r: r.at[0], after)


QK_SCALE = 1.0 / math.sqrt(QK_HEAD)
LOG2E = 1.0 / math.log(2.0)


def _roped_q(q_ext, cos, sin):
    q = jnp.concatenate([q_ext[:, :LANE], q_ext[:, LANE:2 * LANE] * cos + q_ext[:, 2 * LANE:] * sin], axis=1)
    return (q * (QK_SCALE * LOG2E)).astype(BF16)


def _attn_fwd(q_ext, k_full, v_aug, cosp, sinp):
    H, T, _ = k_full.shape
    tq, tkv = _pick(T, ATTN_TQ, LANE), _pick(T, ATTN_TKV, LANE)

    def body(q_ref, cos_ref, sin_ref, k_ref, v_ref, o_ref, lse_ref):
        q = _roped_q(q_ref[...], cos_ref[...], sin_ref[...])
        m = jnp.full((tq, 1), NEG_BIG, F32)
        acc = jnp.zeros((tq, 2 * V_DIM), F32)
        for c in range(T // tkv):
            keys = pl.ds(c * tkv, tkv)
            s = lax.dot_general(q, k_ref[keys, :], DN["nt"], preferred_element_type=F32)
            m_new = jnp.maximum(m, jnp.max(s, axis=-1, keepdims=True))
            p = jnp.exp2(s - m_new).astype(BF16)
            acc = jnp.exp2(m - m_new) * acc + jnp.dot(p, v_ref[keys, :], preferred_element_type=F32)
            m = m_new
        l = acc[:, V_DIM:]
        o_ref[...] = acc[:, :V_DIM] / l
        lse_ref[...] = m + jnp.log2(jnp.max(l, axis=-1, keepdims=True))

    return pl.pallas_call(
        body,
        name="attn_fwd",
        grid=(H, T // tq),
        in_specs=[pl.BlockSpec((tq, Q_EXT), lambda h, i: (i, h)),
                  pl.BlockSpec((tq, LANE), lambda h, i: (i, 0)),
                  pl.BlockSpec((tq, LANE), lambda h, i: (i, 0)),
                  pl.BlockSpec((None, T, K_FULL), lambda h, i: (h, 0, 0)),
                  pl.BlockSpec((None, T, 2 * V_DIM), lambda h, i: (h, 0, 0))],
        out_specs=[pl.BlockSpec((tq, V_DIM), lambda h, i: (i, h)),
                   pl.BlockSpec((None, tq, 1), lambda h, i: (h, i, 0))],
        out_shape=[jax.ShapeDtypeStruct((T, H * V_DIM), F32), jax.ShapeDtypeStruct((H, T, 1), F32)],
        compiler_params=_params(("parallel", "parallel")),
    )(q_ext, cosp, sinp, k_full, v_aug)


def _attn_bwd(q_ext, k_full, v_aug, cosp, sinp, d_out, out, lse_rows):
    H, T, _ = k_full.shape
    tq, tkv = _pick(T, ATTN_TQ, LANE), _pick(T, ATTN_TKV, LANE)
    n_q = T // tq

    def body(q_ref, cos_ref, sin_ref, k_ref, v_ref, do_ref, o_ref, lse_ref, cos_all, sin_all,
             dq_ref, dkv_ref, dkr_ref, dk_acc, dv_acc):
        h, i = pl.program_id(0), pl.program_id(1)

        @pl.when(i == 0)
        def _():
            dk_acc[...] = jnp.zeros_like(dk_acc)
            dv_acc[...] = jnp.zeros_like(dv_acc)

        cos, sin = cos_ref[...], sin_ref[...]
        q, do, lse = _roped_q(q_ref[...], cos, sin), do_ref[...], lse_ref[...]
        delta = lax.dot_general(jnp.ones((SUBLANE, V_DIM), F32), do.astype(F32) * o_ref[...], DN["nt"],
                                preferred_element_type=F32, precision=lax.Precision.HIGHEST)[0:1]
        dq = jnp.zeros((tq, K_FULL), F32)
        for c in range(T // tkv):
            keys = pl.ds(c * tkv, tkv)
            kc = k_ref[keys, :]
            pt = jnp.exp2(lax.dot_general(kc, q, DN["nt"], preferred_element_type=F32) - lse)
            dpt = lax.dot_general(v_ref[keys, pl.ds(0, V_DIM)], do, DN["nt"], preferred_element_type=F32)
            dst = (pt * (dpt - delta)).astype(BF16)
            dv_acc[keys, :] += jnp.dot(pt.astype(BF16), do, preferred_element_type=F32)
            dk_acc[keys, :] += jnp.dot(dst, q, preferred_element_type=F32)
            dq = dq + lax.dot_general(dst, kc, DN["tn"], preferred_element_type=F32)
        r = dq[:, LANE:]
        dq_ref[...] = (jnp.concatenate([dq[:, :LANE], r * cos, r * sin], axis=1) * QK_SCALE).astype(BF16)

        @pl.when(i == n_q - 1)
        def _():
            dk = dk_acc[...] * (1.0 / LOG2E)
            dkv_ref[...] = jnp.concatenate([dk[:, :LANE], dv_acc[...]], axis=1).astype(BF16)
            rk = dk[:, LANE:]
            dkr = jnp.concatenate([rk * cos_all[...], rk * sin_all[...]], axis=1)

            @pl.when(h == 0)
            def _():
                dkr_ref[...] = dkr

            @pl.when(h != 0)
            def _():
                dkr_ref[...] += dkr

    return pl.pallas_call(
        body,
        name="attn_bwd",
        grid=(H, n_q),
        in_specs=[pl.BlockSpec((tq, Q_EXT), lambda h, i: (i, h)),
                  pl.BlockSpec((tq, LANE), lambda h, i: (i, 0)),
                  pl.BlockSpec((tq, LANE), lambda h, i: (i, 0)),
                  pl.BlockSpec((None, T, K_FULL), lambda h, i: (h, 0, 0)),
                  pl.BlockSpec((None, T, 2 * V_DIM), lambda h, i: (h, 0, 0)),
                  pl.BlockSpec((tq, V_DIM), lambda h, i: (i, h)),
                  pl.BlockSpec((tq, V_DIM), lambda h, i: (i, h)),
                  pl.BlockSpec((None, 1, tq), lambda h, i: (h, 0, i)),
                  pl.BlockSpec((T, LANE), lambda h, i: (0, 0)),
                  pl.BlockSpec((T, LANE), lambda h, i: (0, 0))],
        out_specs=[pl.BlockSpec((tq, Q_EXT), lambda h, i: (i, h)),
                   pl.BlockSpec((T, K_FULL), lambda h, i: (0, h)),
                   pl.BlockSpec((T, K_FULL), lambda h, i: (0, 0))],
        out_shape=[jax.ShapeDtypeStruct((T, H * Q_EXT), BF16), jax.ShapeDtypeStruct((T, H * K_FULL), BF16),
                   jax.ShapeDtypeStruct((T, K_FULL), F32)],
        scratch_shapes=[pltpu.VMEM((T, K_FULL), F32), pltpu.VMEM((T, V_DIM), F32)],
        compiler_params=_params(("arbitrary", "arbitrary")),
    )(q_ext, cosp, sinp, k_full, v_aug, d_out, out, lse_rows, cosp, sinp)


def _sgu_forward_math(pb, lng, lnb, ws, bsb, gw, gh):
    u_raw, v_raw = pb[:, :gw], pb[:, gw:]
    u, v = _gelu(u_raw), _gelu(v_raw)
    vc = v - jnp.mean(v, axis=-1, keepdims=True)
    rstd = lax.rsqrt(jnp.mean(vc * vc, axis=-1, keepdims=True) + EPS)
    vhat = vc * rstd
    vln = vhat * lng + lnb
    hd = gw // gh
    s = jnp.concatenate(
        [jnp.dot(ws[g], vln[:, g * hd:(g + 1) * hd].astype(BF16), preferred_element_type=F32) + bsb[g]
         for g in range(gh)], axis=1)
    return u_raw, v_raw, u, rstd, vhat, vln, s


def _sgu_fwd(proj_b, lng, lnb, ws, bsb, gg):
    T, gw2 = proj_b.shape
    gw, gh = gw2 // 2, ws.shape[0]

    def fn(pb, lng, lnb, ws, bsb, gg):
        _, _, u, _, _, _, s = _sgu_forward_math(pb, lng, lnb, ws, bsb, gw, gh)
        return _rms(u * s, gg)

    return _ew("sgu_fwd", fn, (T // CHUNK,),
               [_rows(proj_b, CHUNK), _vec(lng), _vec(lnb), _vec(ws), _vec(bsb), _vec(gg)],
               [_rows_out((T, gw), BF16, CHUNK)])[0]


def _sgu_bwd(proj_b, d_mixed, lng, lnb, ws, ws_t, bsb, gg):
    T, gw2 = proj_b.shape
    gw, gh = gw2 // 2, ws.shape[0]
    hd = gw // gh

    def fn(pb, dmg, lng, lnb, ws, ws_t, bsb, gg):
        u_raw, v_raw, u, rstd, vhat, vln, s = _sgu_forward_math(pb, lng, lnb, ws, bsb, gw, gh)
        dgo, dgg = _rms_bwd(u * s, gg, dmg)
        du, ds = dgo * s, dgo * u
        ones = jnp.ones((SUBLANE, hd), F32)
        dws, dbs, dvln = [], [], []
        for g in range(gh):
            ds_g = ds[:, g * hd:(g + 1) * hd]
            ds_gb = ds_g.astype(BF16)
            dws.append(lax.dot_general(ds_gb, vln[:, g * hd:(g + 1) * hd].astype(BF16), DN["nt"],
                                       preferred_element_type=F32)[None])
            dbs.append(lax.dot_general(ones, ds_g, DN["nt"], preferred_element_type=F32,
                                       precision=lax.Precision.HIGHEST)[None])
            dvln.append(jnp.dot(ws_t[g], ds_gb, preferred_element_type=F32))
        dvln = jnp.concatenate(dvln, axis=1)
        dlng = jnp.sum(dvln * vhat, axis=0, keepdims=True)
        dlnb = jnp.sum(dvln, axis=0, keepdims=True)
        dvh = dvln * lng
        dv = rstd * (dvh - jnp.mean(dvh, axis=-1, keepdims=True) - vhat * jnp.mean(dvh * vhat, axis=-1, keepdims=True))
        d_pb = jnp.concatenate([du * _gelu_grad(u_raw), dv * _gelu_grad(v_raw)], axis=1)
        return d_pb, dgg, jnp.concatenate(dws, axis=0), jnp.concatenate(dbs, axis=0), dlng, dlnb

    return _ew("sgu_bwd", fn, (T // CHUNK,),
               [_rows(proj_b, CHUNK), _rows(d_mixed, CHUNK, gw, 1), _vec(lng), _vec(lnb), _vec(ws), _vec(ws_t),
                _vec(bsb), _vec(gg)],
               [_rows_out((T, gw2), BF16, CHUNK)],
               [_vec_out((1, gw)), _vec_out((gh, CHUNK, CHUNK)), _vec_out((gh, SUBLANE, CHUNK)), _vec_out((1, gw)),
                _vec_out((1, gw))])


def _adamw(name, parts, w, m, v):
    R, C = w.shape
    tm = _pick(R, max(SUBLANE, ADAM_BLOCK_ELEMS // C), SUBLANE)

    def fn(p, w, m, v):
        g = p[0].astype(F32)
        for i in range(1, N_DEV):
            g = g + p[i].astype(F32)
        m2 = ADAM_B1 * m + (1.0 - ADAM_B1) * g
        v2 = ADAM_B2 * v + (1.0 - ADAM_B2) * (g * g)
        m_hat = m2 / (1.0 - ADAM_B1 ** ADAM_STEP)
        v_hat = v2 / (1.0 - ADAM_B2 ** ADAM_STEP)
        delta = -ADAM_LR * (m_hat / (jnp.sqrt(v_hat) + ADAM_EPS) + ADAM_WD * w)
        return g, delta, m2, v2

    return _ew(name, fn, (R // tm,),
               [(parts, (N_DEV, tm, C), lambda i: (0, i, 0)), _rows(w, tm), _rows(m, tm), _rows(v, tm)],
               [_rows_out((R, C), F32, tm)] * 4)


def kernel(x, positions, pre_mix_norm, w_in, q_norm, kv_norm, w_uq, w_ukv, v_ln_gain, v_ln_bias, w_spatial, b_spatial, attn_out_norm, gmlp_out_norm, w_out, post_mix_norm, pre_ffn_norm, w_gate, w_up, w_down, post_ffn_norm, loss_target, m_pre_mix_norm, m_w_in, m_q_norm, m_kv_norm, m_w_uq, m_w_ukv, m_v_ln_gain, m_v_ln_bias, m_w_spatial, m_b_spatial, m_attn_out_norm, m_gmlp_out_norm, m_w_out, m_post_mix_norm, m_pre_ffn_norm, m_w_gate, m_w_up, m_w_down, m_post_ffn_norm, v_pre_mix_norm, v_w_in, v_q_norm, v_kv_norm, v_w_uq, v_w_ukv, v_v_ln_gain, v_v_ln_bias, v_w_spatial, v_b_spatial, v_attn_out_norm, v_gmlp_out_norm, v_w_out, v_post_mix_norm, v_pre_ffn_norm, v_w_gate, v_w_up, v_w_down, v_post_ffn_norm):
    given = dict(locals())
    T, D = x.shape[1], x.shape[2]
    QL, KVL = w_uq.shape[1], w_ukv.shape[1]
    GW, GH = v_ln_gain.shape[1], w_spatial.shape[1]
    IWS = w_in.shape[2]
    IW = IWS * N_DEV
    H = w_ukv.shape[2] * N_DEV // (NOPE_DIM + V_DIM)
    AW = H * V_DIM
    FS = w_gate.shape[2]
    FP = _round_up(FS, LANE)
    C0 = QL + KVL
    assert IW == C0 + ROPE_DIM + 2 * GW and AW + GW == D and AW == GW
    assert QL % KVL == 0 and C0 % K_FULL == 0 and w_spatial.shape[2] == CHUNK and T % CHUNK == 0
    assert w_uq.shape[2] * N_DEV == H * QK_HEAD and D % N_DEV == 0

    x2, tgt = x[0], loss_target[0]
    tm_d = _row_tile(T, D)

    inv_freq = 1.0 / (ROPE_THETA ** (jnp.arange(0, ROPE_DIM, 2, dtype=F32) / ROPE_DIM))
    ang = positions[0].astype(F32)[:, None] * inv_freq
    zpad = jnp.zeros((T, LANE - ROPE_DIM), F32)
    cosp = jnp.concatenate([jnp.cos(ang), jnp.cos(ang), zpad], axis=1)
    sinp = jnp.concatenate([jnp.sin(ang), jnp.sin(ang), zpad], axis=1)

    st_a, tok = _ag_start("ag_a", [w_in[0].astype(BF16), w_uq[0].astype(BF16), w_ukv[0].astype(BF16)], [])
    st_o, tok = _ag_start("ag_o", [w_out[0].astype(BF16)], [tok])
    st_gu, tok = _ag_start("ag_gu", [jnp.pad(w_gate[0].astype(BF16), ((0, 0), (0, FP - FS))),
                                     jnp.pad(w_up[0].astype(BF16), ((0, 0), (0, FP - FS)))], [tok])
    st_d, tok = _ag_start("ag_d", [jnp.pad(w_down[0].astype(BF16), ((0, FP - FS), (0, 0)))], [tok])

    xn = _ew("rms_x", lambda x, g: _rms(x, g), (T // tm_d,), [_rows(x2, tm_d), _vec(pre_mix_norm)],
             [_rows_out((T, D), BF16, tm_d)], deps=[tok])[0]
    st_a, tok = _ag_mid("ag_a", st_a, xn)
    g_in, g_uq, g_ukv = _ag_finish("ag_a", st_a, tok)

    w_in_all = g_in.transpose(1, 0, 2).reshape(D, IW)
    kr_w = w_in_all[:, C0:C0 + ROPE_DIM]
    z64 = jnp.zeros((D, LANE - ROPE_DIM), BF16)
    w_a = jnp.concatenate([w_in_all[:, :C0], kr_w, z64, _rot(kr_w), z64], axis=1)
    w_b = w_in_all[:, C0 + ROPE_DIM:]
    wq = g_uq.transpose(1, 0, 2).reshape(QL, H, QK_HEAD)
    zq = jnp.zeros((QL, H, LANE - ROPE_DIM), BF16)
    wq_ext = jnp.concatenate([wq[..., :NOPE_DIM], wq[..., NOPE_DIM:], zq, _rot(wq[..., NOPE_DIM:]), zq],
                             axis=-1).reshape(QL, H * Q_EXT)
    wkv = g_ukv.transpose(1, 0, 2).reshape(KVL, H * (NOPE_DIM + V_DIM))

    ws = w_spatial[0].astype(BF16)
    ws_t = jnp.swapaxes(ws, 1, 2)
    bsb = jnp.broadcast_to(b_spatial[0][:, :, None], (GH, CHUNK, GW // GH))

    proj_a = _mm2("mm_proj_a", "nn", xn, w_a, F32, tn_pref=w_a.shape[1])
    proj_b = _mm2("mm_proj_b", "nn", xn, w_b, F32)

    tm_q = _row_tile(T, QL)
    qn = _ew("rms_q", lambda x, g: _rms(x, g), (T // tm_q,), [_rows(proj_a, tm_q, QL, 0), _vec(q_norm)],
             [_rows_out((T, QL), BF16, tm_q)])[0]
    kvn = _ew("rms_kv", lambda x, g: _rms(x, g), (T // tm_q,), [_rows(proj_a, tm_q, KVL, QL // KVL), _vec(kv_norm)],
              [_rows_out((T, KVL), BF16, tm_q)])[0]
    q_ext = _mm2("mm_q", "nn", qn, wq_ext, F32)
    kv = _mm2("mm_kv", "nn", kvn, wkv, F32)

    tm_h = _pick(T, 512, SUBLANE)

    def make_kv(kv, kr, c, s):
        k = jnp.concatenate([kv[:, :LANE], kr[:, :LANE] * c + kr[:, LANE:] * s], axis=1)
        return k, jnp.concatenate([kv[:, LANE:], jnp.ones_like(kv[:, LANE:])], axis=1)

    k_full, v_aug = _ew("make_kv", make_kv, (T // tm_h, H),
                        [(kv, (tm_h, K_FULL), lambda i, h: (i, h)),
                         (proj_a, (tm_h, K_FULL), lambda i, h: (i, C0 // K_FULL)),
                         (cosp, (tm_h, LANE), lambda i, h: (i, 0)), (sinp, (tm_h, LANE), lambda i, h: (i, 0))],
                        [((H, T, K_FULL), BF16, (None, tm_h, K_FULL), lambda i, h: (h, i, 0)),
                         ((H, T, 2 * V_DIM), BF16, (None, tm_h, 2 * V_DIM), lambda i, h: (h, i, 0))])

    a_out, lse = _attn_fwd(q_ext, k_full, v_aug, cosp, sinp)
    st_o, tok_o = _ag_mid("ag_o", st_o, a_out)
    st_gu, tok_gu = _ag_mid("ag_gu", st_gu, tok_o)
    tm_a = _row_tile(T, AW)
    mixed_a = _ew("rms_attn", lambda x, g: _rms(x, g), (T // tm_a,), [_rows(a_out, tm_a), _vec(attn_out_norm)],
                  [_rows_out((T, AW), BF16, tm_a)], deps=[tok_gu])[0]
    mixed_g = _sgu_fwd(proj_b, v_ln_gain, v_ln_bias, ws, bsb, gmlp_out_norm)
    wo = _ag_finish("ag_o", st_o, mixed_g)[0].reshape(D, D)

    tm_, tn_ = _pick(T, 1024, LANE), _pick(D, 1024, LANE)
    tk_ = _fit_tk(AW, tm_, tn_, pairs=2)
    nka = AW // tk_
    mix_out = _mm("mm_out", "nn",
                  [(mixed_a, (tm_, tk_), lambda i, j, k: (i, k), wo, (tk_, tn_), lambda i, j, k: (k, j)),
                   (mixed_g, (tm_, tk_), lambda i, j, k: (i, k), wo, (tk_, tn_), lambda i, j, k: (k + nka, j))],
                  (T // tm_, D // tn_, nka), (tm_, tn_), (T, D), F32, (tm_, tn_), lambda i, j, k: (i, j))

    def post_mix(x, mo, g3, g4):
        h = x + _rms(mo, g3)
        return h, _rms(h, g4)

    h_res, hn = _ew("post_mix", post_mix, (T // tm_d,),
                    [_rows(x2, tm_d), _rows(mix_out, tm_d), _vec(post_mix_norm), _vec(pre_ffn_norm)],
                    [_rows_out((T, D), F32, tm_d), _rows_out((T, D), BF16, tm_d)])

    tn_d = _pick(D, 1024, LANE)
    shard_blk = (None, tm_, FP)
    shard_bytes = 2 * tm_ * FP * 2
    tk_d = _fit_tk(D, tm_, FP, out_bytes=4, fixed_extra=2 * shard_bytes)

    def ffn_in(name, w, deps=(), **kw):
        return _mm(name, "nn", [(hn, (tm_, tk_d), lambda j, i, k: (i, k), w, (None, tk_d, FP), lambda j, i, k: (j, k, 0))],
                   (N_DEV, T // tm_, D // tk_d), (tm_, FP), (N_DEV, T, FP), BF16, shard_blk, lambda j, i, k: (j, i, 0),
                   deps=deps, **kw)

    def swiglu(u, g):
        g = g.astype(F32)
        return u, g * jax.nn.sigmoid(g) * u

    wg, wu = _ag_finish("ag_gu", st_gu, hn)
    gate = ffn_in("mm_gate", wg)
    st_d, tok_d = _ag_mid("ag_d", st_d, gate)
    up, act = ffn_in("mm_up", wu, deps=[tok_d], extras=[(gate, shard_blk, lambda j, i, k: (j, i, 0))],
                     epilogue=swiglu, n_out=2)
    wd = _ag_finish("ag_d", st_d, act)[0]
    ffn = _mm("mm_down", "nn",
              [(act, (None, tm_, FP), lambda i, j, k: (k, i, 0), wd, (None, FP, tn_d), lambda i, j, k: (k, 0, j))],
              (T // tm_, D // tn_d, N_DEV), (tm_, tn_d), (T, D), F32, (tm_, tn_d), lambda i, j, k: (i, j))

    def loss_head(h, f, t, g5):
        y = h + _rms(f, g5)
        diff = y - t
        dy = diff / D
        d_f, dg5 = _rms_bwd(f, g5, dy)
        lsum = jnp.sum(jnp.sum(diff * diff, axis=0, keepdims=True), axis=1, keepdims=True) * (0.5 / D)
        return dy, d_f, dg5, lsum

    dy, d_ffn, dg_post_ffn, loss_part = _ew(
        "loss_head", loss_head, (T // tm_d,),
        [_rows(h_res, tm_d), _rows(ffn, tm_d), _rows(tgt, tm_d), _vec(post_ffn_norm)],
        [_rows_out((T, D), F32, tm_d), _rows_out((T, D), BF16, tm_d)], [_vec_out((1, D)), _vec_out((1, 1))])

    def swiglu_bwd(da, g, u):
        g, u = g.astype(F32), u.astype(F32)
        sg = jax.nn.sigmoid(g)
        return da * u * sg * (1.0 + g * (1.0 - sg)), da * g * sg

    d_gate, d_up = _mm("mm_dact", "nt",
                       [(d_ffn, (tm_, tk_d), lambda j, i, k: (i, k), wd, (None, FP, tk_d), lambda j, i, k: (j, 0, k))],
                       (N_DEV, T // tm_, D // tk_d), (tm_, FP), (N_DEV, T, FP), BF16, shard_blk, lambda j, i, k: (j, i, 0),
                       extras=[(gate, shard_blk, lambda j, i, k: (j, i, 0)), (up, shard_blk, lambda j, i, k: (j, i, 0))],
                       epilogue=swiglu_bwd, n_out=2)

    tk_t = _fit_tk(T, tn_d, FP, out_bytes=2)
    dwd = _mm("mm_dwd", "tn",
              [(act, (None, tk_t, FP), lambda j, n, k: (j, k, 0), d_ffn, (tk_t, tn_d), lambda j, n, k: (k, n))],
              (N_DEV, D // tn_d, T // tk_t), (FP, tn_d), (N_DEV, FP, D), BF16, (None, FP, tn_d), lambda j, n, k: (j, 0, n))

    def ffn_dw(name, dz, deps):
        return _mm(name, "tn",
                   [(hn, (tk_t, tn_d), lambda j, i, k: (k, i), dz, (None, tk_t, FP), lambda j, i, k: (j, k, 0))],
                   (N_DEV, D // tn_d, T // tk_t), (tn_d, FP), (N_DEV, D, FP), BF16, (None, tn_d, FP),
                   lambda j, i, k: (j, i, 0), deps=deps)

    x_d, tok = _a2a_start("a2a_d", [dwd], [])
    dwg = ffn_dw("mm_dwg", d_gate, [tok])
    x_g, tok = _a2a_start("a2a_g", [dwg], [])
    dwu = ffn_dw("mm_dwu", d_up, [tok])
    x_u, tok = _a2a_start("a2a_u", [dwu], [])
    d_hn = _mm("mm_dhn", "nt",
               [(d_gate, (None, tm_, FP), lambda i, j, k: (k, i, 0), wg, (None, tn_d, FP), lambda i, j, k: (k, j, 0)),
                (d_up, (None, tm_, FP), lambda i, j, k: (k, i, 0), wu, (None, tn_d, FP), lambda i, j, k: (k, j, 0))],
               (T // tm_, D // tn_d, N_DEV), (tm_, tn_d), (T, D), F32, (tm_, tn_d), lambda i, j, k: (i, j), deps=[tok])

    def pre_ffn_bwd(h, dhn, dy, mo, g4, g3):
        d1, dg4 = _rms_bwd(h, g4, dhn)
        dh = dy + d1
        dmo, dg3 = _rms_bwd(mo, g3, dh)
        return dh, dmo, dg4, dg3

    dh, d_mo, dg_pre_ffn, dg_post_mix = _ew(
        "pre_ffn_bwd", pre_ffn_bwd, (T // tm_d,),
        [_rows(h_res, tm_d), _rows(d_hn, tm_d), _rows(dy, tm_d), _rows(mix_out, tm_d), _vec(pre_ffn_norm),
         _vec(post_mix_norm)],
        [_rows_out((T, D), F32, tm_d), _rows_out((T, D), BF16, tm_d)], [_vec_out((1, D)), _vec_out((1, D))])

    d_mixed = _mm2("mm_dmixed", "nt", d_mo, wo, F32)
    dwo = jnp.concatenate([_mm2("mm_dwo_a", "tn", mixed_a, d_mo, BF16), _mm2("mm_dwo_g", "tn", mixed_g, d_mo, BF16)],
                          axis=0)
    x_o, tok = _a2a_start("a2a_o", [dwo.reshape(N_DEV, D // N_DEV, D)], [])

    d_a_out, dg_attn = _ew("rms_attn_bwd", lambda a, g, d: _rms_bwd(a, g, d), (T // tm_a,),
                           [_rows(a_out, tm_a), _vec(attn_out_norm), _rows(d_mixed, tm_a, AW, 0)],
                           [_rows_out((T, AW), BF16, tm_a)], [_vec_out((1, AW))], deps=[tok])
    d_pb, dg_gmlp, dws, dbs, dlng, dlnb = _sgu_bwd(proj_b, d_mixed, v_ln_gain, v_ln_bias, ws, ws_t, bsb, gmlp_out_norm)

    dq_ext, d_kv, d_kr = _attn_bwd(q_ext, k_full, v_aug, cosp, sinp, d_a_out, a_out, lse.reshape(H, 1, T))

    d_qn = _mm2("mm_dqn", "nt", dq_ext, wq_ext, F32)
    dwq_ext = _mm2("mm_dwq", "tn", qn, dq_ext, F32)
    d_kvn = _mm2("mm_dkvn", "nt", d_kv, wkv, F32)
    dwkv = _mm2("mm_dwkv", "tn", kvn, d_kv, BF16)

    d_qc, dg_q = _ew("rms_q_bwd", lambda a, g, d: _rms_bwd(a, g, d), (T // tm_q,),
                     [_rows(proj_a, tm_q, QL, 0), _vec(q_norm), _rows(d_qn, tm_q)],
                     [_rows_out((T, QL), BF16, tm_q)], [_vec_out((1, QL))])
    d_kvc, dg_kv = _ew("rms_kv_bwd", lambda a, g, d: _rms_bwd(a, g, d), (T // tm_q,),
                       [_rows(proj_a, tm_q, KVL, QL // KVL), _vec(kv_norm), _rows(d_kvn, tm_q)],
                       [_rows_out((T, KVL), BF16, tm_q)], [_vec_out((1, KVL))])
    d_pa = jnp.concatenate([d_qc, d_kvc, d_kr.astype(BF16)], axis=1)

    dwa = _mm2("mm_dwa", "tn", xn, d_pa, F32, tn_pref=w_a.shape[1])
    dwb = _mm2("mm_dwb", "tn", xn, d_pb, BF16)
    dkr_w = dwa[:, C0:C0 + ROPE_DIM] + _rot_t(dwa[:, C0 + LANE:C0 + LANE + ROPE_DIM])
    dw_in = jnp.concatenate([dwa[:, :C0].astype(BF16), dkr_w.astype(BF16), dwb], axis=1)
    dw_in = dw_in.reshape(D, N_DEV, IWS).transpose(1, 0, 2)
    dq3 = dwq_ext.reshape(QL, H, Q_EXT)
    dwq = jnp.concatenate([dq3[..., :NOPE_DIM],
                           dq3[..., LANE:LANE + ROPE_DIM] + _rot_t(dq3[..., 2 * LANE:2 * LANE + ROPE_DIM])], axis=-1)
    dwq = dwq.astype(BF16).reshape(QL, N_DEV, H * QK_HEAD // N_DEV).transpose(1, 0, 2)
    dwkv = dwkv.reshape(KVL, N_DEV, w_ukv.shape[2]).transpose(1, 0, 2)
    x_e, tok = _a2a_start("a2a_e", [dw_in, dwq, dwkv], [])

    d_xn_a = _mm2("mm_dxn_a", "nt", d_pa, w_a, F32, deps=[tok])
    d_xn_b = _mm2("mm_dxn_b", "nt", d_pb, w_b, F32)

    def x_bwd(x, da, db, dh, g0):
        dx, dg0 = _rms_bwd(x, g0, da + db)
        return dh + dx, dg0

    grad_x, dg_pre_mix = _ew("x_bwd", x_bwd, (T // tm_d,),
                             [_rows(x2, tm_d), _rows(d_xn_a, tm_d), _rows(d_xn_b, tm_d), _rows(dh, tm_d),
                              _vec(pre_mix_norm)],
                             [_rows_out((T, D), F32, tm_d)], [_vec_out((1, D))])

    def adam_big(n, parts):
        w3 = given[n]
        shp2 = w3.shape[1:]
        res = _adamw("adamw_" + n, parts, w3.reshape(shp2), given["m_" + n].reshape(shp2), given["v_" + n].reshape(shp2))
        return [r.reshape(w3.shape) for r in res]

    results = {}
    results["w_down"] = adam_big("w_down", _a2a_finish("a2a_d", x_d, grad_x)[0][:, :FS, :])
    results["w_gate"] = adam_big("w_gate", _a2a_finish("a2a_g", x_g, results["w_down"][0])[0][:, :, :FS])
    results["w_up"] = adam_big("w_up", _a2a_finish("a2a_u", x_u, results["w_gate"][0])[0][:, :, :FS])
    results["w_out"] = adam_big("w_out", _a2a_finish("a2a_o", x_o, results["w_up"][0])[0])
    e_in, e_uq, e_ukv = _a2a_finish("a2a_e", x_e, results["w_out"][0])
    results["w_in"] = adam_big("w_in", e_in)
    results["w_uq"] = adam_big("w_uq", e_uq)
    results["w_ukv"] = adam_big("w_ukv", e_ukv)

    small_names = ["pre_mix_norm", "q_norm", "kv_norm", "v_ln_gain", "v_ln_bias", "w_spatial", "b_spatial",
                   "attn_out_norm", "gmlp_out_norm", "post_mix_norm", "pre_ffn_norm", "post_ffn_norm"]
    small_grads = [dg_pre_mix, dg_q, dg_kv, dlng, dlnb, dws, dbs[:, 0, :], dg_attn, dg_gmlp, dg_post_mix, dg_pre_ffn,
                   dg_post_ffn]
    n_small = sum(g.size for g in small_grads)
    pad_small = _round_up(n_small, SUBLANE * LANE) - n_small

    def pack(arrs):
        flat = jnp.concatenate([a.reshape(-1) for a in arrs] + [jnp.zeros((pad_small,), F32)])
        return flat.reshape(-1, LANE)

    small_parts = _all_gather("ag_small_grads", pack(small_grads))
    sm = _adamw("adamw_small", small_parts, pack([given[n] for n in small_names]),
                pack([given["m_" + n] for n in small_names]), pack([given["v_" + n] for n in small_names]))

    off = 0
    for n in small_names:
        shp, size = given[n].shape, given[n].size
        results[n] = [r.reshape(-1)[off:off + size].reshape(shp) for r in sm]
        off += size

    loss = lax.psum(loss_part[0, 0], MESH_AXES)
    order = ["pre_mix_norm", "w_in", "q_norm", "kv_norm", "w_uq", "w_ukv", "v_ln_gain", "v_ln_bias", "w_spatial",
             "b_spatial", "attn_out_norm", "gmlp_out_norm", "w_out", "post_mix_norm", "pre_ffn_norm", "w_gate", "w_up",
             "w_down", "post_ffn_norm"]
    out = [loss, grad_x[None]]
    for kind in range(4):
        out += [results[n][kind] for n in order]
    return tuple(out)
```

```python
import math

import jax
import jax.numpy as jnp
from jax import lax
from jax.experimental import pallas as pl
from jax.experimental.pallas import tpu as pltpu

F32 = jnp.float32
BF16 = jnp.bfloat16
N_DEV = 8
MESH_AXES = ("x", "y", "c")
MESH = pl.DeviceIdType.MESH

NOPE_DIM = 128
ROPE_DIM = 64
V_DIM = 128
ROPE_THETA = 10000.0
CHUNK = 128
EPS = 1e-6
QK_HEAD = NOPE_DIM + ROPE_DIM
Q_EXT = 3 * 128
K_FULL = 2 * 128

ADAM_LR = 0.001
ADAM_B1 = 0.9
ADAM_B2 = 0.999
ADAM_EPS = 1e-08
ADAM_WD = 0.01
ADAM_STEP = 10

LANE = 128
SUBLANE = 8
VMEM_LIMIT = 56 * 1024 * 1024
MM_VMEM_BUDGET = 44 * 1024 * 1024
MXU_DIM = 256
FFN_TILE_MAX = 3072
MM_TK_MAX = 2048
EPILOGUE_ROWS = 256
ATTN_TQ = 512
ATTN_TKV = 512
NEG_BIG = -1e30
ROW_BLOCK_ELEMS = 512 * 1024
ADAM_BLOCK_ELEMS = 256 * 1024

DN = {
    "nn": (((1,), (0,)), ((), ())),
    "nt": (((1,), (1,)), ((), ())),
    "tn": (((0,), (0,)), ((), ())),
}


def _pick(dim, pref, mult):
    t = min(pref, dim)
    t -= t % mult
    while t >= mult:
        if dim % t == 0:
            return t
        t -= mult
    return dim


def _round_up(n, m):
    return (n + m - 1) // m * m


def _params(sem):
    return pltpu.CompilerParams(dimension_semantics=sem, vmem_limit_bytes=VMEM_LIMIT)


def _dep_specs(deps, nd):
    return [pl.BlockSpec(d.shape, lambda *_, r=d.ndim: (0,) * r) for d in deps]


def _mm(name, mode, pairs, grid, acc_shape, out_shape, out_dtype, out_block, out_map, deps=(), extras=(),
        epilogue=None, n_out=1):
    n, ne, nx = len(pairs), len(extras), len(deps)
    nk = grid[-1]
    nd = len(grid)
    dn = DN[mode]
    use_acc = nk > 1 or epilogue is not None

    def body(*refs):
        e_refs = refs[2 * n:2 * n + ne]
        o_refs = refs[2 * n + ne + nx:2 * n + ne + nx + n_out]
        k = pl.program_id(nd - 1)

        def dot(i):
            return lax.dot_general(refs[2 * i][...], refs[2 * i + 1][...], dn, preferred_element_type=F32)

        if not use_acc:
            s = dot(0)
            for i in range(1, n):
                s = s + dot(i)
            o_refs[0][...] = s.astype(o_refs[0].dtype)
            return
        acc = refs[2 * n + ne + nx + n_out]

        @pl.when(k == 0)
        def _():
            acc[...] = dot(0)

        if nk > 1:
            @pl.when(k > 0)
            def _():
                acc[...] += dot(0)

        for i in range(1, n):
            acc[...] += dot(i)

        if epilogue is None:
            @pl.when(k == nk - 1)
            def _():
                o_refs[0][...] = acc[...].astype(o_refs[0].dtype)
        else:
            @pl.when(k == nk - 1)
            def _():
                rows = acc_shape[0]
                slab = _pick(rows, EPILOGUE_ROWS, SUBLANE)
                for r0 in range(0, rows, slab):
                    sl = pl.ds(r0, slab)
                    vals = epilogue(acc[sl, :], *[e[sl, :] for e in e_refs])
                    for r, v in zip(o_refs, vals):
                        r[sl, :] = v.astype(r.dtype)

    in_specs, args = [], []
    for a, a_blk, a_map, b, b_blk, b_map in pairs:
        in_specs += [pl.BlockSpec(a_blk, a_map), pl.BlockSpec(b_blk, b_map)]
        args += [a, b]
    in_specs += [pl.BlockSpec(blk, imap) for (_, blk, imap) in extras]
    args += [e for (e, _, _) in extras]
    if not isinstance(out_dtype, (tuple, list)):
        out_dtype = (out_dtype,) * n_out
    res = pl.pallas_call(
        body,
        name=name,
        grid=grid,
        in_specs=in_specs + _dep_specs(deps, nd),
        out_specs=[pl.BlockSpec(out_block, out_map)] * n_out,
        out_shape=[jax.ShapeDtypeStruct(out_shape, dt) for dt in out_dtype],
        scratch_shapes=[pltpu.VMEM(acc_shape, F32)] if use_acc else [],
        compiler_params=_params(("parallel",) * (nd - 1) + ("arbitrary",)),
    )(*args, *deps)
    return res[0] if n_out == 1 else res


def _fit_tk(K, tm, tn, pairs=1, out_bytes=4, fixed_extra=0, tk_max=MM_TK_MAX):
    fixed = tm * tn * 4 + 2 * tm * tn * out_bytes + fixed_extra
    tk = _pick(K, tk_max, LANE)
    while tk > LANE and fixed + pairs * 4 * (tm + tn) * tk > MM_VMEM_BUDGET:
        smaller = _pick(K, tk - LANE, LANE)
        if smaller >= tk:
            break
        tk = smaller
    return tk


def _mm2(name, mode, a, b, out_dtype, tm_pref=1024, tn_pref=1024, deps=(), extras=(), epilogue=None, n_out=1,
         tk_max=MM_TK_MAX):
    if mode == "nn":
        (M, K), N = a.shape, b.shape[1]
    elif mode == "nt":
        (M, K), N = a.shape, b.shape[0]
    else:
        (K, M), N = a.shape, b.shape[1]
    tm, tn = _pick(M, tm_pref, LANE), _pick(N, tn_pref, LANE)
    extra_bytes = sum(2 * tm * tn * jnp.dtype(e.dtype).itemsize for e in extras)
    tk = _fit_tk(K, tm, tn, out_bytes=n_out * jnp.dtype(out_dtype).itemsize, fixed_extra=extra_bytes, tk_max=tk_max)
    if mode == "tn":
        a_blk, a_map = (tk, tm), lambda i, j, k: (k, i)
    else:
        a_blk, a_map = (tm, tk), lambda i, j, k: (i, k)
    if mode == "nt":
        b_blk, b_map = (tn, tk), lambda i, j, k: (j, k)
    else:
        b_blk, b_map = (tk, tn), lambda i, j, k: (k, j)
    return _mm(name, mode, [(a, a_blk, a_map, b, b_blk, b_map)], (M // tm, N // tn, K // tk), (tm, tn),
               (M, N), out_dtype, (tm, tn), lambda i, j, k: (i, j), deps=deps,
               extras=[(e, (tm, tn), lambda i, j, k: (i, j)) for e in extras], epilogue=epilogue, n_out=n_out)


def _ew(name, fn, grid, ins, outs, accs=(), deps=()):
    ni, no, na = len(ins), len(outs), len(accs)
    nd = len(grid)
    nx = len(deps)

    def body(*refs):
        vals = fn(*[r[...] for r in refs[:ni]])
        if not isinstance(vals, (tuple, list)):
            vals = (vals,)
        for r, v in zip(refs[ni + nx:ni + nx + no], vals[:no]):
            r[...] = v.astype(r.dtype)
        if na:
            first = pl.program_id(nd - 1) == 0
            for r, v in zip(refs[ni + nx + no:], vals[no:]):
                @pl.when(first)
                def _(r=r, v=v):
                    r[...] = v.astype(r.dtype)

                @pl.when(jnp.logical_not(first))
                def _(r=r, v=v):
                    r[...] += v.astype(r.dtype)

    all_outs = list(outs) + list(accs)
    sem = ("parallel",) * (nd - 1) + (("arbitrary",) if na else ("parallel",))
    res = pl.pallas_call(
        body,
        name=name,
        grid=grid,
        in_specs=[pl.BlockSpec(blk, imap) for (_, blk, imap) in ins] + _dep_specs(deps, nd),
        out_specs=[pl.BlockSpec(blk, imap) for (_, _, blk, imap) in all_outs],
        out_shape=[jax.ShapeDtypeStruct(s, d) for (s, d, _, _) in all_outs],
        compiler_params=_params(sem),
    )(*[a for (a, _, _) in ins], *deps)
    return res


def _rows(a, tm, width=None, cblk=0):
    width = a.shape[1] if width is None else width
    return (a, (tm, width), lambda i, c=cblk: (i, c))


def _vec(p):
    return (p, p.shape, lambda i: (0,) * p.ndim)


def _rows_out(shape, dtype, tm):
    return (shape, dtype, (tm, shape[1]), lambda i: (i, 0))


def _vec_out(shape):
    return (shape, F32, shape, lambda i: (0,) * len(shape))


def _row_tile(T, D):
    return _pick(T, max(SUBLANE, ROW_BLOCK_ELEMS // D), SUBLANE)


def _rms(x, g):
    r = lax.rsqrt(jnp.mean(x * x, axis=-1, keepdims=True) + EPS)
    return x * r * g


def _rms_bwd(x, g, dy):
    r = lax.rsqrt(jnp.mean(x * x, axis=-1, keepdims=True) + EPS)
    xh = x * r
    dg = jnp.sum(dy * xh, axis=0, keepdims=True)
    dxh = dy * g
    dx = r * (dxh - xh * jnp.mean(dxh * xh, axis=-1, keepdims=True))
    return dx, dg


GELU_C = math.sqrt(2.0 / math.pi)
GELU_A = 0.044715


def _gelu(x):
    return 0.5 * x * (1.0 + jnp.tanh(GELU_C * (x + GELU_A * x * x * x)))


def _gelu_grad(x):
    t = jnp.tanh(GELU_C * (x + GELU_A * x * x * x))
    return 0.5 * (1.0 + t) + 0.5 * x * (1.0 - t * t) * GELU_C * (1.0 + 3.0 * GELU_A * x * x)


def _rot(w, axis=-1):
    w1, w2 = jnp.split(w, 2, axis=axis)
    return jnp.concatenate([-w2, w1], axis=axis)


def _rot_t(dw, axis=-1):
    d1, d2 = jnp.split(dw, 2, axis=axis)
    return jnp.concatenate([d2, -d1], axis=axis)


def _coords():
    return lax.axis_index("x"), lax.axis_index("y"), lax.axis_index("c")


def _all_gather(name, shard):
    def body(x_ref, o_ref, send_sems, recv_sems, local_sem):
        xi, yi, ci = _coords()
        me, sibling = (xi, yi, ci), (xi, yi, 1 - ci)
        chips = [(1 - xi, yi), (xi, 1 - yi), (1 - xi, 1 - yi)]

        def slot(px, py, pc):
            return o_ref.at[4 * px + 2 * py + pc]

        def copy(k, block, to, src=None):
            return pltpu.make_async_remote_copy(
                src_ref=slot(*block) if src is None else src, dst_ref=slot(*block),
                send_sem=send_sems.at[k], recv_sem=recv_sems.at[k], device_id=to, device_id_type=MESH)

        mine = pltpu.make_async_copy(x_ref, slot(*me), local_sem)
        mine.start()
        first = [copy(0, me, sibling, src=x_ref)]
        first += [copy(1 + j, me, (*chip, ci), src=x_ref) for j, chip in enumerate(chips)]
        for cp in first:
            cp.start()
        passed = [copy(4 + j, (*chip, ci), sibling) for j, chip in enumerate(chips)]
        for j, chip in enumerate(chips):
            copy(1 + j, (*chip, ci), me).wait_recv()
            passed[j].start()
        copy(0, sibling, me).wait_recv()
        for j, chip in enumerate(chips):
            copy(4 + j, (*chip, 1 - ci), me).wait_recv()
        for cp in first + passed:
            cp.wait_send()
        mine.wait()

    return pl.pallas_call(
        body,
        name=name,
        out_shape=jax.ShapeDtypeStruct((N_DEV,) + shard.shape, shard.dtype),
        in_specs=[pl.BlockSpec(memory_space=pltpu.HBM)],
        out_specs=pl.BlockSpec(memory_space=pltpu.HBM),
        scratch_shapes=[pltpu.SemaphoreType.DMA((N_DEV - 1,)), pltpu.SemaphoreType.DMA((N_DEV - 1,)),
                        pltpu.SemaphoreType.DMA],
    )(shard)


HBM_SPEC = pl.BlockSpec(memory_space=pltpu.HBM)
SEM_SPEC = pl.BlockSpec(memory_space=pltpu.SEMAPHORE)
ANY_SPEC = pl.BlockSpec(memory_space=pl.ANY)
EFFECT = pltpu.SideEffectType.DATAFLOW_SIDE_EFFECTING
TOKEN = (SUBLANE, LANE)


def _my_slot():
    xi, yi, ci = _coords()
    return 4 * xi + 2 * yi + ci


def _own_slot_filled(block):
    land = lax.empty((N_DEV,) + block.shape, block.dtype)
    return lax.dynamic_update_slice(land, block[None], (_my_slot(),) + (0,) * block.ndim)


def _split_start(name, n, n_copies, issue, srcs, lands, deps):
    def body(*refs):
        src_refs, land_refs = refs[:n], refs[n:2 * n]
        outs = refs[2 * n + len(deps):]
        send, recv, token = outs[:n], outs[n:2 * n], outs[4 * n]
        for t in range(n):
            issue(t, src_refs[t], land_refs[t], send[t], recv[t])
        token[...] = jnp.zeros_like(token)

    sems = [pltpu.SemaphoreType.DMA((n_copies,))] * (2 * n)
    res = pl.pallas_call(
        body,
        name=name,
        in_specs=[HBM_SPEC] * (2 * n) + [ANY_SPEC] * len(deps),
        out_specs=[SEM_SPEC] * (2 * n) + [HBM_SPEC] * (2 * n) + [pl.BlockSpec(memory_space=pltpu.VMEM)],
        out_shape=sems + [pltpu.HBM(a.shape, a.dtype) for a in list(srcs) + list(lands)]
        + [jax.ShapeDtypeStruct(TOKEN, F32)],
        input_output_aliases={i: 2 * n + i for i in range(2 * n)},
        compiler_params=pltpu.CompilerParams(has_side_effects=EFFECT),
    )(*[pltpu.with_memory_space_constraint(a, pltpu.HBM) for a in list(srcs) + list(lands)], *deps)
    state = dict(n=n, n_copies=n_copies, send=res[:n], recv=res[n:2 * n], srcs=res[2 * n:3 * n],
                 lands=res[3 * n:4 * n])
    return state, res[4 * n]


def _split_wait(name, state, unit_src, after):
    n, n_copies = state["n"], state["n_copies"]

    def body(*refs):
        src_refs, land_refs = refs[:n], refs[n:2 * n]
        send, recv = refs[2 * n:3 * n], refs[3 * n:4 * n]
        me = _coords()
        for t in range(n):
            for k in range(n_copies):
                cp = pltpu.make_async_remote_copy(
                    src_ref=unit_src(src_refs[t]), dst_ref=land_refs[t].at[0], send_sem=send[t].at[k],
                    recv_sem=recv[t].at[k], device_id=me, device_id_type=MESH)
                cp.wait_send()
                cp.wait_recv()

    bufs = list(state["srcs"]) + list(state["lands"])
    res = pl.pallas_call(
        body,
        name=name,
        in_specs=[HBM_SPEC] * (2 * n) + [SEM_SPEC] * (2 * n) + [ANY_SPEC],
        out_specs=[HBM_SPEC] * (2 * n),
        out_shape=[pltpu.HBM(a.shape, a.dtype) for a in bufs],
        input_output_aliases={i: i for i in range(2 * n)},
        compiler_params=pltpu.CompilerParams(has_side_effects=EFFECT),
    )(*bufs, *state["send"], *state["recv"], after)
    return list(res[n:])


def _ag_start(name, shards, deps):
    def issue(t, x_ref, land_ref, send, recv):
        xi, yi, ci = _coords()
        to = [(xi, yi, 1 - ci), (1 - xi, yi, ci), (xi, 1 - yi, ci), (1 - xi, 1 - yi, ci)]
        for k, dev in enumerate(to):
            pltpu.make_async_remote_copy(src_ref=x_ref, dst_ref=land_ref.at[4 * xi + 2 * yi + ci], send_sem=send.at[k],
                                         recv_sem=recv.at[k], device_id=dev, device_id_type=MESH).start()

    return _split_start(name, len(shards), 4, issue, shards, [_own_slot_filled(s) for s in shards], deps)


def _ag_mid(name, state, after):
    lands = _split_wait(name + "_w1", state, lambda r: r, after)

    def issue(t, src_ref, land_ref, send, recv):
        xi, yi, ci = _coords()
        for k, (px, py) in enumerate([(1 - xi, yi), (xi, 1 - yi), (1 - xi, 1 - yi)]):
            rows = land_ref.at[4 * px + 2 * py + ci]
            pltpu.make_async_remote_copy(src_ref=rows, dst_ref=rows, send_sem=send.at[k], recv_sem=recv.at[k],
                                         device_id=(xi, yi, 1 - ci), device_id_type=MESH).start()

    n = len(lands)

    def body(*refs):
        land_refs = refs[:n]
        outs = refs[n:]
        send, recv, token = outs[:n], outs[n:2 * n], outs[3 * n]
        for t in range(n):
            issue(t, land_refs[t], land_refs[t], send[t], recv[t])
        token[...] = jnp.zeros_like(token)

    res = pl.pallas_call(
        body,
        name=name + "_s2",
        in_specs=[HBM_SPEC] * n,
        out_specs=[SEM_SPEC] * (2 * n) + [HBM_SPEC] * n + [pl.BlockSpec(memory_space=pltpu.VMEM)],
        out_shape=[pltpu.SemaphoreType.DMA((3,))] * (2 * n) + [pltpu.HBM(a.shape, a.dtype) for a in lands]
        + [jax.ShapeDtypeStruct(TOKEN, F32)],
        input_output_aliases={i: 2 * n + i for i in range(n)},
        compiler_params=pltpu.CompilerParams(has_side_effects=EFFECT),
    )(*lands)
    state2 = dict(n=n, send=res[:n], recv=res[n:2 * n], lands=res[2 * n:3 * n])
    return state2, res[3 * n]


def _ag_finish(name, state2, after):
    n = state2["n"]

    def body(*refs):
        land_refs, send, recv = refs[:n], refs[n:2 * n], refs[2 * n:3 * n]
        me = _coords()
        for t in range(n):
            for k in range(3):
                cp = pltpu.make_async_remote_copy(
                    src_ref=land_refs[t].at[0], dst_ref=land_refs[t].at[0], send_sem=send[t].at[k],
                    recv_sem=recv[t].at[k], device_id=me, device_id_type=MESH)
                cp.wait_send()
                cp.wait_recv()

    res = pl.pallas_call(
        body,
        name=name + "_w2",
        in_specs=[HBM_SPEC] * n + [SEM_SPEC] * (2 * n) + [ANY_SPEC],
        out_specs=[HBM_SPEC] * n,
        out_shape=[pltpu.HBM(a.shape, a.dtype) for a in state2["lands"]],
        input_output_aliases={i: i for i in range(n)},
        compiler_params=pltpu.CompilerParams(has_side_effects=EFFECT),
    )(*state2["lands"], *state2["send"], *state2["recv"], after)
    return list(res)


def _a2a_start(name, parts, deps):
    def issue(t, x_ref, land_ref, send, recv):
        xi, yi, ci = _coords()
        for k in range(N_DEV - 1):
            px = 1 - xi if (k + 1) & 4 else xi
            py = 1 - yi if (k + 1) & 2 else yi
            pc = 1 - ci if (k + 1) & 1 else ci
            pltpu.make_async_remote_copy(src_ref=x_ref.at[4 * px + 2 * py + pc], dst_ref=land_ref.at[4 * xi + 2 * yi + ci],
                                         send_sem=send.at[k], recv_sem=recv.at[k], device_id=(px, py, pc),
                                         device_id_type=MESH).start()

    lands = [_own_slot_filled(lax.dynamic_index_in_dim(p, _my_slot(), 0, keepdims=False)) for p in parts]
    return _split_start(name, len(parts), N_DEV - 1, issue, parts, lands, deps)


def _a2a_finish(name, state, after):
    return _split_wait(name + "_w", state, lambda r: r.at[0], after)


QK_SCALE = 1.0 / math.sqrt(QK_HEAD)
LOG2E = 1.0 / math.log(2.0)


def _roped_q(q_ext, cos, sin):
    q = jnp.concatenate([q_ext[:, :LANE], q_ext[:, LANE:2 * LANE] * cos + q_ext[:, 2 * LANE:] * sin], axis=1)
    return (q * (QK_SCALE * LOG2E)).astype(BF16)


def _attn_fwd(q_ext, k_full, v_aug, cosp, sinp):
    H, T, _ = k_full.shape
    tq, tkv = _pick(T, ATTN_TQ, LANE), _pick(T, ATTN_TKV, LANE)

    def body(q_ref, cos_ref, sin_ref, k_ref, v_ref, o_ref, lse_ref):
        q = _roped_q(q_ref[...], cos_ref[...], sin_ref[...])
        m = jnp.full((tq, 1), NEG_BIG, F32)
        acc = jnp.zeros((tq, 2 * V_DIM), F32)
        for c in range(T // tkv):
            keys = pl.ds(c * tkv, tkv)
            s = lax.dot_general(q, k_ref[keys, :], DN["nt"], preferred_element_type=F32)
            m_new = jnp.maximum(m, jnp.max(s, axis=-1, keepdims=True))
            p = jnp.exp2(s - m_new).astype(BF16)
            acc = jnp.exp2(m - m_new) * acc + jnp.dot(p, v_ref[keys, :], preferred_element_type=F32)
            m = m_new
        l = acc[:, V_DIM:]
        o_ref[...] = acc[:, :V_DIM] / l
        lse_ref[...] = m + jnp.log2(jnp.max(l, axis=-1, keepdims=True))

    return pl.pallas_call(
        body,
        name="attn_fwd",
        grid=(H, T // tq),
        in_specs=[pl.BlockSpec((tq, Q_EXT), lambda h, i: (i, h)),
                  pl.BlockSpec((tq, LANE), lambda h, i: (i, 0)),
                  pl.BlockSpec((tq, LANE), lambda h, i: (i, 0)),
                  pl.BlockSpec((None, T, K_FULL), lambda h, i: (h, 0, 0)),
                  pl.BlockSpec((None, T, 2 * V_DIM), lambda h, i: (h, 0, 0))],
        out_specs=[pl.BlockSpec((tq, V_DIM), lambda h, i: (i, h)),
                   pl.BlockSpec((None, tq, 1), lambda h, i: (h, i, 0))],
        out_shape=[jax.ShapeDtypeStruct((T, H * V_DIM), F32), jax.ShapeDtypeStruct((H, T, 1), F32)],
        compiler_params=_params(("parallel", "parallel")),
    )(q_ext, cosp, sinp, k_full, v_aug)


def _attn_bwd(q_ext, k_full, v_aug, cosp, sinp, d_out, out, lse_rows):
    H, T, _ = k_full.shape
    tq, tkv = _pick(T, ATTN_TQ, LANE), _pick(T, ATTN_TKV, LANE)
    n_q = T // tq

    def body(q_ref, cos_ref, sin_ref, k_ref, v_ref, do_ref, o_ref, lse_ref, cos_all, sin_all,
             dq_ref, dkv_ref, dkr_ref, dk_acc, dv_acc):
        h, i = pl.program_id(0), pl.program_id(1)

        @pl.when(i == 0)
        def _():
            dk_acc[...] = jnp.zeros_like(dk_acc)
            dv_acc[...] = jnp.zeros_like(dv_acc)

        cos, sin = cos_ref[...], sin_ref[...]
        q, do, lse = _roped_q(q_ref[...], cos, sin), do_ref[...], lse_ref[...]
        delta = lax.dot_general(jnp.ones((SUBLANE, V_DIM), F32), do.astype(F32) * o_ref[...], DN["nt"],
                                preferred_element_type=F32, precision=lax.Precision.HIGHEST)[0:1]
        dq = jnp.zeros((tq, K_FULL), F32)
        for c in range(T // tkv):
            keys = pl.ds(c * tkv, tkv)
            kc = k_ref[keys, :]
            pt = jnp.exp2(lax.dot_general(kc, q, DN["nt"], preferred_element_type=F32) - lse)
            dpt = lax.dot_general(v_ref[keys, pl.ds(0, V_DIM)], do, DN["nt"], preferred_element_type=F32)
            dst = (pt * (dpt - delta)).astype(BF16)
            dv_acc[keys, :] += jnp.dot(pt.astype(BF16), do, preferred_element_type=F32)
            dk_acc[keys, :] += jnp.dot(dst, q, preferred_element_type=F32)
            dq = dq + lax.dot_general(dst, kc, DN["tn"], preferred_element_type=F32)
        r = dq[:, LANE:]
        dq_ref[...] = (jnp.concatenate([dq[:, :LANE], r * cos, r * sin], axis=1) * QK_SCALE).astype(BF16)

        @pl.when(i == n_q - 1)
        def _():
            dk = dk_acc[...] * (1.0 / LOG2E)
            dkv_ref[...] = jnp.concatenate([dk[:, :LANE], dv_acc[...]], axis=1).astype(BF16)
            rk = dk[:, LANE:]
            dkr = jnp.concatenate([rk * cos_all[...], rk * sin_all[...]], axis=1)

            @pl.when(h == 0)
            def _():
                dkr_ref[...] = dkr

            @pl.when(h != 0)
            def _():
                dkr_ref[...] += dkr

    return pl.pallas_call(
        body,
        name="attn_bwd",
        grid=(H, n_q),
        in_specs=[pl.BlockSpec((tq, Q_EXT), lambda h, i: (i, h)),
                  pl.BlockSpec((tq, LANE), lambda h, i: (i, 0)),
                  pl.BlockSpec((tq, LANE), lambda h, i: (i, 0)),
                  pl.BlockSpec((None, T, K_FULL), lambda h, i: (h, 0, 0)),
                  pl.BlockSpec((None, T, 2 * V_DIM), lambda h, i: (h, 0, 0)),
                  pl.BlockSpec((tq, V_DIM), lambda h, i: (i, h)),
                  pl.BlockSpec((tq, V_DIM), lambda h, i: (i, h)),
                  pl.BlockSpec((None, 1, tq), lambda h, i: (h, 0, i)),
                  pl.BlockSpec((T, LANE), lambda h, i: (0, 0)),
                  pl.BlockSpec((T, LANE), lambda h, i: (0, 0))],
        out_specs=[pl.BlockSpec((tq, Q_EXT), lambda h, i: (i, h)),
                   pl.BlockSpec((T, K_FULL), lambda h, i: (0, h)),
                   pl.BlockSpec((T, K_FULL), lambda h, i: (0, 0))],
        out_shape=[jax.ShapeDtypeStruct((T, H * Q_EXT), BF16), jax.ShapeDtypeStruct((T, H * K_FULL), BF16),
                   jax.ShapeDtypeStruct((T, K_FULL), F32)],
        scratch_shapes=[pltpu.VMEM((T, K_FULL), F32), pltpu.VMEM((T, V_DIM), F32)],
        compiler_params=_params(("arbitrary", "arbitrary")),
    )(q_ext, cosp, sinp, k_full, v_aug, d_out, out, lse_rows, cosp, sinp)


def _sgu_forward_math(pb, lng, lnb, ws, bsb, gw, gh):
    u_raw, v_raw = pb[:, :gw], pb[:, gw:]
    u, v = _gelu(u_raw), _gelu(v_raw)
    vc = v - jnp.mean(v, axis=-1, keepdims=True)
    rstd = lax.rsqrt(jnp.mean(vc * vc, axis=-1, keepdims=True) + EPS)
    vhat = vc * rstd
    vln = vhat * lng + lnb
    hd = gw // gh
    s = jnp.concatenate(
        [jnp.dot(ws[g], vln[:, g * hd:(g + 1) * hd].astype(BF16), preferred_element_type=F32) + bsb[g]
         for g in range(gh)], axis=1)
    return u_raw, v_raw, u, rstd, vhat, vln, s


def _sgu_fwd(proj_b, lng, lnb, ws, bsb, gg):
    T, gw2 = proj_b.shape
    gw, gh = gw2 // 2, ws.shape[0]

    def fn(pb, lng, lnb, ws, bsb, gg):
        _, _, u, _, _, _, s = _sgu_forward_math(pb, lng, lnb, ws, bsb, gw, gh)
        return _rms(u * s, gg)

    return _ew("sgu_fwd", fn, (T // CHUNK,),
               [_rows(proj_b, CHUNK), _vec(lng), _vec(lnb), _vec(ws), _vec(bsb), _vec(gg)],
               [_rows_out((T, gw), BF16, CHUNK)])[0]


def _sgu_bwd(proj_b, d_mixed, lng, lnb, ws, ws_t, bsb, gg):
    T, gw2 = proj_b.shape
    gw, gh = gw2 // 2, ws.shape[0]
    hd = gw // gh

    def fn(pb, dmg, lng, lnb, ws, ws_t, bsb, gg):
        u_raw, v_raw, u, rstd, vhat, vln, s = _sgu_forward_math(pb, lng, lnb, ws, bsb, gw, gh)
        dgo, dgg = _rms_bwd(u * s, gg, dmg)
        du, ds = dgo * s, dgo * u
        ones = jnp.ones((SUBLANE, hd), F32)
        dws, dbs, dvln = [], [], []
        for g in range(gh):
            ds_g = ds[:, g * hd:(g + 1) * hd]
            ds_gb = ds_g.astype(BF16)
            dws.append(lax.dot_general(ds_gb, vln[:, g * hd:(g + 1) * hd].astype(BF16), DN["nt"],
                                       preferred_element_type=F32)[None])
            dbs.append(lax.dot_general(ones, ds_g, DN["nt"], preferred_element_type=F32,
                                       precision=lax.Precision.HIGHEST)[None])
            dvln.append(jnp.dot(ws_t[g], ds_gb, preferred_element_type=F32))
        dvln = jnp.concatenate(dvln, axis=1)
        dlng = jnp.sum(dvln * vhat, axis=0, keepdims=True)
        dlnb = jnp.sum(dvln, axis=0, keepdims=True)
        dvh = dvln * lng
        dv = rstd * (dvh - jnp.mean(dvh, axis=-1, keepdims=True) - vhat * jnp.mean(dvh * vhat, axis=-1, keepdims=True))
        d_pb = jnp.concatenate([du * _gelu_grad(u_raw), dv * _gelu_grad(v_raw)], axis=1)
        return d_pb, dgg, jnp.concatenate(dws, axis=0), jnp.concatenate(dbs, axis=0), dlng, dlnb

    return _ew("sgu_bwd", fn, (T // CHUNK,),
               [_rows(proj_b, CHUNK), _rows(d_mixed, CHUNK, gw, 1), _vec(lng), _vec(lnb), _vec(ws), _vec(ws_t),
                _vec(bsb), _vec(gg)],
               [_rows_out((T, gw2), BF16, CHUNK)],
               [_vec_out((1, gw)), _vec_out((gh, CHUNK, CHUNK)), _vec_out((gh, SUBLANE, CHUNK)), _vec_out((1, gw)),
                _vec_out((1, gw))])


def _adamw(name, parts, w, m, v):
    R, C = w.shape
    pack = SUBLANE * (4 // jnp.dtype(parts.dtype).itemsize)
    if R % pack == 0:
        tm, tc = _pick(R, max(pack, ADAM_BLOCK_ELEMS // C), pack), C
    else:
        parts = parts[:, :R]
        tm, tc = R, _pick(C, max(LANE, ADAM_BLOCK_ELEMS // R // LANE * LANE), LANE)

    def fn(p, w, m, v):
        g = p[0].astype(F32)
        for i in range(1, N_DEV):
            g = g + p[i].astype(F32)
        m2 = ADAM_B1 * m + (1.0 - ADAM_B1) * g
        v2 = ADAM_B2 * v + (1.0 - ADAM_B2) * (g * g)
        m_hat = m2 / (1.0 - ADAM_B1 ** ADAM_STEP)
        v_hat = v2 / (1.0 - ADAM_B2 ** ADAM_STEP)
        delta = -ADAM_LR * (m_hat / (jnp.sqrt(v_hat) + ADAM_EPS) + ADAM_WD * w)
        return g, delta, m2, v2

    blk = lambda a: (a, (tm, tc), lambda i, j: (i, j))
    return _ew(name, fn, (R // tm, C // tc),
               [(parts, (N_DEV, tm, tc), lambda i, j: (0, i, j)), blk(w), blk(m), blk(v)],
               [((R, C), F32, (tm, tc), lambda i, j: (i, j))] * 4)


def kernel(x, positions, pre_mix_norm, w_in, q_norm, kv_norm, w_uq, w_ukv, v_ln_gain, v_ln_bias, w_spatial, b_spatial, attn_out_norm, gmlp_out_norm, w_out, post_mix_norm, pre_ffn_norm, w_gate, w_up, w_down, post_ffn_norm, loss_target, m_pre_mix_norm, m_w_in, m_q_norm, m_kv_norm, m_w_uq, m_w_ukv, m_v_ln_gain, m_v_ln_bias, m_w_spatial, m_b_spatial, m_attn_out_norm, m_gmlp_out_norm, m_w_out, m_post_mix_norm, m_pre_ffn_norm, m_w_gate, m_w_up, m_w_down, m_post_ffn_norm, v_pre_mix_norm, v_w_in, v_q_norm, v_kv_norm, v_w_uq, v_w_ukv, v_v_ln_gain, v_v_ln_bias, v_w_spatial, v_b_spatial, v_attn_out_norm, v_gmlp_out_norm, v_w_out, v_post_mix_norm, v_pre_ffn_norm, v_w_gate, v_w_up, v_w_down, v_post_ffn_norm):
    given = dict(locals())
    T, D = x.shape[1], x.shape[2]
    QL, KVL = w_uq.shape[1], w_ukv.shape[1]
    GW, GH = v_ln_gain.shape[1], w_spatial.shape[1]
    IWS = w_in.shape[2]
    IW = IWS * N_DEV
    H = w_ukv.shape[2] * N_DEV // (NOPE_DIM + V_DIM)
    AW = H * V_DIM
    FS = w_gate.shape[2]
    FP = _round_up(FS, LANE)
    C0 = QL + KVL
    assert IW == C0 + ROPE_DIM + 2 * GW and AW + GW == D and AW == GW
    assert QL % KVL == 0 and C0 % K_FULL == 0 and w_spatial.shape[2] == CHUNK and T % CHUNK == 0
    assert w_uq.shape[2] * N_DEV == H * QK_HEAD and D % N_DEV == 0

    x2, tgt = x[0], loss_target[0]
    tm_d = _row_tile(T, D)

    inv_freq = 1.0 / (ROPE_THETA ** (jnp.arange(0, ROPE_DIM, 2, dtype=F32) / ROPE_DIM))
    ang = positions[0].astype(F32)[:, None] * inv_freq
    zpad = jnp.zeros((T, LANE - ROPE_DIM), F32)
    cosp = jnp.concatenate([jnp.cos(ang), jnp.cos(ang), zpad], axis=1)
    sinp = jnp.concatenate([jnp.sin(ang), jnp.sin(ang), zpad], axis=1)

    def shard_t(a):
        return a[0].T

    def padded_rows(a):
        return jnp.pad(a.astype(BF16), ((0, FP - FS), (0, 0)))

    st_a, tok = _ag_start("ag_a", [shard_t(w_in).astype(BF16), w_uq[0].astype(BF16), w_ukv[0].astype(BF16)], [])
    st_o, tok = _ag_start("ag_o", [w_out[0].astype(BF16)], [tok])
    st_gu, tok = _ag_start("ag_gu", [padded_rows(shard_t(w_gate)), padded_rows(shard_t(w_up))], [tok])
    st_d, tok = _ag_start("ag_d", [padded_rows(w_down[0])], [tok])

    xn = _ew("rms_x", lambda x, g: _rms(x, g), (T // tm_d,), [_rows(x2, tm_d), _vec(pre_mix_norm)],
             [_rows_out((T, D), BF16, tm_d)], deps=[tok])[0]
    st_a, tok = _ag_mid("ag_a", st_a, xn)
    g_in, g_uq, g_ukv = _ag_finish("ag_a", st_a, tok)

    w_in_t = g_in.reshape(IW, D)
    kr_w = w_in_t[C0:C0 + ROPE_DIM]
    z64 = jnp.zeros((LANE - ROPE_DIM, D), BF16)
    w_a_t = jnp.concatenate([w_in_t[:C0], kr_w, z64, _rot(kr_w, 0), z64], axis=0)
    w_b_t = w_in_t[C0 + ROPE_DIM:]
    wq = g_uq.transpose(1, 0, 2).reshape(QL, H, QK_HEAD)
    zq = jnp.zeros((QL, H, LANE - ROPE_DIM), BF16)
    wq_ext = jnp.concatenate([wq[..., :NOPE_DIM], wq[..., NOPE_DIM:], zq, _rot(wq[..., NOPE_DIM:]), zq],
                             axis=-1).reshape(QL, H * Q_EXT)
    wkv = g_ukv.transpose(1, 0, 2).reshape(KVL, H * (NOPE_DIM + V_DIM))

    ws = w_spatial[0].astype(BF16)
    ws_t = jnp.swapaxes(ws, 1, 2)
    bsb = jnp.broadcast_to(b_spatial[0][:, :, None], (GH, CHUNK, GW // GH))

    proj_a = _mm2("mm_proj_a", "nt", xn, w_a_t, F32, tn_pref=w_a_t.shape[0])
    proj_b = _mm2("mm_proj_b", "nt", xn, w_b_t, F32)

    tm_q = _row_tile(T, QL)
    qn = _ew("rms_q", lambda x, g: _rms(x, g), (T // tm_q,), [_rows(proj_a, tm_q, QL, 0), _vec(q_norm)],
             [_rows_out((T, QL), BF16, tm_q)])[0]
    kvn = _ew("rms_kv", lambda x, g: _rms(x, g), (T // tm_q,), [_rows(proj_a, tm_q, KVL, QL // KVL), _vec(kv_norm)],
              [_rows_out((T, KVL), BF16, tm_q)])[0]
    q_ext = _mm2("mm_q", "nn", qn, wq_ext, F32)
    kv = _mm2("mm_kv", "nn", kvn, wkv, F32)

    tm_h = _pick(T, 512, SUBLANE)

    def make_kv(kv, kr, c, s):
        k = jnp.concatenate([kv[:, :LANE], kr[:, :LANE] * c + kr[:, LANE:] * s], axis=1)
        return k, jnp.concatenate([kv[:, LANE:], jnp.ones_like(kv[:, LANE:])], axis=1)

    k_full, v_aug = _ew("make_kv", make_kv, (T // tm_h, H),
                        [(kv, (tm_h, K_FULL), lambda i, h: (i, h)),
                         (proj_a, (tm_h, K_FULL), lambda i, h: (i, C0 // K_FULL)),
                         (cosp, (tm_h, LANE), lambda i, h: (i, 0)), (sinp, (tm_h, LANE), lambda i, h: (i, 0))],
                        [((H, T, K_FULL), BF16, (None, tm_h, K_FULL), lambda i, h: (h, i, 0)),
                         ((H, T, 2 * V_DIM), BF16, (None, tm_h, 2 * V_DIM), lambda i, h: (h, i, 0))])

    a_out, lse = _attn_fwd(q_ext, k_full, v_aug, cosp, sinp)
    st_o, tok_o = _ag_mid("ag_o", st_o, a_out)
    st_gu, tok_gu = _ag_mid("ag_gu", st_gu, tok_o)
    tm_a = _row_tile(T, AW)
    mixed_a = _ew("rms_attn", lambda x, g: _rms(x, g), (T // tm_a,), [_rows(a_out, tm_a), _vec(attn_out_norm)],
                  [_rows_out((T, AW), BF16, tm_a)], deps=[tok_gu])[0]
    mixed_g = _sgu_fwd(proj_b, v_ln_gain, v_ln_bias, ws, bsb, gmlp_out_norm)
    wo = _ag_finish("ag_o", st_o, mixed_g)[0].reshape(D, D)

    tm_, tn_ = _pick(T, 1024, LANE), _pick(D, 1024, LANE)
    tk_ = _fit_tk(AW, tm_, tn_, pairs=2)
    nka = AW // tk_
    mix_out = _mm("mm_out", "nn",
                  [(mixed_a, (tm_, tk_), lambda i, j, k: (i, k), wo, (tk_, tn_), lambda i, j, k: (k, j)),
                   (mixed_g, (tm_, tk_), lambda i, j, k: (i, k), wo, (tk_, tn_), lambda i, j, k: (k + nka, j))],
                  (T // tm_, D // tn_, nka), (tm_, tn_), (T, D), F32, (tm_, tn_), lambda i, j, k: (i, j))

    def post_mix(x, mo, g3, g4):
        h = x + _rms(mo, g3)
        return h, _rms(h, g4)

    h_res, hn = _ew("post_mix", post_mix, (T // tm_d,),
                    [_rows(x2, tm_d), _rows(mix_out, tm_d), _vec(post_mix_norm), _vec(pre_ffn_norm)],
                    [_rows_out((T, D), F32, tm_d), _rows_out((T, D), BF16, tm_d)])

    F = N_DEV * FP
    tn_f = _pick(F, FFN_TILE_MAX, MXU_DIM)
    tm_e = _pick(T, 512, LANE)

    def swiglu(u, g):
        g = g.astype(F32)
        return u, g * jax.nn.sigmoid(g) * u

    wg_t, wu_t = [g.reshape(F, D) for g in _ag_finish("ag_gu", st_gu, hn)]
    gate = _mm2("mm_gate", "nt", hn, wg_t, BF16, tn_pref=tn_f)
    st_d, tok_d = _ag_mid("ag_d", st_d, gate)
    up, act = _mm2("mm_up", "nt", hn, wu_t, BF16, tm_pref=tm_e, tn_pref=tn_f, deps=[tok_d], extras=[gate],
                   epilogue=swiglu, n_out=2)
    wd_f = _ag_finish("ag_d", st_d, act)[0].reshape(F, D)
    ffn = _mm2("mm_down", "nn", act, wd_f, F32, tk_max=tn_f)

    def loss_head(h, f, t, g5):
        y = h + _rms(f, g5)
        diff = y - t
        dy = diff / D
        d_f, dg5 = _rms_bwd(f, g5, dy)
        lsum = jnp.sum(jnp.sum(diff * diff, axis=0, keepdims=True), axis=1, keepdims=True) * (0.5 / D)
        return dy, d_f, dg5, lsum

    dy, d_ffn, dg_post_ffn, loss_part = _ew(
        "loss_head", loss_head, (T // tm_d,),
        [_rows(h_res, tm_d), _rows(ffn, tm_d), _rows(tgt, tm_d), _vec(post_ffn_norm)],
        [_rows_out((T, D), F32, tm_d), _rows_out((T, D), BF16, tm_d)], [_vec_out((1, D)), _vec_out((1, 1))])

    def swiglu_bwd(da, g, u):
        g, u = g.astype(F32), u.astype(F32)
        sg = jax.nn.sigmoid(g)
        return da * u * sg * (1.0 + g * (1.0 - sg)), da * g * sg

    d_gate, d_up = _mm2("mm_dact", "nt", d_ffn, wd_f, BF16, tm_pref=tm_e, tn_pref=tn_f, extras=[gate, up],
                        epilogue=swiglu_bwd, n_out=2)

    def ffn_dw(name, z, y, deps):
        return _mm2(name, "tn", z, y, BF16, tm_pref=tn_f, deps=deps).reshape(N_DEV, FP, D)

    x_d, tok = _a2a_start("a2a_d", [ffn_dw("mm_dwd", act, d_ffn, [])], [])
    x_g, tok = _a2a_start("a2a_g", [ffn_dw("mm_dwg", d_gate, hn, [tok])], [])
    x_u, tok = _a2a_start("a2a_u", [ffn_dw("mm_dwu", d_up, hn, [tok])], [])
    d_hn_g = _mm2("mm_dhn_g", "nn", d_gate, wg_t, F32, tk_max=tn_f, deps=[tok])
    d_hn = _mm2("mm_dhn_u", "nn", d_up, wu_t, F32, tk_max=tn_f, extras=[d_hn_g], epilogue=lambda s, g: (s + g,))

    def pre_ffn_bwd(h, dhn, dy, mo, g4, g3):
        d1, dg4 = _rms_bwd(h, g4, dhn)
        dh = dy + d1
        dmo, dg3 = _rms_bwd(mo, g3, dh)
        return dh, dmo, dg4, dg3

    dh, d_mo, dg_pre_ffn, dg_post_mix = _ew(
        "pre_ffn_bwd", pre_ffn_bwd, (T // tm_d,),
        [_rows(h_res, tm_d), _rows(d_hn, tm_d), _rows(dy, tm_d), _rows(mix_out, tm_d), _vec(pre_ffn_norm),
         _vec(post_mix_norm)],
        [_rows_out((T, D), F32, tm_d), _rows_out((T, D), BF16, tm_d)], [_vec_out((1, D)), _vec_out((1, D))])

    d_mixed = _mm2("mm_dmixed", "nt", d_mo, wo, F32)
    dwo = jnp.concatenate([_mm2("mm_dwo_a", "tn", mixed_a, d_mo, BF16), _mm2("mm_dwo_g", "tn", mixed_g, d_mo, BF16)],
                          axis=0)
    x_o, tok = _a2a_start("a2a_o", [dwo.reshape(N_DEV, D // N_DEV, D)], [])

    d_a_out, dg_attn = _ew("rms_attn_bwd", lambda a, g, d: _rms_bwd(a, g, d), (T // tm_a,),
                           [_rows(a_out, tm_a), _vec(attn_out_norm), _rows(d_mixed, tm_a, AW, 0)],
                           [_rows_out((T, AW), BF16, tm_a)], [_vec_out((1, AW))], deps=[tok])
    d_pb, dg_gmlp, dws, dbs, dlng, dlnb = _sgu_bwd(proj_b, d_mixed, v_ln_gain, v_ln_bias, ws, ws_t, bsb, gmlp_out_norm)

    dq_ext, d_kv, d_kr = _attn_bwd(q_ext, k_full, v_aug, cosp, sinp, d_a_out, a_out, lse.reshape(H, 1, T))

    d_qn = _mm2("mm_dqn", "nt", dq_ext, wq_ext, F32)
    dwq_ext = _mm2("mm_dwq", "tn", qn, dq_ext, F32)
    d_kvn = _mm2("mm_dkvn", "nt", d_kv, wkv, F32)
    dwkv = _mm2("mm_dwkv", "tn", kvn, d_kv, BF16)

    d_qc, dg_q = _ew("rms_q_bwd", lambda a, g, d: _rms_bwd(a, g, d), (T // tm_q,),
                     [_rows(proj_a, tm_q, QL, 0), _vec(q_norm), _rows(d_qn, tm_q)],
                     [_rows_out((T, QL), BF16, tm_q)], [_vec_out((1, QL))])
    d_kvc, dg_kv = _ew("rms_kv_bwd", lambda a, g, d: _rms_bwd(a, g, d), (T // tm_q,),
                       [_rows(proj_a, tm_q, KVL, QL // KVL), _vec(kv_norm), _rows(d_kvn, tm_q)],
                       [_rows_out((T, KVL), BF16, tm_q)], [_vec_out((1, KVL))])
    d_pa = jnp.concatenate([d_qc, d_kvc, d_kr.astype(BF16)], axis=1)

    dwa_t = _mm2("mm_dwa", "tn", d_pa, xn, F32, tm_pref=w_a_t.shape[0])
    dwb_t = _mm2("mm_dwb", "tn", d_pb, xn, BF16)
    dkr_w = dwa_t[C0:C0 + ROPE_DIM] + _rot_t(dwa_t[C0 + LANE:C0 + LANE + ROPE_DIM], 0)
    dw_in = jnp.concatenate([dwa_t[:C0].astype(BF16), dkr_w.astype(BF16), dwb_t], axis=0)
    dw_in = dw_in.reshape(N_DEV, IWS, D)
    dq3 = dwq_ext.reshape(QL, H, Q_EXT)
    dwq = jnp.concatenate([dq3[..., :NOPE_DIM],
                           dq3[..., LANE:LANE + ROPE_DIM] + _rot_t(dq3[..., 2 * LANE:2 * LANE + ROPE_DIM])], axis=-1)
    dwq = dwq.astype(BF16).reshape(QL, N_DEV, H * QK_HEAD // N_DEV).transpose(1, 0, 2)
    dwkv = dwkv.reshape(KVL, N_DEV, w_ukv.shape[2]).transpose(1, 0, 2)
    x_e, tok = _a2a_start("a2a_e", [dw_in, dwq, dwkv], [])

    d_xn_a = _mm2("mm_dxn_a", "nn", d_pa, w_a_t, F32, deps=[tok])
    d_xn_b = _mm2("mm_dxn_b", "nn", d_pb, w_b_t, F32)

    def x_bwd(x, da, db, dh, g0):
        dx, dg0 = _rms_bwd(x, g0, da + db)
        return dh + dx, dg0

    grad_x, dg_pre_mix = _ew("x_bwd", x_bwd, (T // tm_d,),
                             [_rows(x2, tm_d), _rows(d_xn_a, tm_d), _rows(d_xn_b, tm_d), _rows(dh, tm_d),
                              _vec(pre_mix_norm)],
                             [_rows_out((T, D), F32, tm_d)], [_vec_out((1, D))])

    def adam_big(n, parts, transposed=False):
        view = shard_t if transposed else (lambda a: a[0])
        res = _adamw("adamw_" + n, parts, view(given[n]), view(given["m_" + n]), view(given["v_" + n]))
        return [(r.T if transposed else r)[None] for r in res]

    results = {}
    results["w_down"] = adam_big("w_down", _a2a_finish("a2a_d", x_d, grad_x)[0])
    results["w_gate"] = adam_big("w_gate", _a2a_finish("a2a_g", x_g, results["w_down"][0])[0], True)
    results["w_up"] = adam_big("w_up", _a2a_finish("a2a_u", x_u, results["w_gate"][0])[0], True)
    results["w_out"] = adam_big("w_out", _a2a_finish("a2a_o", x_o, results["w_up"][0])[0])
    e_in, e_uq, e_ukv = _a2a_finish("a2a_e", x_e, results["w_out"][0])
    results["w_in"] = adam_big("w_in", e_in, True)
    results["w_uq"] = adam_big("w_uq", e_uq)
    results["w_ukv"] = adam_big("w_ukv", e_ukv)

    small_names = ["pre_mix_norm", "q_norm", "kv_norm", "v_ln_gain", "v_ln_bias", "w_spatial", "b_spatial",
                   "attn_out_norm", "gmlp_out_norm", "post_mix_norm", "pre_ffn_norm", "post_ffn_norm"]
    small_grads = [dg_pre_mix, dg_q, dg_kv, dlng, dlnb, dws, dbs[:, 0, :], dg_attn, dg_gmlp, dg_post_mix, dg_pre_ffn,
                   dg_post_ffn]
    n_small = sum(g.size for g in small_grads)
    pad_small = _round_up(n_small, SUBLANE * LANE) - n_small

    def pack(arrs):
        flat = jnp.concatenate([a.reshape(-1) for a in arrs] + [jnp.zeros((pad_small,), F32)])
        return flat.reshape(-1, LANE)

    small_parts = _all_gather("ag_small_grads", pack(small_grads))
    sm = _adamw("adamw_small", small_parts, pack([given[n] for n in small_names]),
                pack([given["m_" + n] for n in small_names]), pack([given["v_" + n] for n in small_names]))

    off = 0
    for n in small_names:
        shp, size = given[n].shape, given[n].size
        results[n] = [r.reshape(-1)[off:off + size].reshape(shp) for r in sm]
        off += size

    loss = lax.psum(loss_part[0, 0], MESH_AXES)
    order = ["pre_mix_norm", "w_in", "q_norm", "kv_norm", "w_uq", "w_ukv", "v_ln_gain", "v_ln_bias", "w_spatial",
             "b_spatial", "attn_out_norm", "gmlp_out_norm", "w_out", "post_mix_norm", "pre_ffn_norm", "w_gate", "w_up",
             "w_down", "post_ffn_norm"]
    out = [loss, grad_x[None]]
    for kind in range(4):
        out += [results[n][kind] for n in order]
    return tuple(out)
```

```python
import math

import jax
import jax.numpy as jnp
from jax import lax
from jax.experimental import pallas as pl
from jax.experimental.pallas import tpu as pltpu

F32 = jnp.float32
BF16 = jnp.bfloat16
N_DEV = 8
MESH_AXES = ("x", "y", "c")
MESH = pl.DeviceIdType.MESH

NOPE_DIM = 128
ROPE_DIM = 64
V_DIM = 128
ROPE_THETA = 10000.0
CHUNK = 128
EPS = 1e-6
QK_HEAD = NOPE_DIM + ROPE_DIM
Q_EXT = 3 * 128
K_FULL = 2 * 128

ADAM_LR = 0.001
ADAM_B1 = 0.9
ADAM_B2 = 0.999
ADAM_EPS = 1e-08
ADAM_WD = 0.01
ADAM_STEP = 10

LANE = 128
SUBLANE = 8
VMEM_LIMIT = 56 * 1024 * 1024
MM_VMEM_BUDGET = 44 * 1024 * 1024
MXU_DIM = 256
FFN_TILE_MAX = 3072
MM_TK_MAX = 2048
EPILOGUE_ROWS = 256
ATTN_TQ = 512
ATTN_TKV = 512
NEG_BIG = -1e30
ROW_BLOCK_ELEMS = 512 * 1024
ADAM_BLOCK_ELEMS = 256 * 1024

DN = {
    "nn": (((1,), (0,)), ((), ())),
    "nt": (((1,), (1,)), ((), ())),
    "tn": (((0,), (0,)), ((), ())),
}


def _pick(dim, pref, mult):
    t = min(pref, dim)
    t -= t % mult
    while t >= mult:
        if dim % t == 0:
            return t
        t -= mult
    return dim


def _round_up(n, m):
    return (n + m - 1) // m * m


def _params(sem):
    return pltpu.CompilerParams(dimension_semantics=sem, vmem_limit_bytes=VMEM_LIMIT)


def _dep_specs(deps, nd):
    return [pl.BlockSpec(d.shape, lambda *_, r=d.ndim: (0,) * r) for d in deps]


def _mm(name, mode, pairs, grid, acc_shape, out_shape, out_dtype, out_block, out_map, deps=(), extras=(),
        epilogue=None, n_out=1):
    n, ne, nx = len(pairs), len(extras), len(deps)
    nk = grid[-1]
    nd = len(grid)
    dn = DN[mode]
    use_acc = nk > 1 or epilogue is not None

    def body(*refs):
        e_refs = refs[2 * n:2 * n + ne]
        o_refs = refs[2 * n + ne + nx:2 * n + ne + nx + n_out]
        k = pl.program_id(nd - 1)

        def dot(i):
            return lax.dot_general(refs[2 * i][...], refs[2 * i + 1][...], dn, preferred_element_type=F32)

        if not use_acc:
            s = dot(0)
            for i in range(1, n):
                s = s + dot(i)
            o_refs[0][...] = s.astype(o_refs[0].dtype)
            return
        acc = refs[2 * n + ne + nx + n_out]

        @pl.when(k == 0)
        def _():
            acc[...] = dot(0)

        if nk > 1:
            @pl.when(k > 0)
            def _():
                acc[...] += dot(0)

        for i in range(1, n):
            acc[...] += dot(i)

        if epilogue is None:
            @pl.when(k == nk - 1)
            def _():
                o_refs[0][...] = acc[...].astype(o_refs[0].dtype)
        else:
            @pl.when(k == nk - 1)
            def _():
                rows = acc_shape[0]
                slab = _pick(rows, EPILOGUE_ROWS, SUBLANE)
                for r0 in range(0, rows, slab):
                    sl = pl.ds(r0, slab)
                    vals = epilogue(acc[sl, :], *[e[sl, :] for e in e_refs])
                    for r, v in zip(o_refs, vals):
                        r[sl, :] = v.astype(r.dtype)

    in_specs, args = [], []
    for a, a_blk, a_map, b, b_blk, b_map in pairs:
        in_specs += [pl.BlockSpec(a_blk, a_map), pl.BlockSpec(b_blk, b_map)]
        args += [a, b]
    in_specs += [pl.BlockSpec(blk, imap) for (_, blk, imap) in extras]
    args += [e for (e, _, _) in extras]
    if not isinstance(out_dtype, (tuple, list)):
        out_dtype = (out_dtype,) * n_out
    res = pl.pallas_call(
        body,
        name=name,
        grid=grid,
        in_specs=in_specs + _dep_specs(deps, nd),
        out_specs=[pl.BlockSpec(out_block, out_map)] * n_out,
        out_shape=[jax.ShapeDtypeStruct(out_shape, dt) for dt in out_dtype],
        scratch_shapes=[pltpu.VMEM(acc_shape, F32)] if use_acc else [],
        compiler_params=_params(("parallel",) * (nd - 1) + ("arbitrary",)),
    )(*args, *deps)
    return res[0] if n_out == 1 else res


def _fit_tk(K, tm, tn, pairs=1, out_bytes=4, fixed_extra=0, tk_max=MM_TK_MAX):
    fixed = tm * tn * 4 + 2 * tm * tn * out_bytes + fixed_extra
    tk = _pick(K, tk_max, LANE)
    while tk > LANE and fixed + pairs * 4 * (tm + tn) * tk > MM_VMEM_BUDGET:
        smaller = _pick(K, tk - LANE, LANE)
        if smaller >= tk:
            break
        tk = smaller
    return tk


def _mm2(name, mode, a, b, out_dtype, tm_pref=1024, tn_pref=1024, deps=(), extras=(), epilogue=None, n_out=1,
         tk_max=MM_TK_MAX):
    if mode == "nn":
        (M, K), N = a.shape, b.shape[1]
    elif mode == "nt":
        (M, K), N = a.shape, b.shape[0]
    else:
        (K, M), N = a.shape, b.shape[1]
    tm, tn = _pick(M, tm_pref, LANE), _pick(N, tn_pref, LANE)
    extra_bytes = sum(2 * tm * tn * jnp.dtype(e.dtype).itemsize for e in extras)
    tk = _fit_tk(K, tm, tn, out_bytes=n_out * jnp.dtype(out_dtype).itemsize, fixed_extra=extra_bytes, tk_max=tk_max)
    if mode == "tn":
        a_blk, a_map = (tk, tm), lambda i, j, k: (k, i)
    else:
        a_blk, a_map = (tm, tk), lambda i, j, k: (i, k)
    if mode == "nt":
        b_blk, b_map = (tn, tk), lambda i, j, k: (j, k)
    else:
        b_blk, b_map = (tk, tn), lambda i, j, k: (k, j)
    return _mm(name, mode, [(a, a_blk, a_map, b, b_blk, b_map)], (M // tm, N // tn, K // tk), (tm, tn),
               (M, N), out_dtype, (tm, tn), lambda i, j, k: (i, j), deps=deps,
               extras=[(e, (tm, tn), lambda i, j, k: (i, j)) for e in extras], epilogue=epilogue, n_out=n_out)


def _ew(name, fn, grid, ins, outs, accs=(), deps=()):
    ni, no, na = len(ins), len(outs), len(accs)
    nd = len(grid)
    nx = len(deps)

    def body(*refs):
        vals = fn(*[r[...] for r in refs[:ni]])
        if not isinstance(vals, (tuple, list)):
            vals = (vals,)
        for r, v in zip(refs[ni + nx:ni + nx + no], vals[:no]):
            r[...] = v.astype(r.dtype)
        if na:
            first = pl.program_id(nd - 1) == 0
            for r, v in zip(refs[ni + nx + no:], vals[no:]):
                @pl.when(first)
                def _(r=r, v=v):
                    r[...] = v.astype(r.dtype)

                @pl.when(jnp.logical_not(first))
                def _(r=r, v=v):
                    r[...] += v.astype(r.dtype)

    all_outs = list(outs) + list(accs)
    sem = ("parallel",) * (nd - 1) + (("arbitrary",) if na else ("parallel",))
    res = pl.pallas_call(
        body,
        name=name,
        grid=grid,
        in_specs=[pl.BlockSpec(blk, imap) for (_, blk, imap) in ins] + _dep_specs(deps, nd),
        out_specs=[pl.BlockSpec(blk, imap) for (_, _, blk, imap) in all_outs],
        out_shape=[jax.ShapeDtypeStruct(s, d) for (s, d, _, _) in all_outs],
        compiler_params=_params(sem),
    )(*[a for (a, _, _) in ins], *deps)
    return res


def _rows(a, tm, width=None, cblk=0):
    width = a.shape[1] if width is None else width
    return (a, (tm, width), lambda i, c=cblk: (i, c))


def _vec(p):
    return (p, p.shape, lambda i: (0,) * p.ndim)


def _rows_out(shape, dtype, tm):
    return (shape, dtype, (tm, shape[1]), lambda i: (i, 0))


def _vec_out(shape):
    return (shape, F32, shape, lambda i: (0,) * len(shape))


def _row_tile(T, D):
    return _pick(T, max(SUBLANE, ROW_BLOCK_ELEMS // D), SUBLANE)


def _rms(x, g):
    r = lax.rsqrt(jnp.mean(x * x, axis=-1, keepdims=True) + EPS)
    return x * r * g


def _rms_bwd(x, g, dy):
    r = lax.rsqrt(jnp.mean(x * x, axis=-1, keepdims=True) + EPS)
    xh = x * r
    dg = jnp.sum(dy * xh, axis=0, keepdims=True)
    dxh = dy * g
    dx = r * (dxh - xh * jnp.mean(dxh * xh, axis=-1, keepdims=True))
    return dx, dg


GELU_C = math.sqrt(2.0 / math.pi)
GELU_A = 0.044715


def _gelu(x):
    return 0.5 * x * (1.0 + jnp.tanh(GELU_C * (x + GELU_A * x * x * x)))


def _gelu_grad(x):
    t = jnp.tanh(GELU_C * (x + GELU_A * x * x * x))
    return 0.5 * (1.0 + t) + 0.5 * x * (1.0 - t * t) * GELU_C * (1.0 + 3.0 * GELU_A * x * x)


def _rot(w, axis=-1):
    w1, w2 = jnp.split(w, 2, axis=axis)
    return jnp.concatenate([-w2, w1], axis=axis)


def _rot_t(dw, axis=-1):
    d1, d2 = jnp.split(dw, 2, axis=axis)
    return jnp.concatenate([d2, -d1], axis=axis)


def _coords():
    return lax.axis_index("x"), lax.axis_index("y"), lax.axis_index("c")


def _all_gather(name, shard, deps=()):
    def body(x_ref, *rest):
        o_ref, send_sems, recv_sems, local_sem = rest[len(deps):]
        xi, yi, ci = _coords()
        me, sibling = (xi, yi, ci), (xi, yi, 1 - ci)
        chips = [(1 - xi, yi), (xi, 1 - yi), (1 - xi, 1 - yi)]

        def slot(px, py, pc):
            return o_ref.at[4 * px + 2 * py + pc]

        def copy(k, block, to, src=None):
            return pltpu.make_async_remote_copy(
                src_ref=slot(*block) if src is None else src, dst_ref=slot(*block),
                send_sem=send_sems.at[k], recv_sem=recv_sems.at[k], device_id=to, device_id_type=MESH)

        mine = pltpu.make_async_copy(x_ref, slot(*me), local_sem)
        mine.start()
        first = [copy(0, me, sibling, src=x_ref)]
        first += [copy(1 + j, me, (*chip, ci), src=x_ref) for j, chip in enumerate(chips)]
        for cp in first:
            cp.start()
        passed = [copy(4 + j, (*chip, ci), sibling) for j, chip in enumerate(chips)]
        for j, chip in enumerate(chips):
            copy(1 + j, (*chip, ci), me).wait_recv()
            passed[j].start()
        copy(0, sibling, me).wait_recv()
        for j, chip in enumerate(chips):
            copy(4 + j, (*chip, 1 - ci), me).wait_recv()
        for cp in first + passed:
            cp.wait_send()
        mine.wait()

    return pl.pallas_call(
        body,
        name=name,
        out_shape=jax.ShapeDtypeStruct((N_DEV,) + shard.shape, shard.dtype),
        in_specs=[pl.BlockSpec(memory_space=pltpu.HBM)] + [pl.BlockSpec(memory_space=pl.ANY)] * len(deps),
        out_specs=pl.BlockSpec(memory_space=pltpu.HBM),
        scratch_shapes=[pltpu.SemaphoreType.DMA((N_DEV - 1,)), pltpu.SemaphoreType.DMA((N_DEV - 1,)),
                        pltpu.SemaphoreType.DMA],
    )(shard, *deps)


HBM_SPEC = pl.BlockSpec(memory_space=pltpu.HBM)
SEM_SPEC = pl.BlockSpec(memory_space=pltpu.SEMAPHORE)
ANY_SPEC = pl.BlockSpec(memory_space=pl.ANY)
EFFECT = pltpu.SideEffectType.DATAFLOW_SIDE_EFFECTING
TOKEN = (SUBLANE, LANE)


def _my_slot():
    xi, yi, ci = _coords()
    return 4 * xi + 2 * yi + ci


def _own_slot_filled(block):
    land = lax.empty((N_DEV,) + block.shape, block.dtype)
    return lax.dynamic_update_slice(land, block[None], (_my_slot(),) + (0,) * block.ndim)


def _split_start(name, n, n_copies, issue, srcs, lands, deps):
    def body(*refs):
        src_refs, land_refs = refs[:n], refs[n:2 * n]
        outs = refs[2 * n + len(deps):]
        send, recv, token = outs[:n], outs[n:2 * n], outs[4 * n]
        for t in range(n):
            issue(t, src_refs[t], land_refs[t], send[t], recv[t])
        token[...] = jnp.zeros_like(token)

    sems = [pltpu.SemaphoreType.DMA((n_copies,))] * (2 * n)
    res = pl.pallas_call(
        body,
        name=name,
        in_specs=[HBM_SPEC] * (2 * n) + [ANY_SPEC] * len(deps),
        out_specs=[SEM_SPEC] * (2 * n) + [HBM_SPEC] * (2 * n) + [pl.BlockSpec(memory_space=pltpu.VMEM)],
        out_shape=sems + [pltpu.HBM(a.shape, a.dtype) for a in list(srcs) + list(lands)]
        + [jax.ShapeDtypeStruct(TOKEN, F32)],
        input_output_aliases={i: 2 * n + i for i in range(2 * n)},
        compiler_params=pltpu.CompilerParams(has_side_effects=EFFECT),
    )(*[pltpu.with_memory_space_constraint(a, pltpu.HBM) for a in list(srcs) + list(lands)], *deps)
    state = dict(n=n, n_copies=n_copies, send=res[:n], recv=res[n:2 * n], srcs=res[2 * n:3 * n],
                 lands=res[3 * n:4 * n])
    return state, res[4 * n]


def _split_wait(name, state, unit_src, after):
    n, n_copies = state["n"], state["n_copies"]

    def body(*refs):
        src_refs, land_refs = refs[:n], refs[n:2 * n]
        send, recv = refs[2 * n:3 * n], refs[3 * n:4 * n]
        me = _coords()
        for t in range(n):
            for k in range(n_copies):
                cp = pltpu.make_async_remote_copy(
                    src_ref=unit_src(src_refs[t]), dst_ref=land_refs[t].at[0], send_sem=send[t].at[k],
                    recv_sem=recv[t].at[k], device_id=me, device_id_type=MESH)
                cp.wait_send()
                cp.wait_recv()

    bufs = list(state["srcs"]) + list(state["lands"])
    res = pl.pallas_call(
        body,
        name=name,
        in_specs=[HBM_SPEC] * (2 * n) + [SEM_SPEC] * (2 * n) + [ANY_SPEC],
        out_specs=[HBM_SPEC] * (2 * n),
        out_shape=[pltpu.HBM(a.shape, a.dtype) for a in bufs],
        input_output_aliases={i: i for i in range(2 * n)},
        compiler_params=pltpu.CompilerParams(has_side_effects=EFFECT),
    )(*bufs, *state["send"], *state["recv"], after)
    return list(res[n:])


def _ag_start(name, shards, deps):
    def issue(t, x_ref, land_ref, send, recv):
        xi, yi, ci = _coords()
        to = [(xi, yi, 1 - ci), (1 - xi, yi, ci), (xi, 1 - yi, ci), (1 - xi, 1 - yi, ci)]
        for k, dev in enumerate(to):
            pltpu.make_async_remote_copy(src_ref=x_ref, dst_ref=land_ref.at[4 * xi + 2 * yi + ci], send_sem=send.at[k],
                                         recv_sem=recv.at[k], device_id=dev, device_id_type=MESH).start()

    return _split_start(name, len(shards), 4, issue, shards, [_own_slot_filled(s) for s in shards], deps)


def _ag_mid(name, state, after):
    lands = _split_wait(name + "_w1", state, lambda r: r, after)

    def issue(t, src_ref, land_ref, send, recv):
        xi, yi, ci = _coords()
        for k, (px, py) in enumerate([(1 - xi, yi), (xi, 1 - yi), (1 - xi, 1 - yi)]):
            rows = land_ref.at[4 * px + 2 * py + ci]
            pltpu.make_async_remote_copy(src_ref=rows, dst_ref=rows, send_sem=send.at[k], recv_sem=recv.at[k],
                                         device_id=(xi, yi, 1 - ci), device_id_type=MESH).start()

    n = len(lands)

    def body(*refs):
        land_refs = refs[:n]
        outs = refs[n:]
        send, recv, token = outs[:n], outs[n:2 * n], outs[3 * n]
        for t in range(n):
            issue(t, land_refs[t], land_refs[t], send[t], recv[t])
        token[...] = jnp.zeros_like(token)

    res = pl.pallas_call(
        body,
        name=name + "_s2",
        in_specs=[HBM_SPEC] * n,
        out_specs=[SEM_SPEC] * (2 * n) + [HBM_SPEC] * n + [pl.BlockSpec(memory_space=pltpu.VMEM)],
        out_shape=[pltpu.SemaphoreType.DMA((3,))] * (2 * n) + [pltpu.HBM(a.shape, a.dtype) for a in lands]
        + [jax.ShapeDtypeStruct(TOKEN, F32)],
        input_output_aliases={i: 2 * n + i for i in range(n)},
        compiler_params=pltpu.CompilerParams(has_side_effects=EFFECT),
    )(*lands)
    state2 = dict(n=n, send=res[:n], recv=res[n:2 * n], lands=res[2 * n:3 * n])
    return state2, res[3 * n]


def _ag_finish(name, state2, after):
    n = state2["n"]

    def body(*refs):
        land_refs, send, recv = refs[:n], refs[n:2 * n], refs[2 * n:3 * n]
        me = _coords()
        for t in range(n):
            for k in range(3):
                cp = pltpu.make_async_remote_copy(
                    src_ref=land_refs[t].at[0], dst_ref=land_refs[t].at[0], send_sem=send[t].at[k],
                    recv_sem=recv[t].at[k], device_id=me, device_id_type=MESH)
                cp.wait_send()
                cp.wait_recv()

    res = pl.pallas_call(
        body,
        name=name + "_w2",
        in_specs=[HBM_SPEC] * n + [SEM_SPEC] * (2 * n) + [ANY_SPEC],
        out_specs=[HBM_SPEC] * n,
        out_shape=[pltpu.HBM(a.shape, a.dtype) for a in state2["lands"]],
        input_output_aliases={i: i for i in range(n)},
        compiler_params=pltpu.CompilerParams(has_side_effects=EFFECT),
    )(*state2["lands"], *state2["send"], *state2["recv"], after)
    return list(res)


def _a2a_start(name, parts, deps):
    def issue(t, x_ref, land_ref, send, recv):
        xi, yi, ci = _coords()
        for k in range(N_DEV - 1):
            px = 1 - xi if (k + 1) & 4 else xi
            py = 1 - yi if (k + 1) & 2 else yi
            pc = 1 - ci if (k + 1) & 1 else ci
            pltpu.make_async_remote_copy(src_ref=x_ref.at[4 * px + 2 * py + pc], dst_ref=land_ref.at[4 * xi + 2 * yi + ci],
                                         send_sem=send.at[k], recv_sem=recv.at[k], device_id=(px, py, pc),
                                         device_id_type=MESH).start()

    lands = [_own_slot_filled(lax.dynamic_index_in_dim(p, _my_slot(), 0, keepdims=False)) for p in parts]
    return _split_start(name, len(parts), N_DEV - 1, issue, parts, lands, deps)


def _a2a_finish(name, state, after):
    return _split_wait(name + "_w", state, lambda r: r.at[0], after)


N_CHIP = N_DEV // 2


def _rs_pair_start(name, parts, deps):
    def issue(t, x_ref, land_ref, send, recv):
        xi, yi, ci = _coords()
        for k in range(N_CHIP):
            pltpu.make_async_remote_copy(src_ref=x_ref.at[2 * k + (1 - ci)], dst_ref=land_ref.at[k], send_sem=send.at[k],
                                         recv_sem=recv.at[k], device_id=(xi, yi, 1 - ci), device_id_type=MESH).start()

    lands = [lax.empty((N_CHIP,) + p.shape[1:], p.dtype) for p in parts]
    return _split_start(name, len(parts), N_CHIP, issue, parts, lands, deps)


def _rs_pair_sum(name, state, after):
    lands = _split_wait(name + "_w", state, lambda r: r.at[0], after)
    sums = []
    for t, (part, land) in enumerate(zip(state["srcs"], lands)):
        _, R, C = land.shape
        pack = SUBLANE * (4 // jnp.dtype(land.dtype).itemsize)
        if R % pack == 0:
            tr, tc = _pick(R, max(pack, ADAM_BLOCK_ELEMS // C), pack), C
        else:
            tr, tc = R, _pick(C, max(LANE, ADAM_BLOCK_ELEMS // R // LANE * LANE), LANE)

        def fn(mine, theirs):
            own = jnp.where(lax.axis_index("c") == 0, mine[0], mine[1])
            return own.astype(F32) + theirs.astype(F32)

        sums.append(_ew(f"{name}_sum{t}", fn, (N_CHIP, R // tr, C // tc),
                        [(part.reshape((N_CHIP, 2) + part.shape[1:]), (None, 2, tr, tc), lambda k, i, j: (k, 0, i, j)),
                         (land, (None, tr, tc), lambda k, i, j: (k, i, j))],
                        [(land.shape, land.dtype, (None, tr, tc), lambda k, i, j: (k, i, j))])[0])
    return sums


def _rs_chip_start(name, sums, deps):
    def issue(t, x_ref, land_ref, send, recv):
        xi, yi, ci = _coords()
        for k, (px, py) in enumerate([(1 - xi, yi), (xi, 1 - yi), (1 - xi, 1 - yi)]):
            pltpu.make_async_remote_copy(src_ref=x_ref.at[2 * px + py], dst_ref=land_ref.at[2 * xi + yi], send_sem=send.at[k],
                                         recv_sem=recv.at[k], device_id=(px, py, ci), device_id_type=MESH).start()

    xi, yi, _ = _coords()
    lands = []
    for s in sums:
        own = lax.dynamic_index_in_dim(s, 2 * xi + yi, 0, keepdims=True)
        lands.append(lax.dynamic_update_slice(lax.empty(s.shape, s.dtype), own, (2 * xi + yi,) + (0,) * (s.ndim - 1)))
    return _split_start(name, len(sums), N_CHIP - 1, issue, sums, lands, deps)


def _rs_chip_finish(name, state, after):
    return _split_wait(name + "_w", state, lambda r: r.at[0], after)


QK_SCALE = 1.0 / math.sqrt(QK_HEAD)
LOG2E = 1.0 / math.log(2.0)


def _roped_q(q_ext, cos, sin):
    q = jnp.concatenate([q_ext[:, :LANE], q_ext[:, LANE:2 * LANE] * cos + q_ext[:, 2 * LANE:] * sin], axis=1)
    return (q * (QK_SCALE * LOG2E)).astype(BF16)


def _attn_fwd(q_ext, k_full, v_aug, cosp, sinp):
    H, T, _ = k_full.shape
    tq, tkv = _pick(T, ATTN_TQ, LANE), _pick(T, ATTN_TKV, LANE)

    def body(q_ref, cos_ref, sin_ref, k_ref, v_ref, o_ref, lse_ref):
        q = _roped_q(q_ref[...], cos_ref[...], sin_ref[...])
        m = jnp.full((tq, 1), NEG_BIG, F32)
        acc = jnp.zeros((tq, 2 * V_DIM), F32)
        for c in range(T // tkv):
            keys = pl.ds(c * tkv, tkv)
            s = lax.dot_general(q, k_ref[keys, :], DN["nt"], preferred_element_type=F32)
            m_new = jnp.maximum(m, jnp.max(s, axis=-1, keepdims=True))
            p = jnp.exp2(s - m_new).astype(BF16)
            acc = jnp.exp2(m - m_new) * acc + jnp.dot(p, v_ref[keys, :], preferred_element_type=F32)
            m = m_new
        l = acc[:, V_DIM:]
        o_ref[...] = acc[:, :V_DIM] / l
        lse_ref[...] = m + jnp.log2(jnp.max(l, axis=-1, keepdims=True))

    return pl.pallas_call(
        body,
        name="attn_fwd",
        grid=(H, T // tq),
        in_specs=[pl.BlockSpec((tq, Q_EXT), lambda h, i: (i, h)),
                  pl.BlockSpec((tq, LANE), lambda h, i: (i, 0)),
                  pl.BlockSpec((tq, LANE), lambda h, i: (i, 0)),
                  pl.BlockSpec((None, T, K_FULL), lambda h, i: (h, 0, 0)),
                  pl.BlockSpec((None, T, 2 * V_DIM), lambda h, i: (h, 0, 0))],
        out_specs=[pl.BlockSpec((tq, V_DIM), lambda h, i: (i, h)),
                   pl.BlockSpec((None, tq, 1), lambda h, i: (h, i, 0))],
        out_shape=[jax.ShapeDtypeStruct((T, H * V_DIM), F32), jax.ShapeDtypeStruct((H, T, 1), F32)],
        compiler_params=_params(("parallel", "parallel")),
    )(q_ext, cosp, sinp, k_full, v_aug)


def _attn_bwd(q_ext, k_full, v_aug, cosp, sinp, d_out, out, lse_rows):
    H, T, _ = k_full.shape
    tq, tkv = _pick(T, ATTN_TQ, LANE), _pick(T, ATTN_TKV, LANE)
    n_q = T // tq

    def body(q_ref, cos_ref, sin_ref, k_ref, v_ref, do_ref, o_ref, lse_ref, cos_all, sin_all,
             dq_ref, dkv_ref, dkr_ref, dk_acc, dv_acc):
        h, i = pl.program_id(0), pl.program_id(1)

        @pl.when(i == 0)
        def _():
            dk_acc[...] = jnp.zeros_like(dk_acc)
            dv_acc[...] = jnp.zeros_like(dv_acc)

        cos, sin = cos_ref[...], sin_ref[...]
        q, do, lse = _roped_q(q_ref[...], cos, sin), do_ref[...], lse_ref[...]
        delta = lax.dot_general(jnp.ones((SUBLANE, V_DIM), F32), do.astype(F32) * o_ref[...], DN["nt"],
                                preferred_element_type=F32, precision=lax.Precision.HIGHEST)[0:1]
        dq = jnp.zeros((tq, K_FULL), F32)
        for c in range(T // tkv):
            keys = pl.ds(c * tkv, tkv)
            kc = k_ref[keys, :]
            pt = jnp.exp2(lax.dot_general(kc, q, DN["nt"], preferred_element_type=F32) - lse)
            dpt = lax.dot_general(v_ref[keys, pl.ds(0, V_DIM)], do, DN["nt"], preferred_element_type=F32)
            dst = (pt * (dpt - delta)).astype(BF16)
            dv_acc[keys, :] += jnp.dot(pt.astype(BF16), do, preferred_element_type=F32)
            dk_acc[keys, :] += jnp.dot(dst, q, preferred_element_type=F32)
            dq = dq + lax.dot_general(dst, kc, DN["tn"], preferred_element_type=F32)
        r = dq[:, LANE:]
        dq_ref[...] = (jnp.concatenate([dq[:, :LANE], r * cos, r * sin], axis=1) * QK_SCALE).astype(BF16)

        @pl.when(i == n_q - 1)
        def _():
            dk = dk_acc[...] * (1.0 / LOG2E)
            dkv_ref[...] = jnp.concatenate([dk[:, :LANE], dv_acc[...]], axis=1).astype(BF16)
            rk = dk[:, LANE:]
            dkr = jnp.concatenate([rk * cos_all[...], rk * sin_all[...]], axis=1)

            @pl.when(h == 0)
            def _():
                dkr_ref[...] = dkr

            @pl.when(h != 0)
            def _():
                dkr_ref[...] += dkr

    return pl.pallas_call(
        body,
        name="attn_bwd",
        grid=(H, n_q),
        in_specs=[pl.BlockSpec((tq, Q_EXT), lambda h, i: (i, h)),
                  pl.BlockSpec((tq, LANE), lambda h, i: (i, 0)),
                  pl.BlockSpec((tq, LANE), lambda h, i: (i, 0)),
                  pl.BlockSpec((None, T, K_FULL), lambda h, i: (h, 0, 0)),
                  pl.BlockSpec((None, T, 2 * V_DIM), lambda h, i: (h, 0, 0)),
                  pl.BlockSpec((tq, V_DIM), lambda h, i: (i, h)),
                  pl.BlockSpec((tq, V_DIM), lambda h, i: (i, h)),
                  pl.BlockSpec((None, 1, tq), lambda h, i: (h, 0, i)),
                  pl.BlockSpec((T, LANE), lambda h, i: (0, 0)),
                  pl.BlockSpec((T, LANE), lambda h, i: (0, 0))],
        out_specs=[pl.BlockSpec((tq, Q_EXT), lambda h, i: (i, h)),
                   pl.BlockSpec((T, K_FULL), lambda h, i: (0, h)),
                   pl.BlockSpec((T, K_FULL), lambda h, i: (0, 0))],
        out_shape=[jax.ShapeDtypeStruct((T, H * Q_EXT), BF16), jax.ShapeDtypeStruct((T, H * K_FULL), BF16),
                   jax.ShapeDtypeStruct((T, K_FULL), F32)],
        scratch_shapes=[pltpu.VMEM((T, K_FULL), F32), pltpu.VMEM((T, V_DIM), F32)],
        compiler_params=_params(("arbitrary", "arbitrary")),
    )(q_ext, cosp, sinp, k_full, v_aug, d_out, out, lse_rows, cosp, sinp)


def _sgu_forward_math(pb, lng, lnb, ws, bsb, gw, gh):
    u_raw, v_raw = pb[:, :gw], pb[:, gw:]
    u, v = _gelu(u_raw), _gelu(v_raw)
    vc = v - jnp.mean(v, axis=-1, keepdims=True)
    rstd = lax.rsqrt(jnp.mean(vc * vc, axis=-1, keepdims=True) + EPS)
    vhat = vc * rstd
    vln = vhat * lng + lnb
    hd = gw // gh
    s = jnp.concatenate(
        [jnp.dot(ws[g], vln[:, g * hd:(g + 1) * hd].astype(BF16), preferred_element_type=F32) + bsb[g]
         for g in range(gh)], axis=1)
    return u_raw, v_raw, u, rstd, vhat, vln, s


def _sgu_fwd(proj_b, lng, lnb, ws, bsb, gg):
    T, gw2 = proj_b.shape
    gw, gh = gw2 // 2, ws.shape[0]

    def fn(pb, lng, lnb, ws, bsb, gg):
        _, _, u, _, _, _, s = _sgu_forward_math(pb, lng, lnb, ws, bsb, gw, gh)
        return _rms(u * s, gg)

    return _ew("sgu_fwd", fn, (T // CHUNK,),
               [_rows(proj_b, CHUNK), _vec(lng), _vec(lnb), _vec(ws), _vec(bsb), _vec(gg)],
               [_rows_out((T, gw), BF16, CHUNK)])[0]


def _sgu_bwd(proj_b, d_mixed, lng, lnb, ws, ws_t, bsb, gg):
    T, gw2 = proj_b.shape
    gw, gh = gw2 // 2, ws.shape[0]
    hd = gw // gh

    def fn(pb, dmg, lng, lnb, ws, ws_t, bsb, gg):
        u_raw, v_raw, u, rstd, vhat, vln, s = _sgu_forward_math(pb, lng, lnb, ws, bsb, gw, gh)
        dgo, dgg = _rms_bwd(u * s, gg, dmg)
        du, ds = dgo * s, dgo * u
        ones = jnp.ones((SUBLANE, hd), F32)
        dws, dbs, dvln = [], [], []
        for g in range(gh):
            ds_g = ds[:, g * hd:(g + 1) * hd]
            ds_gb = ds_g.astype(BF16)
            dws.append(lax.dot_general(ds_gb, vln[:, g * hd:(g + 1) * hd].astype(BF16), DN["nt"],
                                       preferred_element_type=F32)[None])
            dbs.append(lax.dot_general(ones, ds_g, DN["nt"], preferred_element_type=F32,
                                       precision=lax.Precision.HIGHEST)[None])
            dvln.append(jnp.dot(ws_t[g], ds_gb, preferred_element_type=F32))
        dvln = jnp.concatenate(dvln, axis=1)
        dlng = jnp.sum(dvln * vhat, axis=0, keepdims=True)
        dlnb = jnp.sum(dvln, axis=0, keepdims=True)
        dvh = dvln * lng
        dv = rstd * (dvh - jnp.mean(dvh, axis=-1, keepdims=True) - vhat * jnp.mean(dvh * vhat, axis=-1, keepdims=True))
        d_pb = jnp.concatenate([du * _gelu_grad(u_raw), dv * _gelu_grad(v_raw)], axis=1)
        return d_pb, dgg, jnp.concatenate(dws, axis=0), jnp.concatenate(dbs, axis=0), dlng, dlnb

    return _ew("sgu_bwd", fn, (T // CHUNK,),
               [_rows(proj_b, CHUNK), _rows(d_mixed, CHUNK, gw, 1), _vec(lng), _vec(lnb), _vec(ws), _vec(ws_t),
                _vec(bsb), _vec(gg)],
               [_rows_out((T, gw2), BF16, CHUNK)],
               [_vec_out((1, gw)), _vec_out((gh, CHUNK, CHUNK)), _vec_out((gh, SUBLANE, CHUNK)), _vec_out((1, gw)),
                _vec_out((1, gw))])


def _adamw(name, parts, w, m, v):
    R, C = w.shape
    pack = SUBLANE * (4 // jnp.dtype(parts.dtype).itemsize)
    if R % pack == 0:
        tm, tc = _pick(R, max(pack, ADAM_BLOCK_ELEMS // C), pack), C
    else:
        parts = parts[:, :R]
        tm, tc = R, _pick(C, max(LANE, ADAM_BLOCK_ELEMS // R // LANE * LANE), LANE)

    n_parts = parts.shape[0]

    def fn(p, w, m, v):
        g = p[0].astype(F32)
        for i in range(1, n_parts):
            g = g + p[i].astype(F32)
        m2 = ADAM_B1 * m + (1.0 - ADAM_B1) * g
        v2 = ADAM_B2 * v + (1.0 - ADAM_B2) * (g * g)
        m_hat = m2 / (1.0 - ADAM_B1 ** ADAM_STEP)
        v_hat = v2 / (1.0 - ADAM_B2 ** ADAM_STEP)
        delta = -ADAM_LR * (m_hat / (jnp.sqrt(v_hat) + ADAM_EPS) + ADAM_WD * w)
        return g, delta, m2, v2

    blk = lambda a: (a, (tm, tc), lambda i, j: (i, j))
    return _ew(name, fn, (R // tm, C // tc),
               [(parts, (n_parts, tm, tc), lambda i, j: (0, i, j)), blk(w), blk(m), blk(v)],
               [((R, C), F32, (tm, tc), lambda i, j: (i, j))] * 4)


def kernel(x, positions, pre_mix_norm, w_in, q_norm, kv_norm, w_uq, w_ukv, v_ln_gain, v_ln_bias, w_spatial, b_spatial, attn_out_norm, gmlp_out_norm, w_out, post_mix_norm, pre_ffn_norm, w_gate, w_up, w_down, post_ffn_norm, loss_target, m_pre_mix_norm, m_w_in, m_q_norm, m_kv_norm, m_w_uq, m_w_ukv, m_v_ln_gain, m_v_ln_bias, m_w_spatial, m_b_spatial, m_attn_out_norm, m_gmlp_out_norm, m_w_out, m_post_mix_norm, m_pre_ffn_norm, m_w_gate, m_w_up, m_w_down, m_post_ffn_norm, v_pre_mix_norm, v_w_in, v_q_norm, v_kv_norm, v_w_uq, v_w_ukv, v_v_ln_gain, v_v_ln_bias, v_w_spatial, v_b_spatial, v_attn_out_norm, v_gmlp_out_norm, v_w_out, v_post_mix_norm, v_pre_ffn_norm, v_w_gate, v_w_up, v_w_down, v_post_ffn_norm):
    given = dict(locals())
    T, D = x.shape[1], x.shape[2]
    QL, KVL = w_uq.shape[1], w_ukv.shape[1]
    GW, GH = v_ln_gain.shape[1], w_spatial.shape[1]
    IWS = w_in.shape[2]
    IW = IWS * N_DEV
    H = w_ukv.shape[2] * N_DEV // (NOPE_DIM + V_DIM)
    AW = H * V_DIM
    FS = w_gate.shape[2]
    FP = _round_up(FS, LANE)
    C0 = QL + KVL
    assert IW == C0 + ROPE_DIM + 2 * GW and AW + GW == D and AW == GW
    assert QL % KVL == 0 and C0 % K_FULL == 0 and w_spatial.shape[2] == CHUNK and T % CHUNK == 0
    assert w_uq.shape[2] * N_DEV == H * QK_HEAD and D % N_DEV == 0

    x2, tgt = x[0], loss_target[0]
    tm_d = _row_tile(T, D)

    inv_freq = 1.0 / (ROPE_THETA ** (jnp.arange(0, ROPE_DIM, 2, dtype=F32) / ROPE_DIM))
    ang = positions[0].astype(F32)[:, None] * inv_freq
    zpad = jnp.zeros((T, LANE - ROPE_DIM), F32)
    cosp = jnp.concatenate([jnp.cos(ang), jnp.cos(ang), zpad], axis=1)
    sinp = jnp.concatenate([jnp.sin(ang), jnp.sin(ang), zpad], axis=1)

    def shard_t(a):
        return a[0].T

    def padded_rows(a):
        return jnp.pad(a.astype(BF16), ((0, FP - FS), (0, 0)))

    st_a, tok = _ag_start("ag_a", [shard_t(w_in).astype(BF16), w_uq[0].astype(BF16), w_ukv[0].astype(BF16)], [])
    st_o, tok = _ag_start("ag_o", [w_out[0].astype(BF16)], [tok])
    st_gu, tok = _ag_start("ag_gu", [padded_rows(shard_t(w_gate)), padded_rows(shard_t(w_up))], [tok])
    st_d, tok = _ag_start("ag_d", [padded_rows(w_down[0])], [tok])

    xn = _ew("rms_x", lambda x, g: _rms(x, g), (T // tm_d,), [_rows(x2, tm_d), _vec(pre_mix_norm)],
             [_rows_out((T, D), BF16, tm_d)], deps=[tok])[0]
    st_a, tok = _ag_mid("ag_a", st_a, xn)
    g_in, g_uq, g_ukv = _ag_finish("ag_a", st_a, tok)

    w_in_t = g_in.reshape(IW, D)
    kr_w = w_in_t[C0:C0 + ROPE_DIM]
    z64 = jnp.zeros((LANE - ROPE_DIM, D), BF16)
    w_a_t = jnp.concatenate([w_in_t[:C0], kr_w, z64, _rot(kr_w, 0), z64], axis=0)
    w_b_t = w_in_t[C0 + ROPE_DIM:]
    wq = g_uq.transpose(1, 0, 2).reshape(QL, H, QK_HEAD)
    zq = jnp.zeros((QL, H, LANE - ROPE_DIM), BF16)
    wq_ext = jnp.concatenate([wq[..., :NOPE_DIM], wq[..., NOPE_DIM:], zq, _rot(wq[..., NOPE_DIM:]), zq],
                             axis=-1).reshape(QL, H * Q_EXT)
    wkv = g_ukv.transpose(1, 0, 2).reshape(KVL, H * (NOPE_DIM + V_DIM))

    ws = w_spatial[0].astype(BF16)
    ws_t = jnp.swapaxes(ws, 1, 2)
    bsb = jnp.broadcast_to(b_spatial[0][:, :, None], (GH, CHUNK, GW // GH))

    proj_a = _mm2("mm_proj_a", "nt", xn, w_a_t, F32, tn_pref=w_a_t.shape[0])
    proj_b = _mm2("mm_proj_b", "nt", xn, w_b_t, F32)

    tm_q = _row_tile(T, QL)
    qn = _ew("rms_q", lambda x, g: _rms(x, g), (T // tm_q,), [_rows(proj_a, tm_q, QL, 0), _vec(q_norm)],
             [_rows_out((T, QL), BF16, tm_q)])[0]
    kvn = _ew("rms_kv", lambda x, g: _rms(x, g), (T // tm_q,), [_rows(proj_a, tm_q, KVL, QL // KVL), _vec(kv_norm)],
              [_rows_out((T, KVL), BF16, tm_q)])[0]
    q_ext = _mm2("mm_q", "nn", qn, wq_ext, F32)
    kv = _mm2("mm_kv", "nn", kvn, wkv, F32)

    tm_h = _pick(T, 512, SUBLANE)

    def make_kv(kv, kr, c, s):
        k = jnp.concatenate([kv[:, :LANE], kr[:, :LANE] * c + kr[:, LANE:] * s], axis=1)
        return k, jnp.concatenate([kv[:, LANE:], jnp.ones_like(kv[:, LANE:])], axis=1)

    k_full, v_aug = _ew("make_kv", make_kv, (T // tm_h, H),
                        [(kv, (tm_h, K_FULL), lambda i, h: (i, h)),
                         (proj_a, (tm_h, K_FULL), lambda i, h: (i, C0 // K_FULL)),
                         (cosp, (tm_h, LANE), lambda i, h: (i, 0)), (sinp, (tm_h, LANE), lambda i, h: (i, 0))],
                        [((H, T, K_FULL), BF16, (None, tm_h, K_FULL), lambda i, h: (h, i, 0)),
                         ((H, T, 2 * V_DIM), BF16, (None, tm_h, 2 * V_DIM), lambda i, h: (h, i, 0))])

    a_out, lse = _attn_fwd(q_ext, k_full, v_aug, cosp, sinp)
    st_o, tok_o = _ag_mid("ag_o", st_o, a_out)
    tm_a = _row_tile(T, AW)
    mixed_a = _ew("rms_attn", lambda x, g: _rms(x, g), (T // tm_a,), [_rows(a_out, tm_a), _vec(attn_out_norm)],
                  [_rows_out((T, AW), BF16, tm_a)], deps=[tok_o])[0]
    mixed_g = _sgu_fwd(proj_b, v_ln_gain, v_ln_bias, ws, bsb, gmlp_out_norm)
    wo = _ag_finish("ag_o", st_o, mixed_g)[0].reshape(D, D)
    st_gu, tok_gu = _ag_mid("ag_gu", st_gu, wo)

    tm_, tn_ = _pick(T, 1024, LANE), _pick(D, 1024, LANE)
    tk_ = _fit_tk(AW, tm_, tn_, pairs=2)
    nka = AW // tk_
    mix_out = _mm("mm_out", "nn",
                  [(mixed_a, (tm_, tk_), lambda i, j, k: (i, k), wo, (tk_, tn_), lambda i, j, k: (k, j)),
                   (mixed_g, (tm_, tk_), lambda i, j, k: (i, k), wo, (tk_, tn_), lambda i, j, k: (k + nka, j))],
                  (T // tm_, D // tn_, nka), (tm_, tn_), (T, D), F32, (tm_, tn_), lambda i, j, k: (i, j), deps=[tok_gu])

    def post_mix(x, mo, g3, g4):
        h = x + _rms(mo, g3)
        return h, _rms(h, g4)

    h_res, hn = _ew("post_mix", post_mix, (T // tm_d,),
                    [_rows(x2, tm_d), _rows(mix_out, tm_d), _vec(post_mix_norm), _vec(pre_ffn_norm)],
                    [_rows_out((T, D), F32, tm_d), _rows_out((T, D), BF16, tm_d)])

    F = N_DEV * FP
    tn_f = _pick(F, FFN_TILE_MAX, MXU_DIM)
    tm_e = _pick(T, 512, LANE)

    def swiglu(u, g):
        g = g.astype(F32)
        return u, g * jax.nn.sigmoid(g) * u

    wg_t, wu_t = [g.reshape(F, D) for g in _ag_finish("ag_gu", st_gu, hn)]
    gate = _mm2("mm_gate", "nt", hn, wg_t, BF16, tn_pref=tn_f)
    st_d, tok_d = _ag_mid("ag_d", st_d, gate)
    up, act = _mm2("mm_up", "nt", hn, wu_t, BF16, tm_pref=tm_e, tn_pref=tn_f, deps=[tok_d], extras=[gate],
                   epilogue=swiglu, n_out=2)
    wd_f = _ag_finish("ag_d", st_d, act)[0].reshape(F, D)
    ffn = _mm2("mm_down", "nn", act, wd_f, F32, tk_max=tn_f)

    def loss_head(h, f, t, g5):
        y = h + _rms(f, g5)
        diff = y - t
        dy = diff / D
        d_f, dg5 = _rms_bwd(f, g5, dy)
        lsum = jnp.sum(jnp.sum(diff * diff, axis=0, keepdims=True), axis=1, keepdims=True) * (0.5 / D)
        return dy, d_f, dg5, lsum

    dy, d_ffn, dg_post_ffn, loss_part = _ew(
        "loss_head", loss_head, (T // tm_d,),
        [_rows(h_res, tm_d), _rows(ffn, tm_d), _rows(tgt, tm_d), _vec(post_ffn_norm)],
        [_rows_out((T, D), F32, tm_d), _rows_out((T, D), BF16, tm_d)], [_vec_out((1, D)), _vec_out((1, 1))])

    def swiglu_bwd(da, g, u):
        g, u = g.astype(F32), u.astype(F32)
        sg = jax.nn.sigmoid(g)
        return da * u * sg * (1.0 + g * (1.0 - sg)), da * g * sg

    def ffn_dw(name, z, y, deps):
        return _mm2(name, "tn", z, y, BF16, tm_pref=tn_f, deps=deps).reshape(N_DEV, FP, D)

    x_d, tok = _a2a_start("a2a_d", [ffn_dw("mm_dwd", act, d_ffn, [])], [])
    d_gate, d_up = _mm2("mm_dact", "nt", d_ffn, wd_f, BF16, tm_pref=tm_e, tn_pref=tn_f, deps=[tok], extras=[gate, up],
                        epilogue=swiglu_bwd, n_out=2)
    x_g, tok = _a2a_start("a2a_g", [ffn_dw("mm_dwg", d_gate, hn, [])], [])
    x_u, tok = _a2a_start("a2a_u", [ffn_dw("mm_dwu", d_up, hn, [tok])], [])
    d_hn_g = _mm2("mm_dhn_g", "nn", d_gate, wg_t, F32, tk_max=tn_f, deps=[tok])
    d_hn = _mm2("mm_dhn_u", "nn", d_up, wu_t, F32, tk_max=tn_f, extras=[d_hn_g], epilogue=lambda s, g: (s + g,))

    def pre_ffn_bwd(h, dhn, dy, mo, g4, g3):
        d1, dg4 = _rms_bwd(h, g4, dhn)
        dh = dy + d1
        dmo, dg3 = _rms_bwd(mo, g3, dh)
        return dh, dmo, dg4, dg3

    dh, d_mo, dg_pre_ffn, dg_post_mix = _ew(
        "pre_ffn_bwd", pre_ffn_bwd, (T // tm_d,),
        [_rows(h_res, tm_d), _rows(d_hn, tm_d), _rows(dy, tm_d), _rows(mix_out, tm_d), _vec(pre_ffn_norm),
         _vec(post_mix_norm)],
        [_rows_out((T, D), F32, tm_d), _rows_out((T, D), BF16, tm_d)], [_vec_out((1, D)), _vec_out((1, D))])

    d_mixed = _mm2("mm_dmixed", "nt", d_mo, wo, F32)
    dwo = jnp.concatenate([_mm2("mm_dwo_a", "tn", mixed_a, d_mo, BF16), _mm2("mm_dwo_g", "tn", mixed_g, d_mo, BF16)],
                          axis=0)
    x_o, tok = _a2a_start("a2a_o", [dwo.reshape(N_DEV, D // N_DEV, D)], [])

    d_a_out, dg_attn = _ew("rms_attn_bwd", lambda a, g, d: _rms_bwd(a, g, d), (T // tm_a,),
                           [_rows(a_out, tm_a), _vec(attn_out_norm), _rows(d_mixed, tm_a, AW, 0)],
                           [_rows_out((T, AW), BF16, tm_a)], [_vec_out((1, AW))], deps=[tok])
    d_pb, dg_gmlp, dws, dbs, dlng, dlnb = _sgu_bwd(proj_b, d_mixed, v_ln_gain, v_ln_bias, ws, ws_t, bsb, gmlp_out_norm)

    dq_ext, d_kv, d_kr = _attn_bwd(q_ext, k_full, v_aug, cosp, sinp, d_a_out, a_out, lse.reshape(H, 1, T))

    d_qn = _mm2("mm_dqn", "nt", dq_ext, wq_ext, F32)
    dwq_ext = _mm2("mm_dwq", "tn", qn, dq_ext, F32)
    d_kvn = _mm2("mm_dkvn", "nt", d_kv, wkv, F32)
    dwkv = _mm2("mm_dwkv", "tn", kvn, d_kv, BF16)

    d_qc, dg_q = _ew("rms_q_bwd", lambda a, g, d: _rms_bwd(a, g, d), (T // tm_q,),
                     [_rows(proj_a, tm_q, QL, 0), _vec(q_norm), _rows(d_qn, tm_q)],
                     [_rows_out((T, QL), BF16, tm_q)], [_vec_out((1, QL))])
    d_kvc, dg_kv = _ew("rms_kv_bwd", lambda a, g, d: _rms_bwd(a, g, d), (T // tm_q,),
                       [_rows(proj_a, tm_q, KVL, QL // KVL), _vec(kv_norm), _rows(d_kvn, tm_q)],
                       [_rows_out((T, KVL), BF16, tm_q)], [_vec_out((1, KVL))])
    d_pa = jnp.concatenate([d_qc, d_kvc, d_kr.astype(BF16)], axis=1)

    dwa_t = _mm2("mm_dwa", "tn", d_pa, xn, F32, tm_pref=w_a_t.shape[0])
    dwb_t = _mm2("mm_dwb", "tn", d_pb, xn, BF16)
    dkr_w = dwa_t[C0:C0 + ROPE_DIM] + _rot_t(dwa_t[C0 + LANE:C0 + LANE + ROPE_DIM], 0)
    dw_in = jnp.concatenate([dwa_t[:C0].astype(BF16), dkr_w.astype(BF16), dwb_t], axis=0)
    dw_in = dw_in.reshape(N_DEV, IWS, D)
    dq3 = dwq_ext.reshape(QL, H, Q_EXT)
    dwq = jnp.concatenate([dq3[..., :NOPE_DIM],
                           dq3[..., LANE:LANE + ROPE_DIM] + _rot_t(dq3[..., 2 * LANE:2 * LANE + ROPE_DIM])], axis=-1)
    dwq = dwq.astype(BF16).reshape(QL, N_DEV, H * QK_HEAD // N_DEV).transpose(1, 0, 2)
    dwkv = dwkv.reshape(KVL, N_DEV, w_ukv.shape[2]).transpose(1, 0, 2)
    p_e, tok = _rs_pair_start("rs_e", [dw_in, dwq, dwkv], [])
    d_xn_a = _mm2("mm_dxn_a", "nn", d_pa, w_a_t, F32, deps=[tok])
    x_e, tok = _rs_chip_start("rs_e2", _rs_pair_sum("rs_e", p_e, d_xn_a), [])
    d_xn_b = _mm2("mm_dxn_b", "nn", d_pb, w_b_t, F32, deps=[tok])

    def x_bwd(x, da, db, dh, g0):
        dx, dg0 = _rms_bwd(x, g0, da + db)
        return dh + dx, dg0

    grad_x, dg_pre_mix = _ew("x_bwd", x_bwd, (T // tm_d,),
                             [_rows(x2, tm_d), _rows(d_xn_a, tm_d), _rows(d_xn_b, tm_d), _rows(dh, tm_d),
                              _vec(pre_mix_norm)],
                             [_rows_out((T, D), F32, tm_d)], [_vec_out((1, D))])

    def adam_big(n, parts, transposed=False):
        view = shard_t if transposed else (lambda a: a[0])
        res = _adamw("adamw_" + n, parts, view(given[n]), view(given["m_" + n]), view(given["v_" + n]))
        return [(r.T if transposed else r)[None] for r in res]

    small_names = ["pre_mix_norm", "q_norm", "kv_norm", "v_ln_gain", "v_ln_bias", "w_spatial", "b_spatial",
                   "attn_out_norm", "gmlp_out_norm", "post_mix_norm", "pre_ffn_norm", "post_ffn_norm"]
    small_grads = [dg_pre_mix, dg_q, dg_kv, dlng, dlnb, dws, dbs[:, 0, :], dg_attn, dg_gmlp, dg_post_mix, dg_pre_ffn,
                   dg_post_ffn]
    n_small = sum(g.size for g in small_grads)
    pad_small = _round_up(n_small, SUBLANE * LANE) - n_small

    def pack(arrs):
        flat = jnp.concatenate([a.reshape(-1) for a in arrs] + [jnp.zeros((pad_small,), F32)])
        return flat.reshape(-1, LANE)

    st_s, tok = _ag_start("ag_s", [pack(small_grads)], [])

    results = {}
    results["w_down"] = adam_big("w_down", _a2a_finish("a2a_d", x_d, tok)[0])
    results["w_gate"] = adam_big("w_gate", _a2a_finish("a2a_g", x_g, results["w_down"][0])[0], True)
    st_s, tok = _ag_mid("ag_s", st_s, results["w_gate"][0])
    results["w_up"] = adam_big("w_up", _a2a_finish("a2a_u", x_u, tok)[0], True)
    results["w_out"] = adam_big("w_out", _a2a_finish("a2a_o", x_o, results["w_up"][0])[0])
    e_in, e_uq, e_ukv = _rs_chip_finish("rs_e2", x_e, results["w_out"][0])
    results["w_in"] = adam_big("w_in", e_in, True)
    results["w_uq"] = adam_big("w_uq", e_uq)
    results["w_ukv"] = adam_big("w_ukv", e_ukv)

    small_parts = _ag_finish("ag_s", st_s, results["w_ukv"][0])[0]
    sm = _adamw("adamw_small", small_parts, pack([given[n] for n in small_names]),
                pack([given["m_" + n] for n in small_names]), pack([given["v_" + n] for n in small_names]))

    off = 0
    for n in small_names:
        shp, size = given[n].shape, given[n].size
        results[n] = [r.reshape(-1)[off:off + size].reshape(shp) for r in sm]
        off += size

    loss = lax.psum(loss_part[0, 0], MESH_AXES)
    order = ["pre_mix_norm", "w_in", "q_norm", "kv_norm", "w_uq", "w_ukv", "v_ln_gain", "v_ln_bias", "w_spatial",
             "b_spatial", "attn_out_norm", "gmlp_out_norm", "w_out", "post_mix_norm", "pre_ffn_norm", "w_gate", "w_up",
             "w_down", "post_ffn_norm"]
    out = [loss, grad_x[None]]
    for kind in range(4):
        out += [results[n][kind] for n in order]
    return tuple(out)
```

```python
import math

import jax
import jax.numpy as jnp
from jax import lax
from jax.experimental import pallas as pl
from jax.experimental.pallas import tpu as pltpu

F32 = jnp.float32
BF16 = jnp.bfloat16
N_DEV = 8
MESH_AXES = ("x", "y", "c")
MESH = pl.DeviceIdType.MESH

NOPE_DIM = 128
ROPE_DIM = 64
V_DIM = 128
ROPE_THETA = 10000.0
CHUNK = 128
EPS = 1e-6
QK_HEAD = NOPE_DIM + ROPE_DIM
Q_EXT = 3 * 128
K_FULL = 2 * 128

ADAM_LR = 0.001
ADAM_B1 = 0.9
ADAM_B2 = 0.999
ADAM_EPS = 1e-08
ADAM_WD = 0.01
ADAM_STEP = 10

LANE = 128
SUBLANE = 8
VMEM_LIMIT = 56 * 1024 * 1024
MM_VMEM_BUDGET = 44 * 1024 * 1024
MXU_DIM = 256
FFN_TILE_MAX = 3072
MM_TK_MAX = 2048
EPILOGUE_ROWS = 256
ATTN_TQ = 512
ATTN_TKV = 512
NEG_BIG = -1e30
ROW_BLOCK_ELEMS = 512 * 1024
ADAM_BLOCK_ELEMS = 256 * 1024

DN = {
    "nn": (((1,), (0,)), ((), ())),
    "nt": (((1,), (1,)), ((), ())),
    "tn": (((0,), (0,)), ((), ())),
}


def _pick(dim, pref, mult):
    t = min(pref, dim)
    t -= t % mult
    while t >= mult:
        if dim % t == 0:
            return t
        t -= mult
    return dim


def _round_up(n, m):
    return (n + m - 1) // m * m


def _params(sem):
    return pltpu.CompilerParams(dimension_semantics=sem, vmem_limit_bytes=VMEM_LIMIT)


def _dep_specs(deps, nd):
    return [pl.BlockSpec(d.shape, lambda *_, r=d.ndim: (0,) * r) for d in deps]


def _mm(name, mode, pairs, grid, acc_shape, out_shape, out_dtype, out_block, out_map, deps=(), extras=(),
        epilogue=None, n_out=1):
    n, ne, nx = len(pairs), len(extras), len(deps)
    nk = grid[-1]
    nd = len(grid)
    dn = DN[mode]
    use_acc = nk > 1 or epilogue is not None

    def body(*refs):
        e_refs = refs[2 * n:2 * n + ne]
        o_refs = refs[2 * n + ne + nx:2 * n + ne + nx + n_out]
        k = pl.program_id(nd - 1)

        def dot(i):
            return lax.dot_general(refs[2 * i][...], refs[2 * i + 1][...], dn, preferred_element_type=F32)

        if not use_acc:
            s = dot(0)
            for i in range(1, n):
                s = s + dot(i)
            o_refs[0][...] = s.astype(o_refs[0].dtype)
            return
        acc = refs[2 * n + ne + nx + n_out]

        @pl.when(k == 0)
        def _():
            acc[...] = dot(0)

        if nk > 1:
            @pl.when(k > 0)
            def _():
                acc[...] += dot(0)

        for i in range(1, n):
            acc[...] += dot(i)

        if epilogue is None:
            @pl.when(k == nk - 1)
            def _():
                o_refs[0][...] = acc[...].astype(o_refs[0].dtype)
        else:
            @pl.when(k == nk - 1)
            def _():
                rows = acc_shape[0]
                slab = _pick(rows, EPILOGUE_ROWS, SUBLANE)
                for r0 in range(0, rows, slab):
                    sl = pl.ds(r0, slab)
                    vals = epilogue(acc[sl, :], *[e[sl, :] for e in e_refs])
                    for r, v in zip(o_refs, vals):
                        r[sl, :] = v.astype(r.dtype)

    in_specs, args = [], []
    for a, a_blk, a_map, b, b_blk, b_map in pairs:
        in_specs += [pl.BlockSpec(a_blk, a_map), pl.BlockSpec(b_blk, b_map)]
        args += [a, b]
    in_specs += [pl.BlockSpec(blk, imap) for (_, blk, imap) in extras]
    args += [e for (e, _, _) in extras]
    if not isinstance(out_dtype, (tuple, list)):
        out_dtype = (out_dtype,) * n_out
    res = pl.pallas_call(
        body,
        name=name,
        grid=grid,
        in_specs=in_specs + _dep_specs(deps, nd),
        out_specs=[pl.BlockSpec(out_block, out_map)] * n_out,
        out_shape=[jax.ShapeDtypeStruct(out_shape, dt) for dt in out_dtype],
        scratch_shapes=[pltpu.VMEM(acc_shape, F32)] if use_acc else [],
        compiler_params=_params(("parallel",) * (nd - 1) + ("arbitrary",)),
    )(*args, *deps)
    return res[0] if n_out == 1 else res


def _fit_tk(K, tm, tn, pairs=1, out_bytes=4, fixed_extra=0, tk_max=MM_TK_MAX):
    fixed = tm * tn * 4 + 2 * tm * tn * out_bytes + fixed_extra
    tk = _pick(K, tk_max, LANE)
    while tk > LANE and fixed + pairs * 4 * (tm + tn) * tk > MM_VMEM_BUDGET:
        smaller = _pick(K, tk - LANE, LANE)
        if smaller >= tk:
            break
        tk = smaller
    return tk


def _mm2(name, mode, a, b, out_dtype, tm_pref=1024, tn_pref=1024, deps=(), extras=(), epilogue=None, n_out=1,
         tk_max=MM_TK_MAX):
    if mode == "nn":
        (M, K), N = a.shape, b.shape[1]
    elif mode == "nt":
        (M, K), N = a.shape, b.shape[0]
    else:
        (K, M), N = a.shape, b.shape[1]
    tm, tn = _pick(M, tm_pref, LANE), _pick(N, tn_pref, LANE)
    extra_bytes = sum(2 * tm * tn * jnp.dtype(e.dtype).itemsize for e in extras)
    tk = _fit_tk(K, tm, tn, out_bytes=n_out * jnp.dtype(out_dtype).itemsize, fixed_extra=extra_bytes, tk_max=tk_max)
    if mode == "tn":
        a_blk, a_map = (tk, tm), lambda i, j, k: (k, i)
    else:
        a_blk, a_map = (tm, tk), lambda i, j, k: (i, k)
    if mode == "nt":
        b_blk, b_map = (tn, tk), lambda i, j, k: (j, k)
    else:
        b_blk, b_map = (tk, tn), lambda i, j, k: (k, j)
    return _mm(name, mode, [(a, a_blk, a_map, b, b_blk, b_map)], (M // tm, N // tn, K // tk), (tm, tn),
               (M, N), out_dtype, (tm, tn), lambda i, j, k: (i, j), deps=deps,
               extras=[(e, (tm, tn), lambda i, j, k: (i, j)) for e in extras], epilogue=epilogue, n_out=n_out)


def _ew(name, fn, grid, ins, outs, accs=(), deps=()):
    ni, no, na = len(ins), len(outs), len(accs)
    nd = len(grid)
    nx = len(deps)

    def body(*refs):
        vals = fn(*[r[...] for r in refs[:ni]])
        if not isinstance(vals, (tuple, list)):
            vals = (vals,)
        for r, v in zip(refs[ni + nx:ni + nx + no], vals[:no]):
            r[...] = v.astype(r.dtype)
        if na:
            first = pl.program_id(nd - 1) == 0
            for r, v in zip(refs[ni + nx + no:], vals[no:]):
                @pl.when(first)
                def _(r=r, v=v):
                    r[...] = v.astype(r.dtype)

                @pl.when(jnp.logical_not(first))
                def _(r=r, v=v):
                    r[...] += v.astype(r.dtype)

    all_outs = list(outs) + list(accs)
    sem = ("parallel",) * (nd - 1) + (("arbitrary",) if na else ("parallel",))
    res = pl.pallas_call(
        body,
        name=name,
        grid=grid,
        in_specs=[pl.BlockSpec(blk, imap) for (_, blk, imap) in ins] + _dep_specs(deps, nd),
        out_specs=[pl.BlockSpec(blk, imap) for (_, _, blk, imap) in all_outs],
        out_shape=[jax.ShapeDtypeStruct(s, d) for (s, d, _, _) in all_outs],
        compiler_params=_params(sem),
    )(*[a for (a, _, _) in ins], *deps)
    return res


def _rows(a, tm, width=None, cblk=0):
    width = a.shape[1] if width is None else width
    return (a, (tm, width), lambda i, c=cblk: (i, c))


def _vec(p):
    return (p, p.shape, lambda i: (0,) * p.ndim)


def _rows_out(shape, dtype, tm):
    return (shape, dtype, (tm, shape[1]), lambda i: (i, 0))


def _vec_out(shape):
    return (shape, F32, shape, lambda i: (0,) * len(shape))


def _row_tile(T, D):
    return _pick(T, max(SUBLANE, ROW_BLOCK_ELEMS // D), SUBLANE)


def _rms(x, g):
    r = lax.rsqrt(jnp.mean(x * x, axis=-1, keepdims=True) + EPS)
    return x * r * g


def _rms_bwd(x, g, dy):
    r = lax.rsqrt(jnp.mean(x * x, axis=-1, keepdims=True) + EPS)
    xh = x * r
    dg = jnp.sum(dy * xh, axis=0, keepdims=True)
    dxh = dy * g
    dx = r * (dxh - xh * jnp.mean(dxh * xh, axis=-1, keepdims=True))
    return dx, dg


GELU_C = math.sqrt(2.0 / math.pi)
GELU_A = 0.044715


def _sigmoid(x):
    return 0.5 * jnp.tanh(0.5 * x) + 0.5


def _gelu(x):
    return 0.5 * x * (1.0 + jnp.tanh(GELU_C * (x + GELU_A * x * x * x)))


def _gelu_grad(x):
    t = jnp.tanh(GELU_C * (x + GELU_A * x * x * x))
    return 0.5 * (1.0 + t) + 0.5 * x * (1.0 - t * t) * GELU_C * (1.0 + 3.0 * GELU_A * x * x)


def _rot(w, axis=-1):
    w1, w2 = jnp.split(w, 2, axis=axis)
    return jnp.concatenate([-w2, w1], axis=axis)


def _rot_t(dw, axis=-1):
    d1, d2 = jnp.split(dw, 2, axis=axis)
    return jnp.concatenate([d2, -d1], axis=axis)


def _coords():
    return lax.axis_index("x"), lax.axis_index("y"), lax.axis_index("c")


def _all_gather(name, shard, deps=()):
    def body(x_ref, *rest):
        o_ref, send_sems, recv_sems, local_sem = rest[len(deps):]
        xi, yi, ci = _coords()
        me, sibling = (xi, yi, ci), (xi, yi, 1 - ci)
        chips = [(1 - xi, yi), (xi, 1 - yi), (1 - xi, 1 - yi)]

        def slot(px, py, pc):
            return o_ref.at[4 * px + 2 * py + pc]

        def copy(k, block, to, src=None):
            return pltpu.make_async_remote_copy(
                src_ref=slot(*block) if src is None else src, dst_ref=slot(*block),
                send_sem=send_sems.at[k], recv_sem=recv_sems.at[k], device_id=to, device_id_type=MESH)

        mine = pltpu.make_async_copy(x_ref, slot(*me), local_sem)
        mine.start()
        first = [copy(0, me, sibling, src=x_ref)]
        first += [copy(1 + j, me, (*chip, ci), src=x_ref) for j, chip in enumerate(chips)]
        for cp in first:
            cp.start()
        passed = [copy(4 + j, (*chip, ci), sibling) for j, chip in enumerate(chips)]
        for j, chip in enumerate(chips):
            copy(1 + j, (*chip, ci), me).wait_recv()
            passed[j].start()
        copy(0, sibling, me).wait_recv()
        for j, chip in enumerate(chips):
            copy(4 + j, (*chip, 1 - ci), me).wait_recv()
        for cp in first + passed:
            cp.wait_send()
        mine.wait()

    return pl.pallas_call(
        body,
        name=name,
        out_shape=jax.ShapeDtypeStruct((N_DEV,) + shard.shape, shard.dtype),
        in_specs=[pl.BlockSpec(memory_space=pltpu.HBM)] + [pl.BlockSpec(memory_space=pl.ANY)] * len(deps),
        out_specs=pl.BlockSpec(memory_space=pltpu.HBM),
        scratch_shapes=[pltpu.SemaphoreType.DMA((N_DEV - 1,)), pltpu.SemaphoreType.DMA((N_DEV - 1,)),
                        pltpu.SemaphoreType.DMA],
    )(shard, *deps)


HBM_SPEC = pl.BlockSpec(memory_space=pltpu.HBM)
SEM_SPEC = pl.BlockSpec(memory_space=pltpu.SEMAPHORE)
ANY_SPEC = pl.BlockSpec(memory_space=pl.ANY)
EFFECT = pltpu.SideEffectType.DATAFLOW_SIDE_EFFECTING
TOKEN = (SUBLANE, LANE)


def _my_slot():
    xi, yi, ci = _coords()
    return 4 * xi + 2 * yi + ci


def _own_slot_filled(block):
    land = lax.empty((N_DEV,) + block.shape, block.dtype)
    return lax.dynamic_update_slice(land, block[None], (_my_slot(),) + (0,) * block.ndim)


def _split_start(name, n, n_copies, issue, srcs, lands, deps):
    def body(*refs):
        src_refs, land_refs = refs[:n], refs[n:2 * n]
        outs = refs[2 * n + len(deps):]
        send, recv, token = outs[:n], outs[n:2 * n], outs[4 * n]
        for t in range(n):
            issue(t, src_refs[t], land_refs[t], send[t], recv[t])
        token[...] = jnp.zeros_like(token)

    sems = [pltpu.SemaphoreType.DMA((n_copies,))] * (2 * n)
    res = pl.pallas_call(
        body,
        name=name,
        in_specs=[HBM_SPEC] * (2 * n) + [ANY_SPEC] * len(deps),
        out_specs=[SEM_SPEC] * (2 * n) + [HBM_SPEC] * (2 * n) + [pl.BlockSpec(memory_space=pltpu.VMEM)],
        out_shape=sems + [pltpu.HBM(a.shape, a.dtype) for a in list(srcs) + list(lands)]
        + [jax.ShapeDtypeStruct(TOKEN, F32)],
        input_output_aliases={i: 2 * n + i for i in range(2 * n)},
        compiler_params=pltpu.CompilerParams(has_side_effects=EFFECT),
    )(*[pltpu.with_memory_space_constraint(a, pltpu.HBM) for a in list(srcs) + list(lands)], *deps)
    state = dict(n=n, n_copies=n_copies, send=res[:n], recv=res[n:2 * n], srcs=res[2 * n:3 * n],
                 lands=res[3 * n:4 * n])
    return state, res[4 * n]


def _split_wait(name, state, unit_src, after):
    n, n_copies = state["n"], state["n_copies"]

    def body(*refs):
        src_refs, land_refs = refs[:n], refs[n:2 * n]
        send, recv = refs[2 * n:3 * n], refs[3 * n:4 * n]
        me = _coords()
        for t in range(n):
            for k in range(n_copies):
                cp = pltpu.make_async_remote_copy(
                    src_ref=unit_src(src_refs[t]), dst_ref=land_refs[t].at[0], send_sem=send[t].at[k],
                    recv_sem=recv[t].at[k], device_id=me, device_id_type=MESH)
                cp.wait_send()
                cp.wait_recv()

    bufs = list(state["srcs"]) + list(state["lands"])
    res = pl.pallas_call(
        body,
        name=name,
        in_specs=[HBM_SPEC] * (2 * n) + [SEM_SPEC] * (2 * n) + [ANY_SPEC],
        out_specs=[HBM_SPEC] * (2 * n),
        out_shape=[pltpu.HBM(a.shape, a.dtype) for a in bufs],
        input_output_aliases={i: i for i in range(2 * n)},
        compiler_params=pltpu.CompilerParams(has_side_effects=EFFECT),
    )(*bufs, *state["send"], *state["recv"], after)
    return list(res[n:])


def _ag_start(name, shards, deps):
    def issue(t, x_ref, land_ref, send, recv):
        xi, yi, ci = _coords()
        to = [(xi, yi, 1 - ci), (1 - xi, yi, ci), (xi, 1 - yi, ci), (1 - xi, 1 - yi, ci)]
        for k, dev in enumerate(to):
            pltpu.make_async_remote_copy(src_ref=x_ref, dst_ref=land_ref.at[4 * xi + 2 * yi + ci], send_sem=send.at[k],
                                         recv_sem=recv.at[k], device_id=dev, device_id_type=MESH).start()

    return _split_start(name, len(shards), 4, issue, shards, [_own_slot_filled(s) for s in shards], deps)


def _ag_mid(name, state, after):
    lands = _split_wait(name + "_w1", state, lambda r: r, after)

    def issue(t, src_ref, land_ref, send, recv):
        xi, yi, ci = _coords()
        for k, (px, py) in enumerate([(1 - xi, yi), (xi, 1 - yi), (1 - xi, 1 - yi)]):
            rows = land_ref.at[4 * px + 2 * py + ci]
            pltpu.make_async_remote_copy(src_ref=rows, dst_ref=rows, send_sem=send.at[k], recv_sem=recv.at[k],
                                         device_id=(xi, yi, 1 - ci), device_id_type=MESH).start()

    n = len(lands)

    def body(*refs):
        land_refs = refs[:n]
        outs = refs[n:]
        send, recv, token = outs[:n], outs[n:2 * n], outs[3 * n]
        for t in range(n):
            issue(t, land_refs[t], land_refs[t], send[t], recv[t])
        token[...] = jnp.zeros_like(token)

    res = pl.pallas_call(
        body,
        name=name + "_s2",
        in_specs=[HBM_SPEC] * n,
        out_specs=[SEM_SPEC] * (2 * n) + [HBM_SPEC] * n + [pl.BlockSpec(memory_space=pltpu.VMEM)],
        out_shape=[pltpu.SemaphoreType.DMA((3,))] * (2 * n) + [pltpu.HBM(a.shape, a.dtype) for a in lands]
        + [jax.ShapeDtypeStruct(TOKEN, F32)],
        input_output_aliases={i: 2 * n + i for i in range(n)},
        compiler_params=pltpu.CompilerParams(has_side_effects=EFFECT),
    )(*lands)
    state2 = dict(n=n, send=res[:n], recv=res[n:2 * n], lands=res[2 * n:3 * n])
    return state2, res[3 * n]


def _ag_finish(name, state2, after):
    n = state2["n"]

    def body(*refs):
        land_refs, send, recv = refs[:n], refs[n:2 * n], refs[2 * n:3 * n]
        me = _coords()
        for t in range(n):
            for k in range(3):
                cp = pltpu.make_async_remote_copy(
                    src_ref=land_refs[t].at[0], dst_ref=land_refs[t].at[0], send_sem=send[t].at[k],
                    recv_sem=recv[t].at[k], device_id=me, device_id_type=MESH)
                cp.wait_send()
                cp.wait_recv()

    res = pl.pallas_call(
        body,
        name=name + "_w2",
        in_specs=[HBM_SPEC] * n + [SEM_SPEC] * (2 * n) + [ANY_SPEC],
        out_specs=[HBM_SPEC] * n,
        out_shape=[pltpu.HBM(a.shape, a.dtype) for a in state2["lands"]],
        input_output_aliases={i: i for i in range(n)},
        compiler_params=pltpu.CompilerParams(has_side_effects=EFFECT),
    )(*state2["lands"], *state2["send"], *state2["recv"], after)
    return list(res)


def _a2a_start(name, parts, deps):
    def issue(t, x_ref, land_ref, send, recv):
        xi, yi, ci = _coords()
        for k in range(N_DEV - 1):
            px = 1 - xi if (k + 1) & 4 else xi
            py = 1 - yi if (k + 1) & 2 else yi
            pc = 1 - ci if (k + 1) & 1 else ci
            pltpu.make_async_remote_copy(src_ref=x_ref.at[4 * px + 2 * py + pc], dst_ref=land_ref.at[4 * xi + 2 * yi + ci],
                                         send_sem=send.at[k], recv_sem=recv.at[k], device_id=(px, py, pc),
                                         device_id_type=MESH).start()

    lands = [_own_slot_filled(lax.dynamic_index_in_dim(p, _my_slot(), 0, keepdims=False)) for p in parts]
    return _split_start(name, len(parts), N_DEV - 1, issue, parts, lands, deps)


def _a2a_finish(name, state, after):
    return _split_wait(name + "_w", state, lambda r: r.at[0], after)


N_CHIP = N_DEV // 2


def _rs_pair_start(name, parts, deps):
    def issue(t, x_ref, land_ref, send, recv):
        xi, yi, ci = _coords()
        for k in range(N_CHIP):
            pltpu.make_async_remote_copy(src_ref=x_ref.at[2 * k + (1 - ci)], dst_ref=land_ref.at[k], send_sem=send.at[k],
                                         recv_sem=recv.at[k], device_id=(xi, yi, 1 - ci), device_id_type=MESH).start()

    lands = [lax.empty((N_CHIP,) + p.shape[1:], p.dtype) for p in parts]
    return _split_start(name, len(parts), N_CHIP, issue, parts, lands, deps)


def _rs_pair_sum(name, state, after):
    lands = _split_wait(name + "_w", state, lambda r: r.at[0], after)
    sums = []
    for t, (part, land) in enumerate(zip(state["srcs"], lands)):
        _, R, C = land.shape
        pack = SUBLANE * (4 // jnp.dtype(land.dtype).itemsize)
        if R % pack == 0:
            tr, tc = _pick(R, max(pack, ROW_BLOCK_ELEMS // C), pack), C
        else:
            tr, tc = R, _pick(C, max(LANE, ROW_BLOCK_ELEMS // R // LANE * LANE), LANE)

        def fn(mine, theirs):
            own = jnp.where(lax.axis_index("c") == 0, mine[0], mine[1])
            return own.astype(F32) + theirs.astype(F32)

        sums.append(_ew(f"{name}_sum{t}", fn, (N_CHIP, R // tr, C // tc),
                        [(part.reshape((N_CHIP, 2) + part.shape[1:]), (None, 2, tr, tc), lambda k, i, j: (k, 0, i, j)),
                         (land, (None, tr, tc), lambda k, i, j: (k, i, j))],
                        [(land.shape, land.dtype, (None, tr, tc), lambda k, i, j: (k, i, j))])[0])
    return sums


def _rs_chip_start(name, sums, deps):
    def issue(t, x_ref, land_ref, send, recv):
        xi, yi, ci = _coords()
        for k, (px, py) in enumerate([(1 - xi, yi), (xi, 1 - yi), (1 - xi, 1 - yi)]):
            pltpu.make_async_remote_copy(src_ref=x_ref.at[2 * px + py], dst_ref=land_ref.at[2 * xi + yi], send_sem=send.at[k],
                                         recv_sem=recv.at[k], device_id=(px, py, ci), device_id_type=MESH).start()

    xi, yi, _ = _coords()
    lands = []
    for s in sums:
        own = lax.dynamic_index_in_dim(s, 2 * xi + yi, 0, keepdims=True)
        lands.append(lax.dynamic_update_slice(lax.empty(s.shape, s.dtype), own, (2 * xi + yi,) + (0,) * (s.ndim - 1)))
    return _split_start(name, len(sums), N_CHIP - 1, issue, sums, lands, deps)


def _rs_chip_finish(name, state, after):
    return _split_wait(name + "_w", state, lambda r: r.at[0], after)


QK_SCALE = 1.0 / math.sqrt(QK_HEAD)
LOG2E = 1.0 / math.log(2.0)


def _roped_q(q_ext, cos, sin):
    q = jnp.concatenate([q_ext[:, :LANE], q_ext[:, LANE:2 * LANE] * cos + q_ext[:, 2 * LANE:] * sin], axis=1)
    return (q * (QK_SCALE * LOG2E)).astype(BF16)


def _attn_fwd(q_ext, k_full, v_aug, cosp, sinp):
    H, T, _ = k_full.shape
    tq, tkv = _pick(T, ATTN_TQ, LANE), _pick(T, ATTN_TKV, LANE)

    def body(q_ref, cos_ref, sin_ref, k_ref, v_ref, o_ref, lse_ref):
        q = _roped_q(q_ref[...], cos_ref[...], sin_ref[...])
        m = jnp.full((tq, 1), NEG_BIG, F32)
        acc = jnp.zeros((tq, 2 * V_DIM), F32)
        for c in range(T // tkv):
            keys = pl.ds(c * tkv, tkv)
            s = lax.dot_general(q, k_ref[keys, :], DN["nt"], preferred_element_type=F32)
            m_new = jnp.maximum(m, jnp.max(s, axis=-1, keepdims=True))
            p = jnp.exp2(s - m_new).astype(BF16)
            acc = jnp.exp2(m - m_new) * acc + jnp.dot(p, v_ref[keys, :], preferred_element_type=F32)
            m = m_new
        l = acc[:, V_DIM:]
        o_ref[...] = acc[:, :V_DIM] / l
        lse_ref[...] = m + jnp.log2(jnp.max(l, axis=-1, keepdims=True))

    return pl.pallas_call(
        body,
        name="attn_fwd",
        grid=(H, T // tq),
        in_specs=[pl.BlockSpec((tq, Q_EXT), lambda h, i: (i, h)),
                  pl.BlockSpec((tq, LANE), lambda h, i: (i, 0)),
                  pl.BlockSpec((tq, LANE), lambda h, i: (i, 0)),
                  pl.BlockSpec((None, T, K_FULL), lambda h, i: (h, 0, 0)),
                  pl.BlockSpec((None, T, 2 * V_DIM), lambda h, i: (h, 0, 0))],
        out_specs=[pl.BlockSpec((tq, V_DIM), lambda h, i: (i, h)),
                   pl.BlockSpec((None, tq, 1), lambda h, i: (h, i, 0))],
        out_shape=[jax.ShapeDtypeStruct((T, H * V_DIM), F32), jax.ShapeDtypeStruct((H, T, 1), F32)],
        compiler_params=_params(("parallel", "parallel")),
    )(q_ext, cosp, sinp, k_full, v_aug)


def _attn_bwd(q_ext, k_full, v_aug, cosp, sinp, d_out, out, lse_rows):
    H, T, _ = k_full.shape
    tq, tkv = _pick(T, ATTN_TQ, LANE), _pick(T, ATTN_TKV, LANE)
    n_q = T // tq

    def body(q_ref, cos_ref, sin_ref, k_ref, v_ref, do_ref, o_ref, lse_ref, cos_all, sin_all,
             dq_ref, dkv_ref, dkr_ref, dk_acc, dv_acc):
        h, i = pl.program_id(0), pl.program_id(1)

        @pl.when(i == 0)
        def _():
            dk_acc[...] = jnp.zeros_like(dk_acc)
            dv_acc[...] = jnp.zeros_like(dv_acc)

        cos, sin = cos_ref[...], sin_ref[...]
        q, do, lse = _roped_q(q_ref[...], cos, sin), do_ref[...], lse_ref[...]
        delta = lax.dot_general(jnp.ones((SUBLANE, V_DIM), F32), do.astype(F32) * o_ref[...], DN["nt"],
                                preferred_element_type=F32, precision=lax.Precision.HIGHEST)[0:1]
        dq = jnp.zeros((tq, K_FULL), F32)
        for c in range(T // tkv):
            keys = pl.ds(c * tkv, tkv)
            kc = k_ref[keys, :]
            pt = jnp.exp2(lax.dot_general(kc, q, DN["nt"], preferred_element_type=F32) - lse)
            dpt = lax.dot_general(v_ref[keys, pl.ds(0, V_DIM)], do, DN["nt"], preferred_element_type=F32)
            dst = (pt * (dpt - delta)).astype(BF16)
            dv_acc[keys, :] += jnp.dot(pt.astype(BF16), do, preferred_element_type=F32)
            dk_acc[keys, :] += jnp.dot(dst, q, preferred_element_type=F32)
            dq = dq + lax.dot_general(dst, kc, DN["tn"], preferred_element_type=F32)
        r = dq[:, LANE:]
        dq_ref[...] = (jnp.concatenate([dq[:, :LANE], r * cos, r * sin], axis=1) * QK_SCALE).astype(BF16)

        @pl.when(i == n_q - 1)
        def _():
            dk = dk_acc[...] * (1.0 / LOG2E)
            dkv_ref[...] = jnp.concatenate([dk[:, :LANE], dv_acc[...]], axis=1).astype(BF16)
            rk = dk[:, LANE:]
            dkr = jnp.concatenate([rk * cos_all[...], rk * sin_all[...]], axis=1)

            @pl.when(h == 0)
            def _():
                dkr_ref[...] = dkr

            @pl.when(h != 0)
            def _():
                dkr_ref[...] += dkr

    return pl.pallas_call(
        body,
        name="attn_bwd",
        grid=(H, n_q),
        in_specs=[pl.BlockSpec((tq, Q_EXT), lambda h, i: (i, h)),
                  pl.BlockSpec((tq, LANE), lambda h, i: (i, 0)),
                  pl.BlockSpec((tq, LANE), lambda h, i: (i, 0)),
                  pl.BlockSpec((None, T, K_FULL), lambda h, i: (h, 0, 0)),
                  pl.BlockSpec((None, T, 2 * V_DIM), lambda h, i: (h, 0, 0)),
                  pl.BlockSpec((tq, V_DIM), lambda h, i: (i, h)),
                  pl.BlockSpec((tq, V_DIM), lambda h, i: (i, h)),
                  pl.BlockSpec((None, 1, tq), lambda h, i: (h, 0, i)),
                  pl.BlockSpec((T, LANE), lambda h, i: (0, 0)),
                  pl.BlockSpec((T, LANE), lambda h, i: (0, 0))],
        out_specs=[pl.BlockSpec((tq, Q_EXT), lambda h, i: (i, h)),
                   pl.BlockSpec((T, K_FULL), lambda h, i: (0, h)),
                   pl.BlockSpec((T, K_FULL), lambda h, i: (0, 0))],
        out_shape=[jax.ShapeDtypeStruct((T, H * Q_EXT), BF16), jax.ShapeDtypeStruct((T, H * K_FULL), BF16),
                   jax.ShapeDtypeStruct((T, K_FULL), F32)],
        scratch_shapes=[pltpu.VMEM((T, K_FULL), F32), pltpu.VMEM((T, V_DIM), F32)],
        compiler_params=_params(("arbitrary", "arbitrary")),
    )(q_ext, cosp, sinp, k_full, v_aug, d_out, out, lse_rows, cosp, sinp)


def _sgu_forward_math(pb, lng, lnb, ws, bsb, gw, gh):
    u_raw, v_raw = pb[:, :gw], pb[:, gw:]
    u, v = _gelu(u_raw), _gelu(v_raw)
    vc = v - jnp.mean(v, axis=-1, keepdims=True)
    rstd = lax.rsqrt(jnp.mean(vc * vc, axis=-1, keepdims=True) + EPS)
    vhat = vc * rstd
    vln = vhat * lng + lnb
    hd = gw // gh
    s = jnp.concatenate(
        [jnp.dot(ws[g], vln[:, g * hd:(g + 1) * hd].astype(BF16), preferred_element_type=F32) + bsb[g]
         for g in range(gh)], axis=1)
    return u_raw, v_raw, u, rstd, vhat, vln, s


def _sgu_fwd(proj_b, lng, lnb, ws, bsb, gg):
    T, gw2 = proj_b.shape
    gw, gh = gw2 // 2, ws.shape[0]

    def fn(pb, lng, lnb, ws, bsb, gg):
        _, _, u, _, _, _, s = _sgu_forward_math(pb, lng, lnb, ws, bsb, gw, gh)
        return _rms(u * s, gg)

    return _ew("sgu_fwd", fn, (T // CHUNK,),
               [_rows(proj_b, CHUNK), _vec(lng), _vec(lnb), _vec(ws), _vec(bsb), _vec(gg)],
               [_rows_out((T, gw), BF16, CHUNK)])[0]


def _sgu_bwd(proj_b, d_mixed, lng, lnb, ws, ws_t, bsb, gg):
    T, gw2 = proj_b.shape
    gw, gh = gw2 // 2, ws.shape[0]
    hd = gw // gh

    def fn(pb, dmg, lng, lnb, ws, ws_t, bsb, gg):
        u_raw, v_raw, u, rstd, vhat, vln, s = _sgu_forward_math(pb, lng, lnb, ws, bsb, gw, gh)
        dgo, dgg = _rms_bwd(u * s, gg, dmg)
        du, ds = dgo * s, dgo * u
        ones = jnp.ones((SUBLANE, hd), F32)
        dws, dbs, dvln = [], [], []
        for g in range(gh):
            ds_g = ds[:, g * hd:(g + 1) * hd]
            ds_gb = ds_g.astype(BF16)
            dws.append(lax.dot_general(ds_gb, vln[:, g * hd:(g + 1) * hd].astype(BF16), DN["nt"],
                                       preferred_element_type=F32)[None])
            dbs.append(lax.dot_general(ones, ds_g, DN["nt"], preferred_element_type=F32,
                                       precision=lax.Precision.HIGHEST)[None])
            dvln.append(jnp.dot(ws_t[g], ds_gb, preferred_element_type=F32))
        dvln = jnp.concatenate(dvln, axis=1)
        dlng = jnp.sum(dvln * vhat, axis=0, keepdims=True)
        dlnb = jnp.sum(dvln, axis=0, keepdims=True)
        dvh = dvln * lng
        dv = rstd * (dvh - jnp.mean(dvh, axis=-1, keepdims=True) - vhat * jnp.mean(dvh * vhat, axis=-1, keepdims=True))
        d_pb = jnp.concatenate([du * _gelu_grad(u_raw), dv * _gelu_grad(v_raw)], axis=1)
        return d_pb, dgg, jnp.concatenate(dws, axis=0), jnp.concatenate(dbs, axis=0), dlng, dlnb

    return _ew("sgu_bwd", fn, (T // CHUNK,),
               [_rows(proj_b, CHUNK), _rows(d_mixed, CHUNK, gw, 1), _vec(lng), _vec(lnb), _vec(ws), _vec(ws_t),
                _vec(bsb), _vec(gg)],
               [_rows_out((T, gw2), BF16, CHUNK)],
               [_vec_out((1, gw)), _vec_out((gh, CHUNK, CHUNK)), _vec_out((gh, SUBLANE, CHUNK)), _vec_out((1, gw)),
                _vec_out((1, gw))])


def _adamw(name, parts, w, m, v):
    R, C = w.shape
    pack = SUBLANE * (4 // jnp.dtype(parts.dtype).itemsize)
    if R % pack == 0:
        tm, tc = _pick(R, max(pack, ADAM_BLOCK_ELEMS // C), pack), C
    else:
        parts = parts[:, :R]
        tm, tc = R, _pick(C, max(LANE, ADAM_BLOCK_ELEMS // R // LANE * LANE), LANE)

    n_parts = parts.shape[0]

    def fn(p, w, m, v):
        g = p[0].astype(F32)
        for i in range(1, n_parts):
            g = g + p[i].astype(F32)
        m2 = ADAM_B1 * m + (1.0 - ADAM_B1) * g
        v2 = ADAM_B2 * v + (1.0 - ADAM_B2) * (g * g)
        m_hat = m2 / (1.0 - ADAM_B1 ** ADAM_STEP)
        v_hat = v2 / (1.0 - ADAM_B2 ** ADAM_STEP)
        delta = -ADAM_LR * (m_hat / (jnp.sqrt(v_hat) + ADAM_EPS) + ADAM_WD * w)
        return g, delta, m2, v2

    blk = lambda a: (a, (tm, tc), lambda i, j: (i, j))
    return _ew(name, fn, (R // tm, C // tc),
               [(parts, (n_parts, tm, tc), lambda i, j: (0, i, j)), blk(w), blk(m), blk(v)],
               [((R, C), F32, (tm, tc), lambda i, j: (i, j))] * 4)


def kernel(x, positions, pre_mix_norm, w_in, q_norm, kv_norm, w_uq, w_ukv, v_ln_gain, v_ln_bias, w_spatial, b_spatial, attn_out_norm, gmlp_out_norm, w_out, post_mix_norm, pre_ffn_norm, w_gate, w_up, w_down, post_ffn_norm, loss_target, m_pre_mix_norm, m_w_in, m_q_norm, m_kv_norm, m_w_uq, m_w_ukv, m_v_ln_gain, m_v_ln_bias, m_w_spatial, m_b_spatial, m_attn_out_norm, m_gmlp_out_norm, m_w_out, m_post_mix_norm, m_pre_ffn_norm, m_w_gate, m_w_up, m_w_down, m_post_ffn_norm, v_pre_mix_norm, v_w_in, v_q_norm, v_kv_norm, v_w_uq, v_w_ukv, v_v_ln_gain, v_v_ln_bias, v_w_spatial, v_b_spatial, v_attn_out_norm, v_gmlp_out_norm, v_w_out, v_post_mix_norm, v_pre_ffn_norm, v_w_gate, v_w_up, v_w_down, v_post_ffn_norm):
    given = dict(locals())
    T, D = x.shape[1], x.shape[2]
    QL, KVL = w_uq.shape[1], w_ukv.shape[1]
    GW, GH = v_ln_gain.shape[1], w_spatial.shape[1]
    IWS = w_in.shape[2]
    IW = IWS * N_DEV
    H = w_ukv.shape[2] * N_DEV // (NOPE_DIM + V_DIM)
    AW = H * V_DIM
    FS = w_gate.shape[2]
    FP = _round_up(FS, LANE)
    C0 = QL + KVL
    assert IW == C0 + ROPE_DIM + 2 * GW and AW + GW == D and AW == GW
    assert QL % KVL == 0 and C0 % K_FULL == 0 and w_spatial.shape[2] == CHUNK and T % CHUNK == 0
    assert w_uq.shape[2] * N_DEV == H * QK_HEAD and D % N_DEV == 0

    x2, tgt = x[0], loss_target[0]
    tm_d = _row_tile(T, D)

    inv_freq = 1.0 / (ROPE_THETA ** (jnp.arange(0, ROPE_DIM, 2, dtype=F32) / ROPE_DIM))
    ang = positions[0].astype(F32)[:, None] * inv_freq
    zpad = jnp.zeros((T, LANE - ROPE_DIM), F32)
    cosp = jnp.concatenate([jnp.cos(ang), jnp.cos(ang), zpad], axis=1)
    sinp = jnp.concatenate([jnp.sin(ang), jnp.sin(ang), zpad], axis=1)

    def shard_t(a):
        return a[0].T

    def padded_rows(a):
        return jnp.pad(a.astype(BF16), ((0, FP - FS), (0, 0)))

    st_a, tok = _ag_start("ag_a", [shard_t(w_in).astype(BF16), w_uq[0].astype(BF16), w_ukv[0].astype(BF16)], [])
    st_o, tok = _ag_start("ag_o", [w_out[0].astype(BF16)], [tok])
    st_gu, tok = _ag_start("ag_gu", [padded_rows(shard_t(w_gate)), padded_rows(shard_t(w_up))], [tok])
    st_d, tok = _ag_start("ag_d", [padded_rows(w_down[0])], [tok])

    xn = _ew("rms_x", lambda x, g: _rms(x, g), (T // tm_d,), [_rows(x2, tm_d), _vec(pre_mix_norm)],
             [_rows_out((T, D), BF16, tm_d)], deps=[tok])[0]
    st_a, tok = _ag_mid("ag_a", st_a, xn)
    g_in, g_uq, g_ukv = _ag_finish("ag_a", st_a, tok)

    w_in_t = g_in.reshape(IW, D)
    kr_w = w_in_t[C0:C0 + ROPE_DIM]
    z64 = jnp.zeros((LANE - ROPE_DIM, D), BF16)
    w_a_t = jnp.concatenate([w_in_t[:C0], kr_w, z64, _rot(kr_w, 0), z64], axis=0)
    w_b_t = w_in_t[C0 + ROPE_DIM:]
    wq = g_uq.transpose(1, 0, 2).reshape(QL, H, QK_HEAD)
    zq = jnp.zeros((QL, H, LANE - ROPE_DIM), BF16)
    wq_ext = jnp.concatenate([wq[..., :NOPE_DIM], wq[..., NOPE_DIM:], zq, _rot(wq[..., NOPE_DIM:]), zq],
                             axis=-1).reshape(QL, H * Q_EXT)
    wkv = g_ukv.transpose(1, 0, 2).reshape(KVL, H * (NOPE_DIM + V_DIM))

    ws = w_spatial[0].astype(BF16)
    ws_t = jnp.swapaxes(ws, 1, 2)
    bsb = jnp.broadcast_to(b_spatial[0][:, :, None], (GH, CHUNK, GW // GH))

    proj_a = _mm2("mm_proj_a", "nt", xn, w_a_t, F32, tn_pref=w_a_t.shape[0])
    proj_b = _mm2("mm_proj_b", "nt", xn, w_b_t, F32)

    tm_q = _row_tile(T, QL)
    qn = _ew("rms_q", lambda x, g: _rms(x, g), (T // tm_q,), [_rows(proj_a, tm_q, QL, 0), _vec(q_norm)],
             [_rows_out((T, QL), BF16, tm_q)])[0]
    kvn = _ew("rms_kv", lambda x, g: _rms(x, g), (T // tm_q,), [_rows(proj_a, tm_q, KVL, QL // KVL), _vec(kv_norm)],
              [_rows_out((T, KVL), BF16, tm_q)])[0]
    q_ext = _mm2("mm_q", "nn", qn, wq_ext, F32)
    kv = _mm2("mm_kv", "nn", kvn, wkv, F32)

    tm_h = _pick(T, 512, SUBLANE)

    def make_kv(kv, kr, c, s):
        k = jnp.concatenate([kv[:, :LANE], kr[:, :LANE] * c + kr[:, LANE:] * s], axis=1)
        return k, jnp.concatenate([kv[:, LANE:], jnp.ones_like(kv[:, LANE:])], axis=1)

    k_full, v_aug = _ew("make_kv", make_kv, (T // tm_h, H),
                        [(kv, (tm_h, K_FULL), lambda i, h: (i, h)),
                         (proj_a, (tm_h, K_FULL), lambda i, h: (i, C0 // K_FULL)),
                         (cosp, (tm_h, LANE), lambda i, h: (i, 0)), (sinp, (tm_h, LANE), lambda i, h: (i, 0))],
                        [((H, T, K_FULL), BF16, (None, tm_h, K_FULL), lambda i, h: (h, i, 0)),
                         ((H, T, 2 * V_DIM), BF16, (None, tm_h, 2 * V_DIM), lambda i, h: (h, i, 0))])

    a_out, lse = _attn_fwd(q_ext, k_full, v_aug, cosp, sinp)
    st_o, tok_o = _ag_mid("ag_o", st_o, a_out)
    tm_a = _row_tile(T, AW)
    mixed_a = _ew("rms_attn", lambda x, g: _rms(x, g), (T // tm_a,), [_rows(a_out, tm_a), _vec(attn_out_norm)],
                  [_rows_out((T, AW), BF16, tm_a)], deps=[tok_o])[0]
    mixed_g = _sgu_fwd(proj_b, v_ln_gain, v_ln_bias, ws, bsb, gmlp_out_norm)
    wo = _ag_finish("ag_o", st_o, mixed_g)[0].reshape(D, D)
    st_gu, tok_gu = _ag_mid("ag_gu", st_gu, wo)

    tm_, tn_ = _pick(T, 1024, LANE), _pick(D, 1024, LANE)
    tk_ = _fit_tk(AW, tm_, tn_, pairs=2)
    nka = AW // tk_
    mix_out = _mm("mm_out", "nn",
                  [(mixed_a, (tm_, tk_), lambda i, j, k: (i, k), wo, (tk_, tn_), lambda i, j, k: (k, j)),
                   (mixed_g, (tm_, tk_), lambda i, j, k: (i, k), wo, (tk_, tn_), lambda i, j, k: (k + nka, j))],
                  (T // tm_, D // tn_, nka), (tm_, tn_), (T, D), F32, (tm_, tn_), lambda i, j, k: (i, j), deps=[tok_gu])

    def post_mix(x, mo, g3, g4):
        h = x + _rms(mo, g3)
        return h, _rms(h, g4)

    h_res, hn = _ew("post_mix", post_mix, (T // tm_d,),
                    [_rows(x2, tm_d), _rows(mix_out, tm_d), _vec(post_mix_norm), _vec(pre_ffn_norm)],
                    [_rows_out((T, D), F32, tm_d), _rows_out((T, D), BF16, tm_d)])

    F = N_DEV * FP
    tn_f = _pick(F, FFN_TILE_MAX, MXU_DIM)
    tn_e = _pick(F, FFN_TILE_MAX // 2, LANE)

    def swiglu(u, g):
        g = g.astype(F32)
        return u, g * _sigmoid(g) * u

    wg_t, wu_t = [g.reshape(F, D) for g in _ag_finish("ag_gu", st_gu, hn)]
    gate = _mm2("mm_gate", "nt", hn, wg_t, BF16, tn_pref=tn_f)
    st_d, tok_d = _ag_mid("ag_d", st_d, gate)
    up, act = _mm2("mm_up", "nt", hn, wu_t, BF16, tn_pref=tn_e, deps=[tok_d], extras=[gate],
                   epilogue=swiglu, n_out=2)
    wd_f = _ag_finish("ag_d", st_d, act)[0].reshape(F, D)
    ffn = _mm2("mm_down", "nn", act, wd_f, F32, tk_max=tn_f)

    def loss_head(h, f, t, g5):
        y = h + _rms(f, g5)
        diff = y - t
        dy = diff / D
        d_f, dg5 = _rms_bwd(f, g5, dy)
        lsum = jnp.sum(jnp.sum(diff * diff, axis=0, keepdims=True), axis=1, keepdims=True) * (0.5 / D)
        return dy, d_f, dg5, lsum

    dy, d_ffn, dg_post_ffn, loss_part = _ew(
        "loss_head", loss_head, (T // tm_d,),
        [_rows(h_res, tm_d), _rows(ffn, tm_d), _rows(tgt, tm_d), _vec(post_ffn_norm)],
        [_rows_out((T, D), F32, tm_d), _rows_out((T, D), BF16, tm_d)], [_vec_out((1, D)), _vec_out((1, 1))])

    def swiglu_bwd(da, g, u):
        g, u = g.astype(F32), u.astype(F32)
        sg = _sigmoid(g)
        return da * u * sg * (1.0 + g * (1.0 - sg)), da * g * sg

    def ffn_dw(name, z, y, deps):
        return _mm2(name, "tn", z, y, BF16, tm_pref=tn_f, deps=deps).reshape(N_DEV, FP, D)

    x_d, tok = _a2a_start("a2a_d", [ffn_dw("mm_dwd", act, d_ffn, [])], [])
    d_gate, d_up = _mm2("mm_dact", "nt", d_ffn, wd_f, BF16, tn_pref=tn_e, deps=[tok], extras=[gate, up],
                        epilogue=swiglu_bwd, n_out=2)
    x_g, tok = _a2a_start("a2a_g", [ffn_dw("mm_dwg", d_gate, hn, [])], [])
    x_u, tok = _a2a_start("a2a_u", [ffn_dw("mm_dwu", d_up, hn, [tok])], [])
    d_hn_g = _mm2("mm_dhn_g", "nn", d_gate, wg_t, F32, tk_max=tn_f, deps=[tok])
    d_hn = _mm2("mm_dhn_u", "nn", d_up, wu_t, F32, tk_max=tn_f, extras=[d_hn_g], epilogue=lambda s, g: (s + g,))

    def pre_ffn_bwd(h, dhn, dy, mo, g4, g3):
        d1, dg4 = _rms_bwd(h, g4, dhn)
        dh = dy + d1
        dmo, dg3 = _rms_bwd(mo, g3, dh)
        return dh, dmo, dg4, dg3

    dh, d_mo, dg_pre_ffn, dg_post_mix = _ew(
        "pre_ffn_bwd", pre_ffn_bwd, (T // tm_d,),
        [_rows(h_res, tm_d), _rows(d_hn, tm_d), _rows(dy, tm_d), _rows(mix_out, tm_d), _vec(pre_ffn_norm),
         _vec(post_mix_norm)],
        [_rows_out((T, D), F32, tm_d), _rows_out((T, D), BF16, tm_d)], [_vec_out((1, D)), _vec_out((1, D))])

    d_mixed = _mm2("mm_dmixed", "nt", d_mo, wo, F32)
    dwo = jnp.concatenate([_mm2("mm_dwo_a", "tn", mixed_a, d_mo, BF16), _mm2("mm_dwo_g", "tn", mixed_g, d_mo, BF16)],
                          axis=0)
    x_o, tok = _a2a_start("a2a_o", [dwo.reshape(N_DEV, D // N_DEV, D)], [])

    d_a_out, dg_attn = _ew("rms_attn_bwd", lambda a, g, d: _rms_bwd(a, g, d), (T // tm_a,),
                           [_rows(a_out, tm_a), _vec(attn_out_norm), _rows(d_mixed, tm_a, AW, 0)],
                           [_rows_out((T, AW), BF16, tm_a)], [_vec_out((1, AW))], deps=[tok])
    d_pb, dg_gmlp, dws, dbs, dlng, dlnb = _sgu_bwd(proj_b, d_mixed, v_ln_gain, v_ln_bias, ws, ws_t, bsb, gmlp_out_norm)

    dq_ext, d_kv, d_kr = _attn_bwd(q_ext, k_full, v_aug, cosp, sinp, d_a_out, a_out, lse.reshape(H, 1, T))

    d_qn = _mm2("mm_dqn", "nt", dq_ext, wq_ext, F32)
    dwq_ext = _mm2("mm_dwq", "tn", qn, dq_ext, F32)
    d_kvn = _mm2("mm_dkvn", "nt", d_kv, wkv, F32)
    dwkv = _mm2("mm_dwkv", "tn", kvn, d_kv, BF16)

    d_qc, dg_q = _ew("rms_q_bwd", lambda a, g, d: _rms_bwd(a, g, d), (T // tm_q,),
                     [_rows(proj_a, tm_q, QL, 0), _vec(q_norm), _rows(d_qn, tm_q)],
                     [_rows_out((T, QL), BF16, tm_q)], [_vec_out((1, QL))])
    d_kvc, dg_kv = _ew("rms_kv_bwd", lambda a, g, d: _rms_bwd(a, g, d), (T // tm_q,),
                       [_rows(proj_a, tm_q, KVL, QL // KVL), _vec(kv_norm), _rows(d_kvn, tm_q)],
                       [_rows_out((T, KVL), BF16, tm_q)], [_vec_out((1, KVL))])
    d_pa = jnp.concatenate([d_qc, d_kvc, d_kr.astype(BF16)], axis=1)

    dwa_t = _mm2("mm_dwa", "tn", d_pa, xn, F32, tm_pref=w_a_t.shape[0])
    dwb_t = _mm2("mm_dwb", "tn", d_pb, xn, BF16)
    dkr_w = dwa_t[C0:C0 + ROPE_DIM] + _rot_t(dwa_t[C0 + LANE:C0 + LANE + ROPE_DIM], 0)
    dw_in = jnp.concatenate([dwa_t[:C0].astype(BF16), dkr_w.astype(BF16), dwb_t], axis=0)
    dw_in = dw_in.reshape(N_DEV, IWS, D)
    dq3 = dwq_ext.reshape(QL, H, Q_EXT)
    dwq = jnp.concatenate([dq3[..., :NOPE_DIM],
                           dq3[..., LANE:LANE + ROPE_DIM] + _rot_t(dq3[..., 2 * LANE:2 * LANE + ROPE_DIM])], axis=-1)
    dwq = dwq.astype(BF16).reshape(QL, N_DEV, H * QK_HEAD // N_DEV).transpose(1, 0, 2)
    dwkv = dwkv.reshape(KVL, N_DEV, w_ukv.shape[2]).transpose(1, 0, 2)
    p_e, tok = _rs_pair_start("rs_e", [dw_in, dwq, dwkv], [])
    d_xn_a = _mm2("mm_dxn_a", "nn", d_pa, w_a_t, F32, deps=[tok])
    x_e, tok = _rs_chip_start("rs_e2", _rs_pair_sum("rs_e", p_e, d_xn_a), [])
    d_xn_b = _mm2("mm_dxn_b", "nn", d_pb, w_b_t, F32, deps=[tok])

    def x_bwd(x, da, db, dh, g0):
        dx, dg0 = _rms_bwd(x, g0, da + db)
        return dh + dx, dg0

    grad_x, dg_pre_mix = _ew("x_bwd", x_bwd, (T // tm_d,),
                             [_rows(x2, tm_d), _rows(d_xn_a, tm_d), _rows(d_xn_b, tm_d), _rows(dh, tm_d),
                              _vec(pre_mix_norm)],
                             [_rows_out((T, D), F32, tm_d)], [_vec_out((1, D))])

    def adam_big(n, parts, transposed=False):
        view = shard_t if transposed else (lambda a: a[0])
        res = _adamw("adamw_" + n, parts, view(given[n]), view(given["m_" + n]), view(given["v_" + n]))
        done[n] = res[0]
        return [(r.T if transposed else r)[None] for r in res]

    done = {}

    small_names = ["pre_mix_norm", "q_norm", "kv_norm", "v_ln_gain", "v_ln_bias", "w_spatial", "b_spatial",
                   "attn_out_norm", "gmlp_out_norm", "post_mix_norm", "pre_ffn_norm", "post_ffn_norm"]
    small_grads = [dg_pre_mix, dg_q, dg_kv, dlng, dlnb, dws, dbs[:, 0, :], dg_attn, dg_gmlp, dg_post_mix, dg_pre_ffn,
                   dg_post_ffn]
    n_small = sum(g.size for g in small_grads)
    pad_small = _round_up(n_small, SUBLANE * LANE) - n_small

    def pack(arrs):
        flat = jnp.concatenate([a.reshape(-1) for a in arrs] + [jnp.zeros((pad_small,), F32)])
        return flat.reshape(-1, LANE)

    st_s, tok = _ag_start("ag_s", [pack(small_grads)], [])

    results = {}
    results["w_down"] = adam_big("w_down", _a2a_finish("a2a_d", x_d, tok)[0])
    results["w_gate"] = adam_big("w_gate", _a2a_finish("a2a_g", x_g, done["w_down"])[0], True)
    st_s, tok = _ag_mid("ag_s", st_s, done["w_gate"])
    results["w_up"] = adam_big("w_up", _a2a_finish("a2a_u", x_u, tok)[0], True)
    results["w_out"] = adam_big("w_out", _a2a_finish("a2a_o", x_o, done["w_up"])[0])
    e_in, e_uq, e_ukv = _rs_chip_finish("rs_e2", x_e, done["w_out"])
    results["w_in"] = adam_big("w_in", e_in, True)
    results["w_uq"] = adam_big("w_uq", e_uq)
    results["w_ukv"] = adam_big("w_ukv", e_ukv)

    small_parts = _ag_finish("ag_s", st_s, done["w_ukv"])[0]
    sm = _adamw("adamw_small", small_parts, pack([given[n] for n in small_names]),
                pack([given["m_" + n] for n in small_names]), pack([given["v_" + n] for n in small_names]))

    off = 0
    for n in small_names:
        shp, size = given[n].shape, given[n].size
        results[n] = [r.reshape(-1)[off:off + size].reshape(shp) for r in sm]
        off += size

    loss = lax.psum(loss_part[0, 0], MESH_AXES)
    order = ["pre_mix_norm", "w_in", "q_norm", "kv_norm", "w_uq", "w_ukv", "v_ln_gain", "v_ln_bias", "w_spatial",
             "b_spatial", "attn_out_norm", "gmlp_out_norm", "w_out", "post_mix_norm", "pre_ffn_norm", "w_gate", "w_up",
             "w_down", "post_ffn_norm"]
    out = [loss, grad_x[None]]
    for kind in range(4):
        out += [results[n][kind] for n in order]
    return tuple(out)
```

```python
import math

import jax
import jax.numpy as jnp
from jax import lax
from jax.experimental import pallas as pl
from jax.experimental.pallas import tpu as pltpu

F32 = jnp.float32
BF16 = jnp.bfloat16
N_DEV = 8
MESH_AXES = ("x", "y", "c")
MESH = pl.DeviceIdType.MESH

NOPE_DIM = 128
ROPE_DIM = 64
V_DIM = 128
ROPE_THETA = 10000.0
CHUNK = 128
EPS = 1e-6
QK_HEAD = NOPE_DIM + ROPE_DIM
Q_EXT = 3 * 128
K_FULL = 2 * 128

ADAM_LR = 0.001
ADAM_B1 = 0.9
ADAM_B2 = 0.999
ADAM_EPS = 1e-08
ADAM_WD = 0.01
ADAM_STEP = 10

LANE = 128
SUBLANE = 8
VMEM_LIMIT = 56 * 1024 * 1024
MM_VMEM_BUDGET = 44 * 1024 * 1024
MXU_DIM = 256
FFN_TILE_MAX = 3072
MM_TK_MAX = 2048
EPILOGUE_ROWS = 256
ATTN_TQ = 1024
ATTN_TKV = 512
NEG_BIG = -1e30
ROW_BLOCK_ELEMS = 512 * 1024
ADAM_BLOCK_ELEMS = 256 * 1024

DN = {
    "nn": (((1,), (0,)), ((), ())),
    "nt": (((1,), (1,)), ((), ())),
    "tn": (((0,), (0,)), ((), ())),
}


def _pick(dim, pref, mult):
    t = min(pref, dim)
    t -= t % mult
    while t >= mult:
        if dim % t == 0:
            return t
        t -= mult
    return dim


def _round_up(n, m):
    return (n + m - 1) // m * m


def _params(sem):
    return pltpu.CompilerParams(dimension_semantics=sem, vmem_limit_bytes=VMEM_LIMIT)


def _dep_specs(deps, nd):
    return [pl.BlockSpec(d.shape, lambda *_, r=d.ndim: (0,) * r) for d in deps]


def _mm(name, mode, pairs, grid, acc_shape, out_shape, out_dtype, out_block, out_map, deps=(), extras=(),
        epilogue=None, n_out=1):
    n, ne, nx = len(pairs), len(extras), len(deps)
    nk = grid[-1]
    nd = len(grid)
    dn = DN[mode]
    use_acc = nk > 1 or epilogue is not None

    def body(*refs):
        e_refs = refs[2 * n:2 * n + ne]
        o_refs = refs[2 * n + ne + nx:2 * n + ne + nx + n_out]
        k = pl.program_id(nd - 1)

        def dot(i):
            return lax.dot_general(refs[2 * i][...], refs[2 * i + 1][...], dn, preferred_element_type=F32)

        if not use_acc:
            s = dot(0)
            for i in range(1, n):
                s = s + dot(i)
            o_refs[0][...] = s.astype(o_refs[0].dtype)
            return
        acc = refs[2 * n + ne + nx + n_out]

        @pl.when(k == 0)
        def _():
            acc[...] = dot(0)

        if nk > 1:
            @pl.when(k > 0)
            def _():
                acc[...] += dot(0)

        for i in range(1, n):
            acc[...] += dot(i)

        if epilogue is None:
            @pl.when(k == nk - 1)
            def _():
                o_refs[0][...] = acc[...].astype(o_refs[0].dtype)
        else:
            @pl.when(k == nk - 1)
            def _():
                rows = acc_shape[0]
                slab = _pick(rows, EPILOGUE_ROWS, SUBLANE)
                for r0 in range(0, rows, slab):
                    sl = pl.ds(r0, slab)
                    vals = epilogue(acc[sl, :], *[e[sl, :] for e in e_refs])
                    for r, v in zip(o_refs, vals):
                        r[sl, :] = v.astype(r.dtype)

    in_specs, args = [], []
    for a, a_blk, a_map, b, b_blk, b_map in pairs:
        in_specs += [pl.BlockSpec(a_blk, a_map), pl.BlockSpec(b_blk, b_map)]
        args += [a, b]
    in_specs += [pl.BlockSpec(blk, imap) for (_, blk, imap) in extras]
    args += [e for (e, _, _) in extras]
    if not isinstance(out_dtype, (tuple, list)):
        out_dtype = (out_dtype,) * n_out
    res = pl.pallas_call(
        body,
        name=name,
        grid=grid,
        in_specs=in_specs + _dep_specs(deps, nd),
        out_specs=[pl.BlockSpec(out_block, out_map)] * n_out,
        out_shape=[jax.ShapeDtypeStruct(out_shape, dt) for dt in out_dtype],
        scratch_shapes=[pltpu.VMEM(acc_shape, F32)] if use_acc else [],
        compiler_params=_params(("parallel",) * (nd - 1) + ("arbitrary",)),
    )(*args, *deps)
    return res[0] if n_out == 1 else res


def _fit_tk(K, tm, tn, pairs=1, out_bytes=4, fixed_extra=0, tk_max=MM_TK_MAX):
    fixed = tm * tn * 4 + 2 * tm * tn * out_bytes + fixed_extra
    tk = _pick(K, tk_max, LANE)
    while tk > LANE and fixed + pairs * 4 * (tm + tn) * tk > MM_VMEM_BUDGET:
        smaller = _pick(K, tk - LANE, LANE)
        if smaller >= tk:
            break
        tk = smaller
    return tk


def _mm2(name, mode, a, b, out_dtype, tm_pref=1024, tn_pref=1024, deps=(), extras=(), epilogue=None, n_out=1,
         tk_max=MM_TK_MAX):
    if mode == "nn":
        (M, K), N = a.shape, b.shape[1]
    elif mode == "nt":
        (M, K), N = a.shape, b.shape[0]
    else:
        (K, M), N = a.shape, b.shape[1]
    tm, tn = _pick(M, tm_pref, LANE), _pick(N, tn_pref, LANE)
    extra_bytes = sum(2 * tm * tn * jnp.dtype(e.dtype).itemsize for e in extras)
    tk = _fit_tk(K, tm, tn, out_bytes=n_out * jnp.dtype(out_dtype).itemsize, fixed_extra=extra_bytes, tk_max=tk_max)
    if mode == "tn":
        a_blk, a_map = (tk, tm), lambda i, j, k: (k, i)
    else:
        a_blk, a_map = (tm, tk), lambda i, j, k: (i, k)
    if mode == "nt":
        b_blk, b_map = (tn, tk), lambda i, j, k: (j, k)
    else:
        b_blk, b_map = (tk, tn), lambda i, j, k: (k, j)
    return _mm(name, mode, [(a, a_blk, a_map, b, b_blk, b_map)], (M // tm, N // tn, K // tk), (tm, tn),
               (M, N), out_dtype, (tm, tn), lambda i, j, k: (i, j), deps=deps,
               extras=[(e, (tm, tn), lambda i, j, k: (i, j)) for e in extras], epilogue=epilogue, n_out=n_out)


def _ew(name, fn, grid, ins, outs, accs=(), deps=()):
    ni, no, na = len(ins), len(outs), len(accs)
    nd = len(grid)
    nx = len(deps)

    def body(*refs):
        vals = fn(*[r[...] for r in refs[:ni]])
        if not isinstance(vals, (tuple, list)):
            vals = (vals,)
        for r, v in zip(refs[ni + nx:ni + nx + no], vals[:no]):
            r[...] = v.astype(r.dtype)
        if na:
            first = pl.program_id(nd - 1) == 0
            for r, v in zip(refs[ni + nx + no:], vals[no:]):
                @pl.when(first)
                def _(r=r, v=v):
                    r[...] = v.astype(r.dtype)

                @pl.when(jnp.logical_not(first))
                def _(r=r, v=v):
                    r[...] += v.astype(r.dtype)

    all_outs = list(outs) + list(accs)
    sem = ("parallel",) * (nd - 1) + (("arbitrary",) if na else ("parallel",))
    res = pl.pallas_call(
        body,
        name=name,
        grid=grid,
        in_specs=[pl.BlockSpec(blk, imap) for (_, blk, imap) in ins] + _dep_specs(deps, nd),
        out_specs=[pl.BlockSpec(blk, imap) for (_, _, blk, imap) in all_outs],
        out_shape=[jax.ShapeDtypeStruct(s, d) for (s, d, _, _) in all_outs],
        compiler_params=_params(sem),
    )(*[a for (a, _, _) in ins], *deps)
    return res


def _rows(a, tm, width=None, cblk=0):
    width = a.shape[1] if width is None else width
    return (a, (tm, width), lambda i, c=cblk: (i, c))


def _vec(p):
    return (p, p.shape, lambda i: (0,) * p.ndim)


def _rows_out(shape, dtype, tm):
    return (shape, dtype, (tm, shape[1]), lambda i: (i, 0))


def _vec_out(shape):
    return (shape, F32, shape, lambda i: (0,) * len(shape))


def _row_tile(T, D):
    return _pick(T, max(SUBLANE, ROW_BLOCK_ELEMS // D), SUBLANE)


def _rms(x, g):
    r = lax.rsqrt(jnp.mean(x * x, axis=-1, keepdims=True) + EPS)
    return x * r * g


def _rms_bwd(x, g, dy):
    r = lax.rsqrt(jnp.mean(x * x, axis=-1, keepdims=True) + EPS)
    xh = x * r
    dg = jnp.sum(dy * xh, axis=0, keepdims=True)
    dxh = dy * g
    dx = r * (dxh - xh * jnp.mean(dxh * xh, axis=-1, keepdims=True))
    return dx, dg


GELU_C = math.sqrt(2.0 / math.pi)
GELU_A = 0.044715


def _sigmoid(x):
    return 0.5 * jnp.tanh(0.5 * x) + 0.5


def _gelu(x):
    return 0.5 * x * (1.0 + jnp.tanh(GELU_C * (x + GELU_A * x * x * x)))


def _gelu_grad(x):
    t = jnp.tanh(GELU_C * (x + GELU_A * x * x * x))
    return 0.5 * (1.0 + t) + 0.5 * x * (1.0 - t * t) * GELU_C * (1.0 + 3.0 * GELU_A * x * x)


def _rot(w, axis=-1):
    w1, w2 = jnp.split(w, 2, axis=axis)
    return jnp.concatenate([-w2, w1], axis=axis)


def _rot_t(dw, axis=-1):
    d1, d2 = jnp.split(dw, 2, axis=axis)
    return jnp.concatenate([d2, -d1], axis=axis)


def _coords():
    return lax.axis_index("x"), lax.axis_index("y"), lax.axis_index("c")


def _all_gather(name, shard, deps=()):
    def body(x_ref, *rest):
        o_ref, send_sems, recv_sems, local_sem = rest[len(deps):]
        xi, yi, ci = _coords()
        me, sibling = (xi, yi, ci), (xi, yi, 1 - ci)
        chips = [(1 - xi, yi), (xi, 1 - yi), (1 - xi, 1 - yi)]

        def slot(px, py, pc):
            return o_ref.at[4 * px + 2 * py + pc]

        def copy(k, block, to, src=None):
            return pltpu.make_async_remote_copy(
                src_ref=slot(*block) if src is None else src, dst_ref=slot(*block),
                send_sem=send_sems.at[k], recv_sem=recv_sems.at[k], device_id=to, device_id_type=MESH)

        mine = pltpu.make_async_copy(x_ref, slot(*me), local_sem)
        mine.start()
        first = [copy(0, me, sibling, src=x_ref)]
        first += [copy(1 + j, me, (*chip, ci), src=x_ref) for j, chip in enumerate(chips)]
        for cp in first:
            cp.start()
        passed = [copy(4 + j, (*chip, ci), sibling) for j, chip in enumerate(chips)]
        for j, chip in enumerate(chips):
            copy(1 + j, (*chip, ci), me).wait_recv()
            passed[j].start()
        copy(0, sibling, me).wait_recv()
        for j, chip in enumerate(chips):
            copy(4 + j, (*chip, 1 - ci), me).wait_recv()
        for cp in first + passed:
            cp.wait_send()
        mine.wait()

    return pl.pallas_call(
        body,
        name=name,
        out_shape=jax.ShapeDtypeStruct((N_DEV,) + shard.shape, shard.dtype),
        in_specs=[pl.BlockSpec(memory_space=pltpu.HBM)] + [pl.BlockSpec(memory_space=pl.ANY)] * len(deps),
        out_specs=pl.BlockSpec(memory_space=pltpu.HBM),
        scratch_shapes=[pltpu.SemaphoreType.DMA((N_DEV - 1,)), pltpu.SemaphoreType.DMA((N_DEV - 1,)),
                        pltpu.SemaphoreType.DMA],
    )(shard, *deps)


HBM_SPEC = pl.BlockSpec(memory_space=pltpu.HBM)
SEM_SPEC = pl.BlockSpec(memory_space=pltpu.SEMAPHORE)
ANY_SPEC = pl.BlockSpec(memory_space=pl.ANY)
EFFECT = pltpu.SideEffectType.DATAFLOW_SIDE_EFFECTING
TOKEN = (SUBLANE, LANE)


def _my_slot():
    xi, yi, ci = _coords()
    return 4 * xi + 2 * yi + ci


def _own_slot_filled(block):
    land = lax.empty((N_DEV,) + block.shape, block.dtype)
    return lax.dynamic_update_slice(land, block[None], (_my_slot(),) + (0,) * block.ndim)


def _split_start(name, n, n_copies, issue, srcs, lands, deps):
    def body(*refs):
        src_refs, land_refs = refs[:n], refs[n:2 * n]
        outs = refs[2 * n + len(deps):]
        send, recv, token = outs[:n], outs[n:2 * n], outs[4 * n]
        for t in range(n):
            issue(t, src_refs[t], land_refs[t], send[t], recv[t])
        token[...] = jnp.zeros_like(token)

    sems = [pltpu.SemaphoreType.DMA((n_copies,))] * (2 * n)
    res = pl.pallas_call(
        body,
        name=name,
        in_specs=[HBM_SPEC] * (2 * n) + [ANY_SPEC] * len(deps),
        out_specs=[SEM_SPEC] * (2 * n) + [HBM_SPEC] * (2 * n) + [pl.BlockSpec(memory_space=pltpu.VMEM)],
        out_shape=sems + [pltpu.HBM(a.shape, a.dtype) for a in list(srcs) + list(lands)]
        + [jax.ShapeDtypeStruct(TOKEN, F32)],
        input_output_aliases={i: 2 * n + i for i in range(2 * n)},
        compiler_params=pltpu.CompilerParams(has_side_effects=EFFECT),
    )(*[pltpu.with_memory_space_constraint(a, pltpu.HBM) for a in list(srcs) + list(lands)], *deps)
    state = dict(n=n, n_copies=n_copies, send=res[:n], recv=res[n:2 * n], srcs=res[2 * n:3 * n],
                 lands=res[3 * n:4 * n])
    return state, res[4 * n]


def _split_wait(name, state, unit_src, after):
    n, n_copies = state["n"], state["n_copies"]

    def body(*refs):
        src_refs, land_refs = refs[:n], refs[n:2 * n]
        send, recv = refs[2 * n:3 * n], refs[3 * n:4 * n]
        me = _coords()
        for t in range(n):
            for k in range(n_copies):
                cp = pltpu.make_async_remote_copy(
                    src_ref=unit_src(src_refs[t]), dst_ref=land_refs[t].at[0], send_sem=send[t].at[k],
                    recv_sem=recv[t].at[k], device_id=me, device_id_type=MESH)
                cp.wait_send()
                cp.wait_recv()

    bufs = list(state["srcs"]) + list(state["lands"])
    res = pl.pallas_call(
        body,
        name=name,
        in_specs=[HBM_SPEC] * (2 * n) + [SEM_SPEC] * (2 * n) + [ANY_SPEC],
        out_specs=[HBM_SPEC] * (2 * n),
        out_shape=[pltpu.HBM(a.shape, a.dtype) for a in bufs],
        input_output_aliases={i: i for i in range(2 * n)},
        compiler_params=pltpu.CompilerParams(has_side_effects=EFFECT),
    )(*bufs, *state["send"], *state["recv"], after)
    return list(res[n:])


def _ag_start(name, shards, deps):
    def issue(t, x_ref, land_ref, send, recv):
        xi, yi, ci = _coords()
        to = [(xi, yi, 1 - ci), (1 - xi, yi, ci), (xi, 1 - yi, ci), (1 - xi, 1 - yi, ci)]
        for k, dev in enumerate(to):
            pltpu.make_async_remote_copy(src_ref=x_ref, dst_ref=land_ref.at[4 * xi + 2 * yi + ci], send_sem=send.at[k],
                                         recv_sem=recv.at[k], device_id=dev, device_id_type=MESH).start()

    return _split_start(name, len(shards), 4, issue, shards, [_own_slot_filled(s) for s in shards], deps)


def _ag_mid(name, state, after):
    lands = _split_wait(name + "_w1", state, lambda r: r, after)

    def issue(t, src_ref, land_ref, send, recv):
        xi, yi, ci = _coords()
        for k, (px, py) in enumerate([(1 - xi, yi), (xi, 1 - yi), (1 - xi, 1 - yi)]):
            rows = land_ref.at[4 * px + 2 * py + ci]
            pltpu.make_async_remote_copy(src_ref=rows, dst_ref=rows, send_sem=send.at[k], recv_sem=recv.at[k],
                                         device_id=(xi, yi, 1 - ci), device_id_type=MESH).start()

    n = len(lands)

    def body(*refs):
        land_refs = refs[:n]
        outs = refs[n:]
        send, recv, token = outs[:n], outs[n:2 * n], outs[3 * n]
        for t in range(n):
            issue(t, land_refs[t], land_refs[t], send[t], recv[t])
        token[...] = jnp.zeros_like(token)

    res = pl.pallas_call(
        body,
        name=name + "_s2",
        in_specs=[HBM_SPEC] * n,
        out_specs=[SEM_SPEC] * (2 * n) + [HBM_SPEC] * n + [pl.BlockSpec(memory_space=pltpu.VMEM)],
        out_shape=[pltpu.SemaphoreType.DMA((3,))] * (2 * n) + [pltpu.HBM(a.shape, a.dtype) for a in lands]
        + [jax.ShapeDtypeStruct(TOKEN, F32)],
        input_output_aliases={i: 2 * n + i for i in range(n)},
        compiler_params=pltpu.CompilerParams(has_side_effects=EFFECT),
    )(*lands)
    state2 = dict(n=n, send=res[:n], recv=res[n:2 * n], lands=res[2 * n:3 * n])
    return state2, res[3 * n]


def _ag_finish(name, state2, after):
    n = state2["n"]

    def body(*refs):
        land_refs, send, recv = refs[:n], refs[n:2 * n], refs[2 * n:3 * n]
        me = _coords()
        for t in range(n):
            for k in range(3):
                cp = pltpu.make_async_remote_copy(
                    src_ref=land_refs[t].at[0], dst_ref=land_refs[t].at[0], send_sem=send[t].at[k],
                    recv_sem=recv[t].at[k], device_id=me, device_id_type=MESH)
                cp.wait_send()
                cp.wait_recv()

    res = pl.pallas_call(
        body,
        name=name + "_w2",
        in_specs=[HBM_SPEC] * n + [SEM_SPEC] * (2 * n) + [ANY_SPEC],
        out_specs=[HBM_SPEC] * n,
        out_shape=[pltpu.HBM(a.shape, a.dtype) for a in state2["lands"]],
        input_output_aliases={i: i for i in range(n)},
        compiler_params=pltpu.CompilerParams(has_side_effects=EFFECT),
    )(*state2["lands"], *state2["send"], *state2["recv"], after)
    return list(res)


def _a2a_start(name, parts, deps):
    def issue(t, x_ref, land_ref, send, recv):
        xi, yi, ci = _coords()
        for k in range(N_DEV - 1):
            px = 1 - xi if (k + 1) & 4 else xi
            py = 1 - yi if (k + 1) & 2 else yi
            pc = 1 - ci if (k + 1) & 1 else ci
            pltpu.make_async_remote_copy(src_ref=x_ref.at[4 * px + 2 * py + pc], dst_ref=land_ref.at[4 * xi + 2 * yi + ci],
                                         send_sem=send.at[k], recv_sem=recv.at[k], device_id=(px, py, pc),
                                         device_id_type=MESH).start()

    lands = [_own_slot_filled(lax.dynamic_index_in_dim(p, _my_slot(), 0, keepdims=False)) for p in parts]
    return _split_start(name, len(parts), N_DEV - 1, issue, parts, lands, deps)


def _a2a_finish(name, state, after):
    return _split_wait(name + "_w", state, lambda r: r.at[0], after)


N_CHIP = N_DEV // 2


def _rs_pair_start(name, parts, deps):
    def issue(t, x_ref, land_ref, send, recv):
        xi, yi, ci = _coords()
        for k in range(N_CHIP):
            pltpu.make_async_remote_copy(src_ref=x_ref.at[2 * k + (1 - ci)], dst_ref=land_ref.at[k], send_sem=send.at[k],
                                         recv_sem=recv.at[k], device_id=(xi, yi, 1 - ci), device_id_type=MESH).start()

    lands = [lax.empty((N_CHIP,) + p.shape[1:], p.dtype) for p in parts]
    return _split_start(name, len(parts), N_CHIP, issue, parts, lands, deps)


def _rs_pair_sum(name, state, after):
    lands = _split_wait(name + "_w", state, lambda r: r.at[0], after)
    sums = []
    for t, (part, land) in enumerate(zip(state["srcs"], lands)):
        _, R, C = land.shape
        pack = SUBLANE * (4 // jnp.dtype(land.dtype).itemsize)
        if R % pack == 0:
            tr, tc = _pick(R, max(pack, ROW_BLOCK_ELEMS // C), pack), C
        else:
            tr, tc = R, _pick(C, max(LANE, ROW_BLOCK_ELEMS // R // LANE * LANE), LANE)

        def fn(mine, theirs):
            own = jnp.where(lax.axis_index("c") == 0, mine[0], mine[1])
            return own.astype(F32) + theirs.astype(F32)

        sums.append(_ew(f"{name}_sum{t}", fn, (N_CHIP, R // tr, C // tc),
                        [(part.reshape((N_CHIP, 2) + part.shape[1:]), (None, 2, tr, tc), lambda k, i, j: (k, 0, i, j)),
                         (land, (None, tr, tc), lambda k, i, j: (k, i, j))],
                        [(land.shape, land.dtype, (None, tr, tc), lambda k, i, j: (k, i, j))])[0])
    return sums


def _rs_chip_start(name, sums, deps):
    def issue(t, x_ref, land_ref, send, recv):
        xi, yi, ci = _coords()
        for k, (px, py) in enumerate([(1 - xi, yi), (xi, 1 - yi), (1 - xi, 1 - yi)]):
            pltpu.make_async_remote_copy(src_ref=x_ref.at[2 * px + py], dst_ref=land_ref.at[2 * xi + yi], send_sem=send.at[k],
                                         recv_sem=recv.at[k], device_id=(px, py, ci), device_id_type=MESH).start()

    xi, yi, _ = _coords()
    lands = []
    for s in sums:
        own = lax.dynamic_index_in_dim(s, 2 * xi + yi, 0, keepdims=True)
        lands.append(lax.dynamic_update_slice(lax.empty(s.shape, s.dtype), own, (2 * xi + yi,) + (0,) * (s.ndim - 1)))
    return _split_start(name, len(sums), N_CHIP - 1, issue, sums, lands, deps)


def _rs_chip_finish(name, state, after):
    return _split_wait(name + "_w", state, lambda r: r.at[0], after)


QK_SCALE = 1.0 / math.sqrt(QK_HEAD)
LOG2E = 1.0 / math.log(2.0)


def _roped_q(q_ext, cos, sin):
    q = jnp.concatenate([q_ext[:, :LANE], q_ext[:, LANE:2 * LANE] * cos + q_ext[:, 2 * LANE:] * sin], axis=1)
    return (q * (QK_SCALE * LOG2E)).astype(BF16)


def _attn_fwd(q_ext, k_full, v_aug, cosp, sinp):
    H, T, _ = k_full.shape
    tq, tkv = _pick(T, ATTN_TQ, LANE), _pick(T, ATTN_TKV, LANE)

    def body(q_ref, cos_ref, sin_ref, k_ref, v_ref, o_ref, lse_ref):
        q = _roped_q(q_ref[...], cos_ref[...], sin_ref[...])
        m = jnp.full((tq, 1), NEG_BIG, F32)
        acc = jnp.zeros((tq, 2 * V_DIM), F32)
        for c in range(T // tkv):
            keys = pl.ds(c * tkv, tkv)
            s = lax.dot_general(q, k_ref[keys, :], DN["nt"], preferred_element_type=F32)
            m_new = jnp.maximum(m, jnp.max(s, axis=-1, keepdims=True))
            p = jnp.exp2(s - m_new).astype(BF16)
            acc = jnp.exp2(m - m_new) * acc + jnp.dot(p, v_ref[keys, :], preferred_element_type=F32)
            m = m_new
        l = acc[:, V_DIM:]
        o_ref[...] = acc[:, :V_DIM] / l
        lse_ref[...] = m + jnp.log2(jnp.max(l, axis=-1, keepdims=True))

    return pl.pallas_call(
        body,
        name="attn_fwd",
        grid=(H, T // tq),
        in_specs=[pl.BlockSpec((tq, Q_EXT), lambda h, i: (i, h)),
                  pl.BlockSpec((tq, LANE), lambda h, i: (i, 0)),
                  pl.BlockSpec((tq, LANE), lambda h, i: (i, 0)),
                  pl.BlockSpec((None, T, K_FULL), lambda h, i: (h, 0, 0)),
                  pl.BlockSpec((None, T, 2 * V_DIM), lambda h, i: (h, 0, 0))],
        out_specs=[pl.BlockSpec((tq, V_DIM), lambda h, i: (i, h)),
                   pl.BlockSpec((None, tq, 1), lambda h, i: (h, i, 0))],
        out_shape=[jax.ShapeDtypeStruct((T, H * V_DIM), F32), jax.ShapeDtypeStruct((H, T, 1), F32)],
        compiler_params=_params(("parallel", "parallel")),
    )(q_ext, cosp, sinp, k_full, v_aug)


def _attn_bwd(q_ext, k_full, v_aug, cosp, sinp, d_out, out, lse_rows):
    H, T, _ = k_full.shape
    tq, tkv = _pick(T, ATTN_TQ, LANE), _pick(T, ATTN_TKV, LANE)
    n_q = T // tq

    def body(q_ref, cos_ref, sin_ref, k_ref, v_ref, do_ref, o_ref, lse_ref, cos_all, sin_all,
             dq_ref, dkv_ref, dkr_ref, dk_acc, dv_acc):
        h, i = pl.program_id(0), pl.program_id(1)

        @pl.when(i == 0)
        def _():
            dk_acc[...] = jnp.zeros_like(dk_acc)
            dv_acc[...] = jnp.zeros_like(dv_acc)

        cos, sin = cos_ref[...], sin_ref[...]
        q, do, lse = _roped_q(q_ref[...], cos, sin), do_ref[...], lse_ref[...]
        delta = lax.dot_general(jnp.ones((SUBLANE, V_DIM), F32), do.astype(F32) * o_ref[...], DN["nt"],
                                preferred_element_type=F32, precision=lax.Precision.HIGHEST)[0:1]
        dq = jnp.zeros((tq, K_FULL), F32)
        for c in range(T // tkv):
            keys = pl.ds(c * tkv, tkv)
            kc = k_ref[keys, :]
            pt = jnp.exp2(lax.dot_general(kc, q, DN["nt"], preferred_element_type=F32) - lse)
            dpt = lax.dot_general(v_ref[keys, pl.ds(0, V_DIM)], do, DN["nt"], preferred_element_type=F32)
            dst = (pt * (dpt - delta)).astype(BF16)
            dv_acc[keys, :] += jnp.dot(pt.astype(BF16), do, preferred_element_type=F32)
            dk_acc[keys, :] += jnp.dot(dst, q, preferred_element_type=F32)
            dq = dq + lax.dot_general(dst, kc, DN["tn"], preferred_element_type=F32)
        r = dq[:, LANE:]
        dq_ref[...] = (jnp.concatenate([dq[:, :LANE], r * cos, r * sin], axis=1) * QK_SCALE).astype(BF16)

        @pl.when(i == n_q - 1)
        def _():
            dk = dk_acc[...] * (1.0 / LOG2E)
            dkv_ref[...] = jnp.concatenate([dk[:, :LANE], dv_acc[...]], axis=1).astype(BF16)
            rk = dk[:, LANE:]
            dkr = jnp.concatenate([rk * cos_all[...], rk * sin_all[...]], axis=1)

            @pl.when(h == 0)
            def _():
                dkr_ref[...] = dkr

            @pl.when(h != 0)
            def _():
                dkr_ref[...] += dkr

    return pl.pallas_call(
        body,
        name="attn_bwd",
        grid=(H, n_q),
        in_specs=[pl.BlockSpec((tq, Q_EXT), lambda h, i: (i, h)),
                  pl.BlockSpec((tq, LANE), lambda h, i: (i, 0)),
                  pl.BlockSpec((tq, LANE), lambda h, i: (i, 0)),
                  pl.BlockSpec((None, T, K_FULL), lambda h, i: (h, 0, 0)),
                  pl.BlockSpec((None, T, 2 * V_DIM), lambda h, i: (h, 0, 0)),
                  pl.BlockSpec((tq, V_DIM), lambda h, i: (i, h)),
                  pl.BlockSpec((tq, V_DIM), lambda h, i: (i, h)),
                  pl.BlockSpec((None, 1, tq), lambda h, i: (h, 0, i)),
                  pl.BlockSpec((T, LANE), lambda h, i: (0, 0)),
                  pl.BlockSpec((T, LANE), lambda h, i: (0, 0))],
        out_specs=[pl.BlockSpec((tq, Q_EXT), lambda h, i: (i, h)),
                   pl.BlockSpec((T, K_FULL), lambda h, i: (0, h)),
                   pl.BlockSpec((T, K_FULL), lambda h, i: (0, 0))],
        out_shape=[jax.ShapeDtypeStruct((T, H * Q_EXT), BF16), jax.ShapeDtypeStruct((T, H * K_FULL), BF16),
                   jax.ShapeDtypeStruct((T, K_FULL), F32)],
        scratch_shapes=[pltpu.VMEM((T, K_FULL), F32), pltpu.VMEM((T, V_DIM), F32)],
        compiler_params=_params(("arbitrary", "arbitrary")),
    )(q_ext, cosp, sinp, k_full, v_aug, d_out, out, lse_rows, cosp, sinp)


def _sgu_forward_math(pb, lng, lnb, ws, bsb, gw, gh):
    u_raw, v_raw = pb[:, :gw], pb[:, gw:]
    u, v = _gelu(u_raw), _gelu(v_raw)
    vc = v - jnp.mean(v, axis=-1, keepdims=True)
    rstd = lax.rsqrt(jnp.mean(vc * vc, axis=-1, keepdims=True) + EPS)
    vhat = vc * rstd
    vln = vhat * lng + lnb
    hd = gw // gh
    s = jnp.concatenate(
        [jnp.dot(ws[g], vln[:, g * hd:(g + 1) * hd].astype(BF16), preferred_element_type=F32) + bsb[g]
         for g in range(gh)], axis=1)
    return u_raw, v_raw, u, rstd, vhat, vln, s


def _sgu_fwd(proj_b, lng, lnb, ws, bsb, gg):
    T, gw2 = proj_b.shape
    gw, gh = gw2 // 2, ws.shape[0]

    def fn(pb, lng, lnb, ws, bsb, gg):
        _, _, u, _, _, _, s = _sgu_forward_math(pb, lng, lnb, ws, bsb, gw, gh)
        return _rms(u * s, gg)

    return _ew("sgu_fwd", fn, (T // CHUNK,),
               [_rows(proj_b, CHUNK), _vec(lng), _vec(lnb), _vec(ws), _vec(bsb), _vec(gg)],
               [_rows_out((T, gw), BF16, CHUNK)])[0]


def _sgu_bwd(proj_b, d_mixed, lng, lnb, ws, ws_t, bsb, gg):
    T, gw2 = proj_b.shape
    gw, gh = gw2 // 2, ws.shape[0]
    hd = gw // gh

    def fn(pb, dmg, lng, lnb, ws, ws_t, bsb, gg):
        u_raw, v_raw, u, rstd, vhat, vln, s = _sgu_forward_math(pb, lng, lnb, ws, bsb, gw, gh)
        dgo, dgg = _rms_bwd(u * s, gg, dmg)
        du, ds = dgo * s, dgo * u
        ones = jnp.ones((SUBLANE, hd), F32)
        dws, dbs, dvln = [], [], []
        for g in range(gh):
            ds_g = ds[:, g * hd:(g + 1) * hd]
            ds_gb = ds_g.astype(BF16)
            dws.append(lax.dot_general(ds_gb, vln[:, g * hd:(g + 1) * hd].astype(BF16), DN["nt"],
                                       preferred_element_type=F32)[None])
            dbs.append(lax.dot_general(ones, ds_g, DN["nt"], preferred_element_type=F32,
                                       precision=lax.Precision.HIGHEST)[None])
            dvln.append(jnp.dot(ws_t[g], ds_gb, preferred_element_type=F32))
        dvln = jnp.concatenate(dvln, axis=1)
        dlng = jnp.sum(dvln * vhat, axis=0, keepdims=True)
        dlnb = jnp.sum(dvln, axis=0, keepdims=True)
        dvh = dvln * lng
        dv = rstd * (dvh - jnp.mean(dvh, axis=-1, keepdims=True) - vhat * jnp.mean(dvh * vhat, axis=-1, keepdims=True))
        d_pb = jnp.concatenate([du * _gelu_grad(u_raw), dv * _gelu_grad(v_raw)], axis=1)
        return d_pb, dgg, jnp.concatenate(dws, axis=0), jnp.concatenate(dbs, axis=0), dlng, dlnb

    return _ew("sgu_bwd", fn, (T // CHUNK,),
               [_rows(proj_b, CHUNK), _rows(d_mixed, CHUNK, gw, 1), _vec(lng), _vec(lnb), _vec(ws), _vec(ws_t),
                _vec(bsb), _vec(gg)],
               [_rows_out((T, gw2), BF16, CHUNK)],
               [_vec_out((1, gw)), _vec_out((gh, CHUNK, CHUNK)), _vec_out((gh, SUBLANE, CHUNK)), _vec_out((1, gw)),
                _vec_out((1, gw))])


def _adamw(name, parts, w, m, v):
    R, C = w.shape
    pack = SUBLANE * (4 // jnp.dtype(parts.dtype).itemsize)
    if R % pack == 0:
        tm, tc = _pick(R, max(pack, ADAM_BLOCK_ELEMS // C), pack), C
    else:
        parts = parts[:, :R]
        tm, tc = R, _pick(C, max(LANE, ADAM_BLOCK_ELEMS // R // LANE * LANE), LANE)

    n_parts = parts.shape[0]

    def fn(p, w, m, v):
        g = p[0].astype(F32)
        for i in range(1, n_parts):
            g = g + p[i].astype(F32)
        m2 = ADAM_B1 * m + (1.0 - ADAM_B1) * g
        v2 = ADAM_B2 * v + (1.0 - ADAM_B2) * (g * g)
        m_hat = m2 / (1.0 - ADAM_B1 ** ADAM_STEP)
        v_hat = v2 / (1.0 - ADAM_B2 ** ADAM_STEP)
        delta = -ADAM_LR * (m_hat / (jnp.sqrt(v_hat) + ADAM_EPS) + ADAM_WD * w)
        return g, delta, m2, v2

    blk = lambda a: (a, (tm, tc), lambda i, j: (i, j))
    return _ew(name, fn, (R // tm, C // tc),
               [(parts, (n_parts, tm, tc), lambda i, j: (0, i, j)), blk(w), blk(m), blk(v)],
               [((R, C), F32, (tm, tc), lambda i, j: (i, j))] * 4)


def kernel(x, positions, pre_mix_norm, w_in, q_norm, kv_norm, w_uq, w_ukv, v_ln_gain, v_ln_bias, w_spatial, b_spatial, attn_out_norm, gmlp_out_norm, w_out, post_mix_norm, pre_ffn_norm, w_gate, w_up, w_down, post_ffn_norm, loss_target, m_pre_mix_norm, m_w_in, m_q_norm, m_kv_norm, m_w_uq, m_w_ukv, m_v_ln_gain, m_v_ln_bias, m_w_spatial, m_b_spatial, m_attn_out_norm, m_gmlp_out_norm, m_w_out, m_post_mix_norm, m_pre_ffn_norm, m_w_gate, m_w_up, m_w_down, m_post_ffn_norm, v_pre_mix_norm, v_w_in, v_q_norm, v_kv_norm, v_w_uq, v_w_ukv, v_v_ln_gain, v_v_ln_bias, v_w_spatial, v_b_spatial, v_attn_out_norm, v_gmlp_out_norm, v_w_out, v_post_mix_norm, v_pre_ffn_norm, v_w_gate, v_w_up, v_w_down, v_post_ffn_norm):
    given = dict(locals())
    T, D = x.shape[1], x.shape[2]
    QL, KVL = w_uq.shape[1], w_ukv.shape[1]
    GW, GH = v_ln_gain.shape[1], w_spatial.shape[1]
    IWS = w_in.shape[2]
    IW = IWS * N_DEV
    H = w_ukv.shape[2] * N_DEV // (NOPE_DIM + V_DIM)
    AW = H * V_DIM
    FS = w_gate.shape[2]
    FP = _round_up(FS, LANE)
    C0 = QL + KVL
    assert IW == C0 + ROPE_DIM + 2 * GW and AW + GW == D and AW == GW
    assert QL % KVL == 0 and C0 % K_FULL == 0 and w_spatial.shape[2] == CHUNK and T % CHUNK == 0
    assert w_uq.shape[2] * N_DEV == H * QK_HEAD and D % N_DEV == 0

    x2, tgt = x[0], loss_target[0]
    tm_d = _row_tile(T, D)

    inv_freq = 1.0 / (ROPE_THETA ** (jnp.arange(0, ROPE_DIM, 2, dtype=F32) / ROPE_DIM))
    ang = positions[0].astype(F32)[:, None] * inv_freq
    zpad = jnp.zeros((T, LANE - ROPE_DIM), F32)
    cosp = jnp.concatenate([jnp.cos(ang), jnp.cos(ang), zpad], axis=1)
    sinp = jnp.concatenate([jnp.sin(ang), jnp.sin(ang), zpad], axis=1)

    def shard_t(a):
        return a[0].T

    def padded_rows(a):
        return jnp.pad(a.astype(BF16), ((0, FP - FS), (0, 0)))

    st_a, tok = _ag_start("ag_a", [shard_t(w_in).astype(BF16), w_uq[0].astype(BF16), w_ukv[0].astype(BF16)], [])
    st_o, tok = _ag_start("ag_o", [w_out[0].astype(BF16)], [tok])
    st_g, tok = _ag_start("ag_g", [padded_rows(shard_t(w_gate))], [tok])
    st_u, tok = _ag_start("ag_u", [padded_rows(shard_t(w_up))], [tok])
    st_d, tok = _ag_start("ag_d", [padded_rows(w_down[0])], [tok])

    xn = _ew("rms_x", lambda x, g: _rms(x, g), (T // tm_d,), [_rows(x2, tm_d), _vec(pre_mix_norm)],
             [_rows_out((T, D), BF16, tm_d)], deps=[tok])[0]
    st_a, tok = _ag_mid("ag_a", st_a, xn)
    g_in, g_uq, g_ukv = _ag_finish("ag_a", st_a, tok)

    w_in_t = g_in.reshape(IW, D)
    kr_w = w_in_t[C0:C0 + ROPE_DIM]
    z64 = jnp.zeros((LANE - ROPE_DIM, D), BF16)
    w_a_t = jnp.concatenate([w_in_t[:C0], kr_w, z64, _rot(kr_w, 0), z64], axis=0)
    w_b_t = w_in_t[C0 + ROPE_DIM:]
    wq = g_uq.transpose(1, 0, 2).reshape(QL, H, QK_HEAD)
    zq = jnp.zeros((QL, H, LANE - ROPE_DIM), BF16)
    wq_ext = jnp.concatenate([wq[..., :NOPE_DIM], wq[..., NOPE_DIM:], zq, _rot(wq[..., NOPE_DIM:]), zq],
                             axis=-1).reshape(QL, H * Q_EXT)
    wkv = g_ukv.transpose(1, 0, 2).reshape(KVL, H * (NOPE_DIM + V_DIM))

    ws = w_spatial[0].astype(BF16)
    ws_t = jnp.swapaxes(ws, 1, 2)
    bsb = jnp.broadcast_to(b_spatial[0][:, :, None], (GH, CHUNK, GW // GH))

    proj_a = _mm2("mm_proj_a", "nt", xn, w_a_t, F32, tn_pref=w_a_t.shape[0])
    proj_b = _mm2("mm_proj_b", "nt", xn, w_b_t, F32)

    tm_q = _row_tile(T, QL)
    qn = _ew("rms_q", lambda x, g: _rms(x, g), (T // tm_q,), [_rows(proj_a, tm_q, QL, 0), _vec(q_norm)],
             [_rows_out((T, QL), BF16, tm_q)])[0]
    kvn = _ew("rms_kv", lambda x, g: _rms(x, g), (T // tm_q,), [_rows(proj_a, tm_q, KVL, QL // KVL), _vec(kv_norm)],
              [_rows_out((T, KVL), BF16, tm_q)])[0]
    q_ext = _mm2("mm_q", "nn", qn, wq_ext, F32)
    kv = _mm2("mm_kv", "nn", kvn, wkv, F32)

    tm_h = _pick(T, 512, SUBLANE)
    hb = _pick(H, 4, 1)

    def make_kv(kv, kr, c, s):
        k_rope = kr[:, :LANE] * c + kr[:, LANE:] * s
        ones = jnp.ones((kv.shape[0], V_DIM), F32)
        ks = [jnp.concatenate([kv[:, j * K_FULL:j * K_FULL + LANE], k_rope], axis=1) for j in range(hb)]
        vs = [jnp.concatenate([kv[:, j * K_FULL + LANE:(j + 1) * K_FULL], ones], axis=1) for j in range(hb)]
        return jnp.stack(ks), jnp.stack(vs)

    k_full, v_aug = _ew("make_kv", make_kv, (T // tm_h, H // hb),
                        [(kv, (tm_h, hb * K_FULL), lambda i, h: (i, h)),
                         (proj_a, (tm_h, K_FULL), lambda i, h: (i, C0 // K_FULL)),
                         (cosp, (tm_h, LANE), lambda i, h: (i, 0)), (sinp, (tm_h, LANE), lambda i, h: (i, 0))],
                        [((H, T, K_FULL), BF16, (hb, tm_h, K_FULL), lambda i, h: (h, i, 0)),
                         ((H, T, 2 * V_DIM), BF16, (hb, tm_h, 2 * V_DIM), lambda i, h: (h, i, 0))])

    a_out, lse = _attn_fwd(q_ext, k_full, v_aug, cosp, sinp)
    st_o, tok_o = _ag_mid("ag_o", st_o, a_out)
    tm_a = _row_tile(T, AW)
    mixed_a = _ew("rms_attn", lambda x, g: _rms(x, g), (T // tm_a,), [_rows(a_out, tm_a), _vec(attn_out_norm)],
                  [_rows_out((T, AW), BF16, tm_a)], deps=[tok_o])[0]
    mixed_g = _sgu_fwd(proj_b, v_ln_gain, v_ln_bias, ws, bsb, gmlp_out_norm)
    wo = _ag_finish("ag_o", st_o, mixed_g)[0].reshape(D, D)
    st_g, tok_gu = _ag_mid("ag_g", st_g, wo)

    tm_, tn_ = _pick(T, 1024, LANE), _pick(D, 1024, LANE)
    tk_ = _fit_tk(AW, tm_, tn_, pairs=2)
    nka = AW // tk_
    mix_out = _mm("mm_out", "nn",
                  [(mixed_a, (tm_, tk_), lambda i, j, k: (i, k), wo, (tk_, tn_), lambda i, j, k: (k, j)),
                   (mixed_g, (tm_, tk_), lambda i, j, k: (i, k), wo, (tk_, tn_), lambda i, j, k: (k + nka, j))],
                  (T // tm_, D // tn_, nka), (tm_, tn_), (T, D), F32, (tm_, tn_), lambda i, j, k: (i, j), deps=[tok_gu])

    def post_mix(x, mo, g3, g4):
        h = x + _rms(mo, g3)
        return h, _rms(h, g4)

    h_res, hn = _ew("post_mix", post_mix, (T // tm_d,),
                    [_rows(x2, tm_d), _rows(mix_out, tm_d), _vec(post_mix_norm), _vec(pre_ffn_norm)],
                    [_rows_out((T, D), F32, tm_d), _rows_out((T, D), BF16, tm_d)])

    F = N_DEV * FP
    tn_f = _pick(F, FFN_TILE_MAX, MXU_DIM)
    tn_e = _pick(F, FFN_TILE_MAX // 2, LANE)

    def swiglu(u, g):
        g = g.astype(F32)
        return u, g * _sigmoid(g) * u

    st_u, tok_u = _ag_mid("ag_u", st_u, hn)
    wg_t = _ag_finish("ag_g", st_g, tok_u)[0].reshape(F, D)
    gate = _mm2("mm_gate", "nt", hn, wg_t, BF16, tn_pref=tn_f)
    wu_t = _ag_finish("ag_u", st_u, gate)[0].reshape(F, D)
    st_d, tok_d = _ag_mid("ag_d", st_d, wu_t)
    up, act = _mm2("mm_up", "nt", hn, wu_t, BF16, tn_pref=tn_e, deps=[tok_d], extras=[gate],
                   epilogue=swiglu, n_out=2)
    wd_f = _ag_finish("ag_d", st_d, act)[0].reshape(F, D)
    ffn = _mm2("mm_down", "nn", act, wd_f, F32, tk_max=tn_f)

    def loss_head(h, f, t, g5):
        y = h + _rms(f, g5)
        diff = y - t
        dy = diff / D
        d_f, dg5 = _rms_bwd(f, g5, dy)
        lsum = jnp.sum(jnp.sum(diff * diff, axis=0, keepdims=True), axis=1, keepdims=True) * (0.5 / D)
        return dy, d_f, dg5, lsum

    dy, d_ffn, dg_post_ffn, loss_part = _ew(
        "loss_head", loss_head, (T // tm_d,),
        [_rows(h_res, tm_d), _rows(ffn, tm_d), _rows(tgt, tm_d), _vec(post_ffn_norm)],
        [_rows_out((T, D), F32, tm_d), _rows_out((T, D), BF16, tm_d)], [_vec_out((1, D)), _vec_out((1, 1))])

    def swiglu_bwd(da, g, u):
        g, u = g.astype(F32), u.astype(F32)
        sg = _sigmoid(g)
        return da * u * sg * (1.0 + g * (1.0 - sg)), da * g * sg

    def ffn_dw(name, z, y, deps):
        return _mm2(name, "tn", z, y, BF16, tm_pref=tn_f, deps=deps).reshape(N_DEV, FP, D)

    x_d, tok = _a2a_start("a2a_d", [ffn_dw("mm_dwd", act, d_ffn, [])], [])
    d_gate, d_up = _mm2("mm_dact", "nt", d_ffn, wd_f, BF16, tn_pref=tn_e, deps=[tok], extras=[gate, up],
                        epilogue=swiglu_bwd, n_out=2)
    x_g, tok = _a2a_start("a2a_g", [ffn_dw("mm_dwg", d_gate, hn, [])], [])
    x_u, tok = _a2a_start("a2a_u", [ffn_dw("mm_dwu", d_up, hn, [tok])], [])
    d_hn_g = _mm2("mm_dhn_g", "nn", d_gate, wg_t, F32, tk_max=tn_f, deps=[tok])
    d_hn = _mm2("mm_dhn_u", "nn", d_up, wu_t, F32, tk_max=tn_f, extras=[d_hn_g], epilogue=lambda s, g: (s + g,))

    def pre_ffn_bwd(h, dhn, dy, mo, g4, g3):
        d1, dg4 = _rms_bwd(h, g4, dhn)
        dh = dy + d1
        dmo, dg3 = _rms_bwd(mo, g3, dh)
        return dh, dmo, dg4, dg3

    dh, d_mo, dg_pre_ffn, dg_post_mix = _ew(
        "pre_ffn_bwd", pre_ffn_bwd, (T // tm_d,),
        [_rows(h_res, tm_d), _rows(d_hn, tm_d), _rows(dy, tm_d), _rows(mix_out, tm_d), _vec(pre_ffn_norm),
         _vec(post_mix_norm)],
        [_rows_out((T, D), F32, tm_d), _rows_out((T, D), BF16, tm_d)], [_vec_out((1, D)), _vec_out((1, D))])

    d_mixed = _mm2("mm_dmixed", "nt", d_mo, wo, F32)
    dwo = jnp.concatenate([_mm2("mm_dwo_a", "tn", mixed_a, d_mo, BF16), _mm2("mm_dwo_g", "tn", mixed_g, d_mo, BF16)],
                          axis=0)
    x_o, tok = _a2a_start("a2a_o", [dwo.reshape(N_DEV, D // N_DEV, D)], [])

    d_a_out, dg_attn = _ew("rms_attn_bwd", lambda a, g, d: _rms_bwd(a, g, d), (T // tm_a,),
                           [_rows(a_out, tm_a), _vec(attn_out_norm), _rows(d_mixed, tm_a, AW, 0)],
                           [_rows_out((T, AW), BF16, tm_a)], [_vec_out((1, AW))], deps=[tok])
    d_pb, dg_gmlp, dws, dbs, dlng, dlnb = _sgu_bwd(proj_b, d_mixed, v_ln_gain, v_ln_bias, ws, ws_t, bsb, gmlp_out_norm)

    dq_ext, d_kv, d_kr = _attn_bwd(q_ext, k_full, v_aug, cosp, sinp, d_a_out, a_out, lse.reshape(H, 1, T))

    d_qn = _mm2("mm_dqn", "nt", dq_ext, wq_ext, F32)
    dwq_ext = _mm2("mm_dwq", "tn", qn, dq_ext, F32)
    d_kvn = _mm2("mm_dkvn", "nt", d_kv, wkv, F32)
    dwkv = _mm2("mm_dwkv", "tn", kvn, d_kv, BF16)

    d_qc, dg_q = _ew("rms_q_bwd", lambda a, g, d: _rms_bwd(a, g, d), (T // tm_q,),
                     [_rows(proj_a, tm_q, QL, 0), _vec(q_norm), _rows(d_qn, tm_q)],
                     [_rows_out((T, QL), BF16, tm_q)], [_vec_out((1, QL))])
    d_kvc, dg_kv = _ew("rms_kv_bwd", lambda a, g, d: _rms_bwd(a, g, d), (T // tm_q,),
                       [_rows(proj_a, tm_q, KVL, QL // KVL), _vec(kv_norm), _rows(d_kvn, tm_q)],
                       [_rows_out((T, KVL), BF16, tm_q)], [_vec_out((1, KVL))])
    d_pa = jnp.concatenate([d_qc, d_kvc, d_kr.astype(BF16)], axis=1)

    dwa_t = _mm2("mm_dwa", "tn", d_pa, xn, F32, tm_pref=w_a_t.shape[0])
    dwb_t = _mm2("mm_dwb", "tn", d_pb, xn, BF16)
    dkr_w = dwa_t[C0:C0 + ROPE_DIM] + _rot_t(dwa_t[C0 + LANE:C0 + LANE + ROPE_DIM], 0)
    dw_in = jnp.concatenate([dwa_t[:C0].astype(BF16), dkr_w.astype(BF16), dwb_t], axis=0)
    dw_in = dw_in.reshape(N_DEV, IWS, D)
    dq3 = dwq_ext.reshape(QL, H, Q_EXT)
    dwq = jnp.concatenate([dq3[..., :NOPE_DIM],
                           dq3[..., LANE:LANE + ROPE_DIM] + _rot_t(dq3[..., 2 * LANE:2 * LANE + ROPE_DIM])], axis=-1)
    dwq = dwq.astype(BF16).reshape(QL, N_DEV, H * QK_HEAD // N_DEV).transpose(1, 0, 2)
    dwkv = dwkv.reshape(KVL, N_DEV, w_ukv.shape[2]).transpose(1, 0, 2)
    p_e, tok = _rs_pair_start("rs_e", [dw_in, dwq, dwkv], [])
    d_xn_a = _mm2("mm_dxn_a", "nn", d_pa, w_a_t, F32, deps=[tok])
    x_e, tok = _rs_chip_start("rs_e2", _rs_pair_sum("rs_e", p_e, d_xn_a), [])
    d_xn_b = _mm2("mm_dxn_b", "nn", d_pb, w_b_t, F32, deps=[tok])

    def x_bwd(x, da, db, dh, g0):
        dx, dg0 = _rms_bwd(x, g0, da + db)
        return dh + dx, dg0

    grad_x, dg_pre_mix = _ew("x_bwd", x_bwd, (T // tm_d,),
                             [_rows(x2, tm_d), _rows(d_xn_a, tm_d), _rows(d_xn_b, tm_d), _rows(dh, tm_d),
                              _vec(pre_mix_norm)],
                             [_rows_out((T, D), F32, tm_d)], [_vec_out((1, D))])

    def adam_big(n, parts, transposed=False):
        view = shard_t if transposed else (lambda a: a[0])
        res = _adamw("adamw_" + n, parts, view(given[n]), view(given["m_" + n]), view(given["v_" + n]))
        done[n] = res[0]
        return [(r.T if transposed else r)[None] for r in res]

    done = {}

    small_names = ["pre_mix_norm", "q_norm", "kv_norm", "v_ln_gain", "v_ln_bias", "w_spatial", "b_spatial",
                   "attn_out_norm", "gmlp_out_norm", "post_mix_norm", "pre_ffn_norm", "post_ffn_norm"]
    small_grads = [dg_pre_mix, dg_q, dg_kv, dlng, dlnb, dws, dbs[:, 0, :], dg_attn, dg_gmlp, dg_post_mix, dg_pre_ffn,
                   dg_post_ffn]
    n_small = sum(g.size for g in small_grads)
    pad_small = _round_up(n_small, SUBLANE * LANE) - n_small

    def pack(arrs):
        flat = jnp.concatenate([a.reshape(-1) for a in arrs] + [jnp.zeros((pad_small,), F32)])
        return flat.reshape(-1, LANE)

    st_s, tok = _ag_start("ag_s", [pack(small_grads)], [])

    results = {}
    results["w_down"] = adam_big("w_down", _a2a_finish("a2a_d", x_d, tok)[0])
    results["w_gate"] = adam_big("w_gate", _a2a_finish("a2a_g", x_g, done["w_down"])[0], True)
    st_s, tok = _ag_mid("ag_s", st_s, done["w_gate"])
    results["w_up"] = adam_big("w_up", _a2a_finish("a2a_u", x_u, tok)[0], True)
    results["w_out"] = adam_big("w_out", _a2a_finish("a2a_o", x_o, done["w_up"])[0])
    e_in, e_uq, e_ukv = _rs_chip_finish("rs_e2", x_e, done["w_out"])
    results["w_in"] = adam_big("w_in", e_in, True)
    results["w_uq"] = adam_big("w_uq", e_uq)
    results["w_ukv"] = adam_big("w_ukv", e_ukv)

    small_parts = _ag_finish("ag_s", st_s, done["w_ukv"])[0]
    sm = _adamw("adamw_small", small_parts, pack([given[n] for n in small_names]),
                pack([given["m_" + n] for n in small_names]), pack([given["v_" + n] for n in small_names]))

    off = 0
    for n in small_names:
        shp, size = given[n].shape, given[n].size
        results[n] = [r.reshape(-1)[off:off + size].reshape(shp) for r in sm]
        off += size

    loss = lax.psum(loss_part[0, 0], MESH_AXES)
    order = ["pre_mix_norm", "w_in", "q_norm", "kv_norm", "w_uq", "w_ukv", "v_ln_gain", "v_ln_bias", "w_spatial",
             "b_spatial", "attn_out_norm", "gmlp_out_norm", "w_out", "post_mix_norm", "pre_ffn_norm", "w_gate", "w_up",
             "w_down", "post_ffn_norm"]
    out = [loss, grad_x[None]]
    for kind in range(4):
        out += [results[n][kind] for n in order]
    return tuple(out)
```

```python
import math

import jax
import jax.numpy as jnp
from jax import lax
from jax.experimental import pallas as pl
from jax.experimental.pallas import tpu as pltpu

F32 = jnp.float32
BF16 = jnp.bfloat16
N_DEV = 8
MESH_AXES = ("x", "y", "c")
MESH = pl.DeviceIdType.MESH

NOPE_DIM = 128
ROPE_DIM = 64
V_DIM = 128
ROPE_THETA = 10000.0
CHUNK = 128
EPS = 1e-6
QK_HEAD = NOPE_DIM + ROPE_DIM
Q_EXT = 3 * 128
K_FULL = 2 * 128

ADAM_LR = 0.001
ADAM_B1 = 0.9
ADAM_B2 = 0.999
ADAM_EPS = 1e-08
ADAM_WD = 0.01
ADAM_STEP = 10

LANE = 128
SUBLANE = 8
VMEM_LIMIT = 56 * 1024 * 1024
MM_VMEM_BUDGET = 44 * 1024 * 1024
MXU_DIM = 256
FFN_TILE_MAX = 3072
MM_TK_MAX = 4096
EPILOGUE_ROWS = 256
ATTN_TQ = 1024
ATTN_TKV = 512
NEG_BIG = -1e30
ROW_BLOCK_ELEMS = 512 * 1024
ADAM_BLOCK_ELEMS = 256 * 1024

DN = {
    "nn": (((1,), (0,)), ((), ())),
    "nt": (((1,), (1,)), ((), ())),
    "tn": (((0,), (0,)), ((), ())),
}


def _pick(dim, pref, mult):
    t = min(pref, dim)
    t -= t % mult
    while t >= mult:
        if dim % t == 0:
            return t
        t -= mult
    return dim


def _round_up(n, m):
    return (n + m - 1) // m * m


def _params(sem):
    return pltpu.CompilerParams(dimension_semantics=sem, vmem_limit_bytes=VMEM_LIMIT)


def _dep_specs(deps, nd):
    return [pl.BlockSpec(d.shape, lambda *_, r=d.ndim: (0,) * r) for d in deps]


def _mm(name, mode, pairs, grid, acc_shape, out_shape, out_dtype, out_block, out_map, deps=(), extras=(),
        epilogue=None, n_out=1):
    n, ne, nx = len(pairs), len(extras), len(deps)
    nk = grid[-1]
    nd = len(grid)
    dn = DN[mode]
    use_acc = nk > 1 or epilogue is not None

    def body(*refs):
        e_refs = refs[2 * n:2 * n + ne]
        o_refs = refs[2 * n + ne + nx:2 * n + ne + nx + n_out]
        k = pl.program_id(nd - 1)

        def dot(i):
            return lax.dot_general(refs[2 * i][...], refs[2 * i + 1][...], dn, preferred_element_type=F32)

        if not use_acc:
            s = dot(0)
            for i in range(1, n):
                s = s + dot(i)
            o_refs[0][...] = s.astype(o_refs[0].dtype)
            return
        acc = refs[2 * n + ne + nx + n_out]

        @pl.when(k == 0)
        def _():
            acc[...] = dot(0)

        if nk > 1:
            @pl.when(k > 0)
            def _():
                acc[...] += dot(0)

        for i in range(1, n):
            acc[...] += dot(i)

        if epilogue is None:
            @pl.when(k == nk - 1)
            def _():
                o_refs[0][...] = acc[...].astype(o_refs[0].dtype)
        else:
            @pl.when(k == nk - 1)
            def _():
                rows = acc_shape[0]
                slab = _pick(rows, EPILOGUE_ROWS, SUBLANE)
                for r0 in range(0, rows, slab):
                    sl = pl.ds(r0, slab)
                    vals = epilogue(acc[sl, :], *[e[sl, :] for e in e_refs])
                    for r, v in zip(o_refs, vals):
                        r[sl, :] = v.astype(r.dtype)

    in_specs, args = [], []
    for a, a_blk, a_map, b, b_blk, b_map in pairs:
        in_specs += [pl.BlockSpec(a_blk, a_map), pl.BlockSpec(b_blk, b_map)]
        args += [a, b]
    in_specs += [pl.BlockSpec(blk, imap) for (_, blk, imap) in extras]
    args += [e for (e, _, _) in extras]
    if not isinstance(out_dtype, (tuple, list)):
        out_dtype = (out_dtype,) * n_out
    res = pl.pallas_call(
        body,
        name=name,
        grid=grid,
        in_specs=in_specs + _dep_specs(deps, nd),
        out_specs=[pl.BlockSpec(out_block, out_map)] * n_out,
        out_shape=[jax.ShapeDtypeStruct(out_shape, dt) for dt in out_dtype],
        scratch_shapes=[pltpu.VMEM(acc_shape, F32)] if use_acc else [],
        compiler_params=_params(("parallel",) * (nd - 1) + ("arbitrary",)),
    )(*args, *deps)
    return res[0] if n_out == 1 else res


def _fit_tk(K, tm, tn, pairs=1, out_bytes=4, fixed_extra=0, tk_max=MM_TK_MAX, needs_acc=False):
    fixed = tm * tn * 4 + 2 * tm * tn * out_bytes + fixed_extra
    if K <= tk_max and not needs_acc and fixed - tm * tn * 4 + pairs * 4 * (tm + tn) * K <= MM_VMEM_BUDGET:
        return K
    tk = _pick(K, tk_max, LANE)
    while tk > LANE and fixed + pairs * 4 * (tm + tn) * tk > MM_VMEM_BUDGET:
        smaller = _pick(K, tk - LANE, LANE)
        if smaller >= tk:
            break
        tk = smaller
    return tk


def _mm2(name, mode, a, b, out_dtype, tm_pref=1024, tn_pref=1024, deps=(), extras=(), epilogue=None, n_out=1,
         tk_max=MM_TK_MAX):
    if mode == "nn":
        (M, K), N = a.shape, b.shape[1]
    elif mode == "nt":
        (M, K), N = a.shape, b.shape[0]
    else:
        (K, M), N = a.shape, b.shape[1]
    tm, tn = _pick(M, tm_pref, LANE), _pick(N, tn_pref, LANE)
    extra_bytes = sum(2 * tm * tn * jnp.dtype(e.dtype).itemsize for e in extras)
    tk = _fit_tk(K, tm, tn, out_bytes=n_out * jnp.dtype(out_dtype).itemsize, fixed_extra=extra_bytes, tk_max=tk_max,
                 needs_acc=epilogue is not None)
    if mode == "tn":
        a_blk, a_map = (tk, tm), lambda i, j, k: (k, i)
    else:
        a_blk, a_map = (tm, tk), lambda i, j, k: (i, k)
    if mode == "nt":
        b_blk, b_map = (tn, tk), lambda i, j, k: (j, k)
    else:
        b_blk, b_map = (tk, tn), lambda i, j, k: (k, j)
    return _mm(name, mode, [(a, a_blk, a_map, b, b_blk, b_map)], (M // tm, N // tn, K // tk), (tm, tn),
               (M, N), out_dtype, (tm, tn), lambda i, j, k: (i, j), deps=deps,
               extras=[(e, (tm, tn), lambda i, j, k: (i, j)) for e in extras], epilogue=epilogue, n_out=n_out)


def _ew(name, fn, grid, ins, outs, accs=(), deps=()):
    ni, no, na = len(ins), len(outs), len(accs)
    nd = len(grid)
    nx = len(deps)

    def body(*refs):
        vals = fn(*[r[...] for r in refs[:ni]])
        if not isinstance(vals, (tuple, list)):
            vals = (vals,)
        for r, v in zip(refs[ni + nx:ni + nx + no], vals[:no]):
            r[...] = v.astype(r.dtype)
        if na:
            first = pl.program_id(nd - 1) == 0
            for r, v in zip(refs[ni + nx + no:], vals[no:]):
                @pl.when(first)
                def _(r=r, v=v):
                    r[...] = v.astype(r.dtype)

                @pl.when(jnp.logical_not(first))
                def _(r=r, v=v):
                    r[...] += v.astype(r.dtype)

    all_outs = list(outs) + list(accs)
    sem = ("parallel",) * (nd - 1) + (("arbitrary",) if na else ("parallel",))
    res = pl.pallas_call(
        body,
        name=name,
        grid=grid,
        in_specs=[pl.BlockSpec(blk, imap) for (_, blk, imap) in ins] + _dep_specs(deps, nd),
        out_specs=[pl.BlockSpec(blk, imap) for (_, _, blk, imap) in all_outs],
        out_shape=[jax.ShapeDtypeStruct(s, d) for (s, d, _, _) in all_outs],
        compiler_params=_params(sem),
    )(*[a for (a, _, _) in ins], *deps)
    return res


def _cast_pad_rows(name, a, rows_out, deps):
    R, C = a.shape
    tr = _pick(math.gcd(R, rows_out), max(2 * SUBLANE, ROW_BLOCK_ELEMS // C), SUBLANE)
    n_in = R // tr

    def body(a_ref, *rest):
        o_ref = rest[len(deps)]
        i = pl.program_id(0)

        @pl.when(i < n_in)
        def _():
            o_ref[...] = a_ref[...].astype(BF16)

        @pl.when(i >= n_in)
        def _():
            o_ref[...] = jnp.zeros_like(o_ref)

    return pl.pallas_call(
        body,
        name=name,
        grid=(rows_out // tr,),
        in_specs=[pl.BlockSpec((tr, C), lambda i: (jnp.minimum(i, n_in - 1), 0))] + _dep_specs(deps, 1),
        out_specs=pl.BlockSpec((tr, C), lambda i: (i, 0)),
        out_shape=jax.ShapeDtypeStruct((rows_out, C), BF16),
        compiler_params=_params(("parallel",)),
    )(a, *deps)


def _rows(a, tm, width=None, cblk=0):
    width = a.shape[1] if width is None else width
    return (a, (tm, width), lambda i, c=cblk: (i, c))


def _vec(p):
    return (p, p.shape, lambda i: (0,) * p.ndim)


def _rows_out(shape, dtype, tm):
    return (shape, dtype, (tm, shape[1]), lambda i: (i, 0))


def _vec_out(shape):
    return (shape, F32, shape, lambda i: (0,) * len(shape))


def _row_tile(T, D):
    return _pick(T, max(SUBLANE, ROW_BLOCK_ELEMS // D), SUBLANE)


def _rms(x, g):
    r = lax.rsqrt(jnp.mean(x * x, axis=-1, keepdims=True) + EPS)
    return x * r * g


def _rms_bwd(x, g, dy):
    r = lax.rsqrt(jnp.mean(x * x, axis=-1, keepdims=True) + EPS)
    xh = x * r
    dg = jnp.sum(dy * xh, axis=0, keepdims=True)
    dxh = dy * g
    dx = r * (dxh - xh * jnp.mean(dxh * xh, axis=-1, keepdims=True))
    return dx, dg


GELU_C = math.sqrt(2.0 / math.pi)
GELU_A = 0.044715


def _sigmoid(x):
    return 0.5 * jnp.tanh(0.5 * x) + 0.5


def _gelu(x):
    return 0.5 * x * (1.0 + jnp.tanh(GELU_C * (x + GELU_A * x * x * x)))


def _gelu_grad(x):
    t = jnp.tanh(GELU_C * (x + GELU_A * x * x * x))
    return 0.5 * (1.0 + t) + 0.5 * x * (1.0 - t * t) * GELU_C * (1.0 + 3.0 * GELU_A * x * x)


def _rot(w, axis=-1):
    w1, w2 = jnp.split(w, 2, axis=axis)
    return jnp.concatenate([-w2, w1], axis=axis)


def _rot_t(dw, axis=-1):
    d1, d2 = jnp.split(dw, 2, axis=axis)
    return jnp.concatenate([d2, -d1], axis=axis)


def _coords():
    return lax.axis_index("x"), lax.axis_index("y"), lax.axis_index("c")


HBM_SPEC = pl.BlockSpec(memory_space=pltpu.HBM)
SEM_SPEC = pl.BlockSpec(memory_space=pltpu.SEMAPHORE)
ANY_SPEC = pl.BlockSpec(memory_space=pl.ANY)
EFFECT = pltpu.SideEffectType.DATAFLOW_SIDE_EFFECTING
TOKEN = (SUBLANE, LANE)


def _my_slot():
    xi, yi, ci = _coords()
    return 4 * xi + 2 * yi + ci


def _own_slot_filled(block):
    land = lax.empty((N_DEV,) + block.shape, block.dtype)
    return lax.dynamic_update_slice(land, block[None], (_my_slot(),) + (0,) * block.ndim)


def _split_start(name, n, n_copies, issue, srcs, lands, deps):
    def body(*refs):
        src_refs, land_refs = refs[:n], refs[n:2 * n]
        outs = refs[2 * n + len(deps):]
        send, recv, token = outs[:n], outs[n:2 * n], outs[4 * n]
        for t in range(n):
            issue(t, src_refs[t], land_refs[t], send[t], recv[t])
        token[...] = jnp.zeros_like(token)

    sems = [pltpu.SemaphoreType.DMA((n_copies,))] * (2 * n)
    res = pl.pallas_call(
        body,
        name=name,
        in_specs=[HBM_SPEC] * (2 * n) + [ANY_SPEC] * len(deps),
        out_specs=[SEM_SPEC] * (2 * n) + [HBM_SPEC] * (2 * n) + [pl.BlockSpec(memory_space=pltpu.VMEM)],
        out_shape=sems + [pltpu.HBM(a.shape, a.dtype) for a in list(srcs) + list(lands)]
        + [jax.ShapeDtypeStruct(TOKEN, F32)],
        input_output_aliases={i: 2 * n + i for i in range(2 * n)},
        compiler_params=pltpu.CompilerParams(has_side_effects=EFFECT),
    )(*[pltpu.with_memory_space_constraint(a, pltpu.HBM) for a in list(srcs) + list(lands)], *deps)
    state = dict(n=n, n_copies=n_copies, send=res[:n], recv=res[n:2 * n], srcs=res[2 * n:3 * n],
                 lands=res[3 * n:4 * n])
    return state, res[4 * n]


def _split_wait(name, state, unit_src, after):
    n, n_copies = state["n"], state["n_copies"]

    def body(*refs):
        src_refs, land_refs = refs[:n], refs[n:2 * n]
        send, recv = refs[2 * n:3 * n], refs[3 * n:4 * n]
        me = _coords()
        for t in range(n):
            for k in range(n_copies):
                cp = pltpu.make_async_remote_copy(
                    src_ref=unit_src(src_refs[t]), dst_ref=land_refs[t].at[0], send_sem=send[t].at[k],
                    recv_sem=recv[t].at[k], device_id=me, device_id_type=MESH)
                cp.wait_send()
                cp.wait_recv()

    bufs = list(state["srcs"]) + list(state["lands"])
    res = pl.pallas_call(
        body,
        name=name,
        in_specs=[HBM_SPEC] * (2 * n) + [SEM_SPEC] * (2 * n) + [ANY_SPEC],
        out_specs=[HBM_SPEC] * (2 * n),
        out_shape=[pltpu.HBM(a.shape, a.dtype) for a in bufs],
        input_output_aliases={i: i for i in range(2 * n)},
        compiler_params=pltpu.CompilerParams(has_side_effects=EFFECT),
    )(*bufs, *state["send"], *state["recv"], after)
    return list(res[n:])


def _ag_start(name, shards, deps):
    def issue(t, x_ref, land_ref, send, recv):
        xi, yi, ci = _coords()
        to = [(xi, yi, 1 - ci), (1 - xi, yi, ci), (xi, 1 - yi, ci), (1 - xi, 1 - yi, ci)]
        for k, dev in enumerate(to):
            pltpu.make_async_remote_copy(src_ref=x_ref, dst_ref=land_ref.at[4 * xi + 2 * yi + ci], send_sem=send.at[k],
                                         recv_sem=recv.at[k], device_id=dev, device_id_type=MESH).start()

    return _split_start(name, len(shards), 4, issue, shards, [_own_slot_filled(s) for s in shards], deps)


def _ag_mid(name, state, after):
    lands = _split_wait(name + "_w1", state, lambda r: r, after)

    def issue(t, src_ref, land_ref, send, recv):
        xi, yi, ci = _coords()
        for k, (px, py) in enumerate([(1 - xi, yi), (xi, 1 - yi), (1 - xi, 1 - yi)]):
            rows = land_ref.at[4 * px + 2 * py + ci]
            pltpu.make_async_remote_copy(src_ref=rows, dst_ref=rows, send_sem=send.at[k], recv_sem=recv.at[k],
                                         device_id=(xi, yi, 1 - ci), device_id_type=MESH).start()

    n = len(lands)

    def body(*refs):
        land_refs = refs[:n]
        outs = refs[n:]
        send, recv, token = outs[:n], outs[n:2 * n], outs[3 * n]
        for t in range(n):
            issue(t, land_refs[t], land_refs[t], send[t], recv[t])
        token[...] = jnp.zeros_like(token)

    res = pl.pallas_call(
        body,
        name=name + "_s2",
        in_specs=[HBM_SPEC] * n,
        out_specs=[SEM_SPEC] * (2 * n) + [HBM_SPEC] * n + [pl.BlockSpec(memory_space=pltpu.VMEM)],
        out_shape=[pltpu.SemaphoreType.DMA((3,))] * (2 * n) + [pltpu.HBM(a.shape, a.dtype) for a in lands]
        + [jax.ShapeDtypeStruct(TOKEN, F32)],
        input_output_aliases={i: 2 * n + i for i in range(n)},
        compiler_params=pltpu.CompilerParams(has_side_effects=EFFECT),
    )(*lands)
    state2 = dict(n=n, send=res[:n], recv=res[n:2 * n], lands=res[2 * n:3 * n])
    return state2, res[3 * n]


def _ag_finish(name, state2, after):
    n = state2["n"]

    def body(*refs):
        land_refs, send, recv = refs[:n], refs[n:2 * n], refs[2 * n:3 * n]
        me = _coords()
        for t in range(n):
            for k in range(3):
                cp = pltpu.make_async_remote_copy(
                    src_ref=land_refs[t].at[0], dst_ref=land_refs[t].at[0], send_sem=send[t].at[k],
                    recv_sem=recv[t].at[k], device_id=me, device_id_type=MESH)
                cp.wait_send()
                cp.wait_recv()

    res = pl.pallas_call(
        body,
        name=name + "_w2",
        in_specs=[HBM_SPEC] * n + [SEM_SPEC] * (2 * n) + [ANY_SPEC],
        out_specs=[HBM_SPEC] * n,
        out_shape=[pltpu.HBM(a.shape, a.dtype) for a in state2["lands"]],
        input_output_aliases={i: i for i in range(n)},
        compiler_params=pltpu.CompilerParams(has_side_effects=EFFECT),
    )(*state2["lands"], *state2["send"], *state2["recv"], after)
    return list(res)


def _a2a_start(name, parts, deps):
    def issue(t, x_ref, land_ref, send, recv):
        xi, yi, ci = _coords()
        for k in range(N_DEV - 1):
            px = 1 - xi if (k + 1) & 4 else xi
            py = 1 - yi if (k + 1) & 2 else yi
            pc = 1 - ci if (k + 1) & 1 else ci
            pltpu.make_async_remote_copy(src_ref=x_ref.at[4 * px + 2 * py + pc], dst_ref=land_ref.at[4 * xi + 2 * yi + ci],
                                         send_sem=send.at[k], recv_sem=recv.at[k], device_id=(px, py, pc),
                                         device_id_type=MESH).start()

    lands = [_own_slot_filled(lax.dynamic_index_in_dim(p, _my_slot(), 0, keepdims=False)) for p in parts]
    return _split_start(name, len(parts), N_DEV - 1, issue, parts, lands, deps)


def _a2a_finish(name, state, after):
    return _split_wait(name + "_w", state, lambda r: r.at[0], after)


N_CHIP = N_DEV // 2


def _rs_pair_start(name, parts, deps):
    def issue(t, x_ref, land_ref, send, recv):
        xi, yi, ci = _coords()
        for k in range(N_CHIP):
            pltpu.make_async_remote_copy(src_ref=x_ref.at[2 * k + (1 - ci)], dst_ref=land_ref.at[k], send_sem=send.at[k],
                                         recv_sem=recv.at[k], device_id=(xi, yi, 1 - ci), device_id_type=MESH).start()

    lands = [lax.empty((N_CHIP,) + p.shape[1:], p.dtype) for p in parts]
    return _split_start(name, len(parts), N_CHIP, issue, parts, lands, deps)


def _rs_pair_sum(name, state, after):
    lands = _split_wait(name + "_w", state, lambda r: r.at[0], after)
    sums = []
    for t, (part, land) in enumerate(zip(state["srcs"], lands)):
        _, R, C = land.shape
        pack = SUBLANE * (4 // jnp.dtype(land.dtype).itemsize)
        if R % pack == 0:
            tr, tc = _pick(R, max(pack, ROW_BLOCK_ELEMS // C), pack), C
        else:
            tr, tc = R, _pick(C, max(LANE, ROW_BLOCK_ELEMS // R // LANE * LANE), LANE)

        def fn(mine, theirs):
            own = jnp.where(lax.axis_index("c") == 0, mine[0], mine[1])
            return own.astype(F32) + theirs.astype(F32)

        sums.append(_ew(f"{name}_sum{t}", fn, (N_CHIP, R // tr, C // tc),
                        [(part.reshape((N_CHIP, 2) + part.shape[1:]), (None, 2, tr, tc), lambda k, i, j: (k, 0, i, j)),
                         (land, (None, tr, tc), lambda k, i, j: (k, i, j))],
                        [(land.shape, land.dtype, (None, tr, tc), lambda k, i, j: (k, i, j))])[0])
    return sums


def _rs_chip_start(name, sums, deps):
    def issue(t, x_ref, land_ref, send, recv):
        xi, yi, ci = _coords()
        for k, (px, py) in enumerate([(1 - xi, yi), (xi, 1 - yi), (1 - xi, 1 - yi)]):
            pltpu.make_async_remote_copy(src_ref=x_ref.at[2 * px + py], dst_ref=land_ref.at[2 * xi + yi], send_sem=send.at[k],
                                         recv_sem=recv.at[k], device_id=(px, py, ci), device_id_type=MESH).start()

    xi, yi, _ = _coords()
    lands = []
    for s in sums:
        own = lax.dynamic_index_in_dim(s, 2 * xi + yi, 0, keepdims=True)
        lands.append(lax.dynamic_update_slice(lax.empty(s.shape, s.dtype), own, (2 * xi + yi,) + (0,) * (s.ndim - 1)))
    return _split_start(name, len(sums), N_CHIP - 1, issue, sums, lands, deps)


def _rs_chip_finish(name, state, after):
    return _split_wait(name + "_w", state, lambda r: r.at[0], after)


QK_SCALE = 1.0 / math.sqrt(QK_HEAD)
LOG2E = 1.0 / math.log(2.0)


def _roped_q(q_ext, cos, sin):
    q = jnp.concatenate([q_ext[:, :LANE], q_ext[:, LANE:2 * LANE] * cos + q_ext[:, 2 * LANE:] * sin], axis=1)
    return (q * (QK_SCALE * LOG2E)).astype(BF16)


def _attn_fwd(q_ext, k_full, v_aug, cosp, sinp):
    H, T, _ = k_full.shape
    tq, tkv = _pick(T, ATTN_TQ, LANE), _pick(T, ATTN_TKV, LANE)

    def body(q_ref, cos_ref, sin_ref, k_ref, v_ref, o_ref, lse_ref):
        q = _roped_q(q_ref[...], cos_ref[...], sin_ref[...])
        m = jnp.full((tq, 1), NEG_BIG, F32)
        acc = jnp.zeros((tq, 2 * V_DIM), F32)
        for c in range(T // tkv):
            keys = pl.ds(c * tkv, tkv)
            s = lax.dot_general(q, k_ref[keys, :], DN["nt"], preferred_element_type=F32)
            m_new = jnp.maximum(m, jnp.max(s, axis=-1, keepdims=True))
            p = jnp.exp2(s - m_new).astype(BF16)
            acc = jnp.exp2(m - m_new) * acc + jnp.dot(p, v_ref[keys, :], preferred_element_type=F32)
            m = m_new
        l = acc[:, V_DIM:]
        o_ref[...] = acc[:, :V_DIM] / l
        lse_ref[...] = m + jnp.log2(jnp.max(l, axis=-1, keepdims=True))

    return pl.pallas_call(
        body,
        name="attn_fwd",
        grid=(H, T // tq),
        in_specs=[pl.BlockSpec((tq, Q_EXT), lambda h, i: (i, h)),
                  pl.BlockSpec((tq, LANE), lambda h, i: (i, 0)),
                  pl.BlockSpec((tq, LANE), lambda h, i: (i, 0)),
                  pl.BlockSpec((None, T, K_FULL), lambda h, i: (h, 0, 0)),
                  pl.BlockSpec((None, T, 2 * V_DIM), lambda h, i: (h, 0, 0))],
        out_specs=[pl.BlockSpec((tq, V_DIM), lambda h, i: (i, h)),
                   pl.BlockSpec((None, tq, 1), lambda h, i: (h, i, 0))],
        out_shape=[jax.ShapeDtypeStruct((T, H * V_DIM), F32), jax.ShapeDtypeStruct((H, T, 1), F32)],
        compiler_params=_params(("parallel", "parallel")),
    )(q_ext, cosp, sinp, k_full, v_aug)


def _attn_bwd(q_ext, k_full, v_aug, cosp, sinp, d_out, out, lse_rows):
    H, T, _ = k_full.shape
    tq, tkv = _pick(T, ATTN_TQ, LANE), _pick(T, ATTN_TKV, LANE)
    n_q = T // tq

    def body(q_ref, cos_ref, sin_ref, k_ref, v_ref, do_ref, o_ref, lse_ref, cos_all, sin_all,
             dq_ref, dkv_ref, dkr_ref, dk_acc, dv_acc):
        h, i = pl.program_id(0), pl.program_id(1)

        @pl.when(i == 0)
        def _():
            dk_acc[...] = jnp.zeros_like(dk_acc)
            dv_acc[...] = jnp.zeros_like(dv_acc)

        cos, sin = cos_ref[...], sin_ref[...]
        q, do, lse = _roped_q(q_ref[...], cos, sin), do_ref[...], lse_ref[...]
        delta = lax.dot_general(jnp.ones((SUBLANE, V_DIM), F32), do.astype(F32) * o_ref[...], DN["nt"],
                                preferred_element_type=F32, precision=lax.Precision.HIGHEST)[0:1]
        dq = jnp.zeros((tq, K_FULL), F32)
        for c in range(T // tkv):
            keys = pl.ds(c * tkv, tkv)
            kc = k_ref[keys, :]
            pt = jnp.exp2(lax.dot_general(kc, q, DN["nt"], preferred_element_type=F32) - lse)
            dpt = lax.dot_general(v_ref[keys, pl.ds(0, V_DIM)], do, DN["nt"], preferred_element_type=F32)
            dst = (pt * (dpt - delta)).astype(BF16)
            dv_acc[keys, :] += jnp.dot(pt.astype(BF16), do, preferred_element_type=F32)
            dk_acc[keys, :] += jnp.dot(dst, q, preferred_element_type=F32)
            dq = dq + lax.dot_general(dst, kc, DN["tn"], preferred_element_type=F32)
        r = dq[:, LANE:]
        dq_ref[...] = (jnp.concatenate([dq[:, :LANE], r * cos, r * sin], axis=1) * QK_SCALE).astype(BF16)

        @pl.when(i == n_q - 1)
        def _():
            dk = dk_acc[...] * (1.0 / LOG2E)
            dkv_ref[...] = jnp.concatenate([dk[:, :LANE], dv_acc[...]], axis=1).astype(BF16)
            rk = dk[:, LANE:]
            dkr = jnp.concatenate([rk * cos_all[...], rk * sin_all[...]], axis=1)

            @pl.when(h == 0)
            def _():
                dkr_ref[...] = dkr

            @pl.when(h != 0)
            def _():
                dkr_ref[...] += dkr

    return pl.pallas_call(
        body,
        name="attn_bwd",
        grid=(H, n_q),
        in_specs=[pl.BlockSpec((tq, Q_EXT), lambda h, i: (i, h)),
                  pl.BlockSpec((tq, LANE), lambda h, i: (i, 0)),
                  pl.BlockSpec((tq, LANE), lambda h, i: (i, 0)),
                  pl.BlockSpec((None, T, K_FULL), lambda h, i: (h, 0, 0)),
                  pl.BlockSpec((None, T, 2 * V_DIM), lambda h, i: (h, 0, 0)),
                  pl.BlockSpec((tq, V_DIM), lambda h, i: (i, h)),
                  pl.BlockSpec((tq, V_DIM), lambda h, i: (i, h)),
                  pl.BlockSpec((None, 1, tq), lambda h, i: (h, 0, i)),
                  pl.BlockSpec((T, LANE), lambda h, i: (0, 0)),
                  pl.BlockSpec((T, LANE), lambda h, i: (0, 0))],
        out_specs=[pl.BlockSpec((tq, Q_EXT), lambda h, i: (i, h)),
                   pl.BlockSpec((T, K_FULL), lambda h, i: (0, h)),
                   pl.BlockSpec((T, K_FULL), lambda h, i: (0, 0))],
        out_shape=[jax.ShapeDtypeStruct((T, H * Q_EXT), BF16), jax.ShapeDtypeStruct((T, H * K_FULL), BF16),
                   jax.ShapeDtypeStruct((T, K_FULL), F32)],
        scratch_shapes=[pltpu.VMEM((T, K_FULL), F32), pltpu.VMEM((T, V_DIM), F32)],
        compiler_params=_params(("arbitrary", "arbitrary")),
    )(q_ext, cosp, sinp, k_full, v_aug, d_out, out, lse_rows, cosp, sinp)


def _sgu_forward_math(pb, lng, lnb, ws, bsb, gw, gh):
    u_raw, v_raw = pb[:, :gw], pb[:, gw:]
    u, v = _gelu(u_raw), _gelu(v_raw)
    vc = v - jnp.mean(v, axis=-1, keepdims=True)
    rstd = lax.rsqrt(jnp.mean(vc * vc, axis=-1, keepdims=True) + EPS)
    vhat = vc * rstd
    vln = vhat * lng + lnb
    hd = gw // gh
    s = jnp.concatenate(
        [jnp.dot(ws[g], vln[:, g * hd:(g + 1) * hd].astype(BF16), preferred_element_type=F32) + bsb[g]
         for g in range(gh)], axis=1)
    return u_raw, v_raw, u, rstd, vhat, vln, s


def _sgu_fwd(proj_b, lng, lnb, ws, bsb, gg):
    T, gw2 = proj_b.shape
    gw, gh = gw2 // 2, ws.shape[0]

    def fn(pb, lng, lnb, ws, bsb, gg):
        _, _, u, _, _, _, s = _sgu_forward_math(pb, lng, lnb, ws, bsb, gw, gh)
        return _rms(u * s, gg)

    return _ew("sgu_fwd", fn, (T // CHUNK,),
               [_rows(proj_b, CHUNK), _vec(lng), _vec(lnb), _vec(ws), _vec(bsb), _vec(gg)],
               [_rows_out((T, gw), BF16, CHUNK)])[0]


def _sgu_bwd(proj_b, d_mixed, lng, lnb, ws, ws_t, bsb, gg):
    T, gw2 = proj_b.shape
    gw, gh = gw2 // 2, ws.shape[0]
    hd = gw // gh

    def fn(pb, dmg, lng, lnb, ws, ws_t, bsb, gg):
        u_raw, v_raw, u, rstd, vhat, vln, s = _sgu_forward_math(pb, lng, lnb, ws, bsb, gw, gh)
        dgo, dgg = _rms_bwd(u * s, gg, dmg)
        du, ds = dgo * s, dgo * u
        ones = jnp.ones((SUBLANE, hd), F32)
        dws, dbs, dvln = [], [], []
        for g in range(gh):
            ds_g = ds[:, g * hd:(g + 1) * hd]
            ds_gb = ds_g.astype(BF16)
            dws.append(lax.dot_general(ds_gb, vln[:, g * hd:(g + 1) * hd].astype(BF16), DN["nt"],
                                       preferred_element_type=F32)[None])
            dbs.append(lax.dot_general(ones, ds_g, DN["nt"], preferred_element_type=F32,
                                       precision=lax.Precision.HIGHEST)[None])
            dvln.append(jnp.dot(ws_t[g], ds_gb, preferred_element_type=F32))
        dvln = jnp.concatenate(dvln, axis=1)
        dlng = jnp.sum(dvln * vhat, axis=0, keepdims=True)
        dlnb = jnp.sum(dvln, axis=0, keepdims=True)
        dvh = dvln * lng
        dv = rstd * (dvh - jnp.mean(dvh, axis=-1, keepdims=True) - vhat * jnp.mean(dvh * vhat, axis=-1, keepdims=True))
        d_pb = jnp.concatenate([du * _gelu_grad(u_raw), dv * _gelu_grad(v_raw)], axis=1)
        return d_pb, dgg, jnp.concatenate(dws, axis=0), jnp.concatenate(dbs, axis=0), dlng, dlnb

    return _ew("sgu_bwd", fn, (T // CHUNK,),
               [_rows(proj_b, CHUNK), _rows(d_mixed, CHUNK, gw, 1), _vec(lng), _vec(lnb), _vec(ws), _vec(ws_t),
                _vec(bsb), _vec(gg)],
               [_rows_out((T, gw2), BF16, CHUNK)],
               [_vec_out((1, gw)), _vec_out((gh, CHUNK, CHUNK)), _vec_out((gh, SUBLANE, CHUNK)), _vec_out((1, gw)),
                _vec_out((1, gw))])


def _adamw(name, parts, w, m, v):
    R, C = w.shape
    pack = SUBLANE * (4 // jnp.dtype(parts.dtype).itemsize)
    if R % pack == 0:
        tm, tc = _pick(R, max(pack, ADAM_BLOCK_ELEMS // C), pack), C
    else:
        parts = parts[:, :R]
        tm, tc = R, _pick(C, max(LANE, ADAM_BLOCK_ELEMS // R // LANE * LANE), LANE)

    n_parts = parts.shape[0]

    def fn(p, w, m, v):
        g = p[0].astype(F32)
        for i in range(1, n_parts):
            g = g + p[i].astype(F32)
        m2 = ADAM_B1 * m + (1.0 - ADAM_B1) * g
        v2 = ADAM_B2 * v + (1.0 - ADAM_B2) * (g * g)
        m_hat = m2 / (1.0 - ADAM_B1 ** ADAM_STEP)
        v_hat = v2 / (1.0 - ADAM_B2 ** ADAM_STEP)
        delta = -ADAM_LR * (m_hat / (jnp.sqrt(v_hat) + ADAM_EPS) + ADAM_WD * w)
        return g, delta, m2, v2

    blk = lambda a: (a, (tm, tc), lambda i, j: (i, j))
    return _ew(name, fn, (R // tm, C // tc),
               [(parts, (n_parts, tm, tc), lambda i, j: (0, i, j)), blk(w), blk(m), blk(v)],
               [((R, C), F32, (tm, tc), lambda i, j: (i, j))] * 4)


def kernel(x, positions, pre_mix_norm, w_in, q_norm, kv_norm, w_uq, w_ukv, v_ln_gain, v_ln_bias, w_spatial, b_spatial, attn_out_norm, gmlp_out_norm, w_out, post_mix_norm, pre_ffn_norm, w_gate, w_up, w_down, post_ffn_norm, loss_target, m_pre_mix_norm, m_w_in, m_q_norm, m_kv_norm, m_w_uq, m_w_ukv, m_v_ln_gain, m_v_ln_bias, m_w_spatial, m_b_spatial, m_attn_out_norm, m_gmlp_out_norm, m_w_out, m_post_mix_norm, m_pre_ffn_norm, m_w_gate, m_w_up, m_w_down, m_post_ffn_norm, v_pre_mix_norm, v_w_in, v_q_norm, v_kv_norm, v_w_uq, v_w_ukv, v_v_ln_gain, v_v_ln_bias, v_w_spatial, v_b_spatial, v_attn_out_norm, v_gmlp_out_norm, v_w_out, v_post_mix_norm, v_pre_ffn_norm, v_w_gate, v_w_up, v_w_down, v_post_ffn_norm):
    given = dict(locals())
    T, D = x.shape[1], x.shape[2]
    QL, KVL = w_uq.shape[1], w_ukv.shape[1]
    GW, GH = v_ln_gain.shape[1], w_spatial.shape[1]
    IWS = w_in.shape[2]
    IW = IWS * N_DEV
    H = w_ukv.shape[2] * N_DEV // (NOPE_DIM + V_DIM)
    AW = H * V_DIM
    FS = w_gate.shape[2]
    FP = _round_up(FS, LANE)
    C0 = QL + KVL
    assert IW == C0 + ROPE_DIM + 2 * GW and AW + GW == D and AW == GW
    assert QL % KVL == 0 and C0 % K_FULL == 0 and w_spatial.shape[2] == CHUNK and T % CHUNK == 0
    assert w_uq.shape[2] * N_DEV == H * QK_HEAD and D % N_DEV == 0

    x2, tgt = x[0], loss_target[0]
    tm_d = _row_tile(T, D)

    inv_freq = 1.0 / (ROPE_THETA ** (jnp.arange(0, ROPE_DIM, 2, dtype=F32) / ROPE_DIM))
    ang = positions[0].astype(F32)[:, None] * inv_freq
    zpad = jnp.zeros((T, LANE - ROPE_DIM), F32)
    cosp = jnp.concatenate([jnp.cos(ang), jnp.cos(ang), zpad], axis=1)
    sinp = jnp.concatenate([jnp.sin(ang), jnp.sin(ang), zpad], axis=1)

    def shard_t(a):
        return a[0].T

    st_a, tok = _ag_start("ag_a", [shard_t(w_in).astype(BF16), w_uq[0].astype(BF16), w_ukv[0].astype(BF16)], [])
    st_o, tok = _ag_start("ag_o", [w_out[0].astype(BF16)], [tok])
    st_g, tok = _ag_start("ag_g", [_cast_pad_rows("pad_w_gate", shard_t(w_gate), FP, [tok])], [tok])
    st_u, tok = _ag_start("ag_u", [_cast_pad_rows("pad_w_up", shard_t(w_up), FP, [tok])], [tok])
    st_d, tok = _ag_start("ag_d", [_cast_pad_rows("pad_w_down", w_down[0], FP, [tok])], [tok])

    xn = _ew("rms_x", lambda x, g: _rms(x, g), (T // tm_d,), [_rows(x2, tm_d), _vec(pre_mix_norm)],
             [_rows_out((T, D), BF16, tm_d)], deps=[tok])[0]
    st_a, tok = _ag_mid("ag_a", st_a, xn)
    g_in, g_uq, g_ukv = _ag_finish("ag_a", st_a, tok)

    w_in_t = g_in.reshape(IW, D)
    kr_w = w_in_t[C0:C0 + ROPE_DIM]
    z64 = jnp.zeros((LANE - ROPE_DIM, D), BF16)
    w_a_t = jnp.concatenate([w_in_t[:C0], kr_w, z64, _rot(kr_w, 0), z64], axis=0)
    w_b_t = w_in_t[C0 + ROPE_DIM:]
    wq = g_uq.transpose(1, 0, 2).reshape(QL, H, QK_HEAD)
    zq = jnp.zeros((QL, H, LANE - ROPE_DIM), BF16)
    wq_ext = jnp.concatenate([wq[..., :NOPE_DIM], wq[..., NOPE_DIM:], zq, _rot(wq[..., NOPE_DIM:]), zq],
                             axis=-1).reshape(QL, H * Q_EXT)
    wkv = g_ukv.transpose(1, 0, 2).reshape(KVL, H * (NOPE_DIM + V_DIM))

    ws = w_spatial[0].astype(BF16)
    ws_t = jnp.swapaxes(ws, 1, 2)
    bsb = jnp.broadcast_to(b_spatial[0][:, :, None], (GH, CHUNK, GW // GH))

    proj_a = _mm2("mm_proj_a", "nt", xn, w_a_t, F32, tn_pref=w_a_t.shape[0])
    proj_b = _mm2("mm_proj_b", "nt", xn, w_b_t, F32)

    tm_q = _row_tile(T, QL)
    qn = _ew("rms_q", lambda x, g: _rms(x, g), (T // tm_q,), [_rows(proj_a, tm_q, QL, 0), _vec(q_norm)],
             [_rows_out((T, QL), BF16, tm_q)])[0]
    kvn = _ew("rms_kv", lambda x, g: _rms(x, g), (T // tm_q,), [_rows(proj_a, tm_q, KVL, QL // KVL), _vec(kv_norm)],
              [_rows_out((T, KVL), BF16, tm_q)])[0]
    q_ext = _mm2("mm_q", "nn", qn, wq_ext, F32)
    kv = _mm2("mm_kv", "nn", kvn, wkv, F32)

    tm_h = _pick(T, 512, SUBLANE)
    hb = _pick(H, 4, 1)

    def make_kv(kv, kr, c, s):
        k_rope = kr[:, :LANE] * c + kr[:, LANE:] * s
        ones = jnp.ones((kv.shape[0], V_DIM), F32)
        ks = [jnp.concatenate([kv[:, j * K_FULL:j * K_FULL + LANE], k_rope], axis=1) for j in range(hb)]
        vs = [jnp.concatenate([kv[:, j * K_FULL + LANE:(j + 1) * K_FULL], ones], axis=1) for j in range(hb)]
        return jnp.stack(ks), jnp.stack(vs)

    k_full, v_aug = _ew("make_kv", make_kv, (T // tm_h, H // hb),
                        [(kv, (tm_h, hb * K_FULL), lambda i, h: (i, h)),
                         (proj_a, (tm_h, K_FULL), lambda i, h: (i, C0 // K_FULL)),
                         (cosp, (tm_h, LANE), lambda i, h: (i, 0)), (sinp, (tm_h, LANE), lambda i, h: (i, 0))],
                        [((H, T, K_FULL), BF16, (hb, tm_h, K_FULL), lambda i, h: (h, i, 0)),
                         ((H, T, 2 * V_DIM), BF16, (hb, tm_h, 2 * V_DIM), lambda i, h: (h, i, 0))])

    a_out, lse = _attn_fwd(q_ext, k_full, v_aug, cosp, sinp)
    st_o, tok_o = _ag_mid("ag_o", st_o, a_out)
    tm_a = _row_tile(T, AW)
    mixed_a = _ew("rms_attn", lambda x, g: _rms(x, g), (T // tm_a,), [_rows(a_out, tm_a), _vec(attn_out_norm)],
                  [_rows_out((T, AW), BF16, tm_a)], deps=[tok_o])[0]
    mixed_g = _sgu_fwd(proj_b, v_ln_gain, v_ln_bias, ws, bsb, gmlp_out_norm)
    wo = _ag_finish("ag_o", st_o, mixed_g)[0].reshape(D, D)
    st_g, tok_gu = _ag_mid("ag_g", st_g, wo)

    tm_, tn_ = _pick(T, 1024, LANE), _pick(D, 1024, LANE)
    tk_ = _fit_tk(AW, tm_, tn_, pairs=2)
    nka = AW // tk_
    mix_out = _mm("mm_out", "nn",
                  [(mixed_a, (tm_, tk_), lambda i, j, k: (i, k), wo, (tk_, tn_), lambda i, j, k: (k, j)),
                   (mixed_g, (tm_, tk_), lambda i, j, k: (i, k), wo, (tk_, tn_), lambda i, j, k: (k + nka, j))],
                  (T // tm_, D // tn_, nka), (tm_, tn_), (T, D), F32, (tm_, tn_), lambda i, j, k: (i, j), deps=[tok_gu])

    def post_mix(x, mo, g3, g4):
        h = x + _rms(mo, g3)
        return h, _rms(h, g4)

    h_res, hn = _ew("post_mix", post_mix, (T // tm_d,),
                    [_rows(x2, tm_d), _rows(mix_out, tm_d), _vec(post_mix_norm), _vec(pre_ffn_norm)],
                    [_rows_out((T, D), F32, tm_d), _rows_out((T, D), BF16, tm_d)])

    F = N_DEV * FP
    tn_f = _pick(F, FFN_TILE_MAX, MXU_DIM)
    tn_e = _pick(F, FFN_TILE_MAX // 2, LANE)

    def swiglu(u, g):
        g = g.astype(F32)
        return u, g * _sigmoid(g) * u

    st_u, tok_u = _ag_mid("ag_u", st_u, hn)
    wg_t = _ag_finish("ag_g", st_g, tok_u)[0].reshape(F, D)
    gate = _mm2("mm_gate", "nt", hn, wg_t, BF16, tn_pref=tn_f)
    wu_t = _ag_finish("ag_u", st_u, gate)[0].reshape(F, D)
    st_d, tok_d = _ag_mid("ag_d", st_d, wu_t)
    up, act = _mm2("mm_up", "nt", hn, wu_t, BF16, tn_pref=tn_e, deps=[tok_d], extras=[gate],
                   epilogue=swiglu, n_out=2)
    wd_f = _ag_finish("ag_d", st_d, act)[0].reshape(F, D)
    ffn = _mm2("mm_down", "nn", act, wd_f, F32, tk_max=tn_f)

    def loss_head(h, f, t, g5):
        y = h + _rms(f, g5)
        diff = y - t
        dy = diff / D
        d_f, dg5 = _rms_bwd(f, g5, dy)
        lsum = jnp.sum(jnp.sum(diff * diff, axis=0, keepdims=True), axis=1, keepdims=True) * (0.5 / D)
        return dy, d_f, dg5, lsum

    dy, d_ffn, dg_post_ffn, loss_part = _ew(
        "loss_head", loss_head, (T // tm_d,),
        [_rows(h_res, tm_d), _rows(ffn, tm_d), _rows(tgt, tm_d), _vec(post_ffn_norm)],
        [_rows_out((T, D), F32, tm_d), _rows_out((T, D), BF16, tm_d)], [_vec_out((1, D)), _vec_out((1, 1))])

    def swiglu_bwd(da, g, u):
        g, u = g.astype(F32), u.astype(F32)
        sg = _sigmoid(g)
        return da * u * sg * (1.0 + g * (1.0 - sg)), da * g * sg

    def ffn_dw(name, z, y, deps):
        return _mm2(name, "tn", z, y, BF16, tm_pref=tn_f, deps=deps).reshape(N_DEV, FP, D)

    x_d, tok = _a2a_start("a2a_d", [ffn_dw("mm_dwd", act, d_ffn, [])], [])
    d_gate, d_up = _mm2("mm_dact", "nt", d_ffn, wd_f, BF16, tn_pref=tn_e, deps=[tok], extras=[gate, up],
                        epilogue=swiglu_bwd, n_out=2)
    x_g, tok = _a2a_start("a2a_g", [ffn_dw("mm_dwg", d_gate, hn, [])], [])
    x_u, tok = _a2a_start("a2a_u", [ffn_dw("mm_dwu", d_up, hn, [tok])], [])
    d_hn_g = _mm2("mm_dhn_g", "nn", d_gate, wg_t, F32, tk_max=tn_f, deps=[tok])
    d_hn = _mm2("mm_dhn_u", "nn", d_up, wu_t, F32, tk_max=tn_f, extras=[d_hn_g], epilogue=lambda s, g: (s + g,))

    def pre_ffn_bwd(h, dhn, dy, mo, g4, g3):
        d1, dg4 = _rms_bwd(h, g4, dhn)
        dh = dy + d1
        dmo, dg3 = _rms_bwd(mo, g3, dh)
        return dh, dmo, dg4, dg3

    dh, d_mo, dg_pre_ffn, dg_post_mix = _ew(
        "pre_ffn_bwd", pre_ffn_bwd, (T // tm_d,),
        [_rows(h_res, tm_d), _rows(d_hn, tm_d), _rows(dy, tm_d), _rows(mix_out, tm_d), _vec(pre_ffn_norm),
         _vec(post_mix_norm)],
        [_rows_out((T, D), F32, tm_d), _rows_out((T, D), BF16, tm_d)], [_vec_out((1, D)), _vec_out((1, D))])

    d_mixed = _mm2("mm_dmixed", "nt", d_mo, wo, F32)
    dwo = jnp.concatenate([_mm2("mm_dwo_a", "tn", mixed_a, d_mo, BF16), _mm2("mm_dwo_g", "tn", mixed_g, d_mo, BF16)],
                          axis=0)
    x_o, tok = _a2a_start("a2a_o", [dwo.reshape(N_DEV, D // N_DEV, D)], [])

    d_a_out, dg_attn = _ew("rms_attn_bwd", lambda a, g, d: _rms_bwd(a, g, d), (T // tm_a,),
                           [_rows(a_out, tm_a), _vec(attn_out_norm), _rows(d_mixed, tm_a, AW, 0)],
                           [_rows_out((T, AW), BF16, tm_a)], [_vec_out((1, AW))], deps=[tok])
    d_pb, dg_gmlp, dws, dbs, dlng, dlnb = _sgu_bwd(proj_b, d_mixed, v_ln_gain, v_ln_bias, ws, ws_t, bsb, gmlp_out_norm)

    dq_ext, d_kv, d_kr = _attn_bwd(q_ext, k_full, v_aug, cosp, sinp, d_a_out, a_out, lse.reshape(H, 1, T))

    d_qn = _mm2("mm_dqn", "nt", dq_ext, wq_ext, F32)
    dwq_ext = _mm2("mm_dwq", "tn", qn, dq_ext, F32)
    d_kvn = _mm2("mm_dkvn", "nt", d_kv, wkv, F32)
    dwkv = _mm2("mm_dwkv", "tn", kvn, d_kv, BF16)

    d_qc, dg_q = _ew("rms_q_bwd", lambda a, g, d: _rms_bwd(a, g, d), (T // tm_q,),
                     [_rows(proj_a, tm_q, QL, 0), _vec(q_norm), _rows(d_qn, tm_q)],
                     [_rows_out((T, QL), BF16, tm_q)], [_vec_out((1, QL))])
    d_kvc, dg_kv = _ew("rms_kv_bwd", lambda a, g, d: _rms_bwd(a, g, d), (T // tm_q,),
                       [_rows(proj_a, tm_q, KVL, QL // KVL), _vec(kv_norm), _rows(d_kvn, tm_q)],
                       [_rows_out((T, KVL), BF16, tm_q)], [_vec_out((1, KVL))])
    d_pa = jnp.concatenate([d_qc, d_kvc, d_kr.astype(BF16)], axis=1)

    dwa_t = _mm2("mm_dwa", "tn", d_pa, xn, F32, tm_pref=w_a_t.shape[0])
    dwb_t = _mm2("mm_dwb", "tn", d_pb, xn, BF16)
    dkr_w = dwa_t[C0:C0 + ROPE_DIM] + _rot_t(dwa_t[C0 + LANE:C0 + LANE + ROPE_DIM], 0)
    dw_in = jnp.concatenate([dwa_t[:C0].astype(BF16), dkr_w.astype(BF16), dwb_t], axis=0)
    dw_in = dw_in.reshape(N_DEV, IWS, D)
    dq3 = dwq_ext.reshape(QL, H, Q_EXT)
    dwq = jnp.concatenate([dq3[..., :NOPE_DIM],
                           dq3[..., LANE:LANE + ROPE_DIM] + _rot_t(dq3[..., 2 * LANE:2 * LANE + ROPE_DIM])], axis=-1)
    dwq = dwq.astype(BF16).reshape(QL, N_DEV, H * QK_HEAD // N_DEV).transpose(1, 0, 2)
    dwkv = dwkv.reshape(KVL, N_DEV, w_ukv.shape[2]).transpose(1, 0, 2)
    p_e, tok = _rs_pair_start("rs_e", [dw_in, dwq, dwkv], [])
    d_xn_a = _mm2("mm_dxn_a", "nn", d_pa, w_a_t, F32, deps=[tok])
    x_e, tok = _rs_chip_start("rs_e2", _rs_pair_sum("rs_e", p_e, d_xn_a), [])
    d_xn_b = _mm2("mm_dxn_b", "nn", d_pb, w_b_t, F32, deps=[tok])

    def x_bwd(x, da, db, dh, g0):
        dx, dg0 = _rms_bwd(x, g0, da + db)
        return dh + dx, dg0

    grad_x, dg_pre_mix = _ew("x_bwd", x_bwd, (T // tm_d,),
                             [_rows(x2, tm_d), _rows(d_xn_a, tm_d), _rows(d_xn_b, tm_d), _rows(dh, tm_d),
                              _vec(pre_mix_norm)],
                             [_rows_out((T, D), F32, tm_d)], [_vec_out((1, D))])

    def adam_big(n, parts, transposed=False):
        view = shard_t if transposed else (lambda a: a[0])
        res = _adamw("adamw_" + n, parts, view(given[n]), view(given["m_" + n]), view(given["v_" + n]))
        done[n] = res[0]
        return [(r.T if transposed else r)[None] for r in res]

    done = {}

    small_names = ["pre_mix_norm", "q_norm", "kv_norm", "v_ln_gain", "v_ln_bias", "w_spatial", "b_spatial",
                   "attn_out_norm", "gmlp_out_norm", "post_mix_norm", "pre_ffn_norm", "post_ffn_norm"]
    small_grads = [dg_pre_mix, dg_q, dg_kv, dlng, dlnb, dws, dbs[:, 0, :], dg_attn, dg_gmlp, dg_post_mix, dg_pre_ffn,
                   dg_post_ffn]
    n_small = sum(g.size for g in small_grads)
    pad_small = _round_up(n_small, SUBLANE * LANE) - n_small

    def pack(arrs):
        flat = jnp.concatenate([a.reshape(-1) for a in arrs] + [jnp.zeros((pad_small,), F32)])
        return flat.reshape(-1, LANE)

    st_s, tok = _ag_start("ag_s", [pack(small_grads)], [])

    results = {}
    results["w_down"] = adam_big("w_down", _a2a_finish("a2a_d", x_d, tok)[0])
    results["w_gate"] = adam_big("w_gate", _a2a_finish("a2a_g", x_g, done["w_down"])[0], True)
    st_s, tok = _ag_mid("ag_s", st_s, done["w_gate"])
    results["w_up"] = adam_big("w_up", _a2a_finish("a2a_u", x_u, tok)[0], True)
    results["w_out"] = adam_big("w_out", _a2a_finish("a2a_o", x_o, done["w_up"])[0])
    e_in, e_uq, e_ukv = _rs_chip_finish("rs_e2", x_e, done["w_out"])
    results["w_in"] = adam_big("w_in", e_in, True)
    results["w_uq"] = adam_big("w_uq", e_uq)
    results["w_ukv"] = adam_big("w_ukv", e_ukv)

    small_parts = _ag_finish("ag_s", st_s, done["w_ukv"])[0]
    sm = _adamw("adamw_small", small_parts, pack([given[n] for n in small_names]),
                pack([given["m_" + n] for n in small_names]), pack([given["v_" + n] for n in small_names]))

    off = 0
    for n in small_names:
        shp, size = given[n].shape, given[n].size
        results[n] = [r.reshape(-1)[off:off + size].reshape(shp) for r in sm]
        off += size

    loss = lax.psum(loss_part[0, 0], MESH_AXES)
    order = ["pre_mix_norm", "w_in", "q_norm", "kv_norm", "w_uq", "w_ukv", "v_ln_gain", "v_ln_bias", "w_spatial",
             "b_spatial", "attn_out_norm", "gmlp_out_norm", "w_out", "post_mix_norm", "pre_ffn_norm", "w_gate", "w_up",
             "w_down", "post_ffn_norm"]
    out = [loss, grad_x[None]]
    for kind in range(4):
        out += [results[n][kind] for n in order]
    return tuple(out)
```

```python
import math

import jax
import jax.numpy as jnp
from jax import lax
from jax.experimental import pallas as pl
from jax.experimental.pallas import tpu as pltpu

F32 = jnp.float32
BF16 = jnp.bfloat16
N_DEV = 8
MESH_AXES = ("x", "y", "c")
MESH = pl.DeviceIdType.MESH

NOPE_DIM = 128
ROPE_DIM = 64
V_DIM = 128
ROPE_THETA = 10000.0
CHUNK = 128
EPS = 1e-6
QK_HEAD = NOPE_DIM + ROPE_DIM
Q_EXT = 3 * 128
K_FULL = 2 * 128

ADAM_LR = 0.001
ADAM_B1 = 0.9
ADAM_B2 = 0.999
ADAM_EPS = 1e-08
ADAM_WD = 0.01
ADAM_STEP = 10

LANE = 128
SUBLANE = 8
VMEM_LIMIT = 60 * 1024 * 1024
MM_VMEM_BUDGET = 50 * 1024 * 1024
MXU_DIM = 256
FFN_TILE_MAX = 3072
MM_TK_MAX = 4096
EPILOGUE_ROWS = 256
ATTN_TQ = 1024
ATTN_TKV = 512
NEG_BIG = -1e30
ROW_BLOCK_ELEMS = 512 * 1024
ADAM_BLOCK_ELEMS = 256 * 1024

DN = {
    "nn": (((1,), (0,)), ((), ())),
    "nt": (((1,), (1,)), ((), ())),
    "tn": (((0,), (0,)), ((), ())),
}


def _pick(dim, pref, mult):
    t = min(pref, dim)
    t -= t % mult
    while t >= mult:
        if dim % t == 0:
            return t
        t -= mult
    return dim


def _round_up(n, m):
    return (n + m - 1) // m * m


def _params(sem):
    return pltpu.CompilerParams(dimension_semantics=sem, vmem_limit_bytes=VMEM_LIMIT)


def _dep_specs(deps, nd):
    return [pl.BlockSpec(d.shape, lambda *_, r=d.ndim: (0,) * r) for d in deps]


def _mm(name, mode, pairs, grid, acc_shape, out_shape, out_dtype, out_block, out_map, deps=(), extras=(),
        epilogue=None, n_out=1):
    n, ne, nx = len(pairs), len(extras), len(deps)
    nk = grid[-1]
    nd = len(grid)
    dn = DN[mode]
    use_acc = nk > 1 or epilogue is not None

    def body(*refs):
        e_refs = refs[2 * n:2 * n + ne]
        o_refs = refs[2 * n + ne + nx:2 * n + ne + nx + n_out]
        k = pl.program_id(nd - 1)

        def dot(i):
            return lax.dot_general(refs[2 * i][...], refs[2 * i + 1][...], dn, preferred_element_type=F32)

        if not use_acc:
            s = dot(0)
            for i in range(1, n):
                s = s + dot(i)
            o_refs[0][...] = s.astype(o_refs[0].dtype)
            return
        acc = refs[2 * n + ne + nx + n_out]

        @pl.when(k == 0)
        def _():
            acc[...] = dot(0)

        if nk > 1:
            @pl.when(k > 0)
            def _():
                acc[...] += dot(0)

        for i in range(1, n):
            acc[...] += dot(i)

        if epilogue is None:
            @pl.when(k == nk - 1)
            def _():
                o_refs[0][...] = acc[...].astype(o_refs[0].dtype)
        else:
            @pl.when(k == nk - 1)
            def _():
                rows = acc_shape[0]
                slab = _pick(rows, EPILOGUE_ROWS, SUBLANE)
                for r0 in range(0, rows, slab):
                    sl = pl.ds(r0, slab)
                    vals = epilogue(acc[sl, :], *[e[sl, :] for e in e_refs])
                    for r, v in zip(o_refs, vals):
                        r[sl, :] = v.astype(r.dtype)

    in_specs, args = [], []
    for a, a_blk, a_map, b, b_blk, b_map in pairs:
        in_specs += [pl.BlockSpec(a_blk, a_map), pl.BlockSpec(b_blk, b_map)]
        args += [a, b]
    in_specs += [pl.BlockSpec(blk, imap) for (_, blk, imap) in extras]
    args += [e for (e, _, _) in extras]
    if not isinstance(out_dtype, (tuple, list)):
        out_dtype = (out_dtype,) * n_out
    res = pl.pallas_call(
        body,
        name=name,
        grid=grid,
        in_specs=in_specs + _dep_specs(deps, nd),
        out_specs=[pl.BlockSpec(out_block, out_map)] * n_out,
        out_shape=[jax.ShapeDtypeStruct(out_shape, dt) for dt in out_dtype],
        scratch_shapes=[pltpu.VMEM(acc_shape, F32)] if use_acc else [],
        compiler_params=_params(("parallel",) * (nd - 1) + ("arbitrary",)),
    )(*args, *deps)
    return res[0] if n_out == 1 else res


def _fit_tk(K, tm, tn, pairs=1, out_bytes=4, fixed_extra=0, tk_max=MM_TK_MAX, needs_acc=False):
    fixed = tm * tn * 4 + 2 * tm * tn * out_bytes + fixed_extra
    if K <= tk_max and not needs_acc and fixed - tm * tn * 4 + pairs * 4 * (tm + tn) * K <= MM_VMEM_BUDGET:
        return K
    tk = _pick(K, tk_max, LANE)
    while tk > LANE and fixed + pairs * 4 * (tm + tn) * tk > MM_VMEM_BUDGET:
        smaller = _pick(K, tk - LANE, LANE)
        if smaller >= tk:
            break
        tk = smaller
    return tk


def _mm2(name, mode, a, b, out_dtype, tm_pref=1024, tn_pref=1024, deps=(), extras=(), epilogue=None, n_out=1,
         tk_max=MM_TK_MAX):
    if mode == "nn":
        (M, K), N = a.shape, b.shape[1]
    elif mode == "nt":
        (M, K), N = a.shape, b.shape[0]
    else:
        (K, M), N = a.shape, b.shape[1]
    tm, tn = _pick(M, tm_pref, LANE), _pick(N, tn_pref, LANE)
    extra_bytes = sum(2 * tm * tn * jnp.dtype(e.dtype).itemsize for e in extras)
    tk = _fit_tk(K, tm, tn, out_bytes=n_out * jnp.dtype(out_dtype).itemsize, fixed_extra=extra_bytes, tk_max=tk_max,
                 needs_acc=epilogue is not None)
    if mode == "tn":
        a_blk, a_map = (tk, tm), lambda i, j, k: (k, i)
    else:
        a_blk, a_map = (tm, tk), lambda i, j, k: (i, k)
    if mode == "nt":
        b_blk, b_map = (tn, tk), lambda i, j, k: (j, k)
    else:
        b_blk, b_map = (tk, tn), lambda i, j, k: (k, j)
    return _mm(name, mode, [(a, a_blk, a_map, b, b_blk, b_map)], (M // tm, N // tn, K // tk), (tm, tn),
               (M, N), out_dtype, (tm, tn), lambda i, j, k: (i, j), deps=deps,
               extras=[(e, (tm, tn), lambda i, j, k: (i, j)) for e in extras], epilogue=epilogue, n_out=n_out)


def _ew(name, fn, grid, ins, outs, accs=(), deps=()):
    ni, no, na = len(ins), len(outs), len(accs)
    nd = len(grid)
    nx = len(deps)

    def body(*refs):
        vals = fn(*[r[...] for r in refs[:ni]])
        if not isinstance(vals, (tuple, list)):
            vals = (vals,)
        for r, v in zip(refs[ni + nx:ni + nx + no], vals[:no]):
            r[...] = v.astype(r.dtype)
        if na:
            first = pl.program_id(nd - 1) == 0
            for r, v in zip(refs[ni + nx + no:], vals[no:]):
                @pl.when(first)
                def _(r=r, v=v):
                    r[...] = v.astype(r.dtype)

                @pl.when(jnp.logical_not(first))
                def _(r=r, v=v):
                    r[...] += v.astype(r.dtype)

    all_outs = list(outs) + list(accs)
    sem = ("parallel",) * (nd - 1) + (("arbitrary",) if na else ("parallel",))
    res = pl.pallas_call(
        body,
        name=name,
        grid=grid,
        in_specs=[pl.BlockSpec(blk, imap) for (_, blk, imap) in ins] + _dep_specs(deps, nd),
        out_specs=[pl.BlockSpec(blk, imap) for (_, _, blk, imap) in all_outs],
        out_shape=[jax.ShapeDtypeStruct(s, d) for (s, d, _, _) in all_outs],
        compiler_params=_params(sem),
    )(*[a for (a, _, _) in ins], *deps)
    return res


def _cast_pad_rows(name, a, rows_out, deps):
    R, C = a.shape
    tr = _pick(math.gcd(R, rows_out), max(2 * SUBLANE, ROW_BLOCK_ELEMS // C), SUBLANE)
    n_in = R // tr

    def body(a_ref, *rest):
        o_ref = rest[len(deps)]
        i = pl.program_id(0)

        @pl.when(i < n_in)
        def _():
            o_ref[...] = a_ref[...].astype(BF16)

        @pl.when(i >= n_in)
        def _():
            o_ref[...] = jnp.zeros_like(o_ref)

    return pl.pallas_call(
        body,
        name=name,
        grid=(rows_out // tr,),
        in_specs=[pl.BlockSpec((tr, C), lambda i: (jnp.minimum(i, n_in - 1), 0))] + _dep_specs(deps, 1),
        out_specs=pl.BlockSpec((tr, C), lambda i: (i, 0)),
        out_shape=jax.ShapeDtypeStruct((rows_out, C), BF16),
        compiler_params=_params(("parallel",)),
    )(a, *deps)


def _rows(a, tm, width=None, cblk=0):
    width = a.shape[1] if width is None else width
    return (a, (tm, width), lambda i, c=cblk: (i, c))


def _vec(p):
    return (p, p.shape, lambda i: (0,) * p.ndim)


def _rows_out(shape, dtype, tm):
    return (shape, dtype, (tm, shape[1]), lambda i: (i, 0))


def _vec_out(shape):
    return (shape, F32, shape, lambda i: (0,) * len(shape))


def _row_tile(T, D):
    return _pick(T, max(SUBLANE, ROW_BLOCK_ELEMS // D), SUBLANE)


def _rms(x, g):
    r = lax.rsqrt(jnp.mean(x * x, axis=-1, keepdims=True) + EPS)
    return x * r * g


def _rms_bwd(x, g, dy):
    r = lax.rsqrt(jnp.mean(x * x, axis=-1, keepdims=True) + EPS)
    xh = x * r
    dg = jnp.sum(dy * xh, axis=0, keepdims=True)
    dxh = dy * g
    dx = r * (dxh - xh * jnp.mean(dxh * xh, axis=-1, keepdims=True))
    return dx, dg


GELU_C = math.sqrt(2.0 / math.pi)
GELU_A = 0.044715


def _sigmoid(x):
    return 0.5 * jnp.tanh(0.5 * x) + 0.5


def _gelu_tanh(x):
    return jnp.tanh(GELU_C * (x + GELU_A * x * x * x))


def _gelu(x, t):
    return 0.5 * x * (1.0 + t)


def _gelu_grad(x, t):
    return 0.5 * (1.0 + t) + 0.5 * x * (1.0 - t * t) * GELU_C * (1.0 + 3.0 * GELU_A * x * x)


def _rot(w, axis=-1):
    w1, w2 = jnp.split(w, 2, axis=axis)
    return jnp.concatenate([-w2, w1], axis=axis)


def _rot_t(dw, axis=-1):
    d1, d2 = jnp.split(dw, 2, axis=axis)
    return jnp.concatenate([d2, -d1], axis=axis)


def _coords():
    return lax.axis_index("x"), lax.axis_index("y"), lax.axis_index("c")


HBM_SPEC = pl.BlockSpec(memory_space=pltpu.HBM)
SEM_SPEC = pl.BlockSpec(memory_space=pltpu.SEMAPHORE)
ANY_SPEC = pl.BlockSpec(memory_space=pl.ANY)
EFFECT = pltpu.SideEffectType.DATAFLOW_SIDE_EFFECTING
TOKEN = (SUBLANE, LANE)


def _my_slot():
    xi, yi, ci = _coords()
    return 4 * xi + 2 * yi + ci


def _own_slot_filled(block):
    land = lax.empty((N_DEV,) + block.shape, block.dtype)
    return lax.dynamic_update_slice(land, block[None], (_my_slot(),) + (0,) * block.ndim)


def _split_start(name, n, n_copies, issue, srcs, lands, deps):
    def body(*refs):
        src_refs, land_refs = refs[:n], refs[n:2 * n]
        outs = refs[2 * n + len(deps):]
        send, recv, token = outs[:n], outs[n:2 * n], outs[4 * n]
        for t in range(n):
            issue(t, src_refs[t], land_refs[t], send[t], recv[t])
        token[...] = jnp.zeros_like(token)

    sems = [pltpu.SemaphoreType.DMA((n_copies,))] * (2 * n)
    res = pl.pallas_call(
        body,
        name=name,
        in_specs=[HBM_SPEC] * (2 * n) + [ANY_SPEC] * len(deps),
        out_specs=[SEM_SPEC] * (2 * n) + [HBM_SPEC] * (2 * n) + [pl.BlockSpec(memory_space=pltpu.VMEM)],
        out_shape=sems + [pltpu.HBM(a.shape, a.dtype) for a in list(srcs) + list(lands)]
        + [jax.ShapeDtypeStruct(TOKEN, F32)],
        input_output_aliases={i: 2 * n + i for i in range(2 * n)},
        compiler_params=pltpu.CompilerParams(has_side_effects=EFFECT),
    )(*[pltpu.with_memory_space_constraint(a, pltpu.HBM) for a in list(srcs) + list(lands)], *deps)
    state = dict(n=n, n_copies=n_copies, send=res[:n], recv=res[n:2 * n], srcs=res[2 * n:3 * n],
                 lands=res[3 * n:4 * n])
    return state, res[4 * n]


def _split_wait(name, state, unit_src, after):
    n, n_copies = state["n"], state["n_copies"]

    def body(*refs):
        src_refs, land_refs = refs[:n], refs[n:2 * n]
        send, recv = refs[2 * n:3 * n], refs[3 * n:4 * n]
        me = _coords()
        for t in range(n):
            for k in range(n_copies):
                cp = pltpu.make_async_remote_copy(
                    src_ref=unit_src(src_refs[t]), dst_ref=land_refs[t].at[0], send_sem=send[t].at[k],
                    recv_sem=recv[t].at[k], device_id=me, device_id_type=MESH)
                cp.wait_send()
                cp.wait_recv()

    bufs = list(state["srcs"]) + list(state["lands"])
    res = pl.pallas_call(
        body,
        name=name,
        in_specs=[HBM_SPEC] * (2 * n) + [SEM_SPEC] * (2 * n) + [ANY_SPEC],
        out_specs=[HBM_SPEC] * (2 * n),
        out_shape=[pltpu.HBM(a.shape, a.dtype) for a in bufs],
        input_output_aliases={i: i for i in range(2 * n)},
        compiler_params=pltpu.CompilerParams(has_side_effects=EFFECT),
    )(*bufs, *state["send"], *state["recv"], after)
    return list(res[n:])


def _ag_start(name, shards, deps):
    def issue(t, x_ref, land_ref, send, recv):
        xi, yi, ci = _coords()
        to = [(xi, yi, 1 - ci), (1 - xi, yi, ci), (xi, 1 - yi, ci), (1 - xi, 1 - yi, ci)]
        for k, dev in enumerate(to):
            pltpu.make_async_remote_copy(src_ref=x_ref, dst_ref=land_ref.at[4 * xi + 2 * yi + ci], send_sem=send.at[k],
                                         recv_sem=recv.at[k], device_id=dev, device_id_type=MESH).start()

    return _split_start(name, len(shards), 4, issue, shards, [_own_slot_filled(s) for s in shards], deps)


def _ag_mid(name, state, after):
    lands = _split_wait(name + "_w1", state, lambda r: r, after)

    def issue(t, src_ref, land_ref, send, recv):
        xi, yi, ci = _coords()
        for k, (px, py) in enumerate([(1 - xi, yi), (xi, 1 - yi), (1 - xi, 1 - yi)]):
            rows = land_ref.at[4 * px + 2 * py + ci]
            pltpu.make_async_remote_copy(src_ref=rows, dst_ref=rows, send_sem=send.at[k], recv_sem=recv.at[k],
                                         device_id=(xi, yi, 1 - ci), device_id_type=MESH).start()

    n = len(lands)

    def body(*refs):
        land_refs = refs[:n]
        outs = refs[n:]
        send, recv, token = outs[:n], outs[n:2 * n], outs[3 * n]
        for t in range(n):
            issue(t, land_refs[t], land_refs[t], send[t], recv[t])
        token[...] = jnp.zeros_like(token)

    res = pl.pallas_call(
        body,
        name=name + "_s2",
        in_specs=[HBM_SPEC] * n,
        out_specs=[SEM_SPEC] * (2 * n) + [HBM_SPEC] * n + [pl.BlockSpec(memory_space=pltpu.VMEM)],
        out_shape=[pltpu.SemaphoreType.DMA((3,))] * (2 * n) + [pltpu.HBM(a.shape, a.dtype) for a in lands]
        + [jax.ShapeDtypeStruct(TOKEN, F32)],
        input_output_aliases={i: 2 * n + i for i in range(n)},
        compiler_params=pltpu.CompilerParams(has_side_effects=EFFECT),
    )(*lands)
    state2 = dict(n=n, send=res[:n], recv=res[n:2 * n], lands=res[2 * n:3 * n])
    return state2, res[3 * n]


def _ag_finish(name, state2, after):
    n = state2["n"]

    def body(*refs):
        land_refs, send, recv = refs[:n], refs[n:2 * n], refs[2 * n:3 * n]
        me = _coords()
        for t in range(n):
            for k in range(3):
                cp = pltpu.make_async_remote_copy(
                    src_ref=land_refs[t].at[0], dst_ref=land_refs[t].at[0], send_sem=send[t].at[k],
                    recv_sem=recv[t].at[k], device_id=me, device_id_type=MESH)
                cp.wait_send()
                cp.wait_recv()

    res = pl.pallas_call(
        body,
        name=name + "_w2",
        in_specs=[HBM_SPEC] * n + [SEM_SPEC] * (2 * n) + [ANY_SPEC],
        out_specs=[HBM_SPEC] * n,
        out_shape=[pltpu.HBM(a.shape, a.dtype) for a in state2["lands"]],
        input_output_aliases={i: i for i in range(n)},
        compiler_params=pltpu.CompilerParams(has_side_effects=EFFECT),
    )(*state2["lands"], *state2["send"], *state2["recv"], after)
    return list(res)


def _a2a_start(name, parts, deps):
    def issue(t, x_ref, land_ref, send, recv):
        xi, yi, ci = _coords()
        for k in range(N_DEV - 1):
            px = 1 - xi if (k + 1) & 4 else xi
            py = 1 - yi if (k + 1) & 2 else yi
            pc = 1 - ci if (k + 1) & 1 else ci
            pltpu.make_async_remote_copy(src_ref=x_ref.at[4 * px + 2 * py + pc], dst_ref=land_ref.at[4 * xi + 2 * yi + ci],
                                         send_sem=send.at[k], recv_sem=recv.at[k], device_id=(px, py, pc),
                                         device_id_type=MESH).start()

    lands = [_own_slot_filled(lax.dynamic_index_in_dim(p, _my_slot(), 0, keepdims=False)) for p in parts]
    return _split_start(name, len(parts), N_DEV - 1, issue, parts, lands, deps)


def _a2a_finish(name, state, after):
    return _split_wait(name + "_w", state, lambda r: r.at[0], after)


N_CHIP = N_DEV // 2


def _rs_pair_start(name, parts, deps):
    def issue(t, x_ref, land_ref, send, recv):
        xi, yi, ci = _coords()
        for k in range(N_CHIP):
            pltpu.make_async_remote_copy(src_ref=x_ref.at[2 * k + (1 - ci)], dst_ref=land_ref.at[k], send_sem=send.at[k],
                                         recv_sem=recv.at[k], device_id=(xi, yi, 1 - ci), device_id_type=MESH).start()

    lands = [lax.empty((N_CHIP,) + p.shape[1:], p.dtype) for p in parts]
    return _split_start(name, len(parts), N_CHIP, issue, parts, lands, deps)


def _rs_pair_sum(name, state, after):
    lands = _split_wait(name + "_w", state, lambda r: r.at[0], after)
    sums = []
    for t, (part, land) in enumerate(zip(state["srcs"], lands)):
        _, R, C = land.shape
        pack = SUBLANE * (4 // jnp.dtype(land.dtype).itemsize)
        if R % pack == 0:
            tr, tc = _pick(R, max(pack, ROW_BLOCK_ELEMS // C), pack), C
        else:
            tr, tc = R, _pick(C, max(LANE, ROW_BLOCK_ELEMS // R // LANE * LANE), LANE)

        def fn(mine, theirs):
            own = jnp.where(lax.axis_index("c") == 0, mine[0], mine[1])
            return own.astype(F32) + theirs.astype(F32)

        sums.append(_ew(f"{name}_sum{t}", fn, (N_CHIP, R // tr, C // tc),
                        [(part.reshape((N_CHIP, 2) + part.shape[1:]), (None, 2, tr, tc), lambda k, i, j: (k, 0, i, j)),
                         (land, (None, tr, tc), lambda k, i, j: (k, i, j))],
                        [(land.shape, land.dtype, (None, tr, tc), lambda k, i, j: (k, i, j))])[0])
    return sums


def _rs_chip_start(name, sums, deps):
    def issue(t, x_ref, land_ref, send, recv):
        xi, yi, ci = _coords()
        for k, (px, py) in enumerate([(1 - xi, yi), (xi, 1 - yi), (1 - xi, 1 - yi)]):
            pltpu.make_async_remote_copy(src_ref=x_ref.at[2 * px + py], dst_ref=land_ref.at[2 * xi + yi], send_sem=send.at[k],
                                         recv_sem=recv.at[k], device_id=(px, py, ci), device_id_type=MESH).start()

    xi, yi, _ = _coords()
    lands = []
    for s in sums:
        own = lax.dynamic_index_in_dim(s, 2 * xi + yi, 0, keepdims=True)
        lands.append(lax.dynamic_update_slice(lax.empty(s.shape, s.dtype), own, (2 * xi + yi,) + (0,) * (s.ndim - 1)))
    return _split_start(name, len(sums), N_CHIP - 1, issue, sums, lands, deps)


def _rs_chip_finish(name, state, after):
    return _split_wait(name + "_w", state, lambda r: r.at[0], after)


QK_SCALE = 1.0 / math.sqrt(QK_HEAD)
LOG2E = 1.0 / math.log(2.0)


def _roped_q(q_ext, cos, sin):
    q = jnp.concatenate([q_ext[:, :LANE], q_ext[:, LANE:2 * LANE] * cos + q_ext[:, 2 * LANE:] * sin], axis=1)
    return (q * (QK_SCALE * LOG2E)).astype(BF16)


def _attn_fwd(q_ext, k_full, v_aug, cosp, sinp):
    H, T, _ = k_full.shape
    tq, tkv = _pick(T, ATTN_TQ, LANE), _pick(T, ATTN_TKV, LANE)

    def body(q_ref, cos_ref, sin_ref, k_ref, v_ref, o_ref, lse_ref):
        q = _roped_q(q_ref[...], cos_ref[...], sin_ref[...])
        m = jnp.full((tq, 1), NEG_BIG, F32)
        acc = jnp.zeros((tq, 2 * V_DIM), F32)
        for c in range(T // tkv):
            keys = pl.ds(c * tkv, tkv)
            s = lax.dot_general(q, k_ref[keys, :], DN["nt"], preferred_element_type=F32)
            m_new = jnp.maximum(m, jnp.max(s, axis=-1, keepdims=True))
            p = jnp.exp2(s - m_new).astype(BF16)
            acc = jnp.exp2(m - m_new) * acc + jnp.dot(p, v_ref[keys, :], preferred_element_type=F32)
            m = m_new
        l = acc[:, V_DIM:]
        o_ref[...] = acc[:, :V_DIM] / l
        lse_ref[...] = m + jnp.log2(jnp.max(l, axis=-1, keepdims=True))

    return pl.pallas_call(
        body,
        name="attn_fwd",
        grid=(H, T // tq),
        in_specs=[pl.BlockSpec((tq, Q_EXT), lambda h, i: (i, h)),
                  pl.BlockSpec((tq, LANE), lambda h, i: (i, 0)),
                  pl.BlockSpec((tq, LANE), lambda h, i: (i, 0)),
                  pl.BlockSpec((None, T, K_FULL), lambda h, i: (h, 0, 0)),
                  pl.BlockSpec((None, T, 2 * V_DIM), lambda h, i: (h, 0, 0))],
        out_specs=[pl.BlockSpec((tq, V_DIM), lambda h, i: (i, h)),
                   pl.BlockSpec((None, tq, 1), lambda h, i: (h, i, 0))],
        out_shape=[jax.ShapeDtypeStruct((T, H * V_DIM), F32), jax.ShapeDtypeStruct((H, T, 1), F32)],
        compiler_params=_params(("parallel", "parallel")),
    )(q_ext, cosp, sinp, k_full, v_aug)


def _attn_bwd(q_ext, k_full, v_aug, cosp, sinp, d_out, out, lse_rows):
    H, T, _ = k_full.shape
    tq, tkv = _pick(T, ATTN_TQ, LANE), _pick(T, ATTN_TKV, LANE)
    n_q = T // tq

    def body(q_ref, cos_ref, sin_ref, k_ref, v_ref, do_ref, o_ref, lse_ref, cos_all, sin_all,
             dq_ref, dkv_ref, dkr_ref, dk_acc, dv_acc):
        h, i = pl.program_id(0), pl.program_id(1)

        @pl.when(i == 0)
        def _():
            dk_acc[...] = jnp.zeros_like(dk_acc)
            dv_acc[...] = jnp.zeros_like(dv_acc)

        cos, sin = cos_ref[...], sin_ref[...]
        q, do, lse = _roped_q(q_ref[...], cos, sin), do_ref[...], lse_ref[...]
        delta = lax.dot_general(jnp.ones((SUBLANE, V_DIM), F32), do.astype(F32) * o_ref[...], DN["nt"],
                                preferred_element_type=F32, precision=lax.Precision.HIGHEST)[0:1]
        dq = jnp.zeros((tq, K_FULL), F32)
        for c in range(T // tkv):
            keys = pl.ds(c * tkv, tkv)
            kc = k_ref[keys, :]
            pt = jnp.exp2(lax.dot_general(kc, q, DN["nt"], preferred_element_type=F32) - lse)
            dpt = lax.dot_general(v_ref[keys, pl.ds(0, V_DIM)], do, DN["nt"], preferred_element_type=F32)
            dst = (pt * (dpt - delta)).astype(BF16)
            dv_acc[keys, :] += jnp.dot(pt.astype(BF16), do, preferred_element_type=F32)
            dk_acc[keys, :] += jnp.dot(dst, q, preferred_element_type=F32)
            dq = dq + lax.dot_general(dst, kc, DN["tn"], preferred_element_type=F32)
        r = dq[:, LANE:]
        dq_ref[...] = (jnp.concatenate([dq[:, :LANE], r * cos, r * sin], axis=1) * QK_SCALE).astype(BF16)

        @pl.when(i == n_q - 1)
        def _():
            dk = dk_acc[...] * (1.0 / LOG2E)
            dkv_ref[...] = jnp.concatenate([dk[:, :LANE], dv_acc[...]], axis=1).astype(BF16)
            rk = dk[:, LANE:]
            dkr = jnp.concatenate([rk * cos_all[...], rk * sin_all[...]], axis=1)

            @pl.when(h == 0)
            def _():
                dkr_ref[...] = dkr

            @pl.when(h != 0)
            def _():
                dkr_ref[...] += dkr

    return pl.pallas_call(
        body,
        name="attn_bwd",
        grid=(H, n_q),
        in_specs=[pl.BlockSpec((tq, Q_EXT), lambda h, i: (i, h)),
                  pl.BlockSpec((tq, LANE), lambda h, i: (i, 0)),
                  pl.BlockSpec((tq, LANE), lambda h, i: (i, 0)),
                  pl.BlockSpec((None, T, K_FULL), lambda h, i: (h, 0, 0)),
                  pl.BlockSpec((None, T, 2 * V_DIM), lambda h, i: (h, 0, 0)),
                  pl.BlockSpec((tq, V_DIM), lambda h, i: (i, h)),
                  pl.BlockSpec((tq, V_DIM), lambda h, i: (i, h)),
                  pl.BlockSpec((None, 1, tq), lambda h, i: (h, 0, i)),
                  pl.BlockSpec((T, LANE), lambda h, i: (0, 0)),
                  pl.BlockSpec((T, LANE), lambda h, i: (0, 0))],
        out_specs=[pl.BlockSpec((tq, Q_EXT), lambda h, i: (i, h)),
                   pl.BlockSpec((T, K_FULL), lambda h, i: (0, h)),
                   pl.BlockSpec((T, K_FULL), lambda h, i: (0, 0))],
        out_shape=[jax.ShapeDtypeStruct((T, H * Q_EXT), BF16), jax.ShapeDtypeStruct((T, H * K_FULL), BF16),
                   jax.ShapeDtypeStruct((T, K_FULL), F32)],
        scratch_shapes=[pltpu.VMEM((T, K_FULL), F32), pltpu.VMEM((T, V_DIM), F32)],
        compiler_params=_params(("arbitrary", "arbitrary")),
    )(q_ext, cosp, sinp, k_full, v_aug, d_out, out, lse_rows, cosp, sinp)


def _sgu_forward_math(pb, lng, lnb, ws, bsb, gw, gh):
    u_raw, v_raw = pb[:, :gw], pb[:, gw:]
    t_u, t_v = _gelu_tanh(u_raw), _gelu_tanh(v_raw)
    u, v = _gelu(u_raw, t_u), _gelu(v_raw, t_v)
    vc = v - jnp.mean(v, axis=-1, keepdims=True)
    rstd = lax.rsqrt(jnp.mean(vc * vc, axis=-1, keepdims=True) + EPS)
    vhat = vc * rstd
    vln = vhat * lng + lnb
    hd = gw // gh
    s = jnp.concatenate(
        [jnp.dot(ws[g], vln[:, g * hd:(g + 1) * hd].astype(BF16), preferred_element_type=F32) + bsb[g]
         for g in range(gh)], axis=1)
    return (u_raw, t_u), (v_raw, t_v), u, rstd, vhat, vln, s


def _sgu_fwd(proj_b, lng, lnb, ws, bsb, gg):
    T, gw2 = proj_b.shape
    gw, gh = gw2 // 2, ws.shape[0]

    def fn(pb, lng, lnb, ws, bsb, gg):
        _, _, u, _, _, _, s = _sgu_forward_math(pb, lng, lnb, ws, bsb, gw, gh)
        return _rms(u * s, gg)

    return _ew("sgu_fwd", fn, (T // CHUNK,),
               [_rows(proj_b, CHUNK), _vec(lng), _vec(lnb), _vec(ws), _vec(bsb), _vec(gg)],
               [_rows_out((T, gw), BF16, CHUNK)])[0]


def _sgu_bwd(proj_b, d_mixed, lng, lnb, ws, ws_t, bsb, gg):
    T, gw2 = proj_b.shape
    gw, gh = gw2 // 2, ws.shape[0]
    hd = gw // gh

    def fn(pb, dmg, lng, lnb, ws, ws_t, bsb, gg):
        u_pre, v_pre, u, rstd, vhat, vln, s = _sgu_forward_math(pb, lng, lnb, ws, bsb, gw, gh)
        dgo, dgg = _rms_bwd(u * s, gg, dmg)
        du, ds = dgo * s, dgo * u
        ones = jnp.ones((SUBLANE, hd), F32)
        dws, dbs, dvln = [], [], []
        for g in range(gh):
            ds_g = ds[:, g * hd:(g + 1) * hd]
            ds_gb = ds_g.astype(BF16)
            dws.append(lax.dot_general(ds_gb, vln[:, g * hd:(g + 1) * hd].astype(BF16), DN["nt"],
                                       preferred_element_type=F32)[None])
            dbs.append(lax.dot_general(ones, ds_g, DN["nt"], preferred_element_type=F32,
                                       precision=lax.Precision.HIGHEST)[None])
            dvln.append(jnp.dot(ws_t[g], ds_gb, preferred_element_type=F32))
        dvln = jnp.concatenate(dvln, axis=1)
        dlng = jnp.sum(dvln * vhat, axis=0, keepdims=True)
        dlnb = jnp.sum(dvln, axis=0, keepdims=True)
        dvh = dvln * lng
        dv = rstd * (dvh - jnp.mean(dvh, axis=-1, keepdims=True) - vhat * jnp.mean(dvh * vhat, axis=-1, keepdims=True))
        d_pb = jnp.concatenate([du * _gelu_grad(*u_pre), dv * _gelu_grad(*v_pre)], axis=1)
        return d_pb, dgg, jnp.concatenate(dws, axis=0), jnp.concatenate(dbs, axis=0), dlng, dlnb

    return _ew("sgu_bwd", fn, (T // CHUNK,),
               [_rows(proj_b, CHUNK), _rows(d_mixed, CHUNK, gw, 1), _vec(lng), _vec(lnb), _vec(ws), _vec(ws_t),
                _vec(bsb), _vec(gg)],
               [_rows_out((T, gw2), BF16, CHUNK)],
               [_vec_out((1, gw)), _vec_out((gh, CHUNK, CHUNK)), _vec_out((gh, SUBLANE, CHUNK)), _vec_out((1, gw)),
                _vec_out((1, gw))])


def _adamw(name, parts, w, m, v):
    R, C = w.shape
    pack = SUBLANE * (4 // jnp.dtype(parts.dtype).itemsize)
    if R % pack == 0:
        tm, tc = _pick(R, max(pack, ADAM_BLOCK_ELEMS // C), pack), C
    else:
        parts = parts[:, :R]
        tm, tc = R, _pick(C, max(LANE, ADAM_BLOCK_ELEMS // R // LANE * LANE), LANE)

    n_parts = parts.shape[0]

    def fn(p, w, m, v):
        g = p[0].astype(F32)
        for i in range(1, n_parts):
            g = g + p[i].astype(F32)
        m2 = ADAM_B1 * m + (1.0 - ADAM_B1) * g
        v2 = ADAM_B2 * v + (1.0 - ADAM_B2) * (g * g)
        m_hat = m2 / (1.0 - ADAM_B1 ** ADAM_STEP)
        v_hat = v2 / (1.0 - ADAM_B2 ** ADAM_STEP)
        delta = -ADAM_LR * (m_hat / (jnp.sqrt(v_hat) + ADAM_EPS) + ADAM_WD * w)
        return g, delta, m2, v2

    blk = lambda a: (a, (tm, tc), lambda i, j: (i, j))
    return _ew(name, fn, (R // tm, C // tc),
               [(parts, (n_parts, tm, tc), lambda i, j: (0, i, j)), blk(w), blk(m), blk(v)],
               [((R, C), F32, (tm, tc), lambda i, j: (i, j))] * 4)


def kernel(x, positions, pre_mix_norm, w_in, q_norm, kv_norm, w_uq, w_ukv, v_ln_gain, v_ln_bias, w_spatial, b_spatial, attn_out_norm, gmlp_out_norm, w_out, post_mix_norm, pre_ffn_norm, w_gate, w_up, w_down, post_ffn_norm, loss_target, m_pre_mix_norm, m_w_in, m_q_norm, m_kv_norm, m_w_uq, m_w_ukv, m_v_ln_gain, m_v_ln_bias, m_w_spatial, m_b_spatial, m_attn_out_norm, m_gmlp_out_norm, m_w_out, m_post_mix_norm, m_pre_ffn_norm, m_w_gate, m_w_up, m_w_down, m_post_ffn_norm, v_pre_mix_norm, v_w_in, v_q_norm, v_kv_norm, v_w_uq, v_w_ukv, v_v_ln_gain, v_v_ln_bias, v_w_spatial, v_b_spatial, v_attn_out_norm, v_gmlp_out_norm, v_w_out, v_post_mix_norm, v_pre_ffn_norm, v_w_gate, v_w_up, v_w_down, v_post_ffn_norm):
    given = dict(locals())
    T, D = x.shape[1], x.shape[2]
    QL, KVL = w_uq.shape[1], w_ukv.shape[1]
    GW, GH = v_ln_gain.shape[1], w_spatial.shape[1]
    IWS = w_in.shape[2]
    IW = IWS * N_DEV
    H = w_ukv.shape[2] * N_DEV // (NOPE_DIM + V_DIM)
    AW = H * V_DIM
    FS = w_gate.shape[2]
    FP = _round_up(FS, LANE)
    C0 = QL + KVL
    assert IW == C0 + ROPE_DIM + 2 * GW and AW + GW == D and AW == GW
    assert QL % KVL == 0 and C0 % K_FULL == 0 and w_spatial.shape[2] == CHUNK and T % CHUNK == 0
    assert w_uq.shape[2] * N_DEV == H * QK_HEAD and D % N_DEV == 0

    x2, tgt = x[0], loss_target[0]
    tm_d = _row_tile(T, D)

    inv_freq = 1.0 / (ROPE_THETA ** (jnp.arange(0, ROPE_DIM, 2, dtype=F32) / ROPE_DIM))
    ang = positions[0].astype(F32)[:, None] * inv_freq
    zpad = jnp.zeros((T, LANE - ROPE_DIM), F32)
    cosp = jnp.concatenate([jnp.cos(ang), jnp.cos(ang), zpad], axis=1)
    sinp = jnp.concatenate([jnp.sin(ang), jnp.sin(ang), zpad], axis=1)

    def shard_t(a):
        return a[0].T

    st_a, tok = _ag_start("ag_a", [shard_t(w_in).astype(BF16), w_uq[0].astype(BF16), w_ukv[0].astype(BF16)], [])
    st_o, tok = _ag_start("ag_o", [w_out[0].astype(BF16)], [tok])
    st_g, tok = _ag_start("ag_g", [_cast_pad_rows("pad_w_gate", shard_t(w_gate), FP, [tok])], [tok])
    st_u, tok = _ag_start("ag_u", [_cast_pad_rows("pad_w_up", shard_t(w_up), FP, [tok])], [tok])
    st_d, tok = _ag_start("ag_d", [_cast_pad_rows("pad_w_down", w_down[0], FP, [tok])], [tok])

    xn = _ew("rms_x", lambda x, g: _rms(x, g), (T // tm_d,), [_rows(x2, tm_d), _vec(pre_mix_norm)],
             [_rows_out((T, D), BF16, tm_d)], deps=[tok])[0]
    st_a, tok = _ag_mid("ag_a", st_a, xn)
    g_in, g_uq, g_ukv = _ag_finish("ag_a", st_a, tok)

    w_in_t = g_in.reshape(IW, D)
    kr_w = w_in_t[C0:C0 + ROPE_DIM]
    z64 = jnp.zeros((LANE - ROPE_DIM, D), BF16)
    w_a_t = jnp.concatenate([w_in_t[:C0], kr_w, z64, _rot(kr_w, 0), z64], axis=0)
    w_b_t = w_in_t[C0 + ROPE_DIM:]
    wq = g_uq.transpose(1, 0, 2).reshape(QL, H, QK_HEAD)
    zq = jnp.zeros((QL, H, LANE - ROPE_DIM), BF16)
    wq_ext = jnp.concatenate([wq[..., :NOPE_DIM], wq[..., NOPE_DIM:], zq, _rot(wq[..., NOPE_DIM:]), zq],
                             axis=-1).reshape(QL, H * Q_EXT)
    wkv = g_ukv.transpose(1, 0, 2).reshape(KVL, H * (NOPE_DIM + V_DIM))

    ws = w_spatial[0].astype(BF16)
    ws_t = jnp.swapaxes(ws, 1, 2)
    bsb = jnp.broadcast_to(b_spatial[0][:, :, None], (GH, CHUNK, GW // GH))

    proj_a = _mm2("mm_proj_a", "nt", xn, w_a_t, F32, tn_pref=w_a_t.shape[0])
    proj_b = _mm2("mm_proj_b", "nt", xn, w_b_t, F32)

    tm_q = _row_tile(T, QL)
    qn = _ew("rms_q", lambda x, g: _rms(x, g), (T // tm_q,), [_rows(proj_a, tm_q, QL, 0), _vec(q_norm)],
             [_rows_out((T, QL), BF16, tm_q)])[0]
    kvn = _ew("rms_kv", lambda x, g: _rms(x, g), (T // tm_q,), [_rows(proj_a, tm_q, KVL, QL // KVL), _vec(kv_norm)],
              [_rows_out((T, KVL), BF16, tm_q)])[0]
    q_ext = _mm2("mm_q", "nn", qn, wq_ext, F32)
    kv = _mm2("mm_kv", "nn", kvn, wkv, F32)

    tm_h = _pick(T, 512, SUBLANE)
    hb = _pick(H, 4, 1)

    def make_kv(kv, kr, c, s):
        k_rope = kr[:, :LANE] * c + kr[:, LANE:] * s
        ones = jnp.ones((kv.shape[0], V_DIM), F32)
        ks = [jnp.concatenate([kv[:, j * K_FULL:j * K_FULL + LANE], k_rope], axis=1) for j in range(hb)]
        vs = [jnp.concatenate([kv[:, j * K_FULL + LANE:(j + 1) * K_FULL], ones], axis=1) for j in range(hb)]
        return jnp.stack(ks), jnp.stack(vs)

    k_full, v_aug = _ew("make_kv", make_kv, (T // tm_h, H // hb),
                        [(kv, (tm_h, hb * K_FULL), lambda i, h: (i, h)),
                         (proj_a, (tm_h, K_FULL), lambda i, h: (i, C0 // K_FULL)),
                         (cosp, (tm_h, LANE), lambda i, h: (i, 0)), (sinp, (tm_h, LANE), lambda i, h: (i, 0))],
                        [((H, T, K_FULL), BF16, (hb, tm_h, K_FULL), lambda i, h: (h, i, 0)),
                         ((H, T, 2 * V_DIM), BF16, (hb, tm_h, 2 * V_DIM), lambda i, h: (h, i, 0))])

    a_out, lse = _attn_fwd(q_ext, k_full, v_aug, cosp, sinp)
    st_o, tok_o = _ag_mid("ag_o", st_o, a_out)
    tm_a = _row_tile(T, AW)
    mixed_a = _ew("rms_attn", lambda x, g: _rms(x, g), (T // tm_a,), [_rows(a_out, tm_a), _vec(attn_out_norm)],
                  [_rows_out((T, AW), BF16, tm_a)], deps=[tok_o])[0]
    mixed_g = _sgu_fwd(proj_b, v_ln_gain, v_ln_bias, ws, bsb, gmlp_out_norm)
    wo = _ag_finish("ag_o", st_o, mixed_g)[0].reshape(D, D)
    st_g, tok_gu = _ag_mid("ag_g", st_g, wo)

    tm_, tn_ = _pick(T, 1024, LANE), _pick(D, 1024, LANE)
    tk_ = _fit_tk(AW, tm_, tn_, pairs=2)
    nka = AW // tk_
    mix_out = _mm("mm_out", "nn",
                  [(mixed_a, (tm_, tk_), lambda i, j, k: (i, k), wo, (tk_, tn_), lambda i, j, k: (k, j)),
                   (mixed_g, (tm_, tk_), lambda i, j, k: (i, k), wo, (tk_, tn_), lambda i, j, k: (k + nka, j))],
                  (T // tm_, D // tn_, nka), (tm_, tn_), (T, D), F32, (tm_, tn_), lambda i, j, k: (i, j), deps=[tok_gu])

    def post_mix(x, mo, g3, g4):
        h = x + _rms(mo, g3)
        return h, _rms(h, g4)

    h_res, hn = _ew("post_mix", post_mix, (T // tm_d,),
                    [_rows(x2, tm_d), _rows(mix_out, tm_d), _vec(post_mix_norm), _vec(pre_ffn_norm)],
                    [_rows_out((T, D), F32, tm_d), _rows_out((T, D), BF16, tm_d)])

    F = N_DEV * FP
    tn_f = _pick(F, FFN_TILE_MAX, MXU_DIM)
    tn_e = _pick(F, FFN_TILE_MAX // 2, LANE)

    def swiglu(u, g):
        g = g.astype(F32)
        return u, g * _sigmoid(g) * u

    st_u, tok_u = _ag_mid("ag_u", st_u, hn)
    wg_t = _ag_finish("ag_g", st_g, tok_u)[0].reshape(F, D)
    gate = _mm2("mm_gate", "nt", hn, wg_t, BF16, tn_pref=tn_f)
    wu_t = _ag_finish("ag_u", st_u, gate)[0].reshape(F, D)
    st_d, tok_d = _ag_mid("ag_d", st_d, wu_t)
    up, act = _mm2("mm_up", "nt", hn, wu_t, BF16, tn_pref=tn_e, deps=[tok_d], extras=[gate],
                   epilogue=swiglu, n_out=2)
    wd_f = _ag_finish("ag_d", st_d, act)[0].reshape(F, D)
    ffn = _mm2("mm_down", "nn", act, wd_f, F32, tk_max=tn_f)

    def loss_head(h, f, t, g5):
        y = h + _rms(f, g5)
        diff = y - t
        dy = diff / D
        d_f, dg5 = _rms_bwd(f, g5, dy)
        lsum = jnp.sum(jnp.sum(diff * diff, axis=0, keepdims=True), axis=1, keepdims=True) * (0.5 / D)
        return dy, d_f, dg5, lsum

    dy, d_ffn, dg_post_ffn, loss_part = _ew(
        "loss_head", loss_head, (T // tm_d,),
        [_rows(h_res, tm_d), _rows(ffn, tm_d), _rows(tgt, tm_d), _vec(post_ffn_norm)],
        [_rows_out((T, D), F32, tm_d), _rows_out((T, D), BF16, tm_d)], [_vec_out((1, D)), _vec_out((1, 1))])

    def swiglu_bwd(da, g, u):
        g, u = g.astype(F32), u.astype(F32)
        sg = _sigmoid(g)
        return da * u * sg * (1.0 + g * (1.0 - sg)), da * g * sg

    def ffn_dw(name, z, y, deps):
        return _mm2(name, "tn", z, y, BF16, tm_pref=tn_f, deps=deps).reshape(N_DEV, FP, D)

    x_d, tok = _a2a_start("a2a_d", [ffn_dw("mm_dwd", act, d_ffn, [])], [])
    d_gate, d_up = _mm2("mm_dact", "nt", d_ffn, wd_f, BF16, tn_pref=tn_e, deps=[tok], extras=[gate, up],
                        epilogue=swiglu_bwd, n_out=2)
    x_g, tok = _a2a_start("a2a_g", [ffn_dw("mm_dwg", d_gate, hn, [])], [])
    x_u, tok = _a2a_start("a2a_u", [ffn_dw("mm_dwu", d_up, hn, [tok])], [])
    d_hn_g = _mm2("mm_dhn_g", "nn", d_gate, wg_t, F32, tk_max=tn_f, deps=[tok])
    d_hn = _mm2("mm_dhn_u", "nn", d_up, wu_t, F32, tk_max=tn_f, extras=[d_hn_g], epilogue=lambda s, g: (s + g,))

    def pre_ffn_bwd(h, dhn, dy, mo, g4, g3):
        d1, dg4 = _rms_bwd(h, g4, dhn)
        dh = dy + d1
        dmo, dg3 = _rms_bwd(mo, g3, dh)
        return dh, dmo, dg4, dg3

    dh, d_mo, dg_pre_ffn, dg_post_mix = _ew(
        "pre_ffn_bwd", pre_ffn_bwd, (T // tm_d,),
        [_rows(h_res, tm_d), _rows(d_hn, tm_d), _rows(dy, tm_d), _rows(mix_out, tm_d), _vec(pre_ffn_norm),
         _vec(post_mix_norm)],
        [_rows_out((T, D), F32, tm_d), _rows_out((T, D), BF16, tm_d)], [_vec_out((1, D)), _vec_out((1, D))])

    d_mixed = _mm2("mm_dmixed", "nt", d_mo, wo, F32)
    dwo = jnp.concatenate([_mm2("mm_dwo_a", "tn", mixed_a, d_mo, BF16), _mm2("mm_dwo_g", "tn", mixed_g, d_mo, BF16)],
                          axis=0)
    x_o, tok = _a2a_start("a2a_o", [dwo.reshape(N_DEV, D // N_DEV, D)], [])

    d_a_out, dg_attn = _ew("rms_attn_bwd", lambda a, g, d: _rms_bwd(a, g, d), (T // tm_a,),
                           [_rows(a_out, tm_a), _vec(attn_out_norm), _rows(d_mixed, tm_a, AW, 0)],
                           [_rows_out((T, AW), BF16, tm_a)], [_vec_out((1, AW))], deps=[tok])
    d_pb, dg_gmlp, dws, dbs, dlng, dlnb = _sgu_bwd(proj_b, d_mixed, v_ln_gain, v_ln_bias, ws, ws_t, bsb, gmlp_out_norm)

    dq_ext, d_kv, d_kr = _attn_bwd(q_ext, k_full, v_aug, cosp, sinp, d_a_out, a_out, lse.reshape(H, 1, T))

    d_qn = _mm2("mm_dqn", "nt", dq_ext, wq_ext, F32)
    dwq_ext = _mm2("mm_dwq", "tn", qn, dq_ext, F32)
    d_kvn = _mm2("mm_dkvn", "nt", d_kv, wkv, F32)
    dwkv = _mm2("mm_dwkv", "tn", kvn, d_kv, BF16)

    d_qc, dg_q = _ew("rms_q_bwd", lambda a, g, d: _rms_bwd(a, g, d), (T // tm_q,),
                     [_rows(proj_a, tm_q, QL, 0), _vec(q_norm), _rows(d_qn, tm_q)],
                     [_rows_out((T, QL), BF16, tm_q)], [_vec_out((1, QL))])
    d_kvc, dg_kv = _ew("rms_kv_bwd", lambda a, g, d: _rms_bwd(a, g, d), (T // tm_q,),
                       [_rows(proj_a, tm_q, KVL, QL // KVL), _vec(kv_norm), _rows(d_kvn, tm_q)],
                       [_rows_out((T, KVL), BF16, tm_q)], [_vec_out((1, KVL))])
    d_pa = jnp.concatenate([d_qc, d_kvc, d_kr.astype(BF16)], axis=1)

    dwa_t = _mm2("mm_dwa", "tn", d_pa, xn, F32, tm_pref=w_a_t.shape[0])
    dwb_t = _mm2("mm_dwb", "tn", d_pb, xn, BF16)
    dkr_w = dwa_t[C0:C0 + ROPE_DIM] + _rot_t(dwa_t[C0 + LANE:C0 + LANE + ROPE_DIM], 0)
    dw_in = jnp.concatenate([dwa_t[:C0].astype(BF16), dkr_w.astype(BF16), dwb_t], axis=0)
    dw_in = dw_in.reshape(N_DEV, IWS, D)
    dq3 = dwq_ext.reshape(QL, H, Q_EXT)
    dwq = jnp.concatenate([dq3[..., :NOPE_DIM],
                           dq3[..., LANE:LANE + ROPE_DIM] + _rot_t(dq3[..., 2 * LANE:2 * LANE + ROPE_DIM])], axis=-1)
    dwq = dwq.astype(BF16).reshape(QL, N_DEV, H * QK_HEAD // N_DEV).transpose(1, 0, 2)
    dwkv = dwkv.reshape(KVL, N_DEV, w_ukv.shape[2]).transpose(1, 0, 2)
    p_e, tok = _rs_pair_start("rs_e", [dw_in, dwq, dwkv], [])
    d_xn_a = _mm2("mm_dxn_a", "nn", d_pa, w_a_t, F32, deps=[tok])
    x_e, tok = _rs_chip_start("rs_e2", _rs_pair_sum("rs_e", p_e, d_xn_a), [])
    d_xn_b = _mm2("mm_dxn_b", "nn", d_pb, w_b_t, F32, deps=[tok])

    def x_bwd(x, da, db, dh, g0):
        dx, dg0 = _rms_bwd(x, g0, da + db)
        return dh + dx, dg0

    grad_x, dg_pre_mix = _ew("x_bwd", x_bwd, (T // tm_d,),
                             [_rows(x2, tm_d), _rows(d_xn_a, tm_d), _rows(d_xn_b, tm_d), _rows(dh, tm_d),
                              _vec(pre_mix_norm)],
                             [_rows_out((T, D), F32, tm_d)], [_vec_out((1, D))])

    def adam_big(n, parts, transposed=False):
        view = shard_t if transposed else (lambda a: a[0])
        res = _adamw("adamw_" + n, parts, view(given[n]), view(given["m_" + n]), view(given["v_" + n]))
        done[n] = res[0]
        return [(r.T if transposed else r)[None] for r in res]

    done = {}

    small_names = ["pre_mix_norm", "q_norm", "kv_norm", "v_ln_gain", "v_ln_bias", "w_spatial", "b_spatial",
                   "attn_out_norm", "gmlp_out_norm", "post_mix_norm", "pre_ffn_norm", "post_ffn_norm"]
    small_grads = [dg_pre_mix, dg_q, dg_kv, dlng, dlnb, dws, dbs[:, 0, :], dg_attn, dg_gmlp, dg_post_mix, dg_pre_ffn,
                   dg_post_ffn]
    n_small = sum(g.size for g in small_grads)
    pad_small = _round_up(n_small, SUBLANE * LANE) - n_small

    def pack(arrs):
        flat = jnp.concatenate([a.reshape(-1) for a in arrs] + [jnp.zeros((pad_small,), F32)])
        return flat.reshape(-1, LANE)

    st_s, tok = _ag_start("ag_s", [pack(small_grads)], [])

    results = {}
    results["w_down"] = adam_big("w_down", _a2a_finish("a2a_d", x_d, tok)[0])
    results["w_gate"] = adam_big("w_gate", _a2a_finish("a2a_g", x_g, done["w_down"])[0], True)
    st_s, tok = _ag_mid("ag_s", st_s, done["w_gate"])
    results["w_up"] = adam_big("w_up", _a2a_finish("a2a_u", x_u, tok)[0], True)
    results["w_out"] = adam_big("w_out", _a2a_finish("a2a_o", x_o, done["w_up"])[0])
    e_in, e_uq, e_ukv = _rs_chip_finish("rs_e2", x_e, done["w_out"])
    results["w_in"] = adam_big("w_in", e_in, True)
    results["w_uq"] = adam_big("w_uq", e_uq)
    results["w_ukv"] = adam_big("w_ukv", e_ukv)

    small_parts = _ag_finish("ag_s", st_s, done["w_ukv"])[0]
    sm = _adamw("adamw_small", small_parts, pack([given[n] for n in small_names]),
                pack([given["m_" + n] for n in small_names]), pack([given["v_" + n] for n in small_names]))

    off = 0
    for n in small_names:
        shp, size = given[n].shape, given[n].size
        results[n] = [r.reshape(-1)[off:off + size].reshape(shp) for r in sm]
        off += size

    loss = lax.psum(loss_part[0, 0], MESH_AXES)
    order = ["pre_mix_norm", "w_in", "q_norm", "kv_norm", "w_uq", "w_ukv", "v_ln_gain", "v_ln_bias", "w_spatial",
             "b_spatial", "attn_out_norm", "gmlp_out_norm", "w_out", "post_mix_norm", "pre_ffn_norm", "w_gate", "w_up",
             "w_down", "post_ffn_norm"]
    out = [loss, grad_x[None]]
    for kind in range(4):
        out += [results[n][kind] for n in order]
    return tuple(out)
```

```python
import math

import jax
import jax.numpy as jnp
from jax import lax
from jax.experimental import pallas as pl
from jax.experimental.pallas import tpu as pltpu

F32 = jnp.float32
BF16 = jnp.bfloat16
N_DEV = 8
MESH_AXES = ("x", "y", "c")
MESH = pl.DeviceIdType.MESH

NOPE_DIM = 128
ROPE_DIM = 64
V_DIM = 128
ROPE_THETA = 10000.0
CHUNK = 128
EPS = 1e-6
QK_HEAD = NOPE_DIM + ROPE_DIM
Q_EXT = 3 * 128
K_FULL = 2 * 128

ADAM_LR = 0.001
ADAM_B1 = 0.9
ADAM_B2 = 0.999
ADAM_EPS = 1e-08
ADAM_WD = 0.01
ADAM_STEP = 10

LANE = 128
SUBLANE = 8
VMEM_LIMIT = 60 * 1024 * 1024
MM_VMEM_BUDGET = 50 * 1024 * 1024
MXU_DIM = 256
FFN_TILE_MAX = 3072
MM_TK_MAX = 4096
EPILOGUE_ROWS = 128
ATTN_TQ = 1024
ATTN_TKV = 512
NEG_BIG = -1e30
ROW_BLOCK_ELEMS = 1024 * 1024
ADAM_BLOCK_ELEMS = 512 * 1024

DN = {
    "nn": (((1,), (0,)), ((), ())),
    "nt": (((1,), (1,)), ((), ())),
    "tn": (((0,), (0,)), ((), ())),
}


def _pick(dim, pref, mult):
    t = min(pref, dim)
    t -= t % mult
    while t >= mult:
        if dim % t == 0:
            return t
        t -= mult
    return dim


def _round_up(n, m):
    return (n + m - 1) // m * m


def _params(sem):
    return pltpu.CompilerParams(dimension_semantics=sem, vmem_limit_bytes=VMEM_LIMIT)


def _dep_specs(deps, nd):
    return [pl.BlockSpec(d.shape, lambda *_, r=d.ndim: (0,) * r) for d in deps]


def _mm(name, mode, pairs, grid, acc_shape, out_shape, out_dtype, out_block, out_map, deps=(), extras=(),
        epilogue=None, n_out=1):
    n, ne, nx = len(pairs), len(extras), len(deps)
    nk = grid[-1]
    nd = len(grid)
    dn = DN[mode]
    use_acc = nk > 1 or epilogue is not None

    def body(*refs):
        e_refs = refs[2 * n:2 * n + ne]
        o_refs = refs[2 * n + ne + nx:2 * n + ne + nx + n_out]
        k = pl.program_id(nd - 1)

        def dot(i):
            return lax.dot_general(refs[2 * i][...], refs[2 * i + 1][...], dn, preferred_element_type=F32)

        if not use_acc:
            s = dot(0)
            for i in range(1, n):
                s = s + dot(i)
            o_refs[0][...] = s.astype(o_refs[0].dtype)
            return
        acc = refs[2 * n + ne + nx + n_out]

        @pl.when(k == 0)
        def _():
            acc[...] = dot(0)

        if nk > 1:
            @pl.when(k > 0)
            def _():
                acc[...] += dot(0)

        for i in range(1, n):
            acc[...] += dot(i)

        if epilogue is None:
            @pl.when(k == nk - 1)
            def _():
                o_refs[0][...] = acc[...].astype(o_refs[0].dtype)
        else:
            @pl.when(k == nk - 1)
            def _():
                rows = acc_shape[0]
                slab = _pick(rows, EPILOGUE_ROWS, SUBLANE)
                for r0 in range(0, rows, slab):
                    sl = pl.ds(r0, slab)
                    vals = epilogue(acc[sl, :], *[e[sl, :] for e in e_refs])
                    for r, v in zip(o_refs, vals):
                        r[sl, :] = v.astype(r.dtype)

    in_specs, args = [], []
    for a, a_blk, a_map, b, b_blk, b_map in pairs:
        in_specs += [pl.BlockSpec(a_blk, a_map), pl.BlockSpec(b_blk, b_map)]
        args += [a, b]
    in_specs += [pl.BlockSpec(blk, imap) for (_, blk, imap) in extras]
    args += [e for (e, _, _) in extras]
    if not isinstance(out_dtype, (tuple, list)):
        out_dtype = (out_dtype,) * n_out
    res = pl.pallas_call(
        body,
        name=name,
        grid=grid,
        in_specs=in_specs + _dep_specs(deps, nd),
        out_specs=[pl.BlockSpec(out_block, out_map)] * n_out,
        out_shape=[jax.ShapeDtypeStruct(out_shape, dt) for dt in out_dtype],
        scratch_shapes=[pltpu.VMEM(acc_shape, F32)] if use_acc else [],
        compiler_params=_params(("parallel",) * (nd - 1) + ("arbitrary",)),
    )(*args, *deps)
    return res[0] if n_out == 1 else res


def _fit_tk(K, tm, tn, pairs=1, out_bytes=4, fixed_extra=0, tk_max=MM_TK_MAX, needs_acc=False):
    fixed = tm * tn * 4 + 2 * tm * tn * out_bytes + fixed_extra
    if K <= tk_max and not needs_acc and fixed - tm * tn * 4 + pairs * 4 * (tm + tn) * K <= MM_VMEM_BUDGET:
        return K
    tk = _pick(K, tk_max, LANE)
    while tk > LANE and fixed + pairs * 4 * (tm + tn) * tk > MM_VMEM_BUDGET:
        smaller = _pick(K, tk - LANE, LANE)
        if smaller >= tk:
            break
        tk = smaller
    return tk


def _mm2(name, mode, a, b, out_dtype, tm_pref=1024, tn_pref=1024, deps=(), extras=(), epilogue=None, n_out=1,
         tk_max=MM_TK_MAX):
    if mode == "nn":
        (M, K), N = a.shape, b.shape[1]
    elif mode == "nt":
        (M, K), N = a.shape, b.shape[0]
    else:
        (K, M), N = a.shape, b.shape[1]
    tm, tn = _pick(M, tm_pref, LANE), _pick(N, tn_pref, LANE)
    extra_bytes = sum(2 * tm * tn * jnp.dtype(e.dtype).itemsize for e in extras)
    tk = _fit_tk(K, tm, tn, out_bytes=n_out * jnp.dtype(out_dtype).itemsize, fixed_extra=extra_bytes, tk_max=tk_max,
                 needs_acc=epilogue is not None)
    if mode == "tn":
        a_blk, a_map = (tk, tm), lambda i, j, k: (k, i)
    else:
        a_blk, a_map = (tm, tk), lambda i, j, k: (i, k)
    if mode == "nt":
        b_blk, b_map = (tn, tk), lambda i, j, k: (j, k)
    else:
        b_blk, b_map = (tk, tn), lambda i, j, k: (k, j)
    return _mm(name, mode, [(a, a_blk, a_map, b, b_blk, b_map)], (M // tm, N // tn, K // tk), (tm, tn),
               (M, N), out_dtype, (tm, tn), lambda i, j, k: (i, j), deps=deps,
               extras=[(e, (tm, tn), lambda i, j, k: (i, j)) for e in extras], epilogue=epilogue, n_out=n_out)


def _ew(name, fn, grid, ins, outs, accs=(), deps=()):
    ni, no, na = len(ins), len(outs), len(accs)
    nd = len(grid)
    nx = len(deps)

    def body(*refs):
        vals = fn(*[r[...] for r in refs[:ni]])
        if not isinstance(vals, (tuple, list)):
            vals = (vals,)
        for r, v in zip(refs[ni + nx:ni + nx + no], vals[:no]):
            r[...] = v.astype(r.dtype)
        if na:
            first = pl.program_id(nd - 1) == 0
            for r, v in zip(refs[ni + nx + no:], vals[no:]):
                @pl.when(first)
                def _(r=r, v=v):
                    r[...] = v.astype(r.dtype)

                @pl.when(jnp.logical_not(first))
                def _(r=r, v=v):
                    r[...] += v.astype(r.dtype)

    all_outs = list(outs) + list(accs)
    sem = ("parallel",) * (nd - 1) + (("arbitrary",) if na else ("parallel",))
    res = pl.pallas_call(
        body,
        name=name,
        grid=grid,
        in_specs=[pl.BlockSpec(blk, imap) for (_, blk, imap) in ins] + _dep_specs(deps, nd),
        out_specs=[pl.BlockSpec(blk, imap) for (_, _, blk, imap) in all_outs],
        out_shape=[jax.ShapeDtypeStruct(s, d) for (s, d, _, _) in all_outs],
        compiler_params=_params(sem),
    )(*[a for (a, _, _) in ins], *deps)
    return res


def _cast_pad_rows(name, a, rows_out, deps):
    R, C = a.shape
    tr = _pick(math.gcd(R, rows_out), max(2 * SUBLANE, ROW_BLOCK_ELEMS // C), SUBLANE)
    n_in = R // tr

    def body(a_ref, *rest):
        o_ref = rest[len(deps)]
        i = pl.program_id(0)

        @pl.when(i < n_in)
        def _():
            o_ref[...] = a_ref[...].astype(BF16)

        @pl.when(i >= n_in)
        def _():
            o_ref[...] = jnp.zeros_like(o_ref)

    return pl.pallas_call(
        body,
        name=name,
        grid=(rows_out // tr,),
        in_specs=[pl.BlockSpec((tr, C), lambda i: (jnp.minimum(i, n_in - 1), 0))] + _dep_specs(deps, 1),
        out_specs=pl.BlockSpec((tr, C), lambda i: (i, 0)),
        out_shape=jax.ShapeDtypeStruct((rows_out, C), BF16),
        compiler_params=_params(("parallel",)),
    )(a, *deps)


def _rows(a, tm, width=None, cblk=0):
    width = a.shape[1] if width is None else width
    return (a, (tm, width), lambda i, c=cblk: (i, c))


def _vec(p):
    return (p, p.shape, lambda i: (0,) * p.ndim)


def _rows_out(shape, dtype, tm):
    return (shape, dtype, (tm, shape[1]), lambda i: (i, 0))


def _vec_out(shape):
    return (shape, F32, shape, lambda i: (0,) * len(shape))


def _row_tile(T, D):
    return _pick(T, max(SUBLANE, ROW_BLOCK_ELEMS // D), SUBLANE)


def _rms(x, g):
    r = lax.rsqrt(jnp.mean(x * x, axis=-1, keepdims=True) + EPS)
    return x * r * g


def _rms_bwd(x, g, dy):
    r = lax.rsqrt(jnp.mean(x * x, axis=-1, keepdims=True) + EPS)
    xh = x * r
    dg = jnp.sum(dy * xh, axis=0, keepdims=True)
    dxh = dy * g
    dx = r * (dxh - xh * jnp.mean(dxh * xh, axis=-1, keepdims=True))
    return dx, dg


GELU_C = math.sqrt(2.0 / math.pi)
GELU_A = 0.044715


def _sigmoid(x):
    return 0.5 * jnp.tanh(0.5 * x) + 0.5


def _gelu_tanh(x):
    return jnp.tanh(GELU_C * (x + GELU_A * x * x * x))


def _gelu(x, t):
    return 0.5 * x * (1.0 + t)


def _gelu_grad(x, t):
    return 0.5 * (1.0 + t) + 0.5 * x * (1.0 - t * t) * GELU_C * (1.0 + 3.0 * GELU_A * x * x)


def _rot(w, axis=-1):
    w1, w2 = jnp.split(w, 2, axis=axis)
    return jnp.concatenate([-w2, w1], axis=axis)


def _rot_t(dw, axis=-1):
    d1, d2 = jnp.split(dw, 2, axis=axis)
    return jnp.concatenate([d2, -d1], axis=axis)


def _coords():
    return lax.axis_index("x"), lax.axis_index("y"), lax.axis_index("c")


HBM_SPEC = pl.BlockSpec(memory_space=pltpu.HBM)
SEM_SPEC = pl.BlockSpec(memory_space=pltpu.SEMAPHORE)
ANY_SPEC = pl.BlockSpec(memory_space=pl.ANY)
EFFECT = pltpu.SideEffectType.DATAFLOW_SIDE_EFFECTING
TOKEN = (SUBLANE, LANE)


def _my_slot():
    xi, yi, ci = _coords()
    return 4 * xi + 2 * yi + ci


def _own_slot_filled(block):
    land = lax.empty((N_DEV,) + block.shape, block.dtype)
    return lax.dynamic_update_slice(land, block[None], (_my_slot(),) + (0,) * block.ndim)


def _split_start(name, n, n_copies, issue, srcs, lands, deps):
    def body(*refs):
        src_refs, land_refs = refs[:n], refs[n:2 * n]
        outs = refs[2 * n + len(deps):]
        send, recv, token = outs[:n], outs[n:2 * n], outs[4 * n]
        for t in range(n):
            issue(t, src_refs[t], land_refs[t], send[t], recv[t])
        token[...] = jnp.zeros_like(token)

    sems = [pltpu.SemaphoreType.DMA((n_copies,))] * (2 * n)
    res = pl.pallas_call(
        body,
        name=name,
        in_specs=[HBM_SPEC] * (2 * n) + [ANY_SPEC] * len(deps),
        out_specs=[SEM_SPEC] * (2 * n) + [HBM_SPEC] * (2 * n) + [pl.BlockSpec(memory_space=pltpu.VMEM)],
        out_shape=sems + [pltpu.HBM(a.shape, a.dtype) for a in list(srcs) + list(lands)]
        + [jax.ShapeDtypeStruct(TOKEN, F32)],
        input_output_aliases={i: 2 * n + i for i in range(2 * n)},
        compiler_params=pltpu.CompilerParams(has_side_effects=EFFECT),
    )(*[pltpu.with_memory_space_constraint(a, pltpu.HBM) for a in list(srcs) + list(lands)], *deps)
    state = dict(n=n, n_copies=n_copies, send=res[:n], recv=res[n:2 * n], srcs=res[2 * n:3 * n],
                 lands=res[3 * n:4 * n])
    return state, res[4 * n]


def _split_wait(name, state, unit_src, after):
    n, n_copies = state["n"], state["n_copies"]

    def body(*refs):
        src_refs, land_refs = refs[:n], refs[n:2 * n]
        send, recv = refs[2 * n:3 * n], refs[3 * n:4 * n]
        me = _coords()
        for t in range(n):
            for k in range(n_copies):
                cp = pltpu.make_async_remote_copy(
                    src_ref=unit_src(src_refs[t]), dst_ref=land_refs[t].at[0], send_sem=send[t].at[k],
                    recv_sem=recv[t].at[k], device_id=me, device_id_type=MESH)
                cp.wait_send()
                cp.wait_recv()

    bufs = list(state["srcs"]) + list(state["lands"])
    res = pl.pallas_call(
        body,
        name=name,
        in_specs=[HBM_SPEC] * (2 * n) + [SEM_SPEC] * (2 * n) + [ANY_SPEC],
        out_specs=[HBM_SPEC] * (2 * n),
        out_shape=[pltpu.HBM(a.shape, a.dtype) for a in bufs],
        input_output_aliases={i: i for i in range(2 * n)},
        compiler_params=pltpu.CompilerParams(has_side_effects=EFFECT),
    )(*bufs, *state["send"], *state["recv"], after)
    return list(res[n:])


def _ag_start(name, shards, deps):
    def issue(t, x_ref, land_ref, send, recv):
        xi, yi, ci = _coords()
        to = [(xi, yi, 1 - ci), (1 - xi, yi, ci), (xi, 1 - yi, ci), (1 - xi, 1 - yi, ci)]
        for k, dev in enumerate(to):
            pltpu.make_async_remote_copy(src_ref=x_ref, dst_ref=land_ref.at[4 * xi + 2 * yi + ci], send_sem=send.at[k],
                                         recv_sem=recv.at[k], device_id=dev, device_id_type=MESH).start()

    return _split_start(name, len(shards), 4, issue, shards, [_own_slot_filled(s) for s in shards], deps)


def _ag_mid(name, state, after):
    lands = _split_wait(name + "_w1", state, lambda r: r, after)

    def issue(t, src_ref, land_ref, send, recv):
        xi, yi, ci = _coords()
        for k, (px, py) in enumerate([(1 - xi, yi), (xi, 1 - yi), (1 - xi, 1 - yi)]):
            rows = land_ref.at[4 * px + 2 * py + ci]
            pltpu.make_async_remote_copy(src_ref=rows, dst_ref=rows, send_sem=send.at[k], recv_sem=recv.at[k],
                                         device_id=(xi, yi, 1 - ci), device_id_type=MESH).start()

    n = len(lands)

    def body(*refs):
        land_refs = refs[:n]
        outs = refs[n:]
        send, recv, token = outs[:n], outs[n:2 * n], outs[3 * n]
        for t in range(n):
            issue(t, land_refs[t], land_refs[t], send[t], recv[t])
        token[...] = jnp.zeros_like(token)

    res = pl.pallas_call(
        body,
        name=name + "_s2",
        in_specs=[HBM_SPEC] * n,
        out_specs=[SEM_SPEC] * (2 * n) + [HBM_SPEC] * n + [pl.BlockSpec(memory_space=pltpu.VMEM)],
        out_shape=[pltpu.SemaphoreType.DMA((3,))] * (2 * n) + [pltpu.HBM(a.shape, a.dtype) for a in lands]
        + [jax.ShapeDtypeStruct(TOKEN, F32)],
        input_output_aliases={i: 2 * n + i for i in range(n)},
        compiler_params=pltpu.CompilerParams(has_side_effects=EFFECT),
    )(*lands)
    state2 = dict(n=n, send=res[:n], recv=res[n:2 * n], lands=res[2 * n:3 * n])
    return state2, res[3 * n]


def _ag_finish(name, state2, after):
    n = state2["n"]

    def body(*refs):
        land_refs, send, recv = refs[:n], refs[n:2 * n], refs[2 * n:3 * n]
        me = _coords()
        for t in range(n):
            for k in range(3):
                cp = pltpu.make_async_remote_copy(
                    src_ref=land_refs[t].at[0], dst_ref=land_refs[t].at[0], send_sem=send[t].at[k],
                    recv_sem=recv[t].at[k], device_id=me, device_id_type=MESH)
                cp.wait_send()
                cp.wait_recv()

    res = pl.pallas_call(
        body,
        name=name + "_w2",
        in_specs=[HBM_SPEC] * n + [SEM_SPEC] * (2 * n) + [ANY_SPEC],
        out_specs=[HBM_SPEC] * n,
        out_shape=[pltpu.HBM(a.shape, a.dtype) for a in state2["lands"]],
        input_output_aliases={i: i for i in range(n)},
        compiler_params=pltpu.CompilerParams(has_side_effects=EFFECT),
    )(*state2["lands"], *state2["send"], *state2["recv"], after)
    return list(res)


def _a2a_start(name, parts, deps):
    def issue(t, x_ref, land_ref, send, recv):
        xi, yi, ci = _coords()
        for k in range(N_DEV - 1):
            px = 1 - xi if (k + 1) & 4 else xi
            py = 1 - yi if (k + 1) & 2 else yi
            pc = 1 - ci if (k + 1) & 1 else ci
            pltpu.make_async_remote_copy(src_ref=x_ref.at[4 * px + 2 * py + pc], dst_ref=land_ref.at[4 * xi + 2 * yi + ci],
                                         send_sem=send.at[k], recv_sem=recv.at[k], device_id=(px, py, pc),
                                         device_id_type=MESH).start()

    lands = [_own_slot_filled(lax.dynamic_index_in_dim(p, _my_slot(), 0, keepdims=False)) for p in parts]
    return _split_start(name, len(parts), N_DEV - 1, issue, parts, lands, deps)


def _a2a_finish(name, state, after):
    return _split_wait(name + "_w", state, lambda r: r.at[0], after)


N_CHIP = N_DEV // 2


def _rs_pair_start(name, parts, deps):
    def issue(t, x_ref, land_ref, send, recv):
        xi, yi, ci = _coords()
        for k in range(N_CHIP):
            pltpu.make_async_remote_copy(src_ref=x_ref.at[2 * k + (1 - ci)], dst_ref=land_ref.at[k], send_sem=send.at[k],
                                         recv_sem=recv.at[k], device_id=(xi, yi, 1 - ci), device_id_type=MESH).start()

    lands = [lax.empty((N_CHIP,) + p.shape[1:], p.dtype) for p in parts]
    return _split_start(name, len(parts), N_CHIP, issue, parts, lands, deps)


def _rs_pair_sum(name, state, after):
    lands = _split_wait(name + "_w", state, lambda r: r.at[0], after)
    sums = []
    for t, (part, land) in enumerate(zip(state["srcs"], lands)):
        _, R, C = land.shape
        pack = SUBLANE * (4 // jnp.dtype(land.dtype).itemsize)
        if R % pack == 0:
            tr, tc = _pick(R, max(pack, ROW_BLOCK_ELEMS // C), pack), C
        else:
            tr, tc = R, _pick(C, max(LANE, ROW_BLOCK_ELEMS // R // LANE * LANE), LANE)

        def fn(mine, theirs):
            own = jnp.where(lax.axis_index("c") == 0, mine[0], mine[1])
            return own.astype(F32) + theirs.astype(F32)

        sums.append(_ew(f"{name}_sum{t}", fn, (N_CHIP, R // tr, C // tc),
                        [(part.reshape((N_CHIP, 2) + part.shape[1:]), (None, 2, tr, tc), lambda k, i, j: (k, 0, i, j)),
                         (land, (None, tr, tc), lambda k, i, j: (k, i, j))],
                        [(land.shape, land.dtype, (None, tr, tc), lambda k, i, j: (k, i, j))])[0])
    return sums


def _rs_chip_start(name, sums, deps):
    def issue(t, x_ref, land_ref, send, recv):
        xi, yi, ci = _coords()
        for k, (px, py) in enumerate([(1 - xi, yi), (xi, 1 - yi), (1 - xi, 1 - yi)]):
            pltpu.make_async_remote_copy(src_ref=x_ref.at[2 * px + py], dst_ref=land_ref.at[2 * xi + yi], send_sem=send.at[k],
                                         recv_sem=recv.at[k], device_id=(px, py, ci), device_id_type=MESH).start()

    xi, yi, _ = _coords()
    lands = []
    for s in sums:
        own = lax.dynamic_index_in_dim(s, 2 * xi + yi, 0, keepdims=True)
        lands.append(lax.dynamic_update_slice(lax.empty(s.shape, s.dtype), own, (2 * xi + yi,) + (0,) * (s.ndim - 1)))
    return _split_start(name, len(sums), N_CHIP - 1, issue, sums, lands, deps)


def _rs_chip_finish(name, state, after):
    return _split_wait(name + "_w", state, lambda r: r.at[0], after)


QK_SCALE = 1.0 / math.sqrt(QK_HEAD)
LOG2E = 1.0 / math.log(2.0)


def _roped_q(q_ext, cos, sin):
    q = jnp.concatenate([q_ext[:, :LANE], q_ext[:, LANE:2 * LANE] * cos + q_ext[:, 2 * LANE:] * sin], axis=1)
    return (q * (QK_SCALE * LOG2E)).astype(BF16)


def _attn_fwd(q_ext, k_full, v_aug, cosp, sinp):
    H, T, _ = k_full.shape
    tq, tkv = _pick(T, ATTN_TQ, LANE), _pick(T, ATTN_TKV, LANE)

    def body(q_ref, cos_ref, sin_ref, k_ref, v_ref, o_ref, lse_ref):
        q = _roped_q(q_ref[...], cos_ref[...], sin_ref[...])
        m = jnp.full((tq, 1), NEG_BIG, F32)
        acc = jnp.zeros((tq, 2 * V_DIM), F32)
        for c in range(T // tkv):
            keys = pl.ds(c * tkv, tkv)
            s = lax.dot_general(q, k_ref[keys, :], DN["nt"], preferred_element_type=F32)
            m_new = jnp.maximum(m, jnp.max(s, axis=-1, keepdims=True))
            p = jnp.exp2(s - m_new).astype(BF16)
            acc = jnp.exp2(m - m_new) * acc + jnp.dot(p, v_ref[keys, :], preferred_element_type=F32)
            m = m_new
        l = acc[:, V_DIM:]
        o_ref[...] = acc[:, :V_DIM] / l
        lse_ref[...] = m + jnp.log2(jnp.max(l, axis=-1, keepdims=True))

    return pl.pallas_call(
        body,
        name="attn_fwd",
        grid=(H, T // tq),
        in_specs=[pl.BlockSpec((tq, Q_EXT), lambda h, i: (i, h)),
                  pl.BlockSpec((tq, LANE), lambda h, i: (i, 0)),
                  pl.BlockSpec((tq, LANE), lambda h, i: (i, 0)),
                  pl.BlockSpec((None, T, K_FULL), lambda h, i: (h, 0, 0)),
                  pl.BlockSpec((None, T, 2 * V_DIM), lambda h, i: (h, 0, 0))],
        out_specs=[pl.BlockSpec((tq, V_DIM), lambda h, i: (i, h)),
                   pl.BlockSpec((None, tq, 1), lambda h, i: (h, i, 0))],
        out_shape=[jax.ShapeDtypeStruct((T, H * V_DIM), F32), jax.ShapeDtypeStruct((H, T, 1), F32)],
        compiler_params=_params(("parallel", "parallel")),
    )(q_ext, cosp, sinp, k_full, v_aug)


def _attn_bwd(q_ext, k_full, v_aug, cosp, sinp, d_out, out, lse_rows):
    H, T, _ = k_full.shape
    tq, tkv = _pick(T, ATTN_TQ, LANE), _pick(T, ATTN_TKV, LANE)
    n_q = T // tq

    def body(q_ref, cos_ref, sin_ref, k_ref, v_ref, do_ref, o_ref, lse_ref, cos_all, sin_all,
             dq_ref, dkv_ref, dkr_ref, dk_acc, dv_acc):
        h, i = pl.program_id(0), pl.program_id(1)

        @pl.when(i == 0)
        def _():
            dk_acc[...] = jnp.zeros_like(dk_acc)
            dv_acc[...] = jnp.zeros_like(dv_acc)

        cos, sin = cos_ref[...], sin_ref[...]
        q, do, lse = _roped_q(q_ref[...], cos, sin), do_ref[...], lse_ref[...]
        delta = lax.dot_general(jnp.ones((SUBLANE, V_DIM), F32), do.astype(F32) * o_ref[...], DN["nt"],
                                preferred_element_type=F32, precision=lax.Precision.HIGHEST)[0:1]
        dq = jnp.zeros((tq, K_FULL), F32)
        for c in range(T // tkv):
            keys = pl.ds(c * tkv, tkv)
            kc = k_ref[keys, :]
            pt = jnp.exp2(lax.dot_general(kc, q, DN["nt"], preferred_element_type=F32) - lse)
            dpt = lax.dot_general(v_ref[keys, pl.ds(0, V_DIM)], do, DN["nt"], preferred_element_type=F32)
            dst = (pt * (dpt - delta)).astype(BF16)
            dv_acc[keys, :] += jnp.dot(pt.astype(BF16), do, preferred_element_type=F32)
            dk_acc[keys, :] += jnp.dot(dst, q, preferred_element_type=F32)
            dq = dq + lax.dot_general(dst, kc, DN["tn"], preferred_element_type=F32)
        r = dq[:, LANE:]
        dq_ref[...] = (jnp.concatenate([dq[:, :LANE], r * cos, r * sin], axis=1) * QK_SCALE).astype(BF16)

        @pl.when(i == n_q - 1)
        def _():
            dk = dk_acc[...] * (1.0 / LOG2E)
            dkv_ref[...] = jnp.concatenate([dk[:, :LANE], dv_acc[...]], axis=1).astype(BF16)
            rk = dk[:, LANE:]
            dkr = jnp.concatenate([rk * cos_all[...], rk * sin_all[...]], axis=1)

            @pl.when(h == 0)
            def _():
                dkr_ref[...] = dkr

            @pl.when(h != 0)
            def _():
                dkr_ref[...] += dkr

    return pl.pallas_call(
        body,
        name="attn_bwd",
        grid=(H, n_q),
        in_specs=[pl.BlockSpec((tq, Q_EXT), lambda h, i: (i, h)),
                  pl.BlockSpec((tq, LANE), lambda h, i: (i, 0)),
                  pl.BlockSpec((tq, LANE), lambda h, i: (i, 0)),
                  pl.BlockSpec((None, T, K_FULL), lambda h, i: (h, 0, 0)),
                  pl.BlockSpec((None, T, 2 * V_DIM), lambda h, i: (h, 0, 0)),
                  pl.BlockSpec((tq, V_DIM), lambda h, i: (i, h)),
                  pl.BlockSpec((tq, V_DIM), lambda h, i: (i, h)),
                  pl.BlockSpec((None, 1, tq), lambda h, i: (h, 0, i)),
                  pl.BlockSpec((T, LANE), lambda h, i: (0, 0)),
                  pl.BlockSpec((T, LANE), lambda h, i: (0, 0))],
        out_specs=[pl.BlockSpec((tq, Q_EXT), lambda h, i: (i, h)),
                   pl.BlockSpec((T, K_FULL), lambda h, i: (0, h)),
                   pl.BlockSpec((T, K_FULL), lambda h, i: (0, 0))],
        out_shape=[jax.ShapeDtypeStruct((T, H * Q_EXT), BF16), jax.ShapeDtypeStruct((T, H * K_FULL), BF16),
                   jax.ShapeDtypeStruct((T, K_FULL), F32)],
        scratch_shapes=[pltpu.VMEM((T, K_FULL), F32), pltpu.VMEM((T, V_DIM), F32)],
        compiler_params=_params(("arbitrary", "arbitrary")),
    )(q_ext, cosp, sinp, k_full, v_aug, d_out, out, lse_rows, cosp, sinp)


def _sgu_forward_math(pb, lng, lnb, ws, bsb, gw, gh):
    u_raw, v_raw = pb[:, :gw], pb[:, gw:]
    t_u, t_v = _gelu_tanh(u_raw), _gelu_tanh(v_raw)
    u, v = _gelu(u_raw, t_u), _gelu(v_raw, t_v)
    vc = v - jnp.mean(v, axis=-1, keepdims=True)
    rstd = lax.rsqrt(jnp.mean(vc * vc, axis=-1, keepdims=True) + EPS)
    vhat = vc * rstd
    vln = vhat * lng + lnb
    hd = gw // gh
    s = jnp.concatenate(
        [jnp.dot(ws[g], vln[:, g * hd:(g + 1) * hd].astype(BF16), preferred_element_type=F32) + bsb[g]
         for g in range(gh)], axis=1)
    return (u_raw, t_u), (v_raw, t_v), u, rstd, vhat, vln, s


def _sgu_fwd(proj_b, lng, lnb, ws, bsb, gg):
    T, gw2 = proj_b.shape
    gw, gh = gw2 // 2, ws.shape[0]

    def fn(pb, lng, lnb, ws, bsb, gg):
        _, _, u, _, _, _, s = _sgu_forward_math(pb, lng, lnb, ws, bsb, gw, gh)
        return _rms(u * s, gg)

    return _ew("sgu_fwd", fn, (T // CHUNK,),
               [_rows(proj_b, CHUNK), _vec(lng), _vec(lnb), _vec(ws), _vec(bsb), _vec(gg)],
               [_rows_out((T, gw), BF16, CHUNK)])[0]


def _sgu_bwd(proj_b, d_mixed, lng, lnb, ws, ws_t, bsb, gg):
    T, gw2 = proj_b.shape
    gw, gh = gw2 // 2, ws.shape[0]
    hd = gw // gh

    def fn(pb, dmg, lng, lnb, ws, ws_t, bsb, gg):
        u_pre, v_pre, u, rstd, vhat, vln, s = _sgu_forward_math(pb, lng, lnb, ws, bsb, gw, gh)
        dgo, dgg = _rms_bwd(u * s, gg, dmg)
        du, ds = dgo * s, dgo * u
        ones = jnp.ones((SUBLANE, hd), F32)
        dws, dbs, dvln = [], [], []
        for g in range(gh):
            ds_g = ds[:, g * hd:(g + 1) * hd]
            ds_gb = ds_g.astype(BF16)
            dws.append(lax.dot_general(ds_gb, vln[:, g * hd:(g + 1) * hd].astype(BF16), DN["nt"],
                                       preferred_element_type=F32)[None])
            dbs.append(lax.dot_general(ones, ds_g, DN["nt"], preferred_element_type=F32,
                                       precision=lax.Precision.HIGHEST)[None])
            dvln.append(jnp.dot(ws_t[g], ds_gb, preferred_element_type=F32))
        dvln = jnp.concatenate(dvln, axis=1)
        dlng = jnp.sum(dvln * vhat, axis=0, keepdims=True)
        dlnb = jnp.sum(dvln, axis=0, keepdims=True)
        dvh = dvln * lng
        dv = rstd * (dvh - jnp.mean(dvh, axis=-1, keepdims=True) - vhat * jnp.mean(dvh * vhat, axis=-1, keepdims=True))
        d_pb = jnp.concatenate([du * _gelu_grad(*u_pre), dv * _gelu_grad(*v_pre)], axis=1)
        return d_pb, dgg, jnp.concatenate(dws, axis=0), jnp.concatenate(dbs, axis=0), dlng, dlnb

    return _ew("sgu_bwd", fn, (T // CHUNK,),
               [_rows(proj_b, CHUNK), _rows(d_mixed, CHUNK, gw, 1), _vec(lng), _vec(lnb), _vec(ws), _vec(ws_t),
                _vec(bsb), _vec(gg)],
               [_rows_out((T, gw2), BF16, CHUNK)],
               [_vec_out((1, gw)), _vec_out((gh, CHUNK, CHUNK)), _vec_out((gh, SUBLANE, CHUNK)), _vec_out((1, gw)),
                _vec_out((1, gw))])


def _adamw(name, parts, w, m, v):
    R, C = w.shape
    pack = SUBLANE * (4 // jnp.dtype(parts.dtype).itemsize)
    if R % pack == 0:
        tm, tc = _pick(R, max(pack, ADAM_BLOCK_ELEMS // C), pack), C
    else:
        parts = parts[:, :R]
        tm, tc = R, _pick(C, max(LANE, ADAM_BLOCK_ELEMS // R // LANE * LANE), LANE)

    n_parts = parts.shape[0]

    def fn(p, w, m, v):
        g = p[0].astype(F32)
        for i in range(1, n_parts):
            g = g + p[i].astype(F32)
        m2 = ADAM_B1 * m + (1.0 - ADAM_B1) * g
        v2 = ADAM_B2 * v + (1.0 - ADAM_B2) * (g * g)
        m_hat = m2 / (1.0 - ADAM_B1 ** ADAM_STEP)
        v_hat = v2 / (1.0 - ADAM_B2 ** ADAM_STEP)
        delta = -ADAM_LR * (m_hat / (jnp.sqrt(v_hat) + ADAM_EPS) + ADAM_WD * w)
        return g, delta, m2, v2

    blk = lambda a: (a, (tm, tc), lambda i, j: (i, j))
    return _ew(name, fn, (R // tm, C // tc),
               [(parts, (n_parts, tm, tc), lambda i, j: (0, i, j)), blk(w), blk(m), blk(v)],
               [((R, C), F32, (tm, tc), lambda i, j: (i, j))] * 4)


def kernel(x, positions, pre_mix_norm, w_in, q_norm, kv_norm, w_uq, w_ukv, v_ln_gain, v_ln_bias, w_spatial, b_spatial, attn_out_norm, gmlp_out_norm, w_out, post_mix_norm, pre_ffn_norm, w_gate, w_up, w_down, post_ffn_norm, loss_target, m_pre_mix_norm, m_w_in, m_q_norm, m_kv_norm, m_w_uq, m_w_ukv, m_v_ln_gain, m_v_ln_bias, m_w_spatial, m_b_spatial, m_attn_out_norm, m_gmlp_out_norm, m_w_out, m_post_mix_norm, m_pre_ffn_norm, m_w_gate, m_w_up, m_w_down, m_post_ffn_norm, v_pre_mix_norm, v_w_in, v_q_norm, v_kv_norm, v_w_uq, v_w_ukv, v_v_ln_gain, v_v_ln_bias, v_w_spatial, v_b_spatial, v_attn_out_norm, v_gmlp_out_norm, v_w_out, v_post_mix_norm, v_pre_ffn_norm, v_w_gate, v_w_up, v_w_down, v_post_ffn_norm):
    given = dict(locals())
    T, D = x.shape[1], x.shape[2]
    QL, KVL = w_uq.shape[1], w_ukv.shape[1]
    GW, GH = v_ln_gain.shape[1], w_spatial.shape[1]
    IWS = w_in.shape[2]
    IW = IWS * N_DEV
    H = w_ukv.shape[2] * N_DEV // (NOPE_DIM + V_DIM)
    AW = H * V_DIM
    FS = w_gate.shape[2]
    FP = _round_up(FS, LANE)
    C0 = QL + KVL
    assert IW == C0 + ROPE_DIM + 2 * GW and AW + GW == D and AW == GW
    assert QL % KVL == 0 and C0 % K_FULL == 0 and w_spatial.shape[2] == CHUNK and T % CHUNK == 0
    assert w_uq.shape[2] * N_DEV == H * QK_HEAD and D % N_DEV == 0

    x2, tgt = x[0], loss_target[0]
    tm_d = _row_tile(T, D)

    inv_freq = 1.0 / (ROPE_THETA ** (jnp.arange(0, ROPE_DIM, 2, dtype=F32) / ROPE_DIM))
    ang = positions[0].astype(F32)[:, None] * inv_freq
    zpad = jnp.zeros((T, LANE - ROPE_DIM), F32)
    cosp = jnp.concatenate([jnp.cos(ang), jnp.cos(ang), zpad], axis=1)
    sinp = jnp.concatenate([jnp.sin(ang), jnp.sin(ang), zpad], axis=1)

    def shard_t(a):
        return a[0].T

    st_a, tok = _ag_start("ag_a", [shard_t(w_in).astype(BF16), w_uq[0].astype(BF16), w_ukv[0].astype(BF16)], [])
    st_o, tok = _ag_start("ag_o", [w_out[0].astype(BF16)], [tok])
    st_g, tok = _ag_start("ag_g", [_cast_pad_rows("pad_w_gate", shard_t(w_gate), FP, [tok])], [tok])
    st_u, tok = _ag_start("ag_u", [_cast_pad_rows("pad_w_up", shard_t(w_up), FP, [tok])], [tok])
    st_d, tok = _ag_start("ag_d", [_cast_pad_rows("pad_w_down", w_down[0], FP, [tok])], [tok])

    xn = _ew("rms_x", lambda x, g: _rms(x, g), (T // tm_d,), [_rows(x2, tm_d), _vec(pre_mix_norm)],
             [_rows_out((T, D), BF16, tm_d)], deps=[tok])[0]
    st_a, tok = _ag_mid("ag_a", st_a, xn)
    g_in, g_uq, g_ukv = _ag_finish("ag_a", st_a, tok)

    w_in_t = g_in.reshape(IW, D)
    kr_w = w_in_t[C0:C0 + ROPE_DIM]
    z64 = jnp.zeros((LANE - ROPE_DIM, D), BF16)
    w_a_t = jnp.concatenate([w_in_t[:C0], kr_w, z64, _rot(kr_w, 0), z64], axis=0)
    w_b_t = w_in_t[C0 + ROPE_DIM:]
    wq = g_uq.transpose(1, 0, 2).reshape(QL, H, QK_HEAD)
    zq = jnp.zeros((QL, H, LANE - ROPE_DIM), BF16)
    wq_ext = jnp.concatenate([wq[..., :NOPE_DIM], wq[..., NOPE_DIM:], zq, _rot(wq[..., NOPE_DIM:]), zq],
                             axis=-1).reshape(QL, H * Q_EXT)
    wkv = g_ukv.transpose(1, 0, 2).reshape(KVL, H * (NOPE_DIM + V_DIM))

    ws = w_spatial[0].astype(BF16)
    ws_t = jnp.swapaxes(ws, 1, 2)
    bsb = jnp.broadcast_to(b_spatial[0][:, :, None], (GH, CHUNK, GW // GH))

    proj_a = _mm2("mm_proj_a", "nt", xn, w_a_t, F32, tn_pref=w_a_t.shape[0])
    proj_b = _mm2("mm_proj_b", "nt", xn, w_b_t, F32)

    tm_q = _row_tile(T, QL)
    qn = _ew("rms_q", lambda x, g: _rms(x, g), (T // tm_q,), [_rows(proj_a, tm_q, QL, 0), _vec(q_norm)],
             [_rows_out((T, QL), BF16, tm_q)])[0]
    kvn = _ew("rms_kv", lambda x, g: _rms(x, g), (T // tm_q,), [_rows(proj_a, tm_q, KVL, QL // KVL), _vec(kv_norm)],
              [_rows_out((T, KVL), BF16, tm_q)])[0]
    q_ext = _mm2("mm_q", "nn", qn, wq_ext, F32)
    kv = _mm2("mm_kv", "nn", kvn, wkv, F32)

    tm_h = _pick(T, 512, SUBLANE)
    hb = _pick(H, 4, 1)

    def make_kv(kv, kr, c, s):
        k_rope = kr[:, :LANE] * c + kr[:, LANE:] * s
        ones = jnp.ones((kv.shape[0], V_DIM), F32)
        ks = [jnp.concatenate([kv[:, j * K_FULL:j * K_FULL + LANE], k_rope], axis=1) for j in range(hb)]
        vs = [jnp.concatenate([kv[:, j * K_FULL + LANE:(j + 1) * K_FULL], ones], axis=1) for j in range(hb)]
        return jnp.stack(ks), jnp.stack(vs)

    k_full, v_aug = _ew("make_kv", make_kv, (T // tm_h, H // hb),
                        [(kv, (tm_h, hb * K_FULL), lambda i, h: (i, h)),
                         (proj_a, (tm_h, K_FULL), lambda i, h: (i, C0 // K_FULL)),
                         (cosp, (tm_h, LANE), lambda i, h: (i, 0)), (sinp, (tm_h, LANE), lambda i, h: (i, 0))],
                        [((H, T, K_FULL), BF16, (hb, tm_h, K_FULL), lambda i, h: (h, i, 0)),
                         ((H, T, 2 * V_DIM), BF16, (hb, tm_h, 2 * V_DIM), lambda i, h: (h, i, 0))])

    a_out, lse = _attn_fwd(q_ext, k_full, v_aug, cosp, sinp)
    st_o, tok_o = _ag_mid("ag_o", st_o, a_out)
    tm_a = _row_tile(T, AW)
    mixed_a = _ew("rms_attn", lambda x, g: _rms(x, g), (T // tm_a,), [_rows(a_out, tm_a), _vec(attn_out_norm)],
                  [_rows_out((T, AW), BF16, tm_a)], deps=[tok_o])[0]
    mixed_g = _sgu_fwd(proj_b, v_ln_gain, v_ln_bias, ws, bsb, gmlp_out_norm)
    wo = _ag_finish("ag_o", st_o, mixed_g)[0].reshape(D, D)
    st_g, tok_gu = _ag_mid("ag_g", st_g, wo)

    tm_, tn_ = _pick(T, 1024, LANE), _pick(D, 1024, LANE)
    tk_ = _fit_tk(AW, tm_, tn_, pairs=2)
    nka = AW // tk_
    mix_out = _mm("mm_out", "nn",
                  [(mixed_a, (tm_, tk_), lambda i, j, k: (i, k), wo, (tk_, tn_), lambda i, j, k: (k, j)),
                   (mixed_g, (tm_, tk_), lambda i, j, k: (i, k), wo, (tk_, tn_), lambda i, j, k: (k + nka, j))],
                  (T // tm_, D // tn_, nka), (tm_, tn_), (T, D), F32, (tm_, tn_), lambda i, j, k: (i, j), deps=[tok_gu])

    def post_mix(x, mo, g3, g4):
        h = x + _rms(mo, g3)
        return h, _rms(h, g4)

    h_res, hn = _ew("post_mix", post_mix, (T // tm_d,),
                    [_rows(x2, tm_d), _rows(mix_out, tm_d), _vec(post_mix_norm), _vec(pre_ffn_norm)],
                    [_rows_out((T, D), F32, tm_d), _rows_out((T, D), BF16, tm_d)])

    F = N_DEV * FP
    tn_f = _pick(F, FFN_TILE_MAX, MXU_DIM)
    tn_e = _pick(F, FFN_TILE_MAX // 2, LANE)

    def swiglu(u, g):
        g = g.astype(F32)
        return u, g * _sigmoid(g) * u

    st_u, tok_u = _ag_mid("ag_u", st_u, hn)
    wg_t = _ag_finish("ag_g", st_g, tok_u)[0].reshape(F, D)
    gate = _mm2("mm_gate", "nt", hn, wg_t, BF16, tn_pref=tn_f)
    wu_t = _ag_finish("ag_u", st_u, gate)[0].reshape(F, D)
    st_d, tok_d = _ag_mid("ag_d", st_d, wu_t)
    up, act = _mm2("mm_up", "nt", hn, wu_t, BF16, tn_pref=tn_e, deps=[tok_d], extras=[gate],
                   epilogue=swiglu, n_out=2)
    wd_f = _ag_finish("ag_d", st_d, act)[0].reshape(F, D)
    ffn = _mm2("mm_down", "nn", act, wd_f, F32, tk_max=tn_f)

    def loss_head(h, f, t, g5):
        y = h + _rms(f, g5)
        diff = y - t
        dy = diff / D
        d_f, dg5 = _rms_bwd(f, g5, dy)
        lsum = jnp.sum(jnp.sum(diff * diff, axis=0, keepdims=True), axis=1, keepdims=True) * (0.5 / D)
        return dy, d_f, dg5, lsum

    dy, d_ffn, dg_post_ffn, loss_part = _ew(
        "loss_head", loss_head, (T // tm_d,),
        [_rows(h_res, tm_d), _rows(ffn, tm_d), _rows(tgt, tm_d), _vec(post_ffn_norm)],
        [_rows_out((T, D), F32, tm_d), _rows_out((T, D), BF16, tm_d)], [_vec_out((1, D)), _vec_out((1, 1))])

    def swiglu_bwd(da, g, u):
        g, u = g.astype(F32), u.astype(F32)
        sg = _sigmoid(g)
        return da * u * sg * (1.0 + g * (1.0 - sg)), da * g * sg

    def ffn_dw(name, z, y, deps):
        return _mm2(name, "tn", z, y, BF16, tm_pref=tn_f, deps=deps).reshape(N_DEV, FP, D)

    x_d, tok = _a2a_start("a2a_d", [ffn_dw("mm_dwd", act, d_ffn, [])], [])
    d_gate, d_up = _mm2("mm_dact", "nt", d_ffn, wd_f, BF16, tn_pref=tn_e, deps=[tok], extras=[gate, up],
                        epilogue=swiglu_bwd, n_out=2)
    x_g, tok = _a2a_start("a2a_g", [ffn_dw("mm_dwg", d_gate, hn, [])], [])
    x_u, tok = _a2a_start("a2a_u", [ffn_dw("mm_dwu", d_up, hn, [tok])], [])
    d_hn_g = _mm2("mm_dhn_g", "nn", d_gate, wg_t, F32, tk_max=tn_f, deps=[tok])
    d_hn = _mm2("mm_dhn_u", "nn", d_up, wu_t, F32, tk_max=tn_f, extras=[d_hn_g], epilogue=lambda s, g: (s + g,))

    def pre_ffn_bwd(h, dhn, dy, mo, g4, g3):
        d1, dg4 = _rms_bwd(h, g4, dhn)
        dh = dy + d1
        dmo, dg3 = _rms_bwd(mo, g3, dh)
        return dh, dmo, dg4, dg3

    dh, d_mo, dg_pre_ffn, dg_post_mix = _ew(
        "pre_ffn_bwd", pre_ffn_bwd, (T // tm_d,),
        [_rows(h_res, tm_d), _rows(d_hn, tm_d), _rows(dy, tm_d), _rows(mix_out, tm_d), _vec(pre_ffn_norm),
         _vec(post_mix_norm)],
        [_rows_out((T, D), F32, tm_d), _rows_out((T, D), BF16, tm_d)], [_vec_out((1, D)), _vec_out((1, D))])

    d_mixed = _mm2("mm_dmixed", "nt", d_mo, wo, F32)
    dwo = jnp.concatenate([_mm2("mm_dwo_a", "tn", mixed_a, d_mo, BF16), _mm2("mm_dwo_g", "tn", mixed_g, d_mo, BF16)],
                          axis=0)
    x_o, tok = _a2a_start("a2a_o", [dwo.reshape(N_DEV, D // N_DEV, D)], [])

    d_a_out, dg_attn = _ew("rms_attn_bwd", lambda a, g, d: _rms_bwd(a, g, d), (T // tm_a,),
                           [_rows(a_out, tm_a), _vec(attn_out_norm), _rows(d_mixed, tm_a, AW, 0)],
                           [_rows_out((T, AW), BF16, tm_a)], [_vec_out((1, AW))], deps=[tok])
    d_pb, dg_gmlp, dws, dbs, dlng, dlnb = _sgu_bwd(proj_b, d_mixed, v_ln_gain, v_ln_bias, ws, ws_t, bsb, gmlp_out_norm)

    dq_ext, d_kv, d_kr = _attn_bwd(q_ext, k_full, v_aug, cosp, sinp, d_a_out, a_out, lse.reshape(H, 1, T))

    d_qn = _mm2("mm_dqn", "nt", dq_ext, wq_ext, F32)
    dwq_ext = _mm2("mm_dwq", "tn", qn, dq_ext, F32)
    d_kvn = _mm2("mm_dkvn", "nt", d_kv, wkv, F32)
    dwkv = _mm2("mm_dwkv", "tn", kvn, d_kv, BF16)

    d_qc, dg_q = _ew("rms_q_bwd", lambda a, g, d: _rms_bwd(a, g, d), (T // tm_q,),
                     [_rows(proj_a, tm_q, QL, 0), _vec(q_norm), _rows(d_qn, tm_q)],
                     [_rows_out((T, QL), BF16, tm_q)], [_vec_out((1, QL))])
    d_kvc, dg_kv = _ew("rms_kv_bwd", lambda a, g, d: _rms_bwd(a, g, d), (T // tm_q,),
                       [_rows(proj_a, tm_q, KVL, QL // KVL), _vec(kv_norm), _rows(d_kvn, tm_q)],
                       [_rows_out((T, KVL), BF16, tm_q)], [_vec_out((1, KVL))])
    d_pa = jnp.concatenate([d_qc, d_kvc, d_kr.astype(BF16)], axis=1)

    dwa_t = _mm2("mm_dwa", "tn", d_pa, xn, F32, tm_pref=w_a_t.shape[0])
    dwb_t = _mm2("mm_dwb", "tn", d_pb, xn, BF16)
    dkr_w = dwa_t[C0:C0 + ROPE_DIM] + _rot_t(dwa_t[C0 + LANE:C0 + LANE + ROPE_DIM], 0)
    dw_in = jnp.concatenate([dwa_t[:C0].astype(BF16), dkr_w.astype(BF16), dwb_t], axis=0)
    dw_in = dw_in.reshape(N_DEV, IWS, D)
    dq3 = dwq_ext.reshape(QL, H, Q_EXT)
    dwq = jnp.concatenate([dq3[..., :NOPE_DIM],
                           dq3[..., LANE:LANE + ROPE_DIM] + _rot_t(dq3[..., 2 * LANE:2 * LANE + ROPE_DIM])], axis=-1)
    dwq = dwq.astype(BF16).reshape(QL, N_DEV, H * QK_HEAD // N_DEV).transpose(1, 0, 2)
    dwkv = dwkv.reshape(KVL, N_DEV, w_ukv.shape[2]).transpose(1, 0, 2)
    p_e, tok = _rs_pair_start("rs_e", [dw_in, dwq, dwkv], [])
    d_xn_a = _mm2("mm_dxn_a", "nn", d_pa, w_a_t, F32, deps=[tok])
    x_e, tok = _rs_chip_start("rs_e2", _rs_pair_sum("rs_e", p_e, d_xn_a), [])
    d_xn_b = _mm2("mm_dxn_b", "nn", d_pb, w_b_t, F32, deps=[tok])

    def x_bwd(x, da, db, dh, g0):
        dx, dg0 = _rms_bwd(x, g0, da + db)
        return dh + dx, dg0

    grad_x, dg_pre_mix = _ew("x_bwd", x_bwd, (T // tm_d,),
                             [_rows(x2, tm_d), _rows(d_xn_a, tm_d), _rows(d_xn_b, tm_d), _rows(dh, tm_d),
                              _vec(pre_mix_norm)],
                             [_rows_out((T, D), F32, tm_d)], [_vec_out((1, D))])

    def adam_big(n, parts, transposed=False):
        view = shard_t if transposed else (lambda a: a[0])
        res = _adamw("adamw_" + n, parts, view(given[n]), view(given["m_" + n]), view(given["v_" + n]))
        done[n] = res[0]
        return [(r.T if transposed else r)[None] for r in res]

    done = {}

    small_names = ["pre_mix_norm", "q_norm", "kv_norm", "v_ln_gain", "v_ln_bias", "w_spatial", "b_spatial",
                   "attn_out_norm", "gmlp_out_norm", "post_mix_norm", "pre_ffn_norm", "post_ffn_norm"]
    small_grads = [dg_pre_mix, dg_q, dg_kv, dlng, dlnb, dws, dbs[:, 0, :], dg_attn, dg_gmlp, dg_post_mix, dg_pre_ffn,
                   dg_post_ffn]
    n_small = sum(g.size for g in small_grads)
    pad_small = _round_up(n_small, SUBLANE * LANE) - n_small

    def pack(arrs):
        flat = jnp.concatenate([a.reshape(-1) for a in arrs] + [jnp.zeros((pad_small,), F32)])
        return flat.reshape(-1, LANE)

    st_s, tok = _ag_start("ag_s", [pack(small_grads)], [])

    results = {}
    results["w_down"] = adam_big("w_down", _a2a_finish("a2a_d", x_d, tok)[0])
    results["w_gate"] = adam_big("w_gate", _a2a_finish("a2a_g", x_g, done["w_down"])[0], True)
    st_s, tok = _ag_mid("ag_s", st_s, done["w_gate"])
    results["w_up"] = adam_big("w_up", _a2a_finish("a2a_u", x_u, tok)[0], True)
    results["w_out"] = adam_big("w_out", _a2a_finish("a2a_o", x_o, done["w_up"])[0])
    e_in, e_uq, e_ukv = _rs_chip_finish("rs_e2", x_e, done["w_out"])
    results["w_in"] = adam_big("w_in", e_in, True)
    results["w_uq"] = adam_big("w_uq", e_uq)
    results["w_ukv"] = adam_big("w_ukv", e_ukv)

    small_parts = _ag_finish("ag_s", st_s, done["w_ukv"])[0]
    sm = _adamw("adamw_small", small_parts, pack([given[n] for n in small_names]),
                pack([given["m_" + n] for n in small_names]), pack([given["v_" + n] for n in small_names]))

    off = 0
    for n in small_names:
        shp, size = given[n].shape, given[n].size
        results[n] = [r.reshape(-1)[off:off + size].reshape(shp) for r in sm]
        off += size

    loss = lax.psum(loss_part[0, 0], MESH_AXES)
    order = ["pre_mix_norm", "w_in", "q_norm", "kv_norm", "w_uq", "w_ukv", "v_ln_gain", "v_ln_bias", "w_spatial",
             "b_spatial", "attn_out_norm", "gmlp_out_norm", "w_out", "post_mix_norm", "pre_ffn_norm", "w_gate", "w_up",
             "w_down", "post_ffn_norm"]
    out = [loss, grad_x[None]]
    for kind in range(4):
        out += [results[n][kind] for n in order]
    return tuple(out)
```

```python
import math

import jax
import jax.numpy as jnp
from jax import lax
from jax.experimental import pallas as pl
from jax.experimental.pallas import tpu as pltpu

F32 = jnp.float32
BF16 = jnp.bfloat16
N_DEV = 8
MESH_AXES = ("x", "y", "c")
MESH = pl.DeviceIdType.MESH

NOPE_DIM = 128
ROPE_DIM = 64
V_DIM = 128
ROPE_THETA = 10000.0
CHUNK = 128
EPS = 1e-6
QK_HEAD = NOPE_DIM + ROPE_DIM
Q_EXT = 3 * 128
K_FULL = 2 * 128

ADAM_LR = 0.001
ADAM_B1 = 0.9
ADAM_B2 = 0.999
ADAM_EPS = 1e-08
ADAM_WD = 0.01
ADAM_STEP = 10

LANE = 128
SUBLANE = 8
VMEM_LIMIT = 60 * 1024 * 1024
MM_VMEM_BUDGET = 50 * 1024 * 1024
MXU_DIM = 256
FFN_TILE_MAX = 3072
MM_TK_MAX = 4096
EPILOGUE_ROWS = 128
ATTN_TQ = 1024
ATTN_TKV = 512
NEG_BIG = -1e30
ROW_BLOCK_ELEMS = 1024 * 1024
ADAM_BLOCK_ELEMS = 512 * 1024

DN = {
    "nn": (((1,), (0,)), ((), ())),
    "nt": (((1,), (1,)), ((), ())),
    "tn": (((0,), (0,)), ((), ())),
}


def _pick(dim, pref, mult):
    t = min(pref, dim)
    t -= t % mult
    while t >= mult:
        if dim % t == 0:
            return t
        t -= mult
    return dim


def _round_up(n, m):
    return (n + m - 1) // m * m


def _params(sem):
    return pltpu.CompilerParams(dimension_semantics=sem, vmem_limit_bytes=VMEM_LIMIT)


def _dep_specs(deps, nd):
    return [pl.BlockSpec(d.shape, lambda *_, r=d.ndim: (0,) * r) for d in deps]


def _mm(name, mode, pairs, grid, acc_shape, out_shape, out_dtype, out_block, out_map, deps=(), extras=(),
        epilogue=None, n_out=1):
    n, ne, nx = len(pairs), len(extras), len(deps)
    nk = grid[-1]
    nd = len(grid)
    dn = DN[mode]
    use_acc = nk > 1 or epilogue is not None

    def body(*refs):
        e_refs = refs[2 * n:2 * n + ne]
        o_refs = refs[2 * n + ne + nx:2 * n + ne + nx + n_out]
        k = pl.program_id(nd - 1)

        def dot(i):
            return lax.dot_general(refs[2 * i][...], refs[2 * i + 1][...], dn, preferred_element_type=F32)

        if not use_acc:
            s = dot(0)
            for i in range(1, n):
                s = s + dot(i)
            o_refs[0][...] = s.astype(o_refs[0].dtype)
            return
        acc = refs[2 * n + ne + nx + n_out]

        fused_last = epilogue is not None and n == 1 and nk > 1 and mode in ("nn", "nt")

        @pl.when(k == 0)
        def _():
            acc[...] = dot(0)

        if fused_last:
            if nk > 2:
                @pl.when(jnp.logical_and(k > 0, k < nk - 1))
                def _():
                    acc[...] += dot(0)

            @pl.when(k == nk - 1)
            def _():
                rows = acc_shape[0]
                slab = _pick(rows, MXU_DIM, SUBLANE)
                for r0 in range(0, rows, slab):
                    sl = pl.ds(r0, slab)
                    part = lax.dot_general(refs[0][sl, :], refs[1][...], dn, preferred_element_type=F32)
                    vals = epilogue(acc[sl, :] + part, *[e[sl, :] for e in e_refs])
                    for r, v in zip(o_refs, vals):
                        r[sl, :] = v.astype(r.dtype)
            return

        if nk > 1:
            @pl.when(k > 0)
            def _():
                acc[...] += dot(0)

        for i in range(1, n):
            acc[...] += dot(i)

        if epilogue is None:
            @pl.when(k == nk - 1)
            def _():
                o_refs[0][...] = acc[...].astype(o_refs[0].dtype)
        else:
            @pl.when(k == nk - 1)
            def _():
                rows = acc_shape[0]
                slab = _pick(rows, EPILOGUE_ROWS, SUBLANE)
                for r0 in range(0, rows, slab):
                    sl = pl.ds(r0, slab)
                    vals = epilogue(acc[sl, :], *[e[sl, :] for e in e_refs])
                    for r, v in zip(o_refs, vals):
                        r[sl, :] = v.astype(r.dtype)

    in_specs, args = [], []
    for a, a_blk, a_map, b, b_blk, b_map in pairs:
        in_specs += [pl.BlockSpec(a_blk, a_map), pl.BlockSpec(b_blk, b_map)]
        args += [a, b]
    in_specs += [pl.BlockSpec(blk, imap) for (_, blk, imap) in extras]
    args += [e for (e, _, _) in extras]
    if not isinstance(out_dtype, (tuple, list)):
        out_dtype = (out_dtype,) * n_out
    res = pl.pallas_call(
        body,
        name=name,
        grid=grid,
        in_specs=in_specs + _dep_specs(deps, nd),
        out_specs=[pl.BlockSpec(out_block, out_map)] * n_out,
        out_shape=[jax.ShapeDtypeStruct(out_shape, dt) for dt in out_dtype],
        scratch_shapes=[pltpu.VMEM(acc_shape, F32)] if use_acc else [],
        compiler_params=_params(("parallel",) * (nd - 1) + ("arbitrary",)),
    )(*args, *deps)
    return res[0] if n_out == 1 else res


def _fit_tk(K, tm, tn, pairs=1, out_bytes=4, fixed_extra=0, tk_max=MM_TK_MAX, needs_acc=False):
    fixed = tm * tn * 4 + 2 * tm * tn * out_bytes + fixed_extra
    if K <= tk_max and not needs_acc and fixed - tm * tn * 4 + pairs * 4 * (tm + tn) * K <= MM_VMEM_BUDGET:
        return K
    tk = _pick(K, tk_max, LANE)
    while tk > LANE and fixed + pairs * 4 * (tm + tn) * tk > MM_VMEM_BUDGET:
        smaller = _pick(K, tk - LANE, LANE)
        if smaller >= tk:
            break
        tk = smaller
    return tk


def _mm2(name, mode, a, b, out_dtype, tm_pref=1024, tn_pref=1024, deps=(), extras=(), epilogue=None, n_out=1,
         tk_max=MM_TK_MAX):
    if mode == "nn":
        (M, K), N = a.shape, b.shape[1]
    elif mode == "nt":
        (M, K), N = a.shape, b.shape[0]
    else:
        (K, M), N = a.shape, b.shape[1]
    tm, tn = _pick(M, tm_pref, LANE), _pick(N, tn_pref, LANE)
    extra_bytes = sum(2 * tm * tn * jnp.dtype(e.dtype).itemsize for e in extras)
    tk = _fit_tk(K, tm, tn, out_bytes=n_out * jnp.dtype(out_dtype).itemsize, fixed_extra=extra_bytes, tk_max=tk_max,
                 needs_acc=epilogue is not None)
    if mode == "tn":
        a_blk, a_map = (tk, tm), lambda i, j, k: (k, i)
    else:
        a_blk, a_map = (tm, tk), lambda i, j, k: (i, k)
    if mode == "nt":
        b_blk, b_map = (tn, tk), lambda i, j, k: (j, k)
    else:
        b_blk, b_map = (tk, tn), lambda i, j, k: (k, j)
    return _mm(name, mode, [(a, a_blk, a_map, b, b_blk, b_map)], (M // tm, N // tn, K // tk), (tm, tn),
               (M, N), out_dtype, (tm, tn), lambda i, j, k: (i, j), deps=deps,
               extras=[(e, (tm, tn), lambda i, j, k: (i, j)) for e in extras], epilogue=epilogue, n_out=n_out)


def _ew(name, fn, grid, ins, outs, accs=(), deps=()):
    ni, no, na = len(ins), len(outs), len(accs)
    nd = len(grid)
    nx = len(deps)

    def body(*refs):
        vals = fn(*[r[...] for r in refs[:ni]])
        if not isinstance(vals, (tuple, list)):
            vals = (vals,)
        for r, v in zip(refs[ni + nx:ni + nx + no], vals[:no]):
            r[...] = v.astype(r.dtype)
        if na:
            first = pl.program_id(nd - 1) == 0
            for r, v in zip(refs[ni + nx + no:], vals[no:]):
                @pl.when(first)
                def _(r=r, v=v):
                    r[...] = v.astype(r.dtype)

                @pl.when(jnp.logical_not(first))
                def _(r=r, v=v):
                    r[...] += v.astype(r.dtype)

    all_outs = list(outs) + list(accs)
    sem = ("parallel",) * (nd - 1) + (("arbitrary",) if na else ("parallel",))
    res = pl.pallas_call(
        body,
        name=name,
        grid=grid,
        in_specs=[pl.BlockSpec(blk, imap) for (_, blk, imap) in ins] + _dep_specs(deps, nd),
        out_specs=[pl.BlockSpec(blk, imap) for (_, _, blk, imap) in all_outs],
        out_shape=[jax.ShapeDtypeStruct(s, d) for (s, d, _, _) in all_outs],
        compiler_params=_params(sem),
    )(*[a for (a, _, _) in ins], *deps)
    return res


def _cast_pad_rows(name, a, rows_out, deps):
    R, C = a.shape
    tr = _pick(math.gcd(R, rows_out), max(2 * SUBLANE, ROW_BLOCK_ELEMS // C), SUBLANE)
    n_in = R // tr

    def body(a_ref, *rest):
        o_ref = rest[len(deps)]
        i = pl.program_id(0)

        @pl.when(i < n_in)
        def _():
            o_ref[...] = a_ref[...].astype(BF16)

        @pl.when(i >= n_in)
        def _():
            o_ref[...] = jnp.zeros_like(o_ref)

    return pl.pallas_call(
        body,
        name=name,
        grid=(rows_out // tr,),
        in_specs=[pl.BlockSpec((tr, C), lambda i: (jnp.minimum(i, n_in - 1), 0))] + _dep_specs(deps, 1),
        out_specs=pl.BlockSpec((tr, C), lambda i: (i, 0)),
        out_shape=jax.ShapeDtypeStruct((rows_out, C), BF16),
        compiler_params=_params(("parallel",)),
    )(a, *deps)


def _rows(a, tm, width=None, cblk=0):
    width = a.shape[1] if width is None else width
    return (a, (tm, width), lambda i, c=cblk: (i, c))


def _vec(p):
    return (p, p.shape, lambda i: (0,) * p.ndim)


def _rows_out(shape, dtype, tm):
    return (shape, dtype, (tm, shape[1]), lambda i: (i, 0))


def _vec_out(shape):
    return (shape, F32, shape, lambda i: (0,) * len(shape))


def _row_tile(T, D):
    return _pick(T, max(SUBLANE, ROW_BLOCK_ELEMS // D), SUBLANE)


def _rms(x, g):
    r = lax.rsqrt(jnp.mean(x * x, axis=-1, keepdims=True) + EPS)
    return x * r * g


def _rms_bwd(x, g, dy):
    r = lax.rsqrt(jnp.mean(x * x, axis=-1, keepdims=True) + EPS)
    xh = x * r
    dg = jnp.sum(dy * xh, axis=0, keepdims=True)
    dxh = dy * g
    dx = r * (dxh - xh * jnp.mean(dxh * xh, axis=-1, keepdims=True))
    return dx, dg


GELU_C = math.sqrt(2.0 / math.pi)
GELU_A = 0.044715


def _sigmoid(x):
    return 0.5 * jnp.tanh(0.5 * x) + 0.5


def _gelu_tanh(x):
    return jnp.tanh(GELU_C * (x + GELU_A * x * x * x))


def _gelu(x, t):
    return 0.5 * x * (1.0 + t)


def _gelu_grad(x, t):
    return 0.5 * (1.0 + t) + 0.5 * x * (1.0 - t * t) * GELU_C * (1.0 + 3.0 * GELU_A * x * x)


def _rot(w, axis=-1):
    w1, w2 = jnp.split(w, 2, axis=axis)
    return jnp.concatenate([-w2, w1], axis=axis)


def _rot_t(dw, axis=-1):
    d1, d2 = jnp.split(dw, 2, axis=axis)
    return jnp.concatenate([d2, -d1], axis=axis)


def _coords():
    return lax.axis_index("x"), lax.axis_index("y"), lax.axis_index("c")


HBM_SPEC = pl.BlockSpec(memory_space=pltpu.HBM)
SEM_SPEC = pl.BlockSpec(memory_space=pltpu.SEMAPHORE)
ANY_SPEC = pl.BlockSpec(memory_space=pl.ANY)
EFFECT = pltpu.SideEffectType.DATAFLOW_SIDE_EFFECTING
TOKEN = (SUBLANE, LANE)


def _my_slot():
    xi, yi, ci = _coords()
    return 4 * xi + 2 * yi + ci


def _own_slot_filled(block):
    land = lax.empty((N_DEV,) + block.shape, block.dtype)
    return lax.dynamic_update_slice(land, block[None], (_my_slot(),) + (0,) * block.ndim)


def _split_start(name, n, n_copies, issue, srcs, lands, deps):
    def body(*refs):
        src_refs, land_refs = refs[:n], refs[n:2 * n]
        outs = refs[2 * n + len(deps):]
        send, recv, token = outs[:n], outs[n:2 * n], outs[4 * n]
        for t in range(n):
            issue(t, src_refs[t], land_refs[t], send[t], recv[t])
        token[...] = jnp.zeros_like(token)

    sems = [pltpu.SemaphoreType.DMA((n_copies,))] * (2 * n)
    res = pl.pallas_call(
        body,
        name=name,
        in_specs=[HBM_SPEC] * (2 * n) + [ANY_SPEC] * len(deps),
        out_specs=[SEM_SPEC] * (2 * n) + [HBM_SPEC] * (2 * n) + [pl.BlockSpec(memory_space=pltpu.VMEM)],
        out_shape=sems + [pltpu.HBM(a.shape, a.dtype) for a in list(srcs) + list(lands)]
        + [jax.ShapeDtypeStruct(TOKEN, F32)],
        input_output_aliases={i: 2 * n + i for i in range(2 * n)},
        compiler_params=pltpu.CompilerParams(has_side_effects=EFFECT),
    )(*[pltpu.with_memory_space_constraint(a, pltpu.HBM) for a in list(srcs) + list(lands)], *deps)
    state = dict(n=n, n_copies=n_copies, send=res[:n], recv=res[n:2 * n], srcs=res[2 * n:3 * n],
                 lands=res[3 * n:4 * n])
    return state, res[4 * n]


def _split_wait(name, state, unit_src, after):
    n, n_copies = state["n"], state["n_copies"]

    def body(*refs):
        src_refs, land_refs = refs[:n], refs[n:2 * n]
        send, recv = refs[2 * n:3 * n], refs[3 * n:4 * n]
        me = _coords()
        for t in range(n):
            for k in range(n_copies):
                cp = pltpu.make_async_remote_copy(
                    src_ref=unit_src(src_refs[t]), dst_ref=land_refs[t].at[0], send_sem=send[t].at[k],
                    recv_sem=recv[t].at[k], device_id=me, device_id_type=MESH)
                cp.wait_send()
                cp.wait_recv()

    bufs = list(state["srcs"]) + list(state["lands"])
    res = pl.pallas_call(
        body,
        name=name,
        in_specs=[HBM_SPEC] * (2 * n) + [SEM_SPEC] * (2 * n) + [ANY_SPEC],
        out_specs=[HBM_SPEC] * (2 * n),
        out_shape=[pltpu.HBM(a.shape, a.dtype) for a in bufs],
        input_output_aliases={i: i for i in range(2 * n)},
        compiler_params=pltpu.CompilerParams(has_side_effects=EFFECT),
    )(*bufs, *state["send"], *state["recv"], after)
    return list(res[n:])


def _ag_start(name, shards, deps):
    def issue(t, x_ref, land_ref, send, recv):
        xi, yi, ci = _coords()
        to = [(xi, yi, 1 - ci), (1 - xi, yi, ci), (xi, 1 - yi, ci), (1 - xi, 1 - yi, ci)]
        for k, dev in enumerate(to):
            pltpu.make_async_remote_copy(src_ref=x_ref, dst_ref=land_ref.at[4 * xi + 2 * yi + ci], send_sem=send.at[k],
                                         recv_sem=recv.at[k], device_id=dev, device_id_type=MESH).start()

    return _split_start(name, len(shards), 4, issue, shards, [_own_slot_filled(s) for s in shards], deps)


def _ag_mid(name, state, after):
    lands = _split_wait(name + "_w1", state, lambda r: r, after)

    def issue(t, src_ref, land_ref, send, recv):
        xi, yi, ci = _coords()
        for k, (px, py) in enumerate([(1 - xi, yi), (xi, 1 - yi), (1 - xi, 1 - yi)]):
            rows = land_ref.at[4 * px + 2 * py + ci]
            pltpu.make_async_remote_copy(src_ref=rows, dst_ref=rows, send_sem=send.at[k], recv_sem=recv.at[k],
                                         device_id=(xi, yi, 1 - ci), device_id_type=MESH).start()

    n = len(lands)

    def body(*refs):
        land_refs = refs[:n]
        outs = refs[n:]
        send, recv, token = outs[:n], outs[n:2 * n], outs[3 * n]
        for t in range(n):
            issue(t, land_refs[t], land_refs[t], send[t], recv[t])
        token[...] = jnp.zeros_like(token)

    res = pl.pallas_call(
        body,
        name=name + "_s2",
        in_specs=[HBM_SPEC] * n,
        out_specs=[SEM_SPEC] * (2 * n) + [HBM_SPEC] * n + [pl.BlockSpec(memory_space=pltpu.VMEM)],
        out_shape=[pltpu.SemaphoreType.DMA((3,))] * (2 * n) + [pltpu.HBM(a.shape, a.dtype) for a in lands]
        + [jax.ShapeDtypeStruct(TOKEN, F32)],
        input_output_aliases={i: 2 * n + i for i in range(n)},
        compiler_params=pltpu.CompilerParams(has_side_effects=EFFECT),
    )(*lands)
    state2 = dict(n=n, send=res[:n], recv=res[n:2 * n], lands=res[2 * n:3 * n])
    return state2, res[3 * n]


def _ag_finish(name, state2, after):
    n = state2["n"]

    def body(*refs):
        land_refs, send, recv = refs[:n], refs[n:2 * n], refs[2 * n:3 * n]
        me = _coords()
        for t in range(n):
            for k in range(3):
                cp = pltpu.make_async_remote_copy(
                    src_ref=land_refs[t].at[0], dst_ref=land_refs[t].at[0], send_sem=send[t].at[k],
                    recv_sem=recv[t].at[k], device_id=me, device_id_type=MESH)
                cp.wait_send()
                cp.wait_recv()

    res = pl.pallas_call(
        body,
        name=name + "_w2",
        in_specs=[HBM_SPEC] * n + [SEM_SPEC] * (2 * n) + [ANY_SPEC],
        out_specs=[HBM_SPEC] * n,
        out_shape=[pltpu.HBM(a.shape, a.dtype) for a in state2["lands"]],
        input_output_aliases={i: i for i in range(n)},
        compiler_params=pltpu.CompilerParams(has_side_effects=EFFECT),
    )(*state2["lands"], *state2["send"], *state2["recv"], after)
    return list(res)


def _a2a_start(name, parts, deps):
    def issue(t, x_ref, land_ref, send, recv):
        xi, yi, ci = _coords()
        for k in range(N_DEV - 1):
            px = 1 - xi if (k + 1) & 4 else xi
            py = 1 - yi if (k + 1) & 2 else yi
            pc = 1 - ci if (k + 1) & 1 else ci
            pltpu.make_async_remote_copy(src_ref=x_ref.at[4 * px + 2 * py + pc], dst_ref=land_ref.at[4 * xi + 2 * yi + ci],
                                         send_sem=send.at[k], recv_sem=recv.at[k], device_id=(px, py, pc),
                                         device_id_type=MESH).start()

    lands = [_own_slot_filled(lax.dynamic_index_in_dim(p, _my_slot(), 0, keepdims=False)) for p in parts]
    return _split_start(name, len(parts), N_DEV - 1, issue, parts, lands, deps)


def _a2a_finish(name, state, after):
    return _split_wait(name + "_w", state, lambda r: r.at[0], after)


N_CHIP = N_DEV // 2


def _rs_pair_start(name, parts, deps):
    def issue(t, x_ref, land_ref, send, recv):
        xi, yi, ci = _coords()
        for k in range(N_CHIP):
            pltpu.make_async_remote_copy(src_ref=x_ref.at[2 * k + (1 - ci)], dst_ref=land_ref.at[k], send_sem=send.at[k],
                                         recv_sem=recv.at[k], device_id=(xi, yi, 1 - ci), device_id_type=MESH).start()

    lands = [lax.empty((N_CHIP,) + p.shape[1:], p.dtype) for p in parts]
    return _split_start(name, len(parts), N_CHIP, issue, parts, lands, deps)


def _rs_pair_sum(name, state, after):
    lands = _split_wait(name + "_w", state, lambda r: r.at[0], after)
    sums = []
    for t, (part, land) in enumerate(zip(state["srcs"], lands)):
        _, R, C = land.shape
        pack = SUBLANE * (4 // jnp.dtype(land.dtype).itemsize)
        if R % pack == 0:
            tr, tc = _pick(R, max(pack, ROW_BLOCK_ELEMS // C), pack), C
        else:
            tr, tc = R, _pick(C, max(LANE, ROW_BLOCK_ELEMS // R // LANE * LANE), LANE)

        def fn(mine, theirs):
            own = jnp.where(lax.axis_index("c") == 0, mine[0], mine[1])
            return own.astype(F32) + theirs.astype(F32)

        sums.append(_ew(f"{name}_sum{t}", fn, (N_CHIP, R // tr, C // tc),
                        [(part.reshape((N_CHIP, 2) + part.shape[1:]), (None, 2, tr, tc), lambda k, i, j: (k, 0, i, j)),
                         (land, (None, tr, tc), lambda k, i, j: (k, i, j))],
                        [(land.shape, land.dtype, (None, tr, tc), lambda k, i, j: (k, i, j))])[0])
    return sums


def _rs_chip_start(name, sums, deps):
    def issue(t, x_ref, land_ref, send, recv):
        xi, yi, ci = _coords()
        for k, (px, py) in enumerate([(1 - xi, yi), (xi, 1 - yi), (1 - xi, 1 - yi)]):
            pltpu.make_async_remote_copy(src_ref=x_ref.at[2 * px + py], dst_ref=land_ref.at[2 * xi + yi], send_sem=send.at[k],
                                         recv_sem=recv.at[k], device_id=(px, py, ci), device_id_type=MESH).start()

    xi, yi, _ = _coords()
    lands = []
    for s in sums:
        own = lax.dynamic_index_in_dim(s, 2 * xi + yi, 0, keepdims=True)
        lands.append(lax.dynamic_update_slice(lax.empty(s.shape, s.dtype), own, (2 * xi + yi,) + (0,) * (s.ndim - 1)))
    return _split_start(name, len(sums), N_CHIP - 1, issue, sums, lands, deps)


def _rs_chip_finish(name, state, after):
    return _split_wait(name + "_w", state, lambda r: r.at[0], after)


QK_SCALE = 1.0 / math.sqrt(QK_HEAD)
LOG2E = 1.0 / math.log(2.0)


def _roped_q(q_ext, cos, sin):
    q = jnp.concatenate([q_ext[:, :LANE], q_ext[:, LANE:2 * LANE] * cos + q_ext[:, 2 * LANE:] * sin], axis=1)
    return (q * (QK_SCALE * LOG2E)).astype(BF16)


def _attn_fwd(q_ext, k_full, v_aug, cosp, sinp):
    H, T, _ = k_full.shape
    tq, tkv = _pick(T, ATTN_TQ, LANE), _pick(T, ATTN_TKV, LANE)

    def body(q_ref, cos_ref, sin_ref, k_ref, v_ref, o_ref, lse_ref):
        q = _roped_q(q_ref[...], cos_ref[...], sin_ref[...])
        m = jnp.full((tq, 1), NEG_BIG, F32)
        acc = jnp.zeros((tq, 2 * V_DIM), F32)
        for c in range(T // tkv):
            keys = pl.ds(c * tkv, tkv)
            s = lax.dot_general(q, k_ref[keys, :], DN["nt"], preferred_element_type=F32)
            m_new = jnp.maximum(m, jnp.max(s, axis=-1, keepdims=True))
            p = jnp.exp2(s - m_new).astype(BF16)
            acc = jnp.exp2(m - m_new) * acc + jnp.dot(p, v_ref[keys, :], preferred_element_type=F32)
            m = m_new
        l = acc[:, V_DIM:]
        o_ref[...] = acc[:, :V_DIM] / l
        lse_ref[...] = m + jnp.log2(jnp.max(l, axis=-1, keepdims=True))

    return pl.pallas_call(
        body,
        name="attn_fwd",
        grid=(H, T // tq),
        in_specs=[pl.BlockSpec((tq, Q_EXT), lambda h, i: (i, h)),
                  pl.BlockSpec((tq, LANE), lambda h, i: (i, 0)),
                  pl.BlockSpec((tq, LANE), lambda h, i: (i, 0)),
                  pl.BlockSpec((None, T, K_FULL), lambda h, i: (h, 0, 0)),
                  pl.BlockSpec((None, T, 2 * V_DIM), lambda h, i: (h, 0, 0))],
        out_specs=[pl.BlockSpec((tq, V_DIM), lambda h, i: (i, h)),
                   pl.BlockSpec((None, tq, 1), lambda h, i: (h, i, 0))],
        out_shape=[jax.ShapeDtypeStruct((T, H * V_DIM), F32), jax.ShapeDtypeStruct((H, T, 1), F32)],
        compiler_params=_params(("parallel", "parallel")),
    )(q_ext, cosp, sinp, k_full, v_aug)


def _attn_bwd(q_ext, k_full, v_aug, cosp, sinp, d_out, out, lse_rows):
    H, T, _ = k_full.shape
    tq, tkv = _pick(T, ATTN_TQ, LANE), _pick(T, ATTN_TKV, LANE)
    n_q = T // tq

    def body(q_ref, cos_ref, sin_ref, k_ref, v_ref, do_ref, o_ref, lse_ref, cos_all, sin_all,
             dq_ref, dkv_ref, dkr_ref, dk_acc, dv_acc):
        h, i = pl.program_id(0), pl.program_id(1)

        @pl.when(i == 0)
        def _():
            dk_acc[...] = jnp.zeros_like(dk_acc)
            dv_acc[...] = jnp.zeros_like(dv_acc)

        cos, sin = cos_ref[...], sin_ref[...]
        q, do, lse = _roped_q(q_ref[...], cos, sin), do_ref[...], lse_ref[...]
        delta = lax.dot_general(jnp.ones((SUBLANE, V_DIM), F32), do.astype(F32) * o_ref[...], DN["nt"],
                                preferred_element_type=F32, precision=lax.Precision.HIGHEST)[0:1]
        dq = jnp.zeros((tq, K_FULL), F32)
        for c in range(T // tkv):
            keys = pl.ds(c * tkv, tkv)
            kc = k_ref[keys, :]
            pt = jnp.exp2(lax.dot_general(kc, q, DN["nt"], preferred_element_type=F32) - lse)
            dpt = lax.dot_general(v_ref[keys, pl.ds(0, V_DIM)], do, DN["nt"], preferred_element_type=F32)
            dst = (pt * (dpt - delta)).astype(BF16)
            dv_acc[keys, :] += jnp.dot(pt.astype(BF16), do, preferred_element_type=F32)
            dk_acc[keys, :] += jnp.dot(dst, q, preferred_element_type=F32)
            dq = dq + lax.dot_general(dst, kc, DN["tn"], preferred_element_type=F32)
        r = dq[:, LANE:]
        dq_ref[...] = (jnp.concatenate([dq[:, :LANE], r * cos, r * sin], axis=1) * QK_SCALE).astype(BF16)

        @pl.when(i == n_q - 1)
        def _():
            dk = dk_acc[...] * (1.0 / LOG2E)
            dkv_ref[...] = jnp.concatenate([dk[:, :LANE], dv_acc[...]], axis=1).astype(BF16)
            rk = dk[:, LANE:]
            dkr = jnp.concatenate([rk * cos_all[...], rk * sin_all[...]], axis=1)

            @pl.when(h == 0)
            def _():
                dkr_ref[...] = dkr

            @pl.when(h != 0)
            def _():
                dkr_ref[...] += dkr

    return pl.pallas_call(
        body,
        name="attn_bwd",
        grid=(H, n_q),
        in_specs=[pl.BlockSpec((tq, Q_EXT), lambda h, i: (i, h)),
                  pl.BlockSpec((tq, LANE), lambda h, i: (i, 0)),
                  pl.BlockSpec((tq, LANE), lambda h, i: (i, 0)),
                  pl.BlockSpec((None, T, K_FULL), lambda h, i: (h, 0, 0)),
                  pl.BlockSpec((None, T, 2 * V_DIM), lambda h, i: (h, 0, 0)),
                  pl.BlockSpec((tq, V_DIM), lambda h, i: (i, h)),
                  pl.BlockSpec((tq, V_DIM), lambda h, i: (i, h)),
                  pl.BlockSpec((None, 1, tq), lambda h, i: (h, 0, i)),
                  pl.BlockSpec((T, LANE), lambda h, i: (0, 0)),
                  pl.BlockSpec((T, LANE), lambda h, i: (0, 0))],
        out_specs=[pl.BlockSpec((tq, Q_EXT), lambda h, i: (i, h)),
                   pl.BlockSpec((T, K_FULL), lambda h, i: (0, h)),
                   pl.BlockSpec((T, K_FULL), lambda h, i: (0, 0))],
        out_shape=[jax.ShapeDtypeStruct((T, H * Q_EXT), BF16), jax.ShapeDtypeStruct((T, H * K_FULL), BF16),
                   jax.ShapeDtypeStruct((T, K_FULL), F32)],
        scratch_shapes=[pltpu.VMEM((T, K_FULL), F32), pltpu.VMEM((T, V_DIM), F32)],
        compiler_params=_params(("arbitrary", "arbitrary")),
    )(q_ext, cosp, sinp, k_full, v_aug, d_out, out, lse_rows, cosp, sinp)


def _sgu_forward_math(pb, lng, lnb, ws, bsb, gw, gh):
    u_raw, v_raw = pb[:, :gw], pb[:, gw:]
    t_u, t_v = _gelu_tanh(u_raw), _gelu_tanh(v_raw)
    u, v = _gelu(u_raw, t_u), _gelu(v_raw, t_v)
    vc = v - jnp.mean(v, axis=-1, keepdims=True)
    rstd = lax.rsqrt(jnp.mean(vc * vc, axis=-1, keepdims=True) + EPS)
    vhat = vc * rstd
    vln = vhat * lng + lnb
    hd = gw // gh
    s = jnp.concatenate(
        [jnp.dot(ws[g], vln[:, g * hd:(g + 1) * hd].astype(BF16), preferred_element_type=F32) + bsb[g]
         for g in range(gh)], axis=1)
    return (u_raw, t_u), (v_raw, t_v), u, rstd, vhat, vln, s


def _sgu_fwd(proj_b, lng, lnb, ws, bsb, gg):
    T, gw2 = proj_b.shape
    gw, gh = gw2 // 2, ws.shape[0]

    def fn(pb, lng, lnb, ws, bsb, gg):
        _, _, u, _, _, _, s = _sgu_forward_math(pb, lng, lnb, ws, bsb, gw, gh)
        return _rms(u * s, gg)

    return _ew("sgu_fwd", fn, (T // CHUNK,),
               [_rows(proj_b, CHUNK), _vec(lng), _vec(lnb), _vec(ws), _vec(bsb), _vec(gg)],
               [_rows_out((T, gw), BF16, CHUNK)])[0]


def _sgu_bwd(proj_b, d_mixed, lng, lnb, ws, ws_t, bsb, gg):
    T, gw2 = proj_b.shape
    gw, gh = gw2 // 2, ws.shape[0]
    hd = gw // gh

    def fn(pb, dmg, lng, lnb, ws, ws_t, bsb, gg):
        u_pre, v_pre, u, rstd, vhat, vln, s = _sgu_forward_math(pb, lng, lnb, ws, bsb, gw, gh)
        dgo, dgg = _rms_bwd(u * s, gg, dmg)
        du, ds = dgo * s, dgo * u
        ones = jnp.ones((SUBLANE, hd), F32)
        dws, dbs, dvln = [], [], []
        for g in range(gh):
            ds_g = ds[:, g * hd:(g + 1) * hd]
            ds_gb = ds_g.astype(BF16)
            dws.append(lax.dot_general(ds_gb, vln[:, g * hd:(g + 1) * hd].astype(BF16), DN["nt"],
                                       preferred_element_type=F32)[None])
            dbs.append(lax.dot_general(ones, ds_g, DN["nt"], preferred_element_type=F32,
                                       precision=lax.Precision.HIGHEST)[None])
            dvln.append(jnp.dot(ws_t[g], ds_gb, preferred_element_type=F32))
        dvln = jnp.concatenate(dvln, axis=1)
        dlng = jnp.sum(dvln * vhat, axis=0, keepdims=True)
        dlnb = jnp.sum(dvln, axis=0, keepdims=True)
        dvh = dvln * lng
        dv = rstd * (dvh - jnp.mean(dvh, axis=-1, keepdims=True) - vhat * jnp.mean(dvh * vhat, axis=-1, keepdims=True))
        d_pb = jnp.concatenate([du * _gelu_grad(*u_pre), dv * _gelu_grad(*v_pre)], axis=1)
        return d_pb, dgg, jnp.concatenate(dws, axis=0), jnp.concatenate(dbs, axis=0), dlng, dlnb

    return _ew("sgu_bwd", fn, (T // CHUNK,),
               [_rows(proj_b, CHUNK), _rows(d_mixed, CHUNK, gw, 1), _vec(lng), _vec(lnb), _vec(ws), _vec(ws_t),
                _vec(bsb), _vec(gg)],
               [_rows_out((T, gw2), BF16, CHUNK)],
               [_vec_out((1, gw)), _vec_out((gh, CHUNK, CHUNK)), _vec_out((gh, SUBLANE, CHUNK)), _vec_out((1, gw)),
                _vec_out((1, gw))])


def _adamw(name, parts, w, m, v):
    R, C = w.shape
    pack = SUBLANE * (4 // jnp.dtype(parts.dtype).itemsize)
    if R % pack == 0:
        tm, tc = _pick(R, max(pack, ADAM_BLOCK_ELEMS // C), pack), C
    else:
        parts = parts[:, :R]
        tm, tc = R, _pick(C, max(LANE, ADAM_BLOCK_ELEMS // R // LANE * LANE), LANE)

    n_parts = parts.shape[0]

    def fn(p, w, m, v):
        g = p[0].astype(F32)
        for i in range(1, n_parts):
            g = g + p[i].astype(F32)
        m2 = ADAM_B1 * m + (1.0 - ADAM_B1) * g
        v2 = ADAM_B2 * v + (1.0 - ADAM_B2) * (g * g)
        m_hat = m2 / (1.0 - ADAM_B1 ** ADAM_STEP)
        v_hat = v2 / (1.0 - ADAM_B2 ** ADAM_STEP)
        delta = -ADAM_LR * (m_hat / (jnp.sqrt(v_hat) + ADAM_EPS) + ADAM_WD * w)
        return g, delta, m2, v2

    blk = lambda a: (a, (tm, tc), lambda i, j: (i, j))
    return _ew(name, fn, (R // tm, C // tc),
               [(parts, (n_parts, tm, tc), lambda i, j: (0, i, j)), blk(w), blk(m), blk(v)],
               [((R, C), F32, (tm, tc), lambda i, j: (i, j))] * 4)


def kernel(x, positions, pre_mix_norm, w_in, q_norm, kv_norm, w_uq, w_ukv, v_ln_gain, v_ln_bias, w_spatial, b_spatial, attn_out_norm, gmlp_out_norm, w_out, post_mix_norm, pre_ffn_norm, w_gate, w_up, w_down, post_ffn_norm, loss_target, m_pre_mix_norm, m_w_in, m_q_norm, m_kv_norm, m_w_uq, m_w_ukv, m_v_ln_gain, m_v_ln_bias, m_w_spatial, m_b_spatial, m_attn_out_norm, m_gmlp_out_norm, m_w_out, m_post_mix_norm, m_pre_ffn_norm, m_w_gate, m_w_up, m_w_down, m_post_ffn_norm, v_pre_mix_norm, v_w_in, v_q_norm, v_kv_norm, v_w_uq, v_w_ukv, v_v_ln_gain, v_v_ln_bias, v_w_spatial, v_b_spatial, v_attn_out_norm, v_gmlp_out_norm, v_w_out, v_post_mix_norm, v_pre_ffn_norm, v_w_gate, v_w_up, v_w_down, v_post_ffn_norm):
    given = dict(locals())
    T, D = x.shape[1], x.shape[2]
    QL, KVL = w_uq.shape[1], w_ukv.shape[1]
    GW, GH = v_ln_gain.shape[1], w_spatial.shape[1]
    IWS = w_in.shape[2]
    IW = IWS * N_DEV
    H = w_ukv.shape[2] * N_DEV // (NOPE_DIM + V_DIM)
    AW = H * V_DIM
    FS = w_gate.shape[2]
    FP = _round_up(FS, LANE)
    C0 = QL + KVL
    assert IW == C0 + ROPE_DIM + 2 * GW and AW + GW == D and AW == GW
    assert QL % KVL == 0 and C0 % K_FULL == 0 and w_spatial.shape[2] == CHUNK and T % CHUNK == 0
    assert w_uq.shape[2] * N_DEV == H * QK_HEAD and D % N_DEV == 0

    x2, tgt = x[0], loss_target[0]
    tm_d = _row_tile(T, D)

    inv_freq = 1.0 / (ROPE_THETA ** (jnp.arange(0, ROPE_DIM, 2, dtype=F32) / ROPE_DIM))
    ang = positions[0].astype(F32)[:, None] * inv_freq
    zpad = jnp.zeros((T, LANE - ROPE_DIM), F32)
    cosp = jnp.concatenate([jnp.cos(ang), jnp.cos(ang), zpad], axis=1)
    sinp = jnp.concatenate([jnp.sin(ang), jnp.sin(ang), zpad], axis=1)

    def shard_t(a):
        return a[0].T

    st_a, tok = _ag_start("ag_a", [shard_t(w_in).astype(BF16), w_uq[0].astype(BF16), w_ukv[0].astype(BF16)], [])
    st_o, tok = _ag_start("ag_o", [w_out[0].astype(BF16)], [tok])
    st_g, tok = _ag_start("ag_g", [_cast_pad_rows("pad_w_gate", shard_t(w_gate), FP, [tok])], [tok])
    st_u, tok = _ag_start("ag_u", [_cast_pad_rows("pad_w_up", shard_t(w_up), FP, [tok])], [tok])
    st_d, tok = _ag_start("ag_d", [_cast_pad_rows("pad_w_down", w_down[0], FP, [tok])], [tok])

    xn = _ew("rms_x", lambda x, g: _rms(x, g), (T // tm_d,), [_rows(x2, tm_d), _vec(pre_mix_norm)],
             [_rows_out((T, D), BF16, tm_d)], deps=[tok])[0]
    st_a, tok = _ag_mid("ag_a", st_a, xn)
    g_in, g_uq, g_ukv = _ag_finish("ag_a", st_a, tok)

    w_in_t = g_in.reshape(IW, D)
    kr_w = w_in_t[C0:C0 + ROPE_DIM]
    z64 = jnp.zeros((LANE - ROPE_DIM, D), BF16)
    w_a_t = jnp.concatenate([w_in_t[:C0], kr_w, z64, _rot(kr_w, 0), z64], axis=0)
    w_b_t = w_in_t[C0 + ROPE_DIM:]
    wq = g_uq.transpose(1, 0, 2).reshape(QL, H, QK_HEAD)
    zq = jnp.zeros((QL, H, LANE - ROPE_DIM), BF16)
    wq_ext = jnp.concatenate([wq[..., :NOPE_DIM], wq[..., NOPE_DIM:], zq, _rot(wq[..., NOPE_DIM:]), zq],
                             axis=-1).reshape(QL, H * Q_EXT)
    wkv = g_ukv.transpose(1, 0, 2).reshape(KVL, H * (NOPE_DIM + V_DIM))

    ws = w_spatial[0].astype(BF16)
    ws_t = jnp.swapaxes(ws, 1, 2)
    bsb = jnp.broadcast_to(b_spatial[0][:, :, None], (GH, CHUNK, GW // GH))

    proj_a = _mm2("mm_proj_a", "nt", xn, w_a_t, F32, tn_pref=w_a_t.shape[0])
    proj_b = _mm2("mm_proj_b", "nt", xn, w_b_t, F32)

    tm_q = _row_tile(T, QL)
    qn = _ew("rms_q", lambda x, g: _rms(x, g), (T // tm_q,), [_rows(proj_a, tm_q, QL, 0), _vec(q_norm)],
             [_rows_out((T, QL), BF16, tm_q)])[0]
    kvn = _ew("rms_kv", lambda x, g: _rms(x, g), (T // tm_q,), [_rows(proj_a, tm_q, KVL, QL // KVL), _vec(kv_norm)],
              [_rows_out((T, KVL), BF16, tm_q)])[0]
    q_ext = _mm2("mm_q", "nn", qn, wq_ext, F32)
    kv = _mm2("mm_kv", "nn", kvn, wkv, F32)

    tm_h = _pick(T, 512, SUBLANE)
    hb = _pick(H, 4, 1)

    def make_kv(kv, kr, c, s):
        k_rope = kr[:, :LANE] * c + kr[:, LANE:] * s
        ones = jnp.ones((kv.shape[0], V_DIM), F32)
        ks = [jnp.concatenate([kv[:, j * K_FULL:j * K_FULL + LANE], k_rope], axis=1) for j in range(hb)]
        vs = [jnp.concatenate([kv[:, j * K_FULL + LANE:(j + 1) * K_FULL], ones], axis=1) for j in range(hb)]
        return jnp.stack(ks), jnp.stack(vs)

    k_full, v_aug = _ew("make_kv", make_kv, (T // tm_h, H // hb),
                        [(kv, (tm_h, hb * K_FULL), lambda i, h: (i, h)),
                         (proj_a, (tm_h, K_FULL), lambda i, h: (i, C0 // K_FULL)),
                         (cosp, (tm_h, LANE), lambda i, h: (i, 0)), (sinp, (tm_h, LANE), lambda i, h: (i, 0))],
                        [((H, T, K_FULL), BF16, (hb, tm_h, K_FULL), lambda i, h: (h, i, 0)),
                         ((H, T, 2 * V_DIM), BF16, (hb, tm_h, 2 * V_DIM), lambda i, h: (h, i, 0))])

    a_out, lse = _attn_fwd(q_ext, k_full, v_aug, cosp, sinp)
    st_o, tok_o = _ag_mid("ag_o", st_o, a_out)
    tm_a = _row_tile(T, AW)
    mixed_a = _ew("rms_attn", lambda x, g: _rms(x, g), (T // tm_a,), [_rows(a_out, tm_a), _vec(attn_out_norm)],
                  [_rows_out((T, AW), BF16, tm_a)], deps=[tok_o])[0]
    mixed_g = _sgu_fwd(proj_b, v_ln_gain, v_ln_bias, ws, bsb, gmlp_out_norm)
    wo = _ag_finish("ag_o", st_o, mixed_g)[0].reshape(D, D)
    st_g, tok_gu = _ag_mid("ag_g", st_g, wo)

    tm_, tn_ = _pick(T, 1024, LANE), _pick(D, 1024, LANE)
    tk_ = _fit_tk(AW, tm_, tn_, pairs=2)
    nka = AW // tk_
    mix_out = _mm("mm_out", "nn",
                  [(mixed_a, (tm_, tk_), lambda i, j, k: (i, k), wo, (tk_, tn_), lambda i, j, k: (k, j)),
                   (mixed_g, (tm_, tk_), lambda i, j, k: (i, k), wo, (tk_, tn_), lambda i, j, k: (k + nka, j))],
                  (T // tm_, D // tn_, nka), (tm_, tn_), (T, D), F32, (tm_, tn_), lambda i, j, k: (i, j), deps=[tok_gu])

    def post_mix(x, mo, g3, g4):
        h = x + _rms(mo, g3)
        return h, _rms(h, g4)

    h_res, hn = _ew("post_mix", post_mix, (T // tm_d,),
                    [_rows(x2, tm_d), _rows(mix_out, tm_d), _vec(post_mix_norm), _vec(pre_ffn_norm)],
                    [_rows_out((T, D), F32, tm_d), _rows_out((T, D), BF16, tm_d)])

    F = N_DEV * FP
    tn_f = _pick(F, FFN_TILE_MAX, MXU_DIM)
    tn_e = _pick(F, FFN_TILE_MAX // 2, LANE)

    def swiglu(u, g):
        g = g.astype(F32)
        return u, g * _sigmoid(g) * u

    st_u, tok_u = _ag_mid("ag_u", st_u, hn)
    wg_t = _ag_finish("ag_g", st_g, tok_u)[0].reshape(F, D)
    gate = _mm2("mm_gate", "nt", hn, wg_t, BF16, tn_pref=tn_f)
    wu_t = _ag_finish("ag_u", st_u, gate)[0].reshape(F, D)
    st_d, tok_d = _ag_mid("ag_d", st_d, wu_t)
    up, act = _mm2("mm_up", "nt", hn, wu_t, BF16, tn_pref=tn_e, deps=[tok_d], extras=[gate],
                   epilogue=swiglu, n_out=2)
    wd_f = _ag_finish("ag_d", st_d, act)[0].reshape(F, D)
    ffn = _mm2("mm_down", "nn", act, wd_f, F32, tk_max=tn_f)

    def loss_head(h, f, t, g5):
        y = h + _rms(f, g5)
        diff = y - t
        dy = diff / D
        d_f, dg5 = _rms_bwd(f, g5, dy)
        lsum = jnp.sum(jnp.sum(diff * diff, axis=0, keepdims=True), axis=1, keepdims=True) * (0.5 / D)
        return dy, d_f, dg5, lsum

    dy, d_ffn, dg_post_ffn, loss_part = _ew(
        "loss_head", loss_head, (T // tm_d,),
        [_rows(h_res, tm_d), _rows(ffn, tm_d), _rows(tgt, tm_d), _vec(post_ffn_norm)],
        [_rows_out((T, D), F32, tm_d), _rows_out((T, D), BF16, tm_d)], [_vec_out((1, D)), _vec_out((1, 1))])

    def swiglu_bwd(da, g, u):
        g, u = g.astype(F32), u.astype(F32)
        sg = _sigmoid(g)
        return da * u * sg * (1.0 + g * (1.0 - sg)), da * g * sg

    def ffn_dw(name, z, y, deps):
        return _mm2(name, "tn", z, y, BF16, tm_pref=tn_f, deps=deps).reshape(N_DEV, FP, D)

    x_d, tok = _a2a_start("a2a_d", [ffn_dw("mm_dwd", act, d_ffn, [])], [])
    d_gate, d_up = _mm2("mm_dact", "nt", d_ffn, wd_f, BF16, tn_pref=tn_e, deps=[tok], extras=[gate, up],
                        epilogue=swiglu_bwd, n_out=2)
    x_g, tok = _a2a_start("a2a_g", [ffn_dw("mm_dwg", d_gate, hn, [])], [])
    x_u, tok = _a2a_start("a2a_u", [ffn_dw("mm_dwu", d_up, hn, [tok])], [])
    d_hn_g = _mm2("mm_dhn_g", "nn", d_gate, wg_t, F32, tk_max=tn_f, deps=[tok])
    d_hn = _mm2("mm_dhn_u", "nn", d_up, wu_t, F32, tk_max=tn_f, extras=[d_hn_g], epilogue=lambda s, g: (s + g,))

    def pre_ffn_bwd(h, dhn, dy, mo, g4, g3):
        d1, dg4 = _rms_bwd(h, g4, dhn)
        dh = dy + d1
        dmo, dg3 = _rms_bwd(mo, g3, dh)
        return dh, dmo, dg4, dg3

    dh, d_mo, dg_pre_ffn, dg_post_mix = _ew(
        "pre_ffn_bwd", pre_ffn_bwd, (T // tm_d,),
        [_rows(h_res, tm_d), _rows(d_hn, tm_d), _rows(dy, tm_d), _rows(mix_out, tm_d), _vec(pre_ffn_norm),
         _vec(post_mix_norm)],
        [_rows_out((T, D), F32, tm_d), _rows_out((T, D), BF16, tm_d)], [_vec_out((1, D)), _vec_out((1, D))])

    d_mixed = _mm2("mm_dmixed", "nt", d_mo, wo, F32)
    dwo = jnp.concatenate([_mm2("mm_dwo_a", "tn", mixed_a, d_mo, BF16), _mm2("mm_dwo_g", "tn", mixed_g, d_mo, BF16)],
                          axis=0)
    x_o, tok = _a2a_start("a2a_o", [dwo.reshape(N_DEV, D // N_DEV, D)], [])

    d_a_out, dg_attn = _ew("rms_attn_bwd", lambda a, g, d: _rms_bwd(a, g, d), (T // tm_a,),
                           [_rows(a_out, tm_a), _vec(attn_out_norm), _rows(d_mixed, tm_a, AW, 0)],
                           [_rows_out((T, AW), BF16, tm_a)], [_vec_out((1, AW))], deps=[tok])
    d_pb, dg_gmlp, dws, dbs, dlng, dlnb = _sgu_bwd(proj_b, d_mixed, v_ln_gain, v_ln_bias, ws, ws_t, bsb, gmlp_out_norm)

    dq_ext, d_kv, d_kr = _attn_bwd(q_ext, k_full, v_aug, cosp, sinp, d_a_out, a_out, lse.reshape(H, 1, T))

    d_qn = _mm2("mm_dqn", "nt", dq_ext, wq_ext, F32)
    dwq_ext = _mm2("mm_dwq", "tn", qn, dq_ext, F32)
    d_kvn = _mm2("mm_dkvn", "nt", d_kv, wkv, F32)
    dwkv = _mm2("mm_dwkv", "tn", kvn, d_kv, BF16)

    d_qc, dg_q = _ew("rms_q_bwd", lambda a, g, d: _rms_bwd(a, g, d), (T // tm_q,),
                     [_rows(proj_a, tm_q, QL, 0), _vec(q_norm), _rows(d_qn, tm_q)],
                     [_rows_out((T, QL), BF16, tm_q)], [_vec_out((1, QL))])
    d_kvc, dg_kv = _ew("rms_kv_bwd", lambda a, g, d: _rms_bwd(a, g, d), (T // tm_q,),
                       [_rows(proj_a, tm_q, KVL, QL // KVL), _vec(kv_norm), _rows(d_kvn, tm_q)],
                       [_rows_out((T, KVL), BF16, tm_q)], [_vec_out((1, KVL))])
    d_pa = jnp.concatenate([d_qc, d_kvc, d_kr.astype(BF16)], axis=1)

    dwa_t = _mm2("mm_dwa", "tn", d_pa, xn, F32, tm_pref=w_a_t.shape[0])
    dwb_t = _mm2("mm_dwb", "tn", d_pb, xn, BF16)
    dkr_w = dwa_t[C0:C0 + ROPE_DIM] + _rot_t(dwa_t[C0 + LANE:C0 + LANE + ROPE_DIM], 0)
    dw_in = jnp.concatenate([dwa_t[:C0].astype(BF16), dkr_w.astype(BF16), dwb_t], axis=0)
    dw_in = dw_in.reshape(N_DEV, IWS, D)
    dq3 = dwq_ext.reshape(QL, H, Q_EXT)
    dwq = jnp.concatenate([dq3[..., :NOPE_DIM],
                           dq3[..., LANE:LANE + ROPE_DIM] + _rot_t(dq3[..., 2 * LANE:2 * LANE + ROPE_DIM])], axis=-1)
    dwq = dwq.astype(BF16).reshape(QL, N_DEV, H * QK_HEAD // N_DEV).transpose(1, 0, 2)
    dwkv = dwkv.reshape(KVL, N_DEV, w_ukv.shape[2]).transpose(1, 0, 2)
    p_e, tok = _rs_pair_start("rs_e", [dw_in, dwq, dwkv], [])
    d_xn_a = _mm2("mm_dxn_a", "nn", d_pa, w_a_t, F32, deps=[tok])
    x_e, tok = _rs_chip_start("rs_e2", _rs_pair_sum("rs_e", p_e, d_xn_a), [])
    d_xn_b = _mm2("mm_dxn_b", "nn", d_pb, w_b_t, F32, deps=[tok])

    def x_bwd(x, da, db, dh, g0):
        dx, dg0 = _rms_bwd(x, g0, da + db)
        return dh + dx, dg0

    grad_x, dg_pre_mix = _ew("x_bwd", x_bwd, (T // tm_d,),
                             [_rows(x2, tm_d), _rows(d_xn_a, tm_d), _rows(d_xn_b, tm_d), _rows(dh, tm_d),
                              _vec(pre_mix_norm)],
                             [_rows_out((T, D), F32, tm_d)], [_vec_out((1, D))])

    def adam_big(n, parts, transposed=False):
        view = shard_t if transposed else (lambda a: a[0])
        res = _adamw("adamw_" + n, parts, view(given[n]), view(given["m_" + n]), view(given["v_" + n]))
        done[n] = res[0]
        return [(r.T if transposed else r)[None] for r in res]

    done = {}

    small_names = ["pre_mix_norm", "q_norm", "kv_norm", "v_ln_gain", "v_ln_bias", "w_spatial", "b_spatial",
                   "attn_out_norm", "gmlp_out_norm", "post_mix_norm", "pre_ffn_norm", "post_ffn_norm"]
    small_grads = [dg_pre_mix, dg_q, dg_kv, dlng, dlnb, dws, dbs[:, 0, :], dg_attn, dg_gmlp, dg_post_mix, dg_pre_ffn,
                   dg_post_ffn]
    n_small = sum(g.size for g in small_grads)
    pad_small = _round_up(n_small, SUBLANE * LANE) - n_small

    def pack(arrs):
        flat = jnp.concatenate([a.reshape(-1) for a in arrs] + [jnp.zeros((pad_small,), F32)])
        return flat.reshape(-1, LANE)

    st_s, tok = _ag_start("ag_s", [pack(small_grads)], [])

    results = {}
    results["w_down"] = adam_big("w_down", _a2a_finish("a2a_d", x_d, tok)[0])
    results["w_gate"] = adam_big("w_gate", _a2a_finish("a2a_g", x_g, done["w_down"])[0], True)
    st_s, tok = _ag_mid("ag_s", st_s, done["w_gate"])
    results["w_up"] = adam_big("w_up", _a2a_finish("a2a_u", x_u, tok)[0], True)
    results["w_out"] = adam_big("w_out", _a2a_finish("a2a_o", x_o, done["w_up"])[0])
    e_in, e_uq, e_ukv = _rs_chip_finish("rs_e2", x_e, done["w_out"])
    results["w_in"] = adam_big("w_in", e_in, True)
    results["w_uq"] = adam_big("w_uq", e_uq)
    results["w_ukv"] = adam_big("w_ukv", e_ukv)

    small_parts = _ag_finish("ag_s", st_s, done["w_ukv"])[0]
    sm = _adamw("adamw_small", small_parts, pack([given[n] for n in small_names]),
                pack([given["m_" + n] for n in small_names]), pack([given["v_" + n] for n in small_names]))

    off = 0
    for n in small_names:
        shp, size = given[n].shape, given[n].size
        results[n] = [r.reshape(-1)[off:off + size].reshape(shp) for r in sm]
        off += size

    loss = lax.psum(loss_part[0, 0], MESH_AXES)
    order = ["pre_mix_norm", "w_in", "q_norm", "kv_norm", "w_uq", "w_ukv", "v_ln_gain", "v_ln_bias", "w_spatial",
             "b_spatial", "attn_out_norm", "gmlp_out_norm", "w_out", "post_mix_norm", "pre_ffn_norm", "w_gate", "w_up",
             "w_down", "post_ffn_norm"]
    out = [loss, grad_x[None]]
    for kind in range(4):
        out += [results[n][kind] for n in order]
    return tuple(out)
```

```python
import math

import jax
import jax.numpy as jnp
from jax import lax
from jax.experimental import pallas as pl
from jax.experimental.pallas import tpu as pltpu

F32 = jnp.float32
BF16 = jnp.bfloat16
N_DEV = 8
MESH_AXES = ("x", "y", "c")
MESH = pl.DeviceIdType.MESH

NOPE_DIM = 128
ROPE_DIM = 64
V_DIM = 128
ROPE_THETA = 10000.0
CHUNK = 128
EPS = 1e-6
QK_HEAD = NOPE_DIM + ROPE_DIM
Q_EXT = 3 * 128
K_FULL = 2 * 128

ADAM_LR = 0.001
ADAM_B1 = 0.9
ADAM_B2 = 0.999
ADAM_EPS = 1e-08
ADAM_WD = 0.01
ADAM_STEP = 10

LANE = 128
SUBLANE = 8
VMEM_LIMIT = 60 * 1024 * 1024
MM_VMEM_BUDGET = 50 * 1024 * 1024
MXU_DIM = 256
FFN_TILE_MAX = 3072
MM_TK_MAX = 4096
EPILOGUE_ROWS = 128
ATTN_TQ = 2048
ATTN_TKV = 512
NEG_BIG = -1e30
ROW_BLOCK_ELEMS = 1024 * 1024
ADAM_BLOCK_ELEMS = 512 * 1024

DN = {
    "nn": (((1,), (0,)), ((), ())),
    "nt": (((1,), (1,)), ((), ())),
    "tn": (((0,), (0,)), ((), ())),
}


def _pick(dim, pref, mult):
    t = min(pref, dim)
    t -= t % mult
    while t >= mult:
        if dim % t == 0:
            return t
        t -= mult
    return dim


def _round_up(n, m):
    return (n + m - 1) // m * m


def _params(sem):
    return pltpu.CompilerParams(dimension_semantics=sem, vmem_limit_bytes=VMEM_LIMIT)


def _dep_specs(deps, nd):
    return [pl.BlockSpec(d.shape, lambda *_, r=d.ndim: (0,) * r) for d in deps]


def _mm(name, mode, pairs, grid, acc_shape, out_shape, out_dtype, out_block, out_map, deps=(), extras=(),
        epilogue=None, n_out=1):
    n, ne, nx = len(pairs), len(extras), len(deps)
    nk = grid[-1]
    nd = len(grid)
    dn = DN[mode]
    use_acc = nk > 1 or epilogue is not None

    def body(*refs):
        e_refs = refs[2 * n:2 * n + ne]
        o_refs = refs[2 * n + ne + nx:2 * n + ne + nx + n_out]
        k = pl.program_id(nd - 1)

        def dot(i):
            return lax.dot_general(refs[2 * i][...], refs[2 * i + 1][...], dn, preferred_element_type=F32)

        if not use_acc:
            s = dot(0)
            for i in range(1, n):
                s = s + dot(i)
            o_refs[0][...] = s.astype(o_refs[0].dtype)
            return
        acc = refs[2 * n + ne + nx + n_out]

        fused_last = epilogue is not None and n == 1 and nk > 1 and mode in ("nn", "nt")

        @pl.when(k == 0)
        def _():
            acc[...] = dot(0)

        if fused_last:
            if nk > 2:
                @pl.when(jnp.logical_and(k > 0, k < nk - 1))
                def _():
                    acc[...] += dot(0)

            @pl.when(k == nk - 1)
            def _():
                rows = acc_shape[0]
                slab = _pick(rows, MXU_DIM, SUBLANE)
                for r0 in range(0, rows, slab):
                    sl = pl.ds(r0, slab)
                    part = lax.dot_general(refs[0][sl, :], refs[1][...], dn, preferred_element_type=F32)
                    vals = epilogue(acc[sl, :] + part, *[e[sl, :] for e in e_refs])
                    for r, v in zip(o_refs, vals):
                        r[sl, :] = v.astype(r.dtype)
            return

        if nk > 1:
            @pl.when(k > 0)
            def _():
                acc[...] += dot(0)

        for i in range(1, n):
            acc[...] += dot(i)

        if epilogue is None:
            @pl.when(k == nk - 1)
            def _():
                o_refs[0][...] = acc[...].astype(o_refs[0].dtype)
        else:
            @pl.when(k == nk - 1)
            def _():
                rows = acc_shape[0]
                slab = _pick(rows, EPILOGUE_ROWS, SUBLANE)
                for r0 in range(0, rows, slab):
                    sl = pl.ds(r0, slab)
                    vals = epilogue(acc[sl, :], *[e[sl, :] for e in e_refs])
                    for r, v in zip(o_refs, vals):
                        r[sl, :] = v.astype(r.dtype)

    in_specs, args = [], []
    for a, a_blk, a_map, b, b_blk, b_map in pairs:
        in_specs += [pl.BlockSpec(a_blk, a_map), pl.BlockSpec(b_blk, b_map)]
        args += [a, b]
    in_specs += [pl.BlockSpec(blk, imap) for (_, blk, imap) in extras]
    args += [e for (e, _, _) in extras]
    if not isinstance(out_dtype, (tuple, list)):
        out_dtype = (out_dtype,) * n_out
    res = pl.pallas_call(
        body,
        name=name,
        grid=grid,
        in_specs=in_specs + _dep_specs(deps, nd),
        out_specs=[pl.BlockSpec(out_block, out_map)] * n_out,
        out_shape=[jax.ShapeDtypeStruct(out_shape, dt) for dt in out_dtype],
        scratch_shapes=[pltpu.VMEM(acc_shape, F32)] if use_acc else [],
        compiler_params=_params(("parallel",) * (nd - 1) + ("arbitrary",)),
    )(*args, *deps)
    return res[0] if n_out == 1 else res


def _fit_tk(K, tm, tn, pairs=1, out_bytes=4, fixed_extra=0, tk_max=MM_TK_MAX, needs_acc=False):
    fixed = tm * tn * 4 + 2 * tm * tn * out_bytes + fixed_extra
    if K <= tk_max and not needs_acc and fixed - tm * tn * 4 + pairs * 4 * (tm + tn) * K <= MM_VMEM_BUDGET:
        return K
    tk = _pick(K, tk_max, LANE)
    while tk > LANE and fixed + pairs * 4 * (tm + tn) * tk > MM_VMEM_BUDGET:
        smaller = _pick(K, tk - LANE, LANE)
        if smaller >= tk:
            break
        tk = smaller
    return tk


def _mm2(name, mode, a, b, out_dtype, tm_pref=1024, tn_pref=1024, deps=(), extras=(), epilogue=None, n_out=1,
         tk_max=MM_TK_MAX):
    if mode == "nn":
        (M, K), N = a.shape, b.shape[1]
    elif mode == "nt":
        (M, K), N = a.shape, b.shape[0]
    else:
        (K, M), N = a.shape, b.shape[1]
    tm, tn = _pick(M, tm_pref, LANE), _pick(N, tn_pref, LANE)
    extra_bytes = sum(2 * tm * tn * jnp.dtype(e.dtype).itemsize for e in extras)
    tk = _fit_tk(K, tm, tn, out_bytes=n_out * jnp.dtype(out_dtype).itemsize, fixed_extra=extra_bytes, tk_max=tk_max,
                 needs_acc=epilogue is not None)
    if mode == "tn":
        a_blk, a_map = (tk, tm), lambda i, j, k: (k, i)
    else:
        a_blk, a_map = (tm, tk), lambda i, j, k: (i, k)
    if mode == "nt":
        b_blk, b_map = (tn, tk), lambda i, j, k: (j, k)
    else:
        b_blk, b_map = (tk, tn), lambda i, j, k: (k, j)
    return _mm(name, mode, [(a, a_blk, a_map, b, b_blk, b_map)], (M // tm, N // tn, K // tk), (tm, tn),
               (M, N), out_dtype, (tm, tn), lambda i, j, k: (i, j), deps=deps,
               extras=[(e, (tm, tn), lambda i, j, k: (i, j)) for e in extras], epilogue=epilogue, n_out=n_out)


def _ew(name, fn, grid, ins, outs, accs=(), deps=()):
    ni, no, na = len(ins), len(outs), len(accs)
    nd = len(grid)
    nx = len(deps)

    def body(*refs):
        vals = fn(*[r[...] for r in refs[:ni]])
        if not isinstance(vals, (tuple, list)):
            vals = (vals,)
        for r, v in zip(refs[ni + nx:ni + nx + no], vals[:no]):
            r[...] = v.astype(r.dtype)
        if na:
            first = pl.program_id(nd - 1) == 0
            for r, v in zip(refs[ni + nx + no:], vals[no:]):
                @pl.when(first)
                def _(r=r, v=v):
                    r[...] = v.astype(r.dtype)

                @pl.when(jnp.logical_not(first))
                def _(r=r, v=v):
                    r[...] += v.astype(r.dtype)

    all_outs = list(outs) + list(accs)
    sem = ("parallel",) * (nd - 1) + (("arbitrary",) if na else ("parallel",))
    res = pl.pallas_call(
        body,
        name=name,
        grid=grid,
        in_specs=[pl.BlockSpec(blk, imap) for (_, blk, imap) in ins] + _dep_specs(deps, nd),
        out_specs=[pl.BlockSpec(blk, imap) for (_, _, blk, imap) in all_outs],
        out_shape=[jax.ShapeDtypeStruct(s, d) for (s, d, _, _) in all_outs],
        compiler_params=_params(sem),
    )(*[a for (a, _, _) in ins], *deps)
    return res


def _cast_pad_rows(name, a, rows_out, deps):
    R, C = a.shape
    tr = _pick(math.gcd(R, rows_out), max(2 * SUBLANE, ROW_BLOCK_ELEMS // C), SUBLANE)
    n_in = R // tr

    def body(a_ref, *rest):
        o_ref = rest[len(deps)]
        i = pl.program_id(0)

        @pl.when(i < n_in)
        def _():
            o_ref[...] = a_ref[...].astype(BF16)

        @pl.when(i >= n_in)
        def _():
            o_ref[...] = jnp.zeros_like(o_ref)

    return pl.pallas_call(
        body,
        name=name,
        grid=(rows_out // tr,),
        in_specs=[pl.BlockSpec((tr, C), lambda i: (jnp.minimum(i, n_in - 1), 0))] + _dep_specs(deps, 1),
        out_specs=pl.BlockSpec((tr, C), lambda i: (i, 0)),
        out_shape=jax.ShapeDtypeStruct((rows_out, C), BF16),
        compiler_params=_params(("parallel",)),
    )(a, *deps)


def _rows(a, tm, width=None, cblk=0):
    width = a.shape[1] if width is None else width
    return (a, (tm, width), lambda i, c=cblk: (i, c))


def _vec(p):
    return (p, p.shape, lambda i: (0,) * p.ndim)


def _rows_out(shape, dtype, tm):
    return (shape, dtype, (tm, shape[1]), lambda i: (i, 0))


def _vec_out(shape):
    return (shape, F32, shape, lambda i: (0,) * len(shape))


def _row_tile(T, D):
    return _pick(T, max(SUBLANE, ROW_BLOCK_ELEMS // D), SUBLANE)


def _rms(x, g):
    r = lax.rsqrt(jnp.mean(x * x, axis=-1, keepdims=True) + EPS)
    return x * r * g


def _rms_bwd(x, g, dy):
    r = lax.rsqrt(jnp.mean(x * x, axis=-1, keepdims=True) + EPS)
    xh = x * r
    dg = jnp.sum(dy * xh, axis=0, keepdims=True)
    dxh = dy * g
    dx = r * (dxh - xh * jnp.mean(dxh * xh, axis=-1, keepdims=True))
    return dx, dg


GELU_C = math.sqrt(2.0 / math.pi)
GELU_A = 0.044715


def _sigmoid(x):
    return 0.5 * jnp.tanh(0.5 * x) + 0.5


def _gelu_tanh(x):
    return jnp.tanh(GELU_C * (x + GELU_A * x * x * x))


def _gelu(x, t):
    return 0.5 * x * (1.0 + t)


def _gelu_grad(x, t):
    return 0.5 * (1.0 + t) + 0.5 * x * (1.0 - t * t) * GELU_C * (1.0 + 3.0 * GELU_A * x * x)


def _rot(w, axis=-1):
    w1, w2 = jnp.split(w, 2, axis=axis)
    return jnp.concatenate([-w2, w1], axis=axis)


def _rot_t(dw, axis=-1):
    d1, d2 = jnp.split(dw, 2, axis=axis)
    return jnp.concatenate([d2, -d1], axis=axis)


def _coords():
    return lax.axis_index("x"), lax.axis_index("y"), lax.axis_index("c")


HBM_SPEC = pl.BlockSpec(memory_space=pltpu.HBM)
SEM_SPEC = pl.BlockSpec(memory_space=pltpu.SEMAPHORE)
ANY_SPEC = pl.BlockSpec(memory_space=pl.ANY)
EFFECT = pltpu.SideEffectType.DATAFLOW_SIDE_EFFECTING
TOKEN = (SUBLANE, LANE)


def _my_slot():
    xi, yi, ci = _coords()
    return 4 * xi + 2 * yi + ci


def _own_slot_filled(block):
    land = lax.empty((N_DEV,) + block.shape, block.dtype)
    return lax.dynamic_update_slice(land, block[None], (_my_slot(),) + (0,) * block.ndim)


def _split_start(name, n, n_copies, issue, srcs, lands, deps):
    def body(*refs):
        src_refs, land_refs = refs[:n], refs[n:2 * n]
        outs = refs[2 * n + len(deps):]
        send, recv, token = outs[:n], outs[n:2 * n], outs[4 * n]
        for t in range(n):
            issue(t, src_refs[t], land_refs[t], send[t], recv[t])
        token[...] = jnp.zeros_like(token)

    sems = [pltpu.SemaphoreType.DMA((n_copies,))] * (2 * n)
    res = pl.pallas_call(
        body,
        name=name,
        in_specs=[HBM_SPEC] * (2 * n) + [ANY_SPEC] * len(deps),
        out_specs=[SEM_SPEC] * (2 * n) + [HBM_SPEC] * (2 * n) + [pl.BlockSpec(memory_space=pltpu.VMEM)],
        out_shape=sems + [pltpu.HBM(a.shape, a.dtype) for a in list(srcs) + list(lands)]
        + [jax.ShapeDtypeStruct(TOKEN, F32)],
        input_output_aliases={i: 2 * n + i for i in range(2 * n)},
        compiler_params=pltpu.CompilerParams(has_side_effects=EFFECT),
    )(*[pltpu.with_memory_space_constraint(a, pltpu.HBM) for a in list(srcs) + list(lands)], *deps)
    state = dict(n=n, n_copies=n_copies, send=res[:n], recv=res[n:2 * n], srcs=res[2 * n:3 * n],
                 lands=res[3 * n:4 * n])
    return state, res[4 * n]


def _split_wait(name, state, unit_src, after):
    n, n_copies = state["n"], state["n_copies"]

    def body(*refs):
        src_refs, land_refs = refs[:n], refs[n:2 * n]
        send, recv = refs[2 * n:3 * n], refs[3 * n:4 * n]
        me = _coords()
        for t in range(n):
            for k in range(n_copies):
                cp = pltpu.make_async_remote_copy(
                    src_ref=unit_src(src_refs[t]), dst_ref=land_refs[t].at[0], send_sem=send[t].at[k],
                    recv_sem=recv[t].at[k], device_id=me, device_id_type=MESH)
                cp.wait_send()
                cp.wait_recv()

    bufs = list(state["srcs"]) + list(state["lands"])
    res = pl.pallas_call(
        body,
        name=name,
        in_specs=[HBM_SPEC] * (2 * n) + [SEM_SPEC] * (2 * n) + [ANY_SPEC],
        out_specs=[HBM_SPEC] * (2 * n),
        out_shape=[pltpu.HBM(a.shape, a.dtype) for a in bufs],
        input_output_aliases={i: i for i in range(2 * n)},
        compiler_params=pltpu.CompilerParams(has_side_effects=EFFECT),
    )(*bufs, *state["send"], *state["recv"], after)
    return list(res[n:])


def _ag_start(name, shards, deps):
    def issue(t, x_ref, land_ref, send, recv):
        xi, yi, ci = _coords()
        to = [(xi, yi, 1 - ci), (1 - xi, yi, ci), (xi, 1 - yi, ci), (1 - xi, 1 - yi, ci)]
        for k, dev in enumerate(to):
            pltpu.make_async_remote_copy(src_ref=x_ref, dst_ref=land_ref.at[4 * xi + 2 * yi + ci], send_sem=send.at[k],
                                         recv_sem=recv.at[k], device_id=dev, device_id_type=MESH).start()

    return _split_start(name, len(shards), 4, issue, shards, [_own_slot_filled(s) for s in shards], deps)


def _ag_mid(name, state, after):
    lands = _split_wait(name + "_w1", state, lambda r: r, after)

    def issue(t, src_ref, land_ref, send, recv):
        xi, yi, ci = _coords()
        for k, (px, py) in enumerate([(1 - xi, yi), (xi, 1 - yi), (1 - xi, 1 - yi)]):
            rows = land_ref.at[4 * px + 2 * py + ci]
            pltpu.make_async_remote_copy(src_ref=rows, dst_ref=rows, send_sem=send.at[k], recv_sem=recv.at[k],
                                         device_id=(xi, yi, 1 - ci), device_id_type=MESH).start()

    n = len(lands)

    def body(*refs):
        land_refs = refs[:n]
        outs = refs[n:]
        send, recv, token = outs[:n], outs[n:2 * n], outs[3 * n]
        for t in range(n):
            issue(t, land_refs[t], land_refs[t], send[t], recv[t])
        token[...] = jnp.zeros_like(token)

    res = pl.pallas_call(
        body,
        name=name + "_s2",
        in_specs=[HBM_SPEC] * n,
        out_specs=[SEM_SPEC] * (2 * n) + [HBM_SPEC] * n + [pl.BlockSpec(memory_space=pltpu.VMEM)],
        out_shape=[pltpu.SemaphoreType.DMA((3,))] * (2 * n) + [pltpu.HBM(a.shape, a.dtype) for a in lands]
        + [jax.ShapeDtypeStruct(TOKEN, F32)],
        input_output_aliases={i: 2 * n + i for i in range(n)},
        compiler_params=pltpu.CompilerParams(has_side_effects=EFFECT),
    )(*lands)
    state2 = dict(n=n, send=res[:n], recv=res[n:2 * n], lands=res[2 * n:3 * n])
    return state2, res[3 * n]


def _ag_finish(name, state2, after):
    n = state2["n"]

    def body(*refs):
        land_refs, send, recv = refs[:n], refs[n:2 * n], refs[2 * n:3 * n]
        me = _coords()
        for t in range(n):
            for k in range(3):
                cp = pltpu.make_async_remote_copy(
                    src_ref=land_refs[t].at[0], dst_ref=land_refs[t].at[0], send_sem=send[t].at[k],
                    recv_sem=recv[t].at[k], device_id=me, device_id_type=MESH)
                cp.wait_send()
                cp.wait_recv()

    res = pl.pallas_call(
        body,
        name=name + "_w2",
        in_specs=[HBM_SPEC] * n + [SEM_SPEC] * (2 * n) + [ANY_SPEC],
        out_specs=[HBM_SPEC] * n,
        out_shape=[pltpu.HBM(a.shape, a.dtype) for a in state2["lands"]],
        input_output_aliases={i: i for i in range(n)},
        compiler_params=pltpu.CompilerParams(has_side_effects=EFFECT),
    )(*state2["lands"], *state2["send"], *state2["recv"], after)
    return list(res)


def _a2a_start(name, parts, deps):
    def issue(t, x_ref, land_ref, send, recv):
        xi, yi, ci = _coords()
        for k in range(N_DEV - 1):
            px = 1 - xi if (k + 1) & 4 else xi
            py = 1 - yi if (k + 1) & 2 else yi
            pc = 1 - ci if (k + 1) & 1 else ci
            pltpu.make_async_remote_copy(src_ref=x_ref.at[4 * px + 2 * py + pc], dst_ref=land_ref.at[4 * xi + 2 * yi + ci],
                                         send_sem=send.at[k], recv_sem=recv.at[k], device_id=(px, py, pc),
                                         device_id_type=MESH).start()

    lands = [_own_slot_filled(lax.dynamic_index_in_dim(p, _my_slot(), 0, keepdims=False)) for p in parts]
    return _split_start(name, len(parts), N_DEV - 1, issue, parts, lands, deps)


def _a2a_finish(name, state, after):
    return _split_wait(name + "_w", state, lambda r: r.at[0], after)


N_CHIP = N_DEV // 2


def _rs_pair_start(name, parts, deps):
    def issue(t, x_ref, land_ref, send, recv):
        xi, yi, ci = _coords()
        for k in range(N_CHIP):
            pltpu.make_async_remote_copy(src_ref=x_ref.at[2 * k + (1 - ci)], dst_ref=land_ref.at[k], send_sem=send.at[k],
                                         recv_sem=recv.at[k], device_id=(xi, yi, 1 - ci), device_id_type=MESH).start()

    lands = [lax.empty((N_CHIP,) + p.shape[1:], p.dtype) for p in parts]
    return _split_start(name, len(parts), N_CHIP, issue, parts, lands, deps)


def _rs_pair_sum(name, state, after):
    lands = _split_wait(name + "_w", state, lambda r: r.at[0], after)
    sums = []
    for t, (part, land) in enumerate(zip(state["srcs"], lands)):
        _, R, C = land.shape
        pack = SUBLANE * (4 // jnp.dtype(land.dtype).itemsize)
        if R % pack == 0:
            tr, tc = _pick(R, max(pack, ROW_BLOCK_ELEMS // C), pack), C
        else:
            tr, tc = R, _pick(C, max(LANE, ROW_BLOCK_ELEMS // R // LANE * LANE), LANE)

        def fn(mine, theirs):
            own = jnp.where(lax.axis_index("c") == 0, mine[0], mine[1])
            return own.astype(F32) + theirs.astype(F32)

        sums.append(_ew(f"{name}_sum{t}", fn, (N_CHIP, R // tr, C // tc),
                        [(part.reshape((N_CHIP, 2) + part.shape[1:]), (None, 2, tr, tc), lambda k, i, j: (k, 0, i, j)),
                         (land, (None, tr, tc), lambda k, i, j: (k, i, j))],
                        [(land.shape, land.dtype, (None, tr, tc), lambda k, i, j: (k, i, j))])[0])
    return sums


def _rs_chip_start(name, sums, deps):
    def issue(t, x_ref, land_ref, send, recv):
        xi, yi, ci = _coords()
        for k, (px, py) in enumerate([(1 - xi, yi), (xi, 1 - yi), (1 - xi, 1 - yi)]):
            pltpu.make_async_remote_copy(src_ref=x_ref.at[2 * px + py], dst_ref=land_ref.at[2 * xi + yi], send_sem=send.at[k],
                                         recv_sem=recv.at[k], device_id=(px, py, ci), device_id_type=MESH).start()

    xi, yi, _ = _coords()
    lands = []
    for s in sums:
        own = lax.dynamic_index_in_dim(s, 2 * xi + yi, 0, keepdims=True)
        lands.append(lax.dynamic_update_slice(lax.empty(s.shape, s.dtype), own, (2 * xi + yi,) + (0,) * (s.ndim - 1)))
    return _split_start(name, len(sums), N_CHIP - 1, issue, sums, lands, deps)


def _rs_chip_finish(name, state, after):
    return _split_wait(name + "_w", state, lambda r: r.at[0], after)


QK_SCALE = 1.0 / math.sqrt(QK_HEAD)
LOG2E = 1.0 / math.log(2.0)


def _roped_q(q_ext, cos, sin):
    q = jnp.concatenate([q_ext[:, :LANE], q_ext[:, LANE:2 * LANE] * cos + q_ext[:, 2 * LANE:] * sin], axis=1)
    return (q * (QK_SCALE * LOG2E)).astype(BF16)


def _attn_fwd(q_ext, k_full, v_aug, cosp, sinp):
    H, T, _ = k_full.shape
    tq, tkv = _pick(T, ATTN_TQ, LANE), _pick(T, ATTN_TKV, LANE)

    def body(q_ref, cos_ref, sin_ref, k_ref, v_ref, o_ref, lse_ref):
        q = _roped_q(q_ref[...], cos_ref[...], sin_ref[...])
        m = jnp.full((tq, 1), NEG_BIG, F32)
        acc = jnp.zeros((tq, 2 * V_DIM), F32)
        for c in range(T // tkv):
            keys = pl.ds(c * tkv, tkv)
            s = lax.dot_general(q, k_ref[keys, :], DN["nt"], preferred_element_type=F32)
            m_new = jnp.maximum(m, jnp.max(s, axis=-1, keepdims=True))
            p = jnp.exp2(s - m_new).astype(BF16)
            acc = jnp.exp2(m - m_new) * acc + jnp.dot(p, v_ref[keys, :], preferred_element_type=F32)
            m = m_new
        l = acc[:, V_DIM:]
        o_ref[...] = acc[:, :V_DIM] / l
        lse_ref[...] = m + jnp.log2(jnp.max(l, axis=-1, keepdims=True))

    return pl.pallas_call(
        body,
        name="attn_fwd",
        grid=(H, T // tq),
        in_specs=[pl.BlockSpec((tq, Q_EXT), lambda h, i: (i, h)),
                  pl.BlockSpec((tq, LANE), lambda h, i: (i, 0)),
                  pl.BlockSpec((tq, LANE), lambda h, i: (i, 0)),
                  pl.BlockSpec((None, T, K_FULL), lambda h, i: (h, 0, 0)),
                  pl.BlockSpec((None, T, 2 * V_DIM), lambda h, i: (h, 0, 0))],
        out_specs=[pl.BlockSpec((tq, V_DIM), lambda h, i: (i, h)),
                   pl.BlockSpec((None, tq, 1), lambda h, i: (h, i, 0))],
        out_shape=[jax.ShapeDtypeStruct((T, H * V_DIM), F32), jax.ShapeDtypeStruct((H, T, 1), F32)],
        compiler_params=_params(("parallel", "parallel")),
    )(q_ext, cosp, sinp, k_full, v_aug)


def _attn_bwd(q_ext, k_full, v_aug, cosp, sinp, d_out, out, lse_rows):
    H, T, _ = k_full.shape
    tq, tkv = _pick(T, ATTN_TQ, LANE), _pick(T, ATTN_TKV, LANE)
    n_q = T // tq

    def body(q_ref, cos_ref, sin_ref, k_ref, v_ref, do_ref, o_ref, lse_ref, cos_all, sin_all,
             dq_ref, dkv_ref, dkr_ref, dk_acc, dv_acc):
        h, i = pl.program_id(0), pl.program_id(1)

        @pl.when(i == 0)
        def _():
            dk_acc[...] = jnp.zeros_like(dk_acc)
            dv_acc[...] = jnp.zeros_like(dv_acc)

        cos, sin = cos_ref[...], sin_ref[...]
        q, do, lse = _roped_q(q_ref[...], cos, sin), do_ref[...], lse_ref[...]
        delta = lax.dot_general(jnp.ones((SUBLANE, V_DIM), F32), do.astype(F32) * o_ref[...], DN["nt"],
                                preferred_element_type=F32, precision=lax.Precision.HIGHEST)[0:1]
        dq = jnp.zeros((tq, K_FULL), F32)
        for c in range(T // tkv):
            keys = pl.ds(c * tkv, tkv)
            kc = k_ref[keys, :]
            pt = jnp.exp2(lax.dot_general(kc, q, DN["nt"], preferred_element_type=F32) - lse)
            dpt = lax.dot_general(v_ref[keys, pl.ds(0, V_DIM)], do, DN["nt"], preferred_element_type=F32)
            dst = (pt * (dpt - delta)).astype(BF16)
            dv_acc[keys, :] += jnp.dot(pt.astype(BF16), do, preferred_element_type=F32)
            dk_acc[keys, :] += jnp.dot(dst, q, preferred_element_type=F32)
            dq = dq + lax.dot_general(dst, kc, DN["tn"], preferred_element_type=F32)
        r = dq[:, LANE:]
        dq_ref[...] = (jnp.concatenate([dq[:, :LANE], r * cos, r * sin], axis=1) * QK_SCALE).astype(BF16)

        @pl.when(i == n_q - 1)
        def _():
            dk = dk_acc[...] * (1.0 / LOG2E)
            dkv_ref[...] = jnp.concatenate([dk[:, :LANE], dv_acc[...]], axis=1).astype(BF16)
            rk = dk[:, LANE:]
            dkr = jnp.concatenate([rk * cos_all[...], rk * sin_all[...]], axis=1)

            @pl.when(h == 0)
            def _():
                dkr_ref[...] = dkr

            @pl.when(h != 0)
            def _():
                dkr_ref[...] += dkr

    return pl.pallas_call(
        body,
        name="attn_bwd",
        grid=(H, n_q),
        in_specs=[pl.BlockSpec((tq, Q_EXT), lambda h, i: (i, h)),
                  pl.BlockSpec((tq, LANE), lambda h, i: (i, 0)),
                  pl.BlockSpec((tq, LANE), lambda h, i: (i, 0)),
                  pl.BlockSpec((None, T, K_FULL), lambda h, i: (h, 0, 0)),
                  pl.BlockSpec((None, T, 2 * V_DIM), lambda h, i: (h, 0, 0)),
                  pl.BlockSpec((tq, V_DIM), lambda h, i: (i, h)),
                  pl.BlockSpec((tq, V_DIM), lambda h, i: (i, h)),
                  pl.BlockSpec((None, 1, tq), lambda h, i: (h, 0, i)),
                  pl.BlockSpec((T, LANE), lambda h, i: (0, 0)),
                  pl.BlockSpec((T, LANE), lambda h, i: (0, 0))],
        out_specs=[pl.BlockSpec((tq, Q_EXT), lambda h, i: (i, h)),
                   pl.BlockSpec((T, K_FULL), lambda h, i: (0, h)),
                   pl.BlockSpec((T, K_FULL), lambda h, i: (0, 0))],
        out_shape=[jax.ShapeDtypeStruct((T, H * Q_EXT), BF16), jax.ShapeDtypeStruct((T, H * K_FULL), BF16),
                   jax.ShapeDtypeStruct((T, K_FULL), F32)],
        scratch_shapes=[pltpu.VMEM((T, K_FULL), F32), pltpu.VMEM((T, V_DIM), F32)],
        compiler_params=_params(("arbitrary", "arbitrary")),
    )(q_ext, cosp, sinp, k_full, v_aug, d_out, out, lse_rows, cosp, sinp)


def _sgu_forward_math(pb, lng, lnb, ws, bsb, gw, gh):
    u_raw, v_raw = pb[:, :gw], pb[:, gw:]
    t_u, t_v = _gelu_tanh(u_raw), _gelu_tanh(v_raw)
    u, v = _gelu(u_raw, t_u), _gelu(v_raw, t_v)
    vc = v - jnp.mean(v, axis=-1, keepdims=True)
    rstd = lax.rsqrt(jnp.mean(vc * vc, axis=-1, keepdims=True) + EPS)
    vhat = vc * rstd
    vln = vhat * lng + lnb
    hd = gw // gh
    s = jnp.concatenate(
        [jnp.dot(ws[g], vln[:, g * hd:(g + 1) * hd].astype(BF16), preferred_element_type=F32) + bsb[g]
         for g in range(gh)], axis=1)
    return (u_raw, t_u), (v_raw, t_v), u, rstd, vhat, vln, s


def _sgu_fwd(proj_b, lng, lnb, ws, bsb, gg):
    T, gw2 = proj_b.shape
    gw, gh = gw2 // 2, ws.shape[0]

    def fn(pb, lng, lnb, ws, bsb, gg):
        _, _, u, _, _, _, s = _sgu_forward_math(pb, lng, lnb, ws, bsb, gw, gh)
        return _rms(u * s, gg)

    return _ew("sgu_fwd", fn, (T // CHUNK,),
               [_rows(proj_b, CHUNK), _vec(lng), _vec(lnb), _vec(ws), _vec(bsb), _vec(gg)],
               [_rows_out((T, gw), BF16, CHUNK)])[0]


def _sgu_bwd(proj_b, d_mixed, lng, lnb, ws, ws_t, bsb, gg):
    T, gw2 = proj_b.shape
    gw, gh = gw2 // 2, ws.shape[0]
    hd = gw // gh

    def fn(pb, dmg, lng, lnb, ws, ws_t, bsb, gg):
        u_pre, v_pre, u, rstd, vhat, vln, s = _sgu_forward_math(pb, lng, lnb, ws, bsb, gw, gh)
        dgo, dgg = _rms_bwd(u * s, gg, dmg)
        du, ds = dgo * s, dgo * u
        ones = jnp.ones((SUBLANE, hd), F32)
        dws, dbs, dvln = [], [], []
        for g in range(gh):
            ds_g = ds[:, g * hd:(g + 1) * hd]
            ds_gb = ds_g.astype(BF16)
            dws.append(lax.dot_general(ds_gb, vln[:, g * hd:(g + 1) * hd].astype(BF16), DN["nt"],
                                       preferred_element_type=F32)[None])
            dbs.append(lax.dot_general(ones, ds_g, DN["nt"], preferred_element_type=F32,
                                       precision=lax.Precision.HIGHEST)[None])
            dvln.append(jnp.dot(ws_t[g], ds_gb, preferred_element_type=F32))
        dvln = jnp.concatenate(dvln, axis=1)
        dlng = jnp.sum(dvln * vhat, axis=0, keepdims=True)
        dlnb = jnp.sum(dvln, axis=0, keepdims=True)
        dvh = dvln * lng
        dv = rstd * (dvh - jnp.mean(dvh, axis=-1, keepdims=True) - vhat * jnp.mean(dvh * vhat, axis=-1, keepdims=True))
        d_pb = jnp.concatenate([du * _gelu_grad(*u_pre), dv * _gelu_grad(*v_pre)], axis=1)
        return d_pb, dgg, jnp.concatenate(dws, axis=0), jnp.concatenate(dbs, axis=0), dlng, dlnb

    return _ew("sgu_bwd", fn, (T // CHUNK,),
               [_rows(proj_b, CHUNK), _rows(d_mixed, CHUNK, gw, 1), _vec(lng), _vec(lnb), _vec(ws), _vec(ws_t),
                _vec(bsb), _vec(gg)],
               [_rows_out((T, gw2), BF16, CHUNK)],
               [_vec_out((1, gw)), _vec_out((gh, CHUNK, CHUNK)), _vec_out((gh, SUBLANE, CHUNK)), _vec_out((1, gw)),
                _vec_out((1, gw))])


def _adamw(name, parts, w, m, v):
    R, C = w.shape
    pack = SUBLANE * (4 // jnp.dtype(parts.dtype).itemsize)
    if R % pack == 0:
        tm, tc = _pick(R, max(pack, ADAM_BLOCK_ELEMS // C), pack), C
    else:
        parts = parts[:, :R]
        tm, tc = R, _pick(C, max(LANE, ADAM_BLOCK_ELEMS // R // LANE * LANE), LANE)

    n_parts = parts.shape[0]

    def fn(p, w, m, v):
        g = p[0].astype(F32)
        for i in range(1, n_parts):
            g = g + p[i].astype(F32)
        m2 = ADAM_B1 * m + (1.0 - ADAM_B1) * g
        v2 = ADAM_B2 * v + (1.0 - ADAM_B2) * (g * g)
        m_hat = m2 / (1.0 - ADAM_B1 ** ADAM_STEP)
        v_hat = v2 / (1.0 - ADAM_B2 ** ADAM_STEP)
        delta = -ADAM_LR * (m_hat / (jnp.sqrt(v_hat) + ADAM_EPS) + ADAM_WD * w)
        return g, delta, m2, v2

    blk = lambda a: (a, (tm, tc), lambda i, j: (i, j))
    return _ew(name, fn, (R // tm, C // tc),
               [(parts, (n_parts, tm, tc), lambda i, j: (0, i, j)), blk(w), blk(m), blk(v)],
               [((R, C), F32, (tm, tc), lambda i, j: (i, j))] * 4)


def kernel(x, positions, pre_mix_norm, w_in, q_norm, kv_norm, w_uq, w_ukv, v_ln_gain, v_ln_bias, w_spatial, b_spatial, attn_out_norm, gmlp_out_norm, w_out, post_mix_norm, pre_ffn_norm, w_gate, w_up, w_down, post_ffn_norm, loss_target, m_pre_mix_norm, m_w_in, m_q_norm, m_kv_norm, m_w_uq, m_w_ukv, m_v_ln_gain, m_v_ln_bias, m_w_spatial, m_b_spatial, m_attn_out_norm, m_gmlp_out_norm, m_w_out, m_post_mix_norm, m_pre_ffn_norm, m_w_gate, m_w_up, m_w_down, m_post_ffn_norm, v_pre_mix_norm, v_w_in, v_q_norm, v_kv_norm, v_w_uq, v_w_ukv, v_v_ln_gain, v_v_ln_bias, v_w_spatial, v_b_spatial, v_attn_out_norm, v_gmlp_out_norm, v_w_out, v_post_mix_norm, v_pre_ffn_norm, v_w_gate, v_w_up, v_w_down, v_post_ffn_norm):
    given = dict(locals())
    T, D = x.shape[1], x.shape[2]
    QL, KVL = w_uq.shape[1], w_ukv.shape[1]
    GW, GH = v_ln_gain.shape[1], w_spatial.shape[1]
    IWS = w_in.shape[2]
    IW = IWS * N_DEV
    H = w_ukv.shape[2] * N_DEV // (NOPE_DIM + V_DIM)
    AW = H * V_DIM
    FS = w_gate.shape[2]
    FP = _round_up(FS, LANE)
    C0 = QL + KVL
    assert IW == C0 + ROPE_DIM + 2 * GW and AW + GW == D and AW == GW
    assert QL % KVL == 0 and C0 % K_FULL == 0 and w_spatial.shape[2] == CHUNK and T % CHUNK == 0
    assert w_uq.shape[2] * N_DEV == H * QK_HEAD and D % N_DEV == 0

    x2, tgt = x[0], loss_target[0]
    tm_d = _row_tile(T, D)

    inv_freq = 1.0 / (ROPE_THETA ** (jnp.arange(0, ROPE_DIM, 2, dtype=F32) / ROPE_DIM))
    ang = positions[0].astype(F32)[:, None] * inv_freq
    zpad = jnp.zeros((T, LANE - ROPE_DIM), F32)
    cosp = jnp.concatenate([jnp.cos(ang), jnp.cos(ang), zpad], axis=1)
    sinp = jnp.concatenate([jnp.sin(ang), jnp.sin(ang), zpad], axis=1)

    def shard_t(a):
        return a[0].T

    st_a, tok = _ag_start("ag_a", [shard_t(w_in).astype(BF16), w_uq[0].astype(BF16), w_ukv[0].astype(BF16)], [])
    st_o, tok = _ag_start("ag_o", [w_out[0].astype(BF16)], [tok])
    st_g, tok = _ag_start("ag_g", [_cast_pad_rows("pad_w_gate", shard_t(w_gate), FP, [tok])], [tok])
    st_u, tok = _ag_start("ag_u", [_cast_pad_rows("pad_w_up", shard_t(w_up), FP, [tok])], [tok])
    st_d, tok = _ag_start("ag_d", [_cast_pad_rows("pad_w_down", w_down[0], FP, [tok])], [tok])

    xn = _ew("rms_x", lambda x, g: _rms(x, g), (T // tm_d,), [_rows(x2, tm_d), _vec(pre_mix_norm)],
             [_rows_out((T, D), BF16, tm_d)], deps=[tok])[0]
    st_a, tok = _ag_mid("ag_a", st_a, xn)
    g_in, g_uq, g_ukv = _ag_finish("ag_a", st_a, tok)

    w_in_t = g_in.reshape(IW, D)
    kr_w = w_in_t[C0:C0 + ROPE_DIM]
    z64 = jnp.zeros((LANE - ROPE_DIM, D), BF16)
    w_a_t = jnp.concatenate([w_in_t[:C0], kr_w, z64, _rot(kr_w, 0), z64], axis=0)
    w_b_t = w_in_t[C0 + ROPE_DIM:]
    wq = g_uq.transpose(1, 0, 2).reshape(QL, H, QK_HEAD)
    zq = jnp.zeros((QL, H, LANE - ROPE_DIM), BF16)
    wq_ext = jnp.concatenate([wq[..., :NOPE_DIM], wq[..., NOPE_DIM:], zq, _rot(wq[..., NOPE_DIM:]), zq],
                             axis=-1).reshape(QL, H * Q_EXT)
    wkv = g_ukv.transpose(1, 0, 2).reshape(KVL, H * (NOPE_DIM + V_DIM))

    ws = w_spatial[0].astype(BF16)
    ws_t = jnp.swapaxes(ws, 1, 2)
    bsb = jnp.broadcast_to(b_spatial[0][:, :, None], (GH, CHUNK, GW // GH))

    proj_a = _mm2("mm_proj_a", "nt", xn, w_a_t, F32, tn_pref=w_a_t.shape[0])
    proj_b = _mm2("mm_proj_b", "nt", xn, w_b_t, F32)

    tm_q = _row_tile(T, QL)
    qn = _ew("rms_q", lambda x, g: _rms(x, g), (T // tm_q,), [_rows(proj_a, tm_q, QL, 0), _vec(q_norm)],
             [_rows_out((T, QL), BF16, tm_q)])[0]
    kvn = _ew("rms_kv", lambda x, g: _rms(x, g), (T // tm_q,), [_rows(proj_a, tm_q, KVL, QL // KVL), _vec(kv_norm)],
              [_rows_out((T, KVL), BF16, tm_q)])[0]
    q_ext = _mm2("mm_q", "nn", qn, wq_ext, F32)
    kv = _mm2("mm_kv", "nn", kvn, wkv, F32)

    tm_h = _pick(T, 512, SUBLANE)
    hb = _pick(H, 4, 1)

    def make_kv(kv, kr, c, s):
        k_rope = kr[:, :LANE] * c + kr[:, LANE:] * s
        ones = jnp.ones((kv.shape[0], V_DIM), F32)
        ks = [jnp.concatenate([kv[:, j * K_FULL:j * K_FULL + LANE], k_rope], axis=1) for j in range(hb)]
        vs = [jnp.concatenate([kv[:, j * K_FULL + LANE:(j + 1) * K_FULL], ones], axis=1) for j in range(hb)]
        return jnp.stack(ks), jnp.stack(vs)

    k_full, v_aug = _ew("make_kv", make_kv, (T // tm_h, H // hb),
                        [(kv, (tm_h, hb * K_FULL), lambda i, h: (i, h)),
                         (proj_a, (tm_h, K_FULL), lambda i, h: (i, C0 // K_FULL)),
                         (cosp, (tm_h, LANE), lambda i, h: (i, 0)), (sinp, (tm_h, LANE), lambda i, h: (i, 0))],
                        [((H, T, K_FULL), BF16, (hb, tm_h, K_FULL), lambda i, h: (h, i, 0)),
                         ((H, T, 2 * V_DIM), BF16, (hb, tm_h, 2 * V_DIM), lambda i, h: (h, i, 0))])

    a_out, lse = _attn_fwd(q_ext, k_full, v_aug, cosp, sinp)
    st_o, tok_o = _ag_mid("ag_o", st_o, a_out)
    tm_a = _row_tile(T, AW)
    mixed_a = _ew("rms_attn", lambda x, g: _rms(x, g), (T // tm_a,), [_rows(a_out, tm_a), _vec(attn_out_norm)],
                  [_rows_out((T, AW), BF16, tm_a)], deps=[tok_o])[0]
    mixed_g = _sgu_fwd(proj_b, v_ln_gain, v_ln_bias, ws, bsb, gmlp_out_norm)
    wo = _ag_finish("ag_o", st_o, mixed_g)[0].reshape(D, D)
    st_g, tok_gu = _ag_mid("ag_g", st_g, wo)

    tm_, tn_ = _pick(T, 1024, LANE), _pick(D, 1024, LANE)
    tk_ = _fit_tk(AW, tm_, tn_, pairs=2)
    nka = AW // tk_
    mix_out = _mm("mm_out", "nn",
                  [(mixed_a, (tm_, tk_), lambda i, j, k: (i, k), wo, (tk_, tn_), lambda i, j, k: (k, j)),
                   (mixed_g, (tm_, tk_), lambda i, j, k: (i, k), wo, (tk_, tn_), lambda i, j, k: (k + nka, j))],
                  (T // tm_, D // tn_, nka), (tm_, tn_), (T, D), F32, (tm_, tn_), lambda i, j, k: (i, j), deps=[tok_gu])

    def post_mix(x, mo, g3, g4):
        h = x + _rms(mo, g3)
        return h, _rms(h, g4)

    h_res, hn = _ew("post_mix", post_mix, (T // tm_d,),
                    [_rows(x2, tm_d), _rows(mix_out, tm_d), _vec(post_mix_norm), _vec(pre_ffn_norm)],
                    [_rows_out((T, D), F32, tm_d), _rows_out((T, D), BF16, tm_d)])

    F = N_DEV * FP
    tn_f = _pick(F, FFN_TILE_MAX, MXU_DIM)
    tn_e = _pick(F, FFN_TILE_MAX // 2, LANE)

    def swiglu(u, g):
        g = g.astype(F32)
        return u, g * _sigmoid(g) * u

    st_u, tok_u = _ag_mid("ag_u", st_u, hn)
    wg_t = _ag_finish("ag_g", st_g, tok_u)[0].reshape(F, D)
    gate = _mm2("mm_gate", "nt", hn, wg_t, BF16, tn_pref=tn_f)
    wu_t = _ag_finish("ag_u", st_u, gate)[0].reshape(F, D)
    st_d, tok_d = _ag_mid("ag_d", st_d, wu_t)
    up, act = _mm2("mm_up", "nt", hn, wu_t, BF16, tn_pref=tn_e, deps=[tok_d], extras=[gate],
                   epilogue=swiglu, n_out=2)
    wd_f = _ag_finish("ag_d", st_d, act)[0].reshape(F, D)
    ffn = _mm2("mm_down", "nn", act, wd_f, F32, tk_max=tn_f)

    def loss_head(h, f, t, g5):
        y = h + _rms(f, g5)
        diff = y - t
        dy = diff / D
        d_f, dg5 = _rms_bwd(f, g5, dy)
        lsum = jnp.sum(jnp.sum(diff * diff, axis=0, keepdims=True), axis=1, keepdims=True) * (0.5 / D)
        return dy, d_f, dg5, lsum

    dy, d_ffn, dg_post_ffn, loss_part = _ew(
        "loss_head", loss_head, (T // tm_d,),
        [_rows(h_res, tm_d), _rows(ffn, tm_d), _rows(tgt, tm_d), _vec(post_ffn_norm)],
        [_rows_out((T, D), F32, tm_d), _rows_out((T, D), BF16, tm_d)], [_vec_out((1, D)), _vec_out((1, 1))])

    def swiglu_bwd(da, g, u):
        g, u = g.astype(F32), u.astype(F32)
        sg = _sigmoid(g)
        return da * u * sg * (1.0 + g * (1.0 - sg)), da * g * sg

    def ffn_dw(name, z, y, deps):
        return _mm2(name, "tn", z, y, BF16, tm_pref=tn_f, deps=deps).reshape(N_DEV, FP, D)

    x_d, tok = _a2a_start("a2a_d", [ffn_dw("mm_dwd", act, d_ffn, [])], [])
    d_gate, d_up = _mm2("mm_dact", "nt", d_ffn, wd_f, BF16, tn_pref=tn_e, deps=[tok], extras=[gate, up],
                        epilogue=swiglu_bwd, n_out=2)
    x_g, tok = _a2a_start("a2a_g", [ffn_dw("mm_dwg", d_gate, hn, [])], [])
    x_u, tok = _a2a_start("a2a_u", [ffn_dw("mm_dwu", d_up, hn, [tok])], [])
    d_hn_g = _mm2("mm_dhn_g", "nn", d_gate, wg_t, F32, tk_max=tn_f, deps=[tok])
    d_hn = _mm2("mm_dhn_u", "nn", d_up, wu_t, F32, tk_max=tn_f, extras=[d_hn_g], epilogue=lambda s, g: (s + g,))

    def pre_ffn_bwd(h, dhn, dy, mo, g4, g3):
        d1, dg4 = _rms_bwd(h, g4, dhn)
        dh = dy + d1
        dmo, dg3 = _rms_bwd(mo, g3, dh)
        return dh, dmo, dg4, dg3

    dh, d_mo, dg_pre_ffn, dg_post_mix = _ew(
        "pre_ffn_bwd", pre_ffn_bwd, (T // tm_d,),
        [_rows(h_res, tm_d), _rows(d_hn, tm_d), _rows(dy, tm_d), _rows(mix_out, tm_d), _vec(pre_ffn_norm),
         _vec(post_mix_norm)],
        [_rows_out((T, D), F32, tm_d), _rows_out((T, D), BF16, tm_d)], [_vec_out((1, D)), _vec_out((1, D))])

    d_mixed = _mm2("mm_dmixed", "nt", d_mo, wo, F32)
    dwo = jnp.concatenate([_mm2("mm_dwo_a", "tn", mixed_a, d_mo, BF16), _mm2("mm_dwo_g", "tn", mixed_g, d_mo, BF16)],
                          axis=0)
    x_o, tok = _a2a_start("a2a_o", [dwo.reshape(N_DEV, D // N_DEV, D)], [])

    d_a_out, dg_attn = _ew("rms_attn_bwd", lambda a, g, d: _rms_bwd(a, g, d), (T // tm_a,),
                           [_rows(a_out, tm_a), _vec(attn_out_norm), _rows(d_mixed, tm_a, AW, 0)],
                           [_rows_out((T, AW), BF16, tm_a)], [_vec_out((1, AW))], deps=[tok])
    d_pb, dg_gmlp, dws, dbs, dlng, dlnb = _sgu_bwd(proj_b, d_mixed, v_ln_gain, v_ln_bias, ws, ws_t, bsb, gmlp_out_norm)

    dq_ext, d_kv, d_kr = _attn_bwd(q_ext, k_full, v_aug, cosp, sinp, d_a_out, a_out, lse.reshape(H, 1, T))

    d_qn = _mm2("mm_dqn", "nt", dq_ext, wq_ext, F32)
    dwq_ext = _mm2("mm_dwq", "tn", qn, dq_ext, F32)
    d_kvn = _mm2("mm_dkvn", "nt", d_kv, wkv, F32)
    dwkv = _mm2("mm_dwkv", "tn", kvn, d_kv, BF16)

    d_qc, dg_q = _ew("rms_q_bwd", lambda a, g, d: _rms_bwd(a, g, d), (T // tm_q,),
                     [_rows(proj_a, tm_q, QL, 0), _vec(q_norm), _rows(d_qn, tm_q)],
                     [_rows_out((T, QL), BF16, tm_q)], [_vec_out((1, QL))])
    d_kvc, dg_kv = _ew("rms_kv_bwd", lambda a, g, d: _rms_bwd(a, g, d), (T // tm_q,),
                       [_rows(proj_a, tm_q, KVL, QL // KVL), _vec(kv_norm), _rows(d_kvn, tm_q)],
                       [_rows_out((T, KVL), BF16, tm_q)], [_vec_out((1, KVL))])
    d_pa = jnp.concatenate([d_qc, d_kvc, d_kr.astype(BF16)], axis=1)

    dwa_t = _mm2("mm_dwa", "tn", d_pa, xn, F32, tm_pref=w_a_t.shape[0])
    dwb_t = _mm2("mm_dwb", "tn", d_pb, xn, BF16)
    dkr_w = dwa_t[C0:C0 + ROPE_DIM] + _rot_t(dwa_t[C0 + LANE:C0 + LANE + ROPE_DIM], 0)
    dw_in = jnp.concatenate([dwa_t[:C0].astype(BF16), dkr_w.astype(BF16), dwb_t], axis=0)
    dw_in = dw_in.reshape(N_DEV, IWS, D)
    dq3 = dwq_ext.reshape(QL, H, Q_EXT)
    dwq = jnp.concatenate([dq3[..., :NOPE_DIM],
                           dq3[..., LANE:LANE + ROPE_DIM] + _rot_t(dq3[..., 2 * LANE:2 * LANE + ROPE_DIM])], axis=-1)
    dwq = dwq.astype(BF16).reshape(QL, N_DEV, H * QK_HEAD // N_DEV).transpose(1, 0, 2)
    dwkv = dwkv.reshape(KVL, N_DEV, w_ukv.shape[2]).transpose(1, 0, 2)
    p_e, tok = _rs_pair_start("rs_e", [dw_in, dwq, dwkv], [])
    d_xn_a = _mm2("mm_dxn_a", "nn", d_pa, w_a_t, F32, deps=[tok])
    x_e, tok = _rs_chip_start("rs_e2", _rs_pair_sum("rs_e", p_e, d_xn_a), [])
    d_xn_b = _mm2("mm_dxn_b", "nn", d_pb, w_b_t, F32, deps=[tok])

    def x_bwd(x, da, db, dh, g0):
        dx, dg0 = _rms_bwd(x, g0, da + db)
        return dh + dx, dg0

    grad_x, dg_pre_mix = _ew("x_bwd", x_bwd, (T // tm_d,),
                             [_rows(x2, tm_d), _rows(d_xn_a, tm_d), _rows(d_xn_b, tm_d), _rows(dh, tm_d),
                              _vec(pre_mix_norm)],
                             [_rows_out((T, D), F32, tm_d)], [_vec_out((1, D))])

    def adam_big(n, parts, transposed=False):
        view = shard_t if transposed else (lambda a: a[0])
        res = _adamw("adamw_" + n, parts, view(given[n]), view(given["m_" + n]), view(given["v_" + n]))
        done[n] = res[0]
        return [(r.T if transposed else r)[None] for r in res]

    done = {}

    small_names = ["pre_mix_norm", "q_norm", "kv_norm", "v_ln_gain", "v_ln_bias", "w_spatial", "b_spatial",
                   "attn_out_norm", "gmlp_out_norm", "post_mix_norm", "pre_ffn_norm", "post_ffn_norm"]
    small_grads = [dg_pre_mix, dg_q, dg_kv, dlng, dlnb, dws, dbs[:, 0, :], dg_attn, dg_gmlp, dg_post_mix, dg_pre_ffn,
                   dg_post_ffn]
    n_small = sum(g.size for g in small_grads)
    pad_small = _round_up(n_small, SUBLANE * LANE) - n_small

    def pack(arrs):
        flat = jnp.concatenate([a.reshape(-1) for a in arrs] + [jnp.zeros((pad_small,), F32)])
        return flat.reshape(-1, LANE)

    st_s, tok = _ag_start("ag_s", [pack(small_grads)], [])

    results = {}
    results["w_down"] = adam_big("w_down", _a2a_finish("a2a_d", x_d, tok)[0])
    results["w_gate"] = adam_big("w_gate", _a2a_finish("a2a_g", x_g, done["w_down"])[0], True)
    st_s, tok = _ag_mid("ag_s", st_s, done["w_gate"])
    results["w_up"] = adam_big("w_up", _a2a_finish("a2a_u", x_u, tok)[0], True)
    results["w_out"] = adam_big("w_out", _a2a_finish("a2a_o", x_o, done["w_up"])[0])
    e_in, e_uq, e_ukv = _rs_chip_finish("rs_e2", x_e, done["w_out"])
    results["w_in"] = adam_big("w_in", e_in, True)
    results["w_uq"] = adam_big("w_uq", e_uq)
    results["w_ukv"] = adam_big("w_ukv", e_ukv)

    small_parts = _ag_finish("ag_s", st_s, done["w_ukv"])[0]
    sm = _adamw("adamw_small", small_parts, pack([given[n] for n in small_names]),
                pack([given["m_" + n] for n in small_names]), pack([given["v_" + n] for n in small_names]))

    off = 0
    for n in small_names:
        shp, size = given[n].shape, given[n].size
        results[n] = [r.reshape(-1)[off:off + size].reshape(shp) for r in sm]
        off += size

    loss = lax.psum(loss_part[0, 0], MESH_AXES)
    order = ["pre_mix_norm", "w_in", "q_norm", "kv_norm", "w_uq", "w_ukv", "v_ln_gain", "v_ln_bias", "w_spatial",
             "b_spatial", "attn_out_norm", "gmlp_out_norm", "w_out", "post_mix_norm", "pre_ffn_norm", "w_gate", "w_up",
             "w_down", "post_ffn_norm"]
    out = [loss, grad_x[None]]
    for kind in range(4):
        out += [results[n][kind] for n in order]
    return tuple(out)
```
